```python
import math
import jax, jax.numpy as jnp
from jax import lax
import numpy as np

D_MODEL = 1024
BATCH = 4
SEQ = 8192
DEPTH = 4

GRID_W = 64
CTX_LEN = 256
N_EVEN = (DEPTH + 1) // 2
N_ODD = DEPTH // 2

CONV_GROUPS = 4
D_CONV = 256
CONV_WIDTH = 3
RWKV_HEADS = 12
RWKV_HD = 64
D_RWKV = RWKV_HEADS * RWKV_HD
LORA_DECAY = 64
LORA_A = 64
LORA_G = 128
RWKV_COLS = 3 * D_RWKV + 2 * LORA_DECAY + 2 * LORA_A + LORA_G
D_IN_EVEN = 3 * D_CONV + RWKV_COLS
DECAY_SCALE = math.exp(-0.5)
GN_EPS = 64e-5
DIFF_HEADS = 6
DIFF_HD = 64
DIFF_VD = 2 * DIFF_HD
D_DIFF = DIFF_HEADS * DIFF_VD
AXIS_DIM = DIFF_HD // 2
ROPE_BASE = 10000.0
Q_BLOCK = 128
GMLP_GROUPS = 4
D_GMLP = 256
GMLP_GD = D_GMLP // GMLP_GROUPS
GMLP_CHUNK = 128
D_IN_ODD = 3 * D_DIFF + 2 * D_GMLP
N_EXPERTS = 16
EC_FACTOR = 2
D_EXPERT = 2048
ALPHA = (2.0 * DEPTH) ** 0.25
BETA = (8.0 * DEPTH) ** -0.25
LN_EPS = 1e-5
RMS_EPS = 1e-5

kernel_name = "hybrid_dit_conv_rwkv7_diffattn_gmlp_ecmoe"


def layer_norm(x, g, b, eps=LN_EPS):
    xf = x.astype(jnp.float32)
    mu = jnp.mean(xf, axis=-1, keepdims=True)
    var = jnp.mean(jnp.square(xf - mu), axis=-1, keepdims=True)
    return ((xf - mu) * lax.rsqrt(var + eps)).astype(x.dtype) * g + b


def rms_norm(x, g):
    xf = x.astype(jnp.float32)
    return (xf * lax.rsqrt(jnp.mean(jnp.square(xf), axis=-1, keepdims=True) + RMS_EPS)).astype(x.dtype) * g


def shift_prev(u):
    return jnp.pad(u, ((0, 0), (1, 0), (0, 0)))[:, :-1]


def shift_next(u):
    return jnp.pad(u, ((0, 0), (0, 1), (0, 0)))[:, 1:]


def short_conv_mixer(p, conv_w):
    bg, cg, xin = jnp.split(p, 3, axis=-1)
    u = cg * xin
    return bg * (conv_w[0] * shift_prev(u) + conv_w[1] * u + conv_w[2] * shift_next(u))


def rwkv_features(p, shift_mu, decay_up, decay_0, a_up, a_0, g_up, k_xi, k_alpha):
    B, T, _ = p.shape
    H, K = RWKV_HEADS, RWKV_HD
    p = p + shift_mu * (0.5 * (shift_prev(p) + shift_next(p)) - p)
    r, k, v, dd, da, dg = jnp.split(
        p, [D_RWKV, 2 * D_RWKV, 3 * D_RWKV, 3 * D_RWKV + 2 * LORA_DECAY,
            3 * D_RWKV + 2 * LORA_DECAY + 2 * LORA_A], axis=-1)
    dd = dd.reshape(B, T, 2, LORA_DECAY)
    da = da.reshape(B, T, 2, LORA_A)
    w = jnp.exp(-DECAY_SCALE * jax.nn.sigmoid(decay_0 + jnp.einsum("btdr,drc->btdc", jnp.tanh(dd), decay_up)))
    a = jax.nn.sigmoid(a_0 + jnp.einsum("btdr,drc->btdc", da, a_up))
    g = jax.nn.sigmoid(dg) @ g_up
    kk = (k * k_xi).reshape(B, T, H, K)
    kk = kk * lax.rsqrt(jnp.sum(jnp.square(kk.astype(jnp.float32)), axis=-1, keepdims=True) + 1e-12).astype(kk.dtype)
    w = w.reshape(B, T, 2, H, K)
    a = a.reshape(B, T, 2, H, K)
    k_rep = k.reshape(B, T, 1, H, K) * (1.0 + (a - 1.0) * k_alpha.reshape(H, K))
    return (r.reshape(B, T, H, K), w, kk, a, k_rep, v.reshape(B, T, H, K), g)


def wkv_scan(s0, feats, d, reverse):
    r, w, kk, a, k_rep, v, _ = feats

    def step(s, inp):
        r_t, w_t, kk_t, a_t, k_t, v_t = inp
        sa = jnp.einsum("bhvk,bhk->bhv", s, kk_t)
        s = (s * w_t[:, :, None, :] - sa[..., None] * (kk_t * a_t)[:, :, None, :]
             + v_t[..., None] * k_t[:, :, None, :])
        return s, jnp.einsum("bhvk,bhk->bhv", s, r_t)

    xs = tuple(jnp.swapaxes(z, 0, 1).astype(jnp.float32)
               for z in (r, w[:, :, d], kk, a[:, :, d], k_rep[:, :, d], v))
    s_final, ys = lax.scan(step, s0, xs, reverse=reverse)
    return s_final, jnp.swapaxes(ys, 0, 1)


def rwkv_output(y, feats, r_bonus, gn_g, gn_b):
    r, _, _, _, k_rep, v, g = feats
    B, T = r.shape[:2]
    H, K = RWKV_HEADS, RWKV_HD
    yn = layer_norm(y.astype(r.dtype), gn_g.reshape(H, K), gn_b.reshape(H, K), eps=GN_EPS)
    bonus = jnp.sum(r * jnp.mean(k_rep, axis=2) * r_bonus.reshape(H, K), axis=-1, keepdims=True) * v
    return g * (yn + bonus).reshape(B, T, D_RWKV)


def rwkv_mixer(pc, px, shift_mu, decay_up, decay_0, a_up, a_0, g_up, k_xi, k_alpha,
               r_bonus, gn_g, gn_b, ctx_out):
    fc = rwkv_features(pc, shift_mu, decay_up, decay_0, a_up, a_0, g_up, k_xi, k_alpha)
    fx = rwkv_features(px, shift_mu, decay_up, decay_0, a_up, a_0, g_up, k_xi, k_alpha)
    B = px.shape[0]
    y_c, y_x = [], []
    for d, rev in ((0, False), (1, True)):
        s0 = jnp.zeros((B, RWKV_HEADS, RWKV_HD, RWKV_HD), jnp.float32)
        s_ctx, yc_d = wkv_scan(s0, fc, d, rev)
        _, yx_d = wkv_scan(s_ctx, fx, d, rev)
        y_c.append(yc_d)
        y_x.append(yx_d)
    out_x = rwkv_output(y_x[0] + y_x[1], fx, r_bonus, gn_g, gn_b)
    out_c = rwkv_output(y_c[0] + y_c[1], fc, r_bonus, gn_g, gn_b) if ctx_out else None
    return out_c, out_x


def even_mixer(hc, hx, w_in, w_out, conv_w, shift_mu, decay_up, decay_0, a_up, a_0, g_up,
               k_xi, k_alpha, r_bonus, gn_g, gn_b, ctx_out):
    px = hx @ w_in
    pc = hc @ (w_in if ctx_out else w_in[:, 3 * D_CONV:])
    rw_c_in = pc[..., 3 * D_CONV:] if ctx_out else pc
    rw_c, rw_x = rwkv_mixer(rw_c_in, px[..., 3 * D_CONV:], shift_mu, decay_up, decay_0, a_up, a_0,
                            g_up, k_xi, k_alpha, r_bonus, gn_g, gn_b, ctx_out)
    yx = jnp.concatenate([short_conv_mixer(px[..., :3 * D_CONV], conv_w), rw_x], axis=-1) @ w_out
    yc = None
    if ctx_out:
        yc = jnp.concatenate([short_conv_mixer(pc[..., :3 * D_CONV], conv_w), rw_c], axis=-1) @ w_out
    return yc, yx


def axial_rope_tables(T):
    rows = T // GRID_W
    row = jnp.repeat(jnp.arange(rows), GRID_W).astype(jnp.float32)
    col = jnp.tile(jnp.arange(GRID_W), rows).astype(jnp.float32)
    inv = ROPE_BASE ** (-jnp.arange(0, AXIS_DIM, 2, dtype=jnp.float32) / AXIS_DIM)
    ang_r = row[:, None] * inv
    ang_c = col[:, None] * inv
    return jnp.cos(ang_r), jnp.sin(ang_r), jnp.cos(ang_c), jnp.sin(ang_c)


def rotate(x, cos, sin):
    x1, x2 = jnp.split(x, 2, axis=-1)
    cos = cos[None, :, None, None, :].astype(x.dtype)
    sin = sin[None, :, None, None, :].astype(x.dtype)
    return jnp.concatenate([x1 * cos - x2 * sin, x1 * sin + x2 * cos], axis=-1)


def axial_rope(x, tables):
    cr, sr, cc, sc = tables
    return jnp.concatenate([rotate(x[..., :AXIS_DIM], cr, sr), rotate(x[..., AXIS_DIM:], cc, sc)], axis=-1)


def diff_attn(q, k, v, lam):
    s = jnp.einsum("bhmqd,bhmkd->bhmqk", q, k).astype(jnp.float32) * (DIFF_HD ** -0.5)
    p = jax.nn.softmax(s, axis=-1)
    pd = (p[:, :, 0] - lam * p[:, :, 1]).astype(v.dtype)
    return jnp.einsum("bhqk,bhkv->bhqv", pd, v)


def diff_attention_mixer(qc, kc, vc, qx, kx, vx, lam_q1, lam_k1, lam_q2, lam_k2, subln_g, lam_init):
    B, T, _ = qx.shape
    tables = axial_rope_tables(T)

    def split_qk(z):
        return z.reshape(B, z.shape[1], DIFF_HEADS, 2, DIFF_HD)

    def to_bhm(z):
        return jnp.transpose(z, (0, 2, 3, 1, 4))

    def v_heads(z):
        return jnp.transpose(z.reshape(B, z.shape[1], DIFF_HEADS, DIFF_VD), (0, 2, 1, 3))

    def head_norm(o):
        return (rms_norm(o, subln_g) * (1.0 - lam_init)).reshape(B, o.shape[1], D_DIFF)

    lam = (jnp.exp(jnp.sum(lam_q1 * lam_k1).astype(jnp.float32))
           - jnp.exp(jnp.sum(lam_q2 * lam_k2).astype(jnp.float32)) + lam_init)
    kc_h, vc_h = to_bhm(split_qk(kc)), v_heads(vc)
    qx_h = to_bhm(axial_rope(split_qk(qx), tables))
    k_all = jnp.concatenate([kc_h, to_bhm(axial_rope(split_qk(kx), tables))], axis=3)
    v_all = jnp.concatenate([vc_h, v_heads(vx)], axis=2)
    nb = T // Q_BLOCK
    q_blocks = jnp.moveaxis(qx_h.reshape(B, DIFF_HEADS, 2, nb, Q_BLOCK, DIFF_HD), 3, 0)
    o_blocks = lax.map(lambda qb: diff_attn(qb, k_all, v_all, lam), q_blocks)
    out_x = head_norm(jnp.transpose(o_blocks, (1, 0, 3, 2, 4)).reshape(B, T, DIFF_HEADS, DIFF_VD))
    out_c = None
    if qc is not None:
        o_c = diff_attn(to_bhm(split_qk(qc)), kc_h, vc_h, lam)
        out_c = head_norm(jnp.transpose(o_c, (0, 2, 1, 3)))
    return out_c, out_x


def chunk_gmlp(p, ln_g, ln_b, ws, bs):
    B, T, _ = p.shape
    u, v = jnp.split(jax.nn.gelu(p, approximate=False), 2, axis=-1)
    v = layer_norm(v, ln_g, ln_b).reshape(B, T // GMLP_CHUNK, GMLP_CHUNK, GMLP_GROUPS, GMLP_GD)
    mixed = jnp.einsum("gpq,bnqgc->bnpgc", ws, v) + jnp.transpose(bs)[None, None, :, :, None]
    return u * mixed.reshape(B, T, D_GMLP)


def odd_mixer(hc, hx, w_in, w_out, lam_q1, lam_k1, lam_q2, lam_k2, subln_g,
              gmlp_ln_g, gmlp_ln_b, gmlp_ws, gmlp_bs, lam_init, ctx_out):
    px = hx @ w_in
    qx, kx, vx, gx = jnp.split(px, [D_DIFF, 2 * D_DIFF, 3 * D_DIFF], axis=-1)
    if ctx_out:
        qc, kc, vc, gc = jnp.split(hc @ w_in, [D_DIFF, 2 * D_DIFF, 3 * D_DIFF], axis=-1)
    else:
        kc, vc = jnp.split(hc @ w_in[:, D_DIFF:3 * D_DIFF], 2, axis=-1)
        qc, gc = None, None
    att_c, att_x = diff_attention_mixer(qc, kc, vc, qx, kx, vx, lam_q1, lam_k1, lam_q2, lam_k2,
                                        subln_g, lam_init)
    yx = jnp.concatenate([att_x, chunk_gmlp(gx, gmlp_ln_g, gmlp_ln_b, gmlp_ws, gmlp_bs)], axis=-1) @ w_out
    yc = None
    if ctx_out:
        yc = jnp.concatenate([att_c, chunk_gmlp(gc, gmlp_ln_g, gmlp_ln_b, gmlp_ws, gmlp_bs)], axis=-1) @ w_out
    return yc, yx


def expert_choice_ffn(h, w_router, w_e1, w_e3, w_e2):
    B, T, _ = h.shape
    cap = EC_FACTOR * T // N_EXPERTS
    aff = jax.nn.softmax(jnp.einsum("btd,de->bte", h, w_router).astype(jnp.float32), axis=-1)
    gate, idx = lax.top_k(jnp.swapaxes(aff, 1, 2), cap)
    b_idx = jnp.arange(B)[:, None, None]
    xs = h[b_idx, idx]
    hid = jax.nn.silu(jnp.einsum("becd,edf->becf", xs, w_e1)) * jnp.einsum("becd,edf->becf", xs, w_e3)
    ye = jnp.einsum("becf,efd->becd", hid, w_e2) * gate[..., None].astype(h.dtype)
    return jnp.zeros_like(h).at[b_idx, idx].add(ye)


def setup_inputs(seed: int = 0) -> dict:
    key = jax.random.key(seed)
    D = D_MODEL
    specs = [
        ("x", (BATCH, SEQ, D), "n", 0.0, 1.0),
        ("c", (BATCH, D), "n", 0.0, 1.0),
        ("ctx", (BATCH, CTX_LEN, D), "n", 0.0, 1.0),
        ("c_ctx", (D,), "n", 0.0, 1.0),
        ("w_mod", (DEPTH, D, 6 * D), "n", 0.0, 0.5 * D ** -0.5),
        ("b_mod", (DEPTH, 6 * D), "n", 0.0, 0.02),
        ("ln_g", (DEPTH, 2, D), "n", 1.0, 0.02),
        ("ln_b", (DEPTH, 2, D), "n", 0.0, 0.02),
        ("even_w_in", (N_EVEN, D, D_IN_EVEN), "n", 0.0, D ** -0.5),
        ("even_w_out", (N_EVEN, D_CONV + D_RWKV, D), "n", 0.0, BETA * (D_CONV + D_RWKV) ** -0.5),
        ("conv_w", (N_EVEN, CONV_WIDTH, D_CONV), "n", 0.0, CONV_WIDTH ** -0.5),
        ("shift_mu", (N_EVEN, RWKV_COLS), "u", 0.0, 1.0),
        ("decay_up", (N_EVEN, 2, LORA_DECAY, D_RWKV), "n", 0.0, 0.1 * LORA_DECAY ** -0.5),
        ("decay_0", (N_EVEN, 2, D_RWKV), "n", 0.0, 1.0),
        ("a_up", (N_EVEN, 2, LORA_A, D_RWKV), "n", 0.0, 0.5 * LORA_A ** -0.5),
        ("a_0", (N_EVEN, 2, D_RWKV), "n", 0.0, 1.0),
        ("g_up", (N_EVEN, LORA_G, D_RWKV), "n", 0.0, LORA_G ** -0.5),
        ("k_xi", (N_EVEN, D_RWKV), "n", 0.85, 0.05),
        ("k_alpha", (N_EVEN, D_RWKV), "n", 1.0, 0.05),
        ("r_bonus", (N_EVEN, D_RWKV), "n", 0.0, 0.1),
        ("gn_g", (N_EVEN, D_RWKV), "n", 1.0, 0.02),
        ("gn_b", (N_EVEN, D_RWKV), "n", 0.0, 0.02),
        ("odd_w_in", (N_ODD, D, D_IN_ODD), "n", 0.0, D ** -0.5),
        ("odd_w_out", (N_ODD, D_DIFF + D_GMLP, D), "n", 0.0, BETA * (D_DIFF + D_GMLP) ** -0.5),
        ("lam_q1", (N_ODD, DIFF_HD), "n", 0.0, 0.1),
        ("lam_k1", (N_ODD, DIFF_HD), "n", 0.0, 0.1),
        ("lam_q2", (N_ODD, DIFF_HD), "n", 0.0, 0.1),
        ("lam_k2", (N_ODD, DIFF_HD), "n", 0.0, 0.1),
        ("subln_g", (N_ODD, DIFF_VD), "n", 1.0, 0.02),
        ("gmlp_ln_g", (N_ODD, D_GMLP), "n", 1.0, 0.02),
        ("gmlp_ln_b", (N_ODD, D_GMLP), "n", 0.0, 0.02),
        ("gmlp_ws", (N_ODD, GMLP_GROUPS, GMLP_CHUNK, GMLP_CHUNK), "n", 0.0, 0.5 * GMLP_CHUNK ** -0.5),
        ("gmlp_bs", (N_ODD, GMLP_GROUPS, GMLP_CHUNK), "n", 1.0, 0.1),
        ("w_router", (DEPTH, D, N_EXPERTS), "n", 0.0, D ** -0.5),
        ("w_e1", (DEPTH, N_EXPERTS, D, D_EXPERT), "n", 0.0, D ** -0.5),
        ("w_e3", (DEPTH, N_EXPERTS, D, D_EXPERT), "n", 0.0, D ** -0.5),
        ("w_e2", (DEPTH, N_EXPERTS, D_EXPERT, D), "n", 0.0, BETA * D_EXPERT ** -0.5),
    ]
    keys = jax.random.split(key, len(specs))
    out = {}
    for i, (name, shape, kind, centre, scale) in enumerate(specs):
        if kind == "u":
            sample = jax.random.uniform(keys[i], shape, jnp.float32)
        else:
            sample = jax.random.normal(keys[i], shape, jnp.float32)
        out[name] = centre + scale * sample
    return out


def reference(x, c, ctx, c_ctx, w_mod, b_mod, ln_g, ln_b, even_w_in, even_w_out, conv_w, shift_mu,
              decay_up, decay_0, a_up, a_0, g_up, k_xi, k_alpha, r_bonus, gn_g, gn_b,
              odd_w_in, odd_w_out, lam_q1, lam_k1, lam_q2, lam_k2, subln_g, gmlp_ln_g, gmlp_ln_b,
              gmlp_ws, gmlp_bs, w_router, w_e1, w_e3, w_e2):
    for l in range(DEPTH):
        ctx_out = l < DEPTH - 1
        i = l // 2
        mod_x = jax.nn.silu(c) @ w_mod[l] + b_mod[l]
        mod_c = jax.nn.silu(c_ctx) @ w_mod[l] + b_mod[l]
        sh_m, sc_m, g_m, sh_f, sc_f, g_f = jnp.split(mod_x[:, None, :], 6, axis=-1)
        csh_m, csc_m, cg_m, csh_f, csc_f, cg_f = jnp.split(mod_c, 6)
        hx = x * (1.0 + sc_m) + sh_m
        hc = ctx * (1.0 + csc_m) + csh_m
        if l % 2 == 0:
            yc, yx = even_mixer(hc, hx, even_w_in[i], even_w_out[i], conv_w[i], shift_mu[i],
                                decay_up[i], decay_0[i], a_up[i], a_0[i], g_up[i], k_xi[i],
                                k_alpha[i], r_bonus[i], gn_g[i], gn_b[i], ctx_out)
        else:
            lam_init = 0.8 - 0.6 * math.exp(-0.3 * l)
            yc, yx = odd_mixer(hc, hx, odd_w_in[i], odd_w_out[i], lam_q1[i], lam_k1[i], lam_q2[i],
                               lam_k2[i], subln_g[i], gmlp_ln_g[i], gmlp_ln_b[i], gmlp_ws[i],
                               gmlp_bs[i], lam_init, ctx_out)
        x = layer_norm(ALPHA * x + g_m * yx, ln_g[l, 0], ln_b[l, 0])
        x = layer_norm(ALPHA * x + g_f * expert_choice_ffn(x * (1.0 + sc_f) + sh_f, w_router[l],
                                                           w_e1[l], w_e3[l], w_e2[l]),
                       ln_g[l, 1], ln_b[l, 1])
        if ctx_out:
            ctx = layer_norm(ALPHA * ctx + cg_m * yc, ln_g[l, 0], ln_b[l, 0])
            ctx = layer_norm(ALPHA * ctx + cg_f * expert_choice_ffn(ctx * (1.0 + csc_f) + csh_f, w_router[l],
                                                                    w_e1[l], w_e3[l], w_e2[l]),
                             ln_g[l, 1], ln_b[l, 1])
    return x
```

```python
import functools
import math

import jax
import jax.numpy as jnp
from jax import lax
from jax.experimental import pallas as pl
from jax.experimental.pallas import tpu as pltpu

F32 = jnp.float32
BF16 = jnp.bfloat16
HI = lax.Precision.HIGHEST

D_MODEL = 1024
DEPTH = 4
GRID_W = 64
D_CONV = 256
RWKV_HEADS = 12
RWKV_HD = 64
D_RWKV = RWKV_HEADS * RWKV_HD
LORA = 64
LORA_G = 128
RWKV_COLS = 3 * D_RWKV + 4 * LORA + LORA_G
DECAY_SCALE = math.exp(-0.5)
GN_EPS = 64e-5
N_PAIRS = RWKV_HEADS // 2
SCAN_CHUNK = 64
DIFF_HEADS = 6
DIFF_HD = 64
DIFF_VD = 2 * DIFF_HD
D_DIFF = DIFF_HEADS * DIFF_VD
AXIS_DIM = DIFF_HD // 2
ROPE_BASE = 10000.0
D_GMLP = 256
GMLP_CHUNK = 128
N_EXPERTS = 16
EC_FACTOR = 2
D_EXPERT = 2048
ALPHA = (2.0 * DEPTH) ** 0.25
LN_EPS = 1e-5
RMS_EPS = 1e-5

LANES = 128
SUBLANES = 8
VMEM_LIMIT = 56 * 1024 * 1024


def _cparams(sem):
    return pltpu.CompilerParams(dimension_semantics=sem, vmem_limit_bytes=VMEM_LIMIT)


def _bdot(a, b):
    return jnp.dot(a.astype(BF16), b.astype(BF16), preferred_element_type=F32)


def _hdot(a, b):
    return jnp.dot(a, b, preferred_element_type=F32, precision=HI)


def _half_mask(shape, dtype=F32):
    lane = lax.broadcasted_iota(jnp.int32, shape, len(shape) - 1)
    return (lane < 64).astype(dtype)


def _seg_ones():
    i = lax.broadcasted_iota(jnp.int32, (LANES, LANES), 0)
    j = lax.broadcasted_iota(jnp.int32, (LANES, LANES), 1)
    return ((i // 64) == (j // 64)).astype(F32)


def _layer_norm_rows(z, g, b, eps):
    mu = jnp.mean(z, axis=-1, keepdims=True)
    zc = z - mu
    var = jnp.mean(zc * zc, axis=-1, keepdims=True)
    return zc * lax.rsqrt(var + eps) * g + b


def _mod_kernel(c_ref, w_ref, b_ref, o_ref):
    a = c_ref[...]
    a = a * jax.nn.sigmoid(a)
    o_ref[0] = _hdot(a, w_ref[0]) + b_ref[0]


def _modulation(c8, w_mod, b_mod):
    tn = 1536
    nl, d, n6 = w_mod.shape
    return pl.pallas_call(
        _mod_kernel,
        grid=(nl, n6 // tn),
        in_specs=[pl.BlockSpec((8, d), lambda l, j: (0, 0)),
                  pl.BlockSpec((1, d, tn), lambda l, j: (l, 0, j)),
                  pl.BlockSpec((1, 1, tn), lambda l, j: (l, 0, j))],
        out_specs=pl.BlockSpec((1, 8, tn), lambda l, j: (l, 0, j)),
        out_shape=jax.ShapeDtypeStruct((nl, 8, n6), F32),
        compiler_params=_cparams(("parallel", "parallel")),
        name="modulation",
    )(c8, w_mod, b_mod.reshape(nl, 1, n6))


def _inproj_kernel(x_ref, sc_ref, sh_ref, w_ref, *o_refs, splits):
    h = x_ref[0] * (1.0 + sc_ref[0]) + sh_ref[0]
    y = _bdot(h, w_ref[...])
    off = 0
    for o, s in zip(o_refs, splits):
        o[0] = y[:, off:off + s].astype(o.dtype)
        off += s


def _inproj(x, sc, sh, w_bf16, splits, tm=256):
    b, t, d = x.shape
    tm = min(tm, t)
    dout = w_bf16.shape[1]
    return pl.pallas_call(
        functools.partial(_inproj_kernel, splits=splits),
        grid=(b, t // tm),
        in_specs=[pl.BlockSpec((1, tm, d), lambda i, j: (i, j, 0)),
                  pl.BlockSpec((1, 1, d), lambda i, j: (i, 0, 0)),
                  pl.BlockSpec((1, 1, d), lambda i, j: (i, 0, 0)),
                  pl.BlockSpec((d, dout), lambda i, j: (0, 0))],
        out_specs=[pl.BlockSpec((1, tm, s), lambda i, j: (i, j, 0)) for s in splits],
        out_shape=[jax.ShapeDtypeStruct((b, t, s), F32) for s in splits],
        compiler_params=_cparams(("parallel", "parallel")),
        name="inproj",
    )(x, sc, sh, w_bf16)


def _shifted(p, prev_row, next_row):
    tb = p.shape[0]
    row = lax.broadcasted_iota(jnp.int32, p.shape, 0)
    p_prev = jnp.where(row == 0, prev_row, pltpu.roll(p, 1, axis=0))
    p_next = jnp.where(row == tb - 1, next_row, pltpu.roll(p, tb - 1, axis=0))
    return p_prev, p_next


def _edge_rows(pp_ref, pn_ref):
    i = pl.program_id(1)
    nb = pl.num_programs(1)
    prev_row = jnp.where(i > 0, pp_ref[0, SUBLANES - 1:SUBLANES, :], 0.0)
    next_row = jnp.where(i < nb - 1, pn_ref[0, 0:1, :], 0.0)
    return prev_row, next_row


def _feat_kernel(p_ref, pp_ref, pn_ref, mu_ref, dup_ref, d0_ref, aup_ref, a0_ref, gup_ref, kxi_ref, kal_ref,
                 rb_ref, r_o, kk_o, v_o, kr0_o, kr1_o, lw0_o, lw1_o, b0_o, b1_o, g_o, gbv_o):
    p = p_ref[0]
    prev_row, next_row = _edge_rows(pp_ref, pn_ref)
    p_prev, p_next = _shifted(p, prev_row, next_row)
    p = p + mu_ref[...] * (0.5 * (p_prev + p_next) - p)
    dr = D_RWKV
    r = p[:, 0:dr]
    k = p[:, dr:2 * dr]
    v = p[:, 2 * dr:3 * dr]
    c0 = 3 * dr
    dd = jnp.tanh(p[:, c0:c0 + 2 * LORA])
    da = p[:, c0 + 2 * LORA:c0 + 4 * LORA]
    dg = jax.nn.sigmoid(p[:, c0 + 4 * LORA:c0 + 4 * LORA + LORA_G])
    g = _hdot(dg, gup_ref[...])
    lw, a = [], []
    for d in range(2):
        lw.append(-DECAY_SCALE * jax.nn.sigmoid(d0_ref[d:d + 1, :] + _hdot(dd[:, d * LORA:(d + 1) * LORA], dup_ref[d])))
        a.append(jax.nn.sigmoid(a0_ref[d:d + 1, :] + _hdot(da[:, d * LORA:(d + 1) * LORA], aup_ref[d])))
    kx = k * kxi_ref[...]
    kal = kal_ref[...]
    kr = [k * (1.0 + (a[d] - 1.0) * kal) for d in range(2)]
    bon = r * (0.5 * (kr[0] + kr[1])) * rb_ref[...]
    ones = _seg_ones()
    g_o[0] = g
    for j in range(N_PAIRS):
        sl = slice(j * LANES, (j + 1) * LANES)
        kxj = kx[:, sl]
        kk = kxj * lax.rsqrt(_hdot(kxj * kxj, ones) + 1e-12)
        r_o[0, j] = r[:, sl]
        kk_o[0, j] = kk
        v_o[0, j] = v[:, sl]
        kr0_o[0, j] = kr[0][:, sl]
        kr1_o[0, j] = kr[1][:, sl]
        lw0_o[0, j] = lw[0][:, sl]
        lw1_o[0, j] = lw[1][:, sl]
        b0_o[0, j] = kk * a[0][:, sl]
        b1_o[0, j] = kk * a[1][:, sl]
        gbv_o[0, :, sl] = g[:, sl] * _hdot(bon[:, sl], ones) * v[:, sl]


def _rwkv_features(p, prm, tb=256):
    b, t, cols = p.shape
    tb = min(tb, t)
    nb8 = t // SUBLANES
    r8 = tb // SUBLANES
    full = lambda a: pl.BlockSpec(a.shape, lambda i, j: (0,) * a.ndim)
    params = [prm["shift_mu"].reshape(1, cols), prm["decay_up"], prm["decay_0"], prm["a_up"], prm["a_0"],
              prm["g_up"], prm["k_xi"].reshape(1, -1), prm["k_alpha"].reshape(1, -1), prm["r_bonus"].reshape(1, -1)]
    packed = jax.ShapeDtypeStruct((b, N_PAIRS, t, LANES), F32)
    flat = jax.ShapeDtypeStruct((b, t, D_RWKV), F32)
    pspec = pl.BlockSpec((1, N_PAIRS, tb, LANES), lambda i, j: (i, 0, j, 0))
    fspec = pl.BlockSpec((1, tb, D_RWKV), lambda i, j: (i, j, 0))
    return pl.pallas_call(
        _feat_kernel,
        grid=(b, t // tb),
        in_specs=[pl.BlockSpec((1, tb, cols), lambda i, j: (i, j, 0)),
                  pl.BlockSpec((1, SUBLANES, cols), lambda i, j: (i, jnp.maximum(j * r8 - 1, 0), 0)),
                  pl.BlockSpec((1, SUBLANES, cols), lambda i, j: (i, jnp.minimum((j + 1) * r8, nb8 - 1), 0))]
                 + [full(a) for a in params],
        out_specs=[pspec] * 9 + [fspec] * 2,
        out_shape=[packed] * 9 + [flat] * 2,
        compiler_params=_cparams(("parallel", "parallel")),
        name="rwkv_features",
    )(p, p, p, *params)


def _scan_chunk(r, lw, kap, bb, kr, v, st, rev):
    n = SCAN_CHUNK
    m0 = _half_mask((n, LANES))
    m1 = 1.0 - m0
    ti = lax.broadcasted_iota(jnp.int32, (n, n), 0)
    tj = lax.broadcasted_iota(jnp.int32, (n, n), 1)
    tri = ((ti <= tj) if rev else (ti >= tj)).astype(F32)
    cs = _hdot(tri, lw)
    tot = jnp.sum(lw, axis=0, keepdims=True)
    e_neg = jnp.exp(-cs)
    e_tot = jnp.exp(tot - cs)

    def stack(x):
        return jnp.concatenate([x * m0, x * m1], axis=0).astype(BF16)

    kh_s = stack(kap * jnp.exp(cs - lw))
    bh_s = stack(bb * e_neg)
    kq_s = stack(kr * e_neg)
    rh = r * jnp.exp(cs)
    rh_s = stack(rh)
    kt_s = stack(kr * e_tot)
    bt_s = stack(bb * e_tot)
    v_s = stack(v)

    lhs = jnp.concatenate([kh_s, rh_s], axis=0)
    rhs = jnp.concatenate([bh_s, kq_s], axis=0)
    amat = lax.dot_general(lhs, rhs, (((1,), (1,)), ((), ())), preferred_element_type=F32)
    si = lax.broadcasted_iota(jnp.int32, (2 * n, 2 * n), 0)
    sj = lax.broadcasted_iota(jnp.int32, (2 * n, 2 * n), 1)
    same = (si // n) == (sj // n)
    ri, rj = si % n, sj % n
    strict = same & ((ri < rj) if rev else (ri > rj))
    incl = same & ((ri <= rj) if rev else (ri >= rj))
    a1 = jnp.where(strict, amat[:2 * n, :2 * n], 0.0)
    a2 = jnp.where(strict, amat[:2 * n, 2 * n:], 0.0)
    a4 = jnp.where(incl, amat[2 * n:, :2 * n], 0.0)
    a3 = jnp.where(incl, amat[2 * n:, 2 * n:], 0.0)
    eye = (si == sj).astype(F32)
    tinv = eye - a1
    pw = a1
    for _ in range(5):
        pw = _bdot(pw, pw)
        tinv = tinv + _bdot(tinv, pw)
    w_s = _bdot(a2, v_s)
    mm = _bdot(tinv, jnp.concatenate([kh_s, w_s.astype(BF16)], axis=1))
    gmat = _bdot(bt_s.astype(F32).T, mm)
    phi_t = eye * jnp.exp(tot) - gmat[:, :LANES]
    psi_t = _bdot(kt_s.astype(F32).T, v_s) - gmat[:, LANES:]
    qy = _bdot(a4, mm)
    y0_s = _bdot(a3, v_s) - qy[:, LANES:]
    q = rh - (qy[:n, :LANES] + qy[n:, :LANES])
    y0 = y0_s[:n] + y0_s[n:]
    y = _bdot(q, st) + y0
    st_new = _bdot(phi_t, st) + psi_t
    return y, st_new


def _scan_kernel(r_ref, lw_ref, kap_ref, b_ref, kr_ref, v_ref, s0_ref, y_ref, sT_ref, st_scr, *, rev):
    c = pl.program_id(1)

    @pl.when(c == 0)
    def _():
        st_scr[...] = s0_ref[0]

    for hp in range(N_PAIRS):
        y, st = _scan_chunk(r_ref[0, hp], lw_ref[0, hp], kap_ref[0, hp], b_ref[0, hp], kr_ref[0, hp],
                            v_ref[0, hp], st_scr[hp], rev)
        y_ref[0, hp] = y
        st_scr[hp] = st

    @pl.when(c == pl.num_programs(1) - 1)
    def _():
        sT_ref[0] = st_scr[...]


def _rwkv_scan(r, lw, kap, bb, kr, v, s0, rev):
    b, _, t, _ = r.shape
    n = SCAN_CHUNK
    nc = t // n
    cidx = (lambda c: nc - 1 - c) if rev else (lambda c: c)
    fspec = pl.BlockSpec((1, N_PAIRS, n, LANES), lambda i, c: (i, 0, cidx(c), 0))
    sspec = pl.BlockSpec((1, N_PAIRS, LANES, LANES), lambda i, c: (i, 0, 0, 0))
    return pl.pallas_call(
        functools.partial(_scan_kernel, rev=rev),
        grid=(b, nc),
        in_specs=[fspec] * 6 + [sspec],
        out_specs=[fspec, sspec],
        out_shape=[jax.ShapeDtypeStruct(r.shape, F32), jax.ShapeDtypeStruct(s0.shape, F32)],
        scratch_shapes=[pltpu.VMEM((N_PAIRS, LANES, LANES), F32)],
        compiler_params=_cparams(("parallel", "arbitrary")),
        name="rwkv_scan_rev" if rev else "rwkv_scan_fwd",
    )(r, lw, kap, bb, kr, v, s0)


def _evenpost_kernel(y0_ref, y1_ref, g_ref, gbv_ref, pc_ref, pcp_ref, pcn_ref, cw_ref, gng_ref, gnb_ref,
                     conv_o, rw_o):
    pc = pc_ref[0]
    prev_row, next_row = _edge_rows(pcp_ref, pcn_ref)
    dc = D_CONV

    def gated(z):
        return z[:, dc:2 * dc] * z[:, 2 * dc:3 * dc]

    u = gated(pc)
    u_prev, u_next = _shifted(u, gated(prev_row), gated(next_row))
    cw = cw_ref[...]
    conv_o[0] = (pc[:, 0:dc] * (cw[0:1] * u_prev + cw[1:2] * u + cw[2:3] * u_next)).astype(conv_o.dtype)
    ones = _seg_ones() * (1.0 / RWKV_HD)
    for j in range(N_PAIRS):
        sl = slice(j * LANES, (j + 1) * LANES)
        y = y0_ref[0, j] + y1_ref[0, j]
        mu = _hdot(y, ones)
        yc = y - mu
        var = _hdot(yc * yc, ones)
        yn = yc * lax.rsqrt(var + GN_EPS) * gng_ref[:, sl] + gnb_ref[:, sl]
        rw_o[0, :, sl] = (g_ref[0, :, sl] * yn + gbv_ref[0, :, sl]).astype(rw_o.dtype)


def _even_post(y0, y1, g, gbv, pc, conv_w, gn_g, gn_b, tb=256):
    b, _, t, _ = y0.shape
    tb = min(tb, t)
    nb8 = t // SUBLANES
    r8 = tb // SUBLANES
    c3 = pc.shape[-1]
    pspec = pl.BlockSpec((1, N_PAIRS, tb, LANES), lambda i, j: (i, 0, j, 0))
    fspec = pl.BlockSpec((1, tb, D_RWKV), lambda i, j: (i, j, 0))
    full = lambda a: pl.BlockSpec(a.shape, lambda i, j: (0,) * a.ndim)
    gn_g = gn_g.reshape(1, -1)
    gn_b = gn_b.reshape(1, -1)
    return pl.pallas_call(
        _evenpost_kernel,
        grid=(b, t // tb),
        in_specs=[pspec, pspec, fspec, fspec,
                  pl.BlockSpec((1, tb, c3), lambda i, j: (i, j, 0)),
                  pl.BlockSpec((1, SUBLANES, c3), lambda i, j: (i, jnp.maximum(j * r8 - 1, 0), 0)),
                  pl.BlockSpec((1, SUBLANES, c3), lambda i, j: (i, jnp.minimum((j + 1) * r8, nb8 - 1), 0)),
                  full(conv_w), full(gn_g), full(gn_b)],
        out_specs=[pl.BlockSpec((1, tb, D_CONV), lambda i, j: (i, j, 0)), fspec],
        out_shape=[jax.ShapeDtypeStruct((b, t, D_CONV), BF16), jax.ShapeDtypeStruct((b, t, D_RWKV), BF16)],
        compiler_params=_cparams(("parallel", "parallel")),
        name="even_post",
    )(y0, y1, g, gbv, pc, pc, pc, conv_w, gn_g, gn_b)


def _outproj_kernel(a_ref, b_ref, wa_ref, wb_ref, x_ref, gm_ref, lg_ref, lb_ref, o_ref):
    yx = _bdot(a_ref[0], wa_ref[...]) + _bdot(b_ref[0], wb_ref[...])
    z = ALPHA * x_ref[0] + gm_ref[0] * yx
    o_ref[0] = _layer_norm_rows(z, lg_ref[...], lb_ref[...], LN_EPS)


def _outproj(ma, mb, w_out_bf16, x, gm, ln_g, ln_b, tm=512):
    b, t, d = x.shape
    tm = min(tm, t)
    da, db = ma.shape[-1], mb.shape[-1]
    wa, wb = w_out_bf16[:da], w_out_bf16[da:]
    row = lambda w: pl.BlockSpec((1, tm, w), lambda i, j: (i, j, 0))
    full = lambda a: pl.BlockSpec(a.shape, lambda i, j: (0,) * a.ndim)
    ln_g = ln_g.reshape(1, d)
    ln_b = ln_b.reshape(1, d)
    return pl.pallas_call(
        _outproj_kernel,
        grid=(b, t // tm),
        in_specs=[row(da), row(db), full(wa), full(wb), row(d),
                  pl.BlockSpec((1, 1, d), lambda i, j: (i, 0, 0)), full(ln_g), full(ln_b)],
        out_specs=row(d),
        out_shape=jax.ShapeDtypeStruct((b, t, d), F32),
        compiler_params=_cparams(("parallel", "parallel")),
        name="outproj_norm",
    )(ma, mb, wa, wb, x, gm, ln_g, ln_b)


def _rope_kernel(x_ref, cos_ref, sin_ref, o_ref, *, scale):
    cos = cos_ref[...]
    sin = sin_ref[...]
    lane = lax.broadcasted_iota(jnp.int32, cos.shape, 1)
    first = (lane % (2 * (AXIS_DIM // 2))) < (AXIS_DIM // 2)
    half = AXIS_DIM // 2
    for j in range(DIFF_HEADS):
        sl = slice(j * LANES, (j + 1) * LANES)
        xs = x_ref[0, :, sl]
        partner = jnp.where(first, -pltpu.roll(xs, LANES - half, axis=1), pltpu.roll(xs, half, axis=1))
        o_ref[0, :, sl] = ((xs * cos + partner * sin) * scale).astype(o_ref.dtype)


def _rope(x, cos, sin, scale, tb=512):
    b, t, w = x.shape
    tb = min(tb, t)
    return pl.pallas_call(
        functools.partial(_rope_kernel, scale=scale),
        grid=(b, t // tb),
        in_specs=[pl.BlockSpec((1, tb, w), lambda i, j: (i, j, 0)),
                  pl.BlockSpec((tb, LANES), lambda i, j: (j, 0)),
                  pl.BlockSpec((tb, LANES), lambda i, j: (j, 0))],
        out_specs=pl.BlockSpec((1, tb, w), lambda i, j: (i, j, 0)),
        out_shape=jax.ShapeDtypeStruct((b, t, w), BF16),
        compiler_params=_cparams(("parallel", "parallel")),
        name="axial_rope",
    )(x, cos, sin)


def _rope_tables(t):
    rows = t // GRID_W
    row = jnp.repeat(jnp.arange(rows), GRID_W).astype(F32)
    col = jnp.tile(jnp.arange(GRID_W), rows).astype(F32)
    inv = ROPE_BASE ** (-jnp.arange(0, AXIS_DIM, 2, dtype=F32) / AXIS_DIM)
    ang_r = row[:, None] * inv
    ang_c = col[:, None] * inv
    ang = jnp.concatenate([ang_r, ang_r, ang_c, ang_c], axis=-1)
    ang = jnp.concatenate([ang, ang], axis=-1)
    return jnp.cos(ang), jnp.sin(ang)


def _attn_kernel(lam_ref, q_ref, k_ref, v_ref, g_ref, o_ref, *, tk, out_scale):
    q = q_ref[0]
    tq = q.shape[0]
    hm = _half_mask(q.shape, BF16)
    q1 = q * hm
    q2 = q - q1
    nk = k_ref.shape[1] // tk

    def body(j, carry):
        m1, l1, acc1, m2, l2, acc2 = carry
        start = pl.multiple_of(j * tk, tk)
        kb = k_ref[0, pl.ds(start, tk), :]
        vb = v_ref[0, pl.ds(start, tk), :]

        def upd(qh, m, l, acc):
            s = lax.dot_general(qh, kb, (((1,), (1,)), ((), ())), preferred_element_type=F32)
            mn = jnp.maximum(m, jnp.max(s, axis=-1, keepdims=True))
            p = jnp.exp(s - mn)
            corr = jnp.exp(m - mn)
            l = corr * l + jnp.sum(p, axis=-1, keepdims=True)
            acc = corr * acc + jnp.dot(p.astype(BF16), vb, preferred_element_type=F32)
            return mn, l, acc

        m1, l1, acc1 = upd(q1, m1, l1, acc1)
        m2, l2, acc2 = upd(q2, m2, l2, acc2)
        return m1, l1, acc1, m2, l2, acc2

    neg = jnp.full((tq, 1), -1e30, F32)
    zero1 = jnp.zeros((tq, 1), F32)
    zacc = jnp.zeros((tq, LANES), F32)
    m1, l1, acc1, m2, l2, acc2 = lax.fori_loop(0, nk, body, (neg, zero1, zacc, neg, zero1, zacc))
    o = acc1 / l1 - lam_ref[0] * (acc2 / l2)
    o = o * lax.rsqrt(jnp.mean(o * o, axis=-1, keepdims=True) + RMS_EPS) * g_ref[...] * out_scale
    o_ref[0] = o.astype(o_ref.dtype)


def _diff_attention(q, k, v, lam, subln_g, lam_init, tq=256, tk=256):
    b, t, w = q.shape
    tkk = k.shape[1]
    tq = min(tq, t)
    tk = math.gcd(tk, tkk)
    assert tk % LANES == 0 and t % tq == 0
    subln_g = subln_g.reshape(1, DIFF_VD)
    return pl.pallas_call(
        functools.partial(_attn_kernel, tk=tk, out_scale=1.0 - lam_init),
        grid=(b, DIFF_HEADS, t // tq),
        in_specs=[pl.BlockSpec(memory_space=pltpu.SMEM),
                  pl.BlockSpec((1, tq, LANES), lambda i, h, j: (i, j, h)),
                  pl.BlockSpec((1, tkk, LANES), lambda i, h, j: (i, 0, h)),
                  pl.BlockSpec((1, tkk, LANES), lambda i, h, j: (i, 0, h)),
                  pl.BlockSpec((1, DIFF_VD), lambda i, h, j: (0, 0))],
        out_specs=pl.BlockSpec((1, tq, LANES), lambda i, h, j: (i, j, h)),
        out_shape=jax.ShapeDtypeStruct((b, t, w), BF16),
        compiler_params=_cparams(("parallel", "parallel", "parallel")),
        name="diff_attention",
    )(lam, q, k, v, subln_g)


def _gmlp_kernel(p_ref, lg_ref, lb_ref, ws_ref, bs_ref, o_ref):
    p = p_ref[0]
    tb = p.shape[0]
    ge = 0.5 * p * (1.0 + lax.erf(p * (2.0 ** -0.5)))
    u = ge[:, :D_GMLP]
    v = _layer_norm_rows(ge[:, D_GMLP:], lg_ref[...], lb_ref[...], LN_EPS)
    hm = _half_mask((GMLP_CHUNK, LANES))
    for c in range(tb // GMLP_CHUNK):
        rs = slice(c * GMLP_CHUNK, (c + 1) * GMLP_CHUNK)
        for j in range(D_GMLP // LANES):
            sl = slice(j * LANES, (j + 1) * LANES)
            vc = v[rs, sl]
            va = vc * hm
            mixed = _bdot(ws_ref[2 * j], va) + _bdot(ws_ref[2 * j + 1], vc - va) + bs_ref[:, sl]
            o_ref[0, rs, sl] = (u[rs, sl] * mixed).astype(o_ref.dtype)


def _chunk_gmlp(p, ln_g, ln_b, ws, bs, tb=256):
    b, t, w = p.shape
    tb = min(tb, t)
    ln_g = ln_g.reshape(1, -1)
    ln_b = ln_b.reshape(1, -1)
    bs_t = jnp.repeat(jnp.transpose(bs), D_GMLP // ws.shape[0], axis=1)
    full = lambda a: pl.BlockSpec(a.shape, lambda i, j: (0,) * a.ndim)
    return pl.pallas_call(
        _gmlp_kernel,
        grid=(b, t // tb),
        in_specs=[pl.BlockSpec((1, tb, w), lambda i, j: (i, j, 0)), full(ln_g), full(ln_b), full(ws), full(bs_t)],
        out_specs=pl.BlockSpec((1, tb, D_GMLP), lambda i, j: (i, j, 0)),
        out_shape=jax.ShapeDtypeStruct((b, t, D_GMLP), BF16),
        compiler_params=_cparams(("parallel", "parallel")),
        name="chunk_gmlp",
    )(p, ln_g, ln_b, ws, bs_t)


def _router_kernel(x_ref, sc_ref, sh_ref, wr_ref, h_o, aff_o):
    h = x_ref[0] * (1.0 + sc_ref[0]) + sh_ref[0]
    h_o[0] = h.astype(h_o.dtype)
    logits = lax.dot_general(wr_ref[...], h, (((1,), (1,)), ((), ())), preferred_element_type=F32, precision=HI)
    m = jnp.max(logits, axis=0, keepdims=True)
    e = jnp.exp(logits - m)
    aff_o[0] = e / jnp.sum(e, axis=0, keepdims=True)


def _router(x, sc, sh, w_router_t, tm=512):
    b, t, d = x.shape
    tm = min(tm, t)
    e = w_router_t.shape[0]
    return pl.pallas_call(
        _router_kernel,
        grid=(b, t // tm),
        in_specs=[pl.BlockSpec((1, tm, d), lambda i, j: (i, j, 0)),
                  pl.BlockSpec((1, 1, d), lambda i, j: (i, 0, 0)),
                  pl.BlockSpec((1, 1, d), lambda i, j: (i, 0, 0)),
                  pl.BlockSpec((e, d), lambda i, j: (0, 0))],
        out_specs=[pl.BlockSpec((1, tm, d), lambda i, j: (i, j, 0)),
                   pl.BlockSpec((1, e, tm), lambda i, j: (i, 0, j))],
        out_shape=[jax.ShapeDtypeStruct((b, t, d), BF16), jax.ShapeDtypeStruct((b, e, t), F32)],
        compiler_params=_cparams(("parallel", "parallel")),
        name="router",
    )(x, sc, sh, w_router_t)


def _ffn_kernel(xs_ref, gate_ref, w1_ref, w3_ref, w2_ref, o_ref):
    x = xs_ref[0, 0]
    h1 = jnp.dot(x, w1_ref[0], preferred_element_type=F32)
    h3 = jnp.dot(x, w3_ref[0], preferred_element_type=F32)
    hid = (h1 * jax.nn.sigmoid(h1)) * h3
    y = jnp.dot(hid.astype(BF16), w2_ref[0], preferred_element_type=F32)
    o_ref[0, 0] = (y * gate_ref[0, 0]).astype(o_ref.dtype)


def _expert_ffn(xs, gate, w1, w3, w2, tc=512):
    b, e, c, d = xs.shape
    f = w1.shape[-1]
    tc = min(tc, c)
    return pl.pallas_call(
        _ffn_kernel,
        grid=(e, b, c // tc),
        in_specs=[pl.BlockSpec((1, 1, tc, d), lambda ei, bi, ci: (bi, ei, ci, 0)),
                  pl.BlockSpec((1, 1, tc, 1), lambda ei, bi, ci: (bi, ei, ci, 0)),
                  pl.BlockSpec((1, d, f), lambda ei, bi, ci: (ei, 0, 0)),
                  pl.BlockSpec((1, d, f), lambda ei, bi, ci: (ei, 0, 0)),
                  pl.BlockSpec((1, f, d), lambda ei, bi, ci: (ei, 0, 0))],
        out_specs=pl.BlockSpec((1, 1, tc, d), lambda ei, bi, ci: (bi, ei, ci, 0)),
        out_shape=jax.ShapeDtypeStruct((b, e, c, d), F32),
        compiler_params=_cparams(("parallel", "parallel", "parallel")),
        name="expert_ffn",
    )(xs, gate, w1, w3, w2)


def _resnorm_kernel(x_ref, f_ref, gf_ref, lg_ref, lb_ref, o_ref):
    z = ALPHA * x_ref[0] + gf_ref[0] * f_ref[0]
    o_ref[0] = _layer_norm_rows(z, lg_ref[...], lb_ref[...], LN_EPS)


def _resnorm(x, f, gf, ln_g, ln_b, tm=512):
    b, t, d = x.shape
    tm = min(tm, t)
    row = pl.BlockSpec((1, tm, d), lambda i, j: (i, j, 0))
    vec = pl.BlockSpec((1, d), lambda i, j: (0, 0))
    return pl.pallas_call(
        _resnorm_kernel,
        grid=(b, t // tm),
        in_specs=[row, row, pl.BlockSpec((1, 1, d), lambda i, j: (i, 0, 0)), vec, vec],
        out_specs=row,
        out_shape=jax.ShapeDtypeStruct((b, t, d), F32),
        compiler_params=_cparams(("parallel", "parallel")),
        name="residual_norm",
    )(x, f, gf, ln_g.reshape(1, d), ln_b.reshape(1, d))


def _moe(x, sc, sh, gf, w_router_t, w1, w3, w2, ln_g, ln_b):
    b, t, d = x.shape
    cap = EC_FACTOR * t // N_EXPERTS
    h, aff = _router(x, sc, sh, w_router_t)
    gate, idx = lax.top_k(aff, cap)
    xs = jnp.take_along_axis(h[:, None], idx[..., None], axis=2)
    ye = _expert_ffn(xs, gate[..., None], w1, w3, w2)
    b_idx = jnp.arange(b)[:, None, None]
    f = jnp.zeros((b, t, d), F32).at[b_idx, idx].add(ye)
    return _resnorm(x, f, gf, ln_g, ln_b)


def _even_layer(hx_in, hc_in, prm, ctx_out):
    w_in = prm["w_in"]
    splits = (3 * D_CONV, RWKV_COLS)
    pcx, prx = _inproj(*hx_in, w_in, splits)
    pcc, prc = _inproj(*hc_in, w_in, splits)
    fx = _rwkv_features(prx, prm)
    fc = _rwkv_features(prc, prm)
    b = prx.shape[0]
    zero = jnp.zeros((b, N_PAIRS, LANES, LANES), F32)
    ys_x, ys_c = [], []
    for d, rev in ((0, False), (1, True)):
        def args(f):
            r, kk, v, kr0, kr1, lw0, lw1, b0, b1 = f[:9]
            return (r, (lw0, lw1)[d], kk, (b0, b1)[d], (kr0, kr1)[d], v)
        yc, s_ctx = _rwkv_scan(*args(fc), zero, rev)
        yx, _ = _rwkv_scan(*args(fx), s_ctx, rev)
        ys_x.append(yx)
        ys_c.append(yc)
    out_x = _even_post(ys_x[0], ys_x[1], fx[9], fx[10], pcx, prm["conv_w"], prm["gn_g"], prm["gn_b"])
    out_c = None
    if ctx_out:
        out_c = _even_post(ys_c[0], ys_c[1], fc[9], fc[10], pcc, prm["conv_w"], prm["gn_g"], prm["gn_b"])
    return out_c, out_x


def _odd_layer(hx_in, hc_in, prm, lam_init, ctx_out):
    w_in = prm["w_in"]
    splits = (D_DIFF, D_DIFF, D_DIFF, 2 * D_GMLP)
    qx, kx, vx, gx = _inproj(*hx_in, w_in, splits)
    qc, kc, vc, gc = _inproj(*hc_in, w_in, splits)
    t = qx.shape[1]
    cos, sin = _rope_tables(t)
    qscale = DIFF_HD ** -0.5
    q_rot = _rope(qx, cos, sin, qscale)
    k_rot = _rope(kx, cos, sin, 1.0)
    k_all = jnp.concatenate([kc.astype(BF16), k_rot], axis=1)
    v_all = jnp.concatenate([vc.astype(BF16), vx.astype(BF16)], axis=1)
    lam = (jnp.exp(jnp.sum(prm["lam_q1"] * prm["lam_k1"])) - jnp.exp(jnp.sum(prm["lam_q2"] * prm["lam_k2"]))
           + lam_init).reshape(1).astype(F32)
    att_x = _diff_attention(q_rot, k_all, v_all, lam, prm["subln_g"], lam_init)
    gm_x = _chunk_gmlp(gx, prm["gmlp_ln_g"], prm["gmlp_ln_b"], prm["gmlp_ws"], prm["gmlp_bs"])
    out_c = None
    if ctx_out:
        att_c = _diff_attention((qc * qscale).astype(BF16), kc.astype(BF16), vc.astype(BF16), lam,
                                prm["subln_g"], lam_init)
        gm_c = _chunk_gmlp(gc, prm["gmlp_ln_g"], prm["gmlp_ln_b"], prm["gmlp_ws"], prm["gmlp_bs"])
        out_c = (att_c, gm_c)
    return out_c, (att_x, gm_x)


def kernel(x, c, ctx, c_ctx, w_mod, b_mod, ln_g, ln_b, even_w_in, even_w_out, conv_w, shift_mu, decay_up, decay_0, a_up, a_0, g_up, k_xi, k_alpha, r_bonus, gn_g, gn_b, odd_w_in, odd_w_out, lam_q1, lam_k1, lam_q2, lam_k2, subln_g, gmlp_ln_g, gmlp_ln_b, gmlp_ws, gmlp_bs, w_router, w_e1, w_e3, w_e2):
    bsz, _, d = x.shape
    assert bsz <= SUBLANES - 1
    c8 = jnp.zeros((SUBLANES, d), F32).at[:bsz].set(c).at[bsz].set(c_ctx)
    mod = _modulation(c8, w_mod, b_mod)
    for l in range(DEPTH):
        ctx_out = l < DEPTH - 1
        i = l // 2
        mx = mod[l, :bsz].reshape(bsz, 1, 6, d)
        mc = jnp.broadcast_to(mod[l, bsz].reshape(1, 1, 6, d), (bsz, 1, 6, d))
        part = lambda m, n: m[:, :, n]
        hx_in = (x, part(mx, 1), part(mx, 0))
        hc_in = (ctx, part(mc, 1), part(mc, 0))
        if l % 2 == 0:
            prm = dict(w_in=even_w_in[i].astype(BF16), conv_w=conv_w[i], shift_mu=shift_mu[i],
                       decay_up=decay_up[i], decay_0=decay_0[i], a_up=a_up[i], a_0=a_0[i], g_up=g_up[i],
                       k_xi=k_xi[i], k_alpha=k_alpha[i], r_bonus=r_bonus[i], gn_g=gn_g[i], gn_b=gn_b[i])
            out_c, out_x = _even_layer(hx_in, hc_in, prm, ctx_out)
            w_out = even_w_out[i].astype(BF16)
        else:
            lam_init = 0.8 - 0.6 * math.exp(-0.3 * l)
            prm = dict(w_in=odd_w_in[i].astype(BF16), lam_q1=lam_q1[i], lam_k1=lam_k1[i], lam_q2=lam_q2[i],
                       lam_k2=lam_k2[i], subln_g=subln_g[i], gmlp_ln_g=gmlp_ln_g[i], gmlp_ln_b=gmlp_ln_b[i],
                       gmlp_ws=gmlp_ws[i], gmlp_bs=gmlp_bs[i])
            out_c, out_x = _odd_layer(hx_in, hc_in, prm, lam_init, ctx_out)
            w_out = odd_w_out[i].astype(BF16)
        wr_t = jnp.transpose(w_router[l])
        w1, w3, w2 = w_e1[l].astype(BF16), w_e3[l].astype(BF16), w_e2[l].astype(BF16)
        x = _outproj(out_x[0], out_x[1], w_out, x, part(mx, 2), ln_g[l, 0], ln_b[l, 0])
        x = _moe(x, part(mx, 4), part(mx, 3), part(mx, 5), wr_t, w1, w3, w2, ln_g[l, 1], ln_b[l, 1])
        if ctx_out:
            ctx = _outproj(out_c[0], out_c[1], w_out, ctx, part(mc, 2), ln_g[l, 0], ln_b[l, 0])
            ctx = _moe(ctx, part(mc, 4), part(mc, 3), part(mc, 5), wr_t, w1, w3, w2, ln_g[l, 1], ln_b[l, 1])
    return x
```

```python
import functools
import math

import jax
import jax.numpy as jnp
from jax import lax
from jax.experimental import pallas as pl
from jax.experimental.pallas import tpu as pltpu

F32 = jnp.float32
BF16 = jnp.bfloat16
HI = lax.Precision.HIGHEST

D_MODEL = 1024
DEPTH = 4
GRID_W = 64
D_CONV = 256
RWKV_HEADS = 12
RWKV_HD = 64
D_RWKV = RWKV_HEADS * RWKV_HD
LORA = 64
LORA_G = 128
RWKV_COLS = 3 * D_RWKV + 4 * LORA + LORA_G
DECAY_SCALE = math.exp(-0.5)
GN_EPS = 64e-5
N_PAIRS = RWKV_HEADS // 2
SCAN_CHUNK = 64
DIFF_HEADS = 6
DIFF_HD = 64
DIFF_VD = 2 * DIFF_HD
D_DIFF = DIFF_HEADS * DIFF_VD
AXIS_DIM = DIFF_HD // 2
ROPE_BASE = 10000.0
D_GMLP = 256
GMLP_CHUNK = 128
N_EXPERTS = 16
EC_FACTOR = 2
D_EXPERT = 2048
ALPHA = (2.0 * DEPTH) ** 0.25
LN_EPS = 1e-5
RMS_EPS = 1e-5

LANES = 128
SUBLANES = 8
VMEM_LIMIT = 56 * 1024 * 1024


def _cparams(sem):
    return pltpu.CompilerParams(dimension_semantics=sem, vmem_limit_bytes=VMEM_LIMIT)


def _bdot(a, b):
    return jnp.dot(a.astype(BF16), b.astype(BF16), preferred_element_type=F32)


def _hdot(a, b):
    return jnp.dot(a, b, preferred_element_type=F32, precision=HI)


def _half_mask(shape, dtype=F32):
    lane = lax.broadcasted_iota(jnp.int32, shape, len(shape) - 1)
    return (lane < 64).astype(dtype)


def _seg_ones():
    i = lax.broadcasted_iota(jnp.int32, (LANES, LANES), 0)
    j = lax.broadcasted_iota(jnp.int32, (LANES, LANES), 1)
    return ((i // 64) == (j // 64)).astype(F32)


def _layer_norm_rows(z, g, b, eps):
    mu = jnp.mean(z, axis=-1, keepdims=True)
    zc = z - mu
    var = jnp.mean(zc * zc, axis=-1, keepdims=True)
    return zc * lax.rsqrt(var + eps) * g + b


def _mod_kernel(c_ref, w_ref, b_ref, o_ref):
    a = c_ref[...]
    a = a * jax.nn.sigmoid(a)
    o_ref[0] = _hdot(a, w_ref[0]) + b_ref[0]


def _modulation(c8, w_mod, b_mod):
    tn = 1536
    nl, d, n6 = w_mod.shape
    return pl.pallas_call(
        _mod_kernel,
        grid=(nl, n6 // tn),
        in_specs=[pl.BlockSpec((8, d), lambda l, j: (0, 0)),
                  pl.BlockSpec((1, d, tn), lambda l, j: (l, 0, j)),
                  pl.BlockSpec((1, 1, tn), lambda l, j: (l, 0, j))],
        out_specs=pl.BlockSpec((1, 8, tn), lambda l, j: (l, 0, j)),
        out_shape=jax.ShapeDtypeStruct((nl, 8, n6), F32),
        compiler_params=_cparams(("parallel", "parallel")),
        name="modulation",
    )(c8, w_mod, b_mod.reshape(nl, 1, n6))


def _inproj_kernel(x_ref, sc_ref, sh_ref, w_ref, *o_refs, splits):
    h = x_ref[0] * (1.0 + sc_ref[0]) + sh_ref[0]
    y = _bdot(h, w_ref[...])
    off = 0
    for o, s in zip(o_refs, splits):
        o[0] = y[:, off:off + s].astype(o.dtype)
        off += s


def _inproj(x, sc, sh, w_bf16, splits, tm=256):
    b, t, d = x.shape
    tm = min(tm, t)
    dout = w_bf16.shape[1]
    return pl.pallas_call(
        functools.partial(_inproj_kernel, splits=splits),
        grid=(b, t // tm),
        in_specs=[pl.BlockSpec((1, tm, d), lambda i, j: (i, j, 0)),
                  pl.BlockSpec((1, 1, d), lambda i, j: (i, 0, 0)),
                  pl.BlockSpec((1, 1, d), lambda i, j: (i, 0, 0)),
                  pl.BlockSpec((d, dout), lambda i, j: (0, 0))],
        out_specs=[pl.BlockSpec((1, tm, s), lambda i, j: (i, j, 0)) for s in splits],
        out_shape=[jax.ShapeDtypeStruct((b, t, s), F32) for s in splits],
        compiler_params=_cparams(("parallel", "parallel")),
        name="inproj",
    )(x, sc, sh, w_bf16)


def _shifted(p, prev_row, next_row):
    tb = p.shape[0]
    row = lax.broadcasted_iota(jnp.int32, p.shape, 0)
    p_prev = jnp.where(row == 0, prev_row, pltpu.roll(p, 1, axis=0))
    p_next = jnp.where(row == tb - 1, next_row, pltpu.roll(p, tb - 1, axis=0))
    return p_prev, p_next


def _edge_rows(pp_ref, pn_ref):
    i = pl.program_id(1)
    nb = pl.num_programs(1)
    prev_row = jnp.where(i > 0, pp_ref[0, SUBLANES - 1:SUBLANES, :], 0.0)
    next_row = jnp.where(i < nb - 1, pn_ref[0, 0:1, :], 0.0)
    return prev_row, next_row


def _feat_kernel(p_ref, pp_ref, pn_ref, mu_ref, dup_ref, d0_ref, aup_ref, a0_ref, gup_ref, kxi_ref, kal_ref,
                 rb_ref, r_o, kk_o, v_o, kr0_o, kr1_o, lw0_o, lw1_o, b0_o, b1_o, g_o, gbv_o):
    p = p_ref[0]
    prev_row, next_row = _edge_rows(pp_ref, pn_ref)
    p_prev, p_next = _shifted(p, prev_row, next_row)
    p = p + mu_ref[...] * (0.5 * (p_prev + p_next) - p)
    dr = D_RWKV
    r = p[:, 0:dr]
    k = p[:, dr:2 * dr]
    v = p[:, 2 * dr:3 * dr]
    c0 = 3 * dr
    dd = jnp.tanh(p[:, c0:c0 + 2 * LORA])
    da = p[:, c0 + 2 * LORA:c0 + 4 * LORA]
    dg = jax.nn.sigmoid(p[:, c0 + 4 * LORA:c0 + 4 * LORA + LORA_G])
    g = _hdot(dg, gup_ref[...])
    lw, a = [], []
    for d in range(2):
        lw.append(-DECAY_SCALE * jax.nn.sigmoid(d0_ref[d:d + 1, :] + _hdot(dd[:, d * LORA:(d + 1) * LORA], dup_ref[d])))
        a.append(jax.nn.sigmoid(a0_ref[d:d + 1, :] + _hdot(da[:, d * LORA:(d + 1) * LORA], aup_ref[d])))
    kx = k * kxi_ref[...]
    kal = kal_ref[...]
    kr = [k * (1.0 + (a[d] - 1.0) * kal) for d in range(2)]
    bon = r * (0.5 * (kr[0] + kr[1])) * rb_ref[...]
    ones = _seg_ones()
    g_o[0] = g
    for j in range(N_PAIRS):
        sl = slice(j * LANES, (j + 1) * LANES)
        kxj = kx[:, sl]
        kk = kxj * lax.rsqrt(_hdot(kxj * kxj, ones) + 1e-12)
        r_o[0, j] = r[:, sl]
        kk_o[0, j] = kk
        v_o[0, j] = v[:, sl]
        kr0_o[0, j] = kr[0][:, sl]
        kr1_o[0, j] = kr[1][:, sl]
        lw0_o[0, j] = lw[0][:, sl]
        lw1_o[0, j] = lw[1][:, sl]
        b0_o[0, j] = kk * a[0][:, sl]
        b1_o[0, j] = kk * a[1][:, sl]
        gbv_o[0, :, sl] = g[:, sl] * _hdot(bon[:, sl], ones) * v[:, sl]


def _rwkv_features(p, prm, tb=256):
    b, t, cols = p.shape
    tb = min(tb, t)
    nb8 = t // SUBLANES
    r8 = tb // SUBLANES
    full = lambda a: pl.BlockSpec(a.shape, lambda i, j: (0,) * a.ndim)
    params = [prm["shift_mu"].reshape(1, cols), prm["decay_up"], prm["decay_0"], prm["a_up"], prm["a_0"],
              prm["g_up"], prm["k_xi"].reshape(1, -1), prm["k_alpha"].reshape(1, -1), prm["r_bonus"].reshape(1, -1)]
    packed = jax.ShapeDtypeStruct((b, N_PAIRS, t, LANES), F32)
    flat = jax.ShapeDtypeStruct((b, t, D_RWKV), F32)
    pspec = pl.BlockSpec((1, N_PAIRS, tb, LANES), lambda i, j: (i, 0, j, 0))
    fspec = pl.BlockSpec((1, tb, D_RWKV), lambda i, j: (i, j, 0))
    return pl.pallas_call(
        _feat_kernel,
        grid=(b, t // tb),
        in_specs=[pl.BlockSpec((1, tb, cols), lambda i, j: (i, j, 0)),
                  pl.BlockSpec((1, SUBLANES, cols), lambda i, j: (i, jnp.maximum(j * r8 - 1, 0), 0)),
                  pl.BlockSpec((1, SUBLANES, cols), lambda i, j: (i, jnp.minimum((j + 1) * r8, nb8 - 1), 0))]
                 + [full(a) for a in params],
        out_specs=[pspec] * 9 + [fspec] * 2,
        out_shape=[packed] * 9 + [flat] * 2,
        compiler_params=_cparams(("parallel", "parallel")),
        name="rwkv_features",
    )(p, p, p, *params)


def _scan_chunks(feats, states, rev):
    n = SCAN_CHUNK
    m0 = _half_mask((n, LANES))
    m1 = 1.0 - m0
    ti = lax.broadcasted_iota(jnp.int32, (n, n), 0)
    tj = lax.broadcasted_iota(jnp.int32, (n, n), 1)
    tri = ((ti <= tj) if rev else (ti >= tj)).astype(F32)
    si = lax.broadcasted_iota(jnp.int32, (2 * n, 2 * n), 0)
    sj = lax.broadcasted_iota(jnp.int32, (2 * n, 2 * n), 1)
    same = (si // n) == (sj // n)
    ri, rj = si % n, sj % n
    strict = same & ((ri < rj) if rev else (ri > rj))
    incl = same & ((ri <= rj) if rev else (ri >= rj))
    eye = (si == sj).astype(F32)

    def stack(x):
        return jnp.concatenate([x * m0, x * m1], axis=0).astype(BF16)

    pre = []
    for r, lw, kap, bb, kr, v in feats:
        cs = _hdot(tri, lw)
        tot = jnp.sum(lw, axis=0, keepdims=True)
        e_neg = jnp.exp(-cs)
        e_tot = jnp.exp(tot - cs)
        rh = r * jnp.exp(cs)
        pre.append(dict(kh=stack(kap * jnp.exp(cs - lw)), bh=stack(bb * e_neg), kq=stack(kr * e_neg), rh=rh,
                        rhs=stack(rh), kt=stack(kr * e_tot), bt=stack(bb * e_tot), v=stack(v), gl=jnp.exp(tot)))
    nt = (((1,), (1,)), ((), ()))
    amats = [lax.dot_general(jnp.concatenate([p["kh"], p["rhs"]], axis=0), jnp.concatenate([p["bh"], p["kq"]], axis=0),
                             nt, preferred_element_type=F32) for p in pre]
    a1 = [jnp.where(strict, a[:2 * n, :2 * n], 0.0) for a in amats]
    a2 = [jnp.where(strict, a[:2 * n, 2 * n:], 0.0).astype(BF16) for a in amats]
    a4 = [jnp.where(incl, a[2 * n:, :2 * n], 0.0).astype(BF16) for a in amats]
    a3 = [jnp.where(incl, a[2 * n:, 2 * n:], 0.0).astype(BF16) for a in amats]
    tinv = [eye - a for a in a1]
    pw = a1
    for _ in range(5):
        pw = [_bdot(p, p) for p in pw]
        tinv = [t + _bdot(t, p) for t, p in zip(tinv, pw)]
    w_s = [_bdot(a, p["v"]) for a, p in zip(a2, pre)]
    mm = [_bdot(t, jnp.concatenate([p["kh"], w.astype(BF16)], axis=1)).astype(BF16)
          for t, p, w in zip(tinv, pre, w_s)]
    gmat = [_bdot(p["bt"].astype(F32).T, m) for p, m in zip(pre, mm)]
    phi_t = [eye * p["gl"] - g[:, :LANES] for p, g in zip(pre, gmat)]
    psi_t = [_bdot(p["kt"].astype(F32).T, p["v"]) - g[:, LANES:] for p, g in zip(pre, gmat)]
    qy = [_bdot(a, m) for a, m in zip(a4, mm)]
    y0_s = [_bdot(a, p["v"]) - q[:, LANES:] for a, p, q in zip(a3, pre, qy)]
    ys, sts = [], []
    for p, q, y0, ph, ps, st in zip(pre, qy, y0_s, phi_t, psi_t, states):
        qm = p["rh"] - (q[:n, :LANES] + q[n:, :LANES])
        ys.append(_bdot(qm, st) + (y0[:n] + y0[n:]))
        sts.append(_bdot(ph, st) + ps)
    return ys, sts


def _scan_kernel(r_ref, lw_ref, kap_ref, b_ref, kr_ref, v_ref, s0_ref, y_ref, sT_ref, st_scr, *, rev):
    c = pl.program_id(1)

    @pl.when(c == 0)
    def _():
        st_scr[...] = s0_ref[0]

    feats = [(r_ref[0, hp], lw_ref[0, hp], kap_ref[0, hp], b_ref[0, hp], kr_ref[0, hp], v_ref[0, hp])
             for hp in range(N_PAIRS)]
    ys, sts = _scan_chunks(feats, [st_scr[hp] for hp in range(N_PAIRS)], rev)
    for hp in range(N_PAIRS):
        y_ref[0, hp] = ys[hp]
        st_scr[hp] = sts[hp]

    @pl.when(c == pl.num_programs(1) - 1)
    def _():
        sT_ref[0] = st_scr[...]


def _rwkv_scan(r, lw, kap, bb, kr, v, s0, rev):
    b, _, t, _ = r.shape
    n = SCAN_CHUNK
    nc = t // n
    cidx = (lambda c: nc - 1 - c) if rev else (lambda c: c)
    fspec = pl.BlockSpec((1, N_PAIRS, n, LANES), lambda i, c: (i, 0, cidx(c), 0))
    sspec = pl.BlockSpec((1, N_PAIRS, LANES, LANES), lambda i, c: (i, 0, 0, 0))
    return pl.pallas_call(
        functools.partial(_scan_kernel, rev=rev),
        grid=(b, nc),
        in_specs=[fspec] * 6 + [sspec],
        out_specs=[fspec, sspec],
        out_shape=[jax.ShapeDtypeStruct(r.shape, F32), jax.ShapeDtypeStruct(s0.shape, F32)],
        scratch_shapes=[pltpu.VMEM((N_PAIRS, LANES, LANES), F32)],
        compiler_params=_cparams(("parallel", "arbitrary")),
        name="rwkv_scan_rev" if rev else "rwkv_scan_fwd",
    )(r, lw, kap, bb, kr, v, s0)


def _evenpost_kernel(y0_ref, y1_ref, g_ref, gbv_ref, pc_ref, pcp_ref, pcn_ref, cw_ref, gng_ref, gnb_ref,
                     conv_o, rw_o):
    pc = pc_ref[0]
    prev_row, next_row = _edge_rows(pcp_ref, pcn_ref)
    dc = D_CONV

    def gated(z):
        return z[:, dc:2 * dc] * z[:, 2 * dc:3 * dc]

    u = gated(pc)
    u_prev, u_next = _shifted(u, gated(prev_row), gated(next_row))
    cw = cw_ref[...]
    conv_o[0] = (pc[:, 0:dc] * (cw[0:1] * u_prev + cw[1:2] * u + cw[2:3] * u_next)).astype(conv_o.dtype)
    ones = _seg_ones() * (1.0 / RWKV_HD)
    for j in range(N_PAIRS):
        sl = slice(j * LANES, (j + 1) * LANES)
        y = y0_ref[0, j] + y1_ref[0, j]
        mu = _hdot(y, ones)
        yc = y - mu
        var = _hdot(yc * yc, ones)
        yn = yc * lax.rsqrt(var + GN_EPS) * gng_ref[:, sl] + gnb_ref[:, sl]
        rw_o[0, :, sl] = (g_ref[0, :, sl] * yn + gbv_ref[0, :, sl]).astype(rw_o.dtype)


def _even_post(y0, y1, g, gbv, pc, conv_w, gn_g, gn_b, tb=256):
    b, _, t, _ = y0.shape
    tb = min(tb, t)
    nb8 = t // SUBLANES
    r8 = tb // SUBLANES
    c3 = pc.shape[-1]
    pspec = pl.BlockSpec((1, N_PAIRS, tb, LANES), lambda i, j: (i, 0, j, 0))
    fspec = pl.BlockSpec((1, tb, D_RWKV), lambda i, j: (i, j, 0))
    full = lambda a: pl.BlockSpec(a.shape, lambda i, j: (0,) * a.ndim)
    gn_g = gn_g.reshape(1, -1)
    gn_b = gn_b.reshape(1, -1)
    return pl.pallas_call(
        _evenpost_kernel,
        grid=(b, t // tb),
        in_specs=[pspec, pspec, fspec, fspec,
                  pl.BlockSpec((1, tb, c3), lambda i, j: (i, j, 0)),
                  pl.BlockSpec((1, SUBLANES, c3), lambda i, j: (i, jnp.maximum(j * r8 - 1, 0), 0)),
                  pl.BlockSpec((1, SUBLANES, c3), lambda i, j: (i, jnp.minimum((j + 1) * r8, nb8 - 1), 0)),
                  full(conv_w), full(gn_g), full(gn_b)],
        out_specs=[pl.BlockSpec((1, tb, D_CONV), lambda i, j: (i, j, 0)), fspec],
        out_shape=[jax.ShapeDtypeStruct((b, t, D_CONV), BF16), jax.ShapeDtypeStruct((b, t, D_RWKV), BF16)],
        compiler_params=_cparams(("parallel", "parallel")),
        name="even_post",
    )(y0, y1, g, gbv, pc, pc, pc, conv_w, gn_g, gn_b)


def _outproj_kernel(a_ref, b_ref, wa_ref, wb_ref, x_ref, gm_ref, lg_ref, lb_ref, o_ref):
    yx = _bdot(a_ref[0], wa_ref[...]) + _bdot(b_ref[0], wb_ref[...])
    z = ALPHA * x_ref[0] + gm_ref[0] * yx
    o_ref[0] = _layer_norm_rows(z, lg_ref[...], lb_ref[...], LN_EPS)


def _outproj(ma, mb, w_out_bf16, x, gm, ln_g, ln_b, tm=512):
    b, t, d = x.shape
    tm = min(tm, t)
    da, db = ma.shape[-1], mb.shape[-1]
    wa, wb = w_out_bf16[:da], w_out_bf16[da:]
    row = lambda w: pl.BlockSpec((1, tm, w), lambda i, j: (i, j, 0))
    full = lambda a: pl.BlockSpec(a.shape, lambda i, j: (0,) * a.ndim)
    ln_g = ln_g.reshape(1, d)
    ln_b = ln_b.reshape(1, d)
    return pl.pallas_call(
        _outproj_kernel,
        grid=(b, t // tm),
        in_specs=[row(da), row(db), full(wa), full(wb), row(d),
                  pl.BlockSpec((1, 1, d), lambda i, j: (i, 0, 0)), full(ln_g), full(ln_b)],
        out_specs=row(d),
        out_shape=jax.ShapeDtypeStruct((b, t, d), F32),
        compiler_params=_cparams(("parallel", "parallel")),
        name="outproj_norm",
    )(ma, mb, wa, wb, x, gm, ln_g, ln_b)


def _rope_kernel(x_ref, cos_ref, sin_ref, o_ref, *, scale):
    cos = cos_ref[...]
    sin = sin_ref[...]
    lane = lax.broadcasted_iota(jnp.int32, cos.shape, 1)
    first = (lane % (2 * (AXIS_DIM // 2))) < (AXIS_DIM // 2)
    half = AXIS_DIM // 2
    for j in range(DIFF_HEADS):
        sl = slice(j * LANES, (j + 1) * LANES)
        xs = x_ref[0, :, sl]
        partner = jnp.where(first, -pltpu.roll(xs, LANES - half, axis=1), pltpu.roll(xs, half, axis=1))
        o_ref[0, :, sl] = ((xs * cos + partner * sin) * scale).astype(o_ref.dtype)


def _rope(x, cos, sin, scale, tb=512):
    b, t, w = x.shape
    tb = min(tb, t)
    return pl.pallas_call(
        functools.partial(_rope_kernel, scale=scale),
        grid=(b, t // tb),
        in_specs=[pl.BlockSpec((1, tb, w), lambda i, j: (i, j, 0)),
                  pl.BlockSpec((tb, LANES), lambda i, j: (j, 0)),
                  pl.BlockSpec((tb, LANES), lambda i, j: (j, 0))],
        out_specs=pl.BlockSpec((1, tb, w), lambda i, j: (i, j, 0)),
        out_shape=jax.ShapeDtypeStruct((b, t, w), BF16),
        compiler_params=_cparams(("parallel", "parallel")),
        name="axial_rope",
    )(x, cos, sin)


def _rope_tables(t):
    rows = t // GRID_W
    row = jnp.repeat(jnp.arange(rows), GRID_W).astype(F32)
    col = jnp.tile(jnp.arange(GRID_W), rows).astype(F32)
    inv = ROPE_BASE ** (-jnp.arange(0, AXIS_DIM, 2, dtype=F32) / AXIS_DIM)
    ang_r = row[:, None] * inv
    ang_c = col[:, None] * inv
    ang = jnp.concatenate([ang_r, ang_r, ang_c, ang_c], axis=-1)
    ang = jnp.concatenate([ang, ang], axis=-1)
    return jnp.cos(ang), jnp.sin(ang)


def _attn_kernel(lam_ref, q_ref, k_ref, v_ref, g_ref, o_ref, *, tk, out_scale):
    q = q_ref[0]
    tq = q.shape[0]
    hm = _half_mask(q.shape, BF16)
    q1 = q * hm
    qs = (q1, q - q1)
    nk = k_ref.shape[1] // tk
    nsl = tk // LANES

    def scores(qh, j):
        kb = k_ref[0, pl.ds(pl.multiple_of(j * tk, tk), tk), :]
        return lax.dot_general(qh, kb, (((1,), (1,)), ((), ())), preferred_element_type=F32)

    def col(s, c):
        return s[:, c * LANES:(c + 1) * LANES]

    def body(j, carry):
        vb = v_ref[0, pl.ds(pl.multiple_of(j * tk, tk), tk), :]
        ss = [scores(qh, j) for qh in qs]
        out = []
        for s, (m, ls, acc) in zip(ss, (carry[:3], carry[3:])):
            mx = col(s, 0)
            for c in range(1, nsl):
                mx = jnp.maximum(mx, col(s, c))
            m_new = jnp.maximum(m, jnp.broadcast_to(jnp.max(mx, axis=-1, keepdims=True), (tq, LANES)))
            corr = jnp.exp2(m - m_new)
            ps = [jnp.exp2(col(s, c) - m_new) for c in range(nsl)]
            ls = corr * ls
            for pc in ps:
                ls = ls + pc
            p = jnp.concatenate([pc.astype(BF16) for pc in ps], axis=1)
            acc = corr * acc + jnp.dot(p, vb, preferred_element_type=F32)
            out += [m_new, ls, acc]
        return tuple(out)

    neg = jnp.full((tq, LANES), -jnp.inf, F32)
    zero = jnp.zeros((tq, LANES), F32)
    _, ls1, acc1, _, ls2, acc2 = lax.fori_loop(0, nk, body, (neg, zero, zero, neg, zero, zero))
    l1 = jnp.sum(ls1, axis=-1, keepdims=True)
    l2 = jnp.sum(ls2, axis=-1, keepdims=True)
    o = acc1 / l1 - lam_ref[0] * (acc2 / l2)
    o = o * lax.rsqrt(jnp.mean(o * o, axis=-1, keepdims=True) + RMS_EPS) * g_ref[...] * out_scale
    o_ref[0] = o.astype(o_ref.dtype)


def _diff_attention(q, k, v, lam, subln_g, lam_init, tq=512, tk=2816):
    b, t, w = q.shape
    tkk = k.shape[1]
    tq = min(tq, t)
    tk = math.gcd(tk, tkk)
    assert tk % LANES == 0 and t % tq == 0
    subln_g = subln_g.reshape(1, DIFF_VD)
    return pl.pallas_call(
        functools.partial(_attn_kernel, tk=tk, out_scale=1.0 - lam_init),
        grid=(b, DIFF_HEADS, t // tq),
        in_specs=[pl.BlockSpec(memory_space=pltpu.SMEM),
                  pl.BlockSpec((1, tq, LANES), lambda i, h, j: (i, j, h)),
                  pl.BlockSpec((1, tkk, LANES), lambda i, h, j: (i, 0, h)),
                  pl.BlockSpec((1, tkk, LANES), lambda i, h, j: (i, 0, h)),
                  pl.BlockSpec((1, DIFF_VD), lambda i, h, j: (0, 0))],
        out_specs=pl.BlockSpec((1, tq, LANES), lambda i, h, j: (i, j, h)),
        out_shape=jax.ShapeDtypeStruct((b, t, w), BF16),
        compiler_params=_cparams(("parallel", "parallel", "parallel")),
        name="diff_attention",
    )(lam, q, k, v, subln_g)


def _gmlp_kernel(p_ref, lg_ref, lb_ref, ws_ref, bs_ref, o_ref):
    p = p_ref[0]
    tb = p.shape[0]
    ge = 0.5 * p * (1.0 + lax.erf(p * (2.0 ** -0.5)))
    u = ge[:, :D_GMLP]
    v = _layer_norm_rows(ge[:, D_GMLP:], lg_ref[...], lb_ref[...], LN_EPS)
    hm = _half_mask((GMLP_CHUNK, LANES))
    for c in range(tb // GMLP_CHUNK):
        rs = slice(c * GMLP_CHUNK, (c + 1) * GMLP_CHUNK)
        for j in range(D_GMLP // LANES):
            sl = slice(j * LANES, (j + 1) * LANES)
            vc = v[rs, sl]
            va = vc * hm
            mixed = _bdot(ws_ref[2 * j], va) + _bdot(ws_ref[2 * j + 1], vc - va) + bs_ref[:, sl]
            o_ref[0, rs, sl] = (u[rs, sl] * mixed).astype(o_ref.dtype)


def _chunk_gmlp(p, ln_g, ln_b, ws, bs, tb=256):
    b, t, w = p.shape
    tb = min(tb, t)
    ln_g = ln_g.reshape(1, -1)
    ln_b = ln_b.reshape(1, -1)
    bs_t = jnp.repeat(jnp.transpose(bs), D_GMLP // ws.shape[0], axis=1)
    full = lambda a: pl.BlockSpec(a.shape, lambda i, j: (0,) * a.ndim)
    return pl.pallas_call(
        _gmlp_kernel,
        grid=(b, t // tb),
        in_specs=[pl.BlockSpec((1, tb, w), lambda i, j: (i, j, 0)), full(ln_g), full(ln_b), full(ws), full(bs_t)],
        out_specs=pl.BlockSpec((1, tb, D_GMLP), lambda i, j: (i, j, 0)),
        out_shape=jax.ShapeDtypeStruct((b, t, D_GMLP), BF16),
        compiler_params=_cparams(("parallel", "parallel")),
        name="chunk_gmlp",
    )(p, ln_g, ln_b, ws, bs_t)


def _router_kernel(x_ref, sc_ref, sh_ref, wr_ref, h_o, aff_o):
    h = x_ref[0] * (1.0 + sc_ref[0]) + sh_ref[0]
    h_o[0] = h.astype(h_o.dtype)
    logits = lax.dot_general(wr_ref[...], h, (((1,), (1,)), ((), ())), preferred_element_type=F32, precision=HI)
    m = jnp.max(logits, axis=0, keepdims=True)
    e = jnp.exp(logits - m)
    aff_o[0] = e / jnp.sum(e, axis=0, keepdims=True)


def _router(x, sc, sh, w_router_t, tm=512):
    b, t, d = x.shape
    tm = min(tm, t)
    e = w_router_t.shape[0]
    return pl.pallas_call(
        _router_kernel,
        grid=(b, t // tm),
        in_specs=[pl.BlockSpec((1, tm, d), lambda i, j: (i, j, 0)),
                  pl.BlockSpec((1, 1, d), lambda i, j: (i, 0, 0)),
                  pl.BlockSpec((1, 1, d), lambda i, j: (i, 0, 0)),
                  pl.BlockSpec((e, d), lambda i, j: (0, 0))],
        out_specs=[pl.BlockSpec((1, tm, d), lambda i, j: (i, j, 0)),
                   pl.BlockSpec((1, e, tm), lambda i, j: (i, 0, j))],
        out_shape=[jax.ShapeDtypeStruct((b, t, d), BF16), jax.ShapeDtypeStruct((b, e, t), F32)],
        compiler_params=_cparams(("parallel", "parallel")),
        name="router",
    )(x, sc, sh, w_router_t)


def _ffn_kernel(xs_ref, gate_ref, w1_ref, w3_ref, w2_ref, o_ref):
    x = xs_ref[0, 0]
    h1 = jnp.dot(x, w1_ref[0], preferred_element_type=F32)
    h3 = jnp.dot(x, w3_ref[0], preferred_element_type=F32)
    hid = (h1 * jax.nn.sigmoid(h1)) * h3
    y = jnp.dot(hid.astype(BF16), w2_ref[0], preferred_element_type=F32)
    o_ref[0, 0] = (y * gate_ref[0, 0]).astype(o_ref.dtype)


def _expert_ffn(xs, gate, w1, w3, w2, tc=512):
    b, e, c, d = xs.shape
    f = w1.shape[-1]
    tc = min(tc, c)
    return pl.pallas_call(
        _ffn_kernel,
        grid=(e, b, c // tc),
        in_specs=[pl.BlockSpec((1, 1, tc, d), lambda ei, bi, ci: (bi, ei, ci, 0)),
                  pl.BlockSpec((1, 1, tc, 1), lambda ei, bi, ci: (bi, ei, ci, 0)),
                  pl.BlockSpec((1, d, f), lambda ei, bi, ci: (ei, 0, 0)),
                  pl.BlockSpec((1, d, f), lambda ei, bi, ci: (ei, 0, 0)),
                  pl.BlockSpec((1, f, d), lambda ei, bi, ci: (ei, 0, 0))],
        out_specs=pl.BlockSpec((1, 1, tc, d), lambda ei, bi, ci: (bi, ei, ci, 0)),
        out_shape=jax.ShapeDtypeStruct((b, e, c, d), F32),
        compiler_params=_cparams(("parallel", "parallel", "parallel")),
        name="expert_ffn",
    )(xs, gate, w1, w3, w2)


def _cast_kernel(x_ref, o_ref):
    o_ref[...] = x_ref[...].astype(o_ref.dtype)


def _to_bf16(w, l):
    _, e, r, c = w.shape
    return pl.pallas_call(
        _cast_kernel,
        grid=(e,),
        in_specs=[pl.BlockSpec((None, 1, r, c), lambda i: (l, i, 0, 0))],
        out_specs=pl.BlockSpec((1, r, c), lambda i: (i, 0, 0)),
        out_shape=jax.ShapeDtypeStruct((e, r, c), BF16),
        compiler_params=_cparams(("parallel",)),
        name="cast_bf16",
    )(w)


def _resnorm_kernel(x_ref, f_ref, gf_ref, lg_ref, lb_ref, o_ref):
    z = ALPHA * x_ref[0] + gf_ref[0] * f_ref[0]
    o_ref[0] = _layer_norm_rows(z, lg_ref[...], lb_ref[...], LN_EPS)


def _resnorm(x, f, gf, ln_g, ln_b, tm=512):
    b, t, d = x.shape
    tm = min(tm, t)
    row = pl.BlockSpec((1, tm, d), lambda i, j: (i, j, 0))
    vec = pl.BlockSpec((1, d), lambda i, j: (0, 0))
    return pl.pallas_call(
        _resnorm_kernel,
        grid=(b, t // tm),
        in_specs=[row, row, pl.BlockSpec((1, 1, d), lambda i, j: (i, 0, 0)), vec, vec],
        out_specs=row,
        out_shape=jax.ShapeDtypeStruct((b, t, d), F32),
        compiler_params=_cparams(("parallel", "parallel")),
        name="residual_norm",
    )(x, f, gf, ln_g.reshape(1, d), ln_b.reshape(1, d))


def _moe(x, sc, sh, gf, w_router_t, w1, w3, w2, ln_g, ln_b):
    b, t, d = x.shape
    cap = EC_FACTOR * t // N_EXPERTS
    h, aff = _router(x, sc, sh, w_router_t)
    gate, idx = lax.top_k(aff, cap)
    xs = jnp.take_along_axis(h[:, None], idx[..., None], axis=2)
    ye = _expert_ffn(xs, gate[..., None], w1, w3, w2)
    b_idx = jnp.arange(b)[:, None, None]
    f = jnp.zeros((b, t, d), F32).at[b_idx, idx].add(ye)
    return _resnorm(x, f, gf, ln_g, ln_b)


def _even_layer(hx_in, hc_in, prm, ctx_out):
    w_in = prm["w_in"]
    splits = (3 * D_CONV, RWKV_COLS)
    pcx, prx = _inproj(*hx_in, w_in, splits)
    pcc, prc = _inproj(*hc_in, w_in, splits)
    fx = _rwkv_features(prx, prm)
    fc = _rwkv_features(prc, prm)
    b = prx.shape[0]
    zero = jnp.zeros((b, N_PAIRS, LANES, LANES), F32)
    ys_x, ys_c = [], []
    for d, rev in ((0, False), (1, True)):
        def args(f):
            r, kk, v, kr0, kr1, lw0, lw1, b0, b1 = f[:9]
            return (r, (lw0, lw1)[d], kk, (b0, b1)[d], (kr0, kr1)[d], v)
        yc, s_ctx = _rwkv_scan(*args(fc), zero, rev)
        yx, _ = _rwkv_scan(*args(fx), s_ctx, rev)
        ys_x.append(yx)
        ys_c.append(yc)
    out_x = _even_post(ys_x[0], ys_x[1], fx[9], fx[10], pcx, prm["conv_w"], prm["gn_g"], prm["gn_b"])
    out_c = None
    if ctx_out:
        out_c = _even_post(ys_c[0], ys_c[1], fc[9], fc[10], pcc, prm["conv_w"], prm["gn_g"], prm["gn_b"])
    return out_c, out_x


def _odd_layer(hx_in, hc_in, prm, lam_init, ctx_out):
    w_in = prm["w_in"]
    splits = (D_DIFF, D_DIFF, D_DIFF, 2 * D_GMLP)
    qx, kx, vx, gx = _inproj(*hx_in, w_in, splits)
    qc, kc, vc, gc = _inproj(*hc_in, w_in, splits)
    t = qx.shape[1]
    cos, sin = _rope_tables(t)
    qscale = DIFF_HD ** -0.5 * math.log2(math.e)
    q_rot = _rope(qx, cos, sin, qscale)
    k_rot = _rope(kx, cos, sin, 1.0)
    k_all = jnp.concatenate([kc.astype(BF16), k_rot], axis=1)
    v_all = jnp.concatenate([vc.astype(BF16), vx.astype(BF16)], axis=1)
    lam = (jnp.exp(jnp.sum(prm["lam_q1"] * prm["lam_k1"])) - jnp.exp(jnp.sum(prm["lam_q2"] * prm["lam_k2"]))
           + lam_init).reshape(1).astype(F32)
    att_x = _diff_attention(q_rot, k_all, v_all, lam, prm["subln_g"], lam_init)
    gm_x = _chunk_gmlp(gx, prm["gmlp_ln_g"], prm["gmlp_ln_b"], prm["gmlp_ws"], prm["gmlp_bs"])
    out_c = None
    if ctx_out:
        att_c = _diff_attention((qc * qscale).astype(BF16), kc.astype(BF16), vc.astype(BF16), lam,
                                prm["subln_g"], lam_init)
        gm_c = _chunk_gmlp(gc, prm["gmlp_ln_g"], prm["gmlp_ln_b"], prm["gmlp_ws"], prm["gmlp_bs"])
        out_c = (att_c, gm_c)
    return out_c, (att_x, gm_x)


def kernel(x, c, ctx, c_ctx, w_mod, b_mod, ln_g, ln_b, even_w_in, even_w_out, conv_w, shift_mu, decay_up, decay_0, a_up, a_0, g_up, k_xi, k_alpha, r_bonus, gn_g, gn_b, odd_w_in, odd_w_out, lam_q1, lam_k1, lam_q2, lam_k2, subln_g, gmlp_ln_g, gmlp_ln_b, gmlp_ws, gmlp_bs, w_router, w_e1, w_e3, w_e2):
    bsz, _, d = x.shape
    assert bsz <= SUBLANES - 1
    c8 = jnp.zeros((SUBLANES, d), F32).at[:bsz].set(c).at[bsz].set(c_ctx)
    mod = _modulation(c8, w_mod, b_mod)
    for l in range(DEPTH):
        ctx_out = l < DEPTH - 1
        i = l // 2
        mx = mod[l, :bsz].reshape(bsz, 1, 6, d)
        mc = jnp.broadcast_to(mod[l, bsz].reshape(1, 1, 6, d), (bsz, 1, 6, d))
        part = lambda m, n: m[:, :, n]
        hx_in = (x, part(mx, 1), part(mx, 0))
        hc_in = (ctx, part(mc, 1), part(mc, 0))
        if l % 2 == 0:
            prm = dict(w_in=even_w_in[i].astype(BF16), conv_w=conv_w[i], shift_mu=shift_mu[i],
                       decay_up=decay_up[i], decay_0=decay_0[i], a_up=a_up[i], a_0=a_0[i], g_up=g_up[i],
                       k_xi=k_xi[i], k_alpha=k_alpha[i], r_bonus=r_bonus[i], gn_g=gn_g[i], gn_b=gn_b[i])
            out_c, out_x = _even_layer(hx_in, hc_in, prm, ctx_out)
            w_out = even_w_out[i].astype(BF16)
        else:
            lam_init = 0.8 - 0.6 * math.exp(-0.3 * l)
            prm = dict(w_in=odd_w_in[i].astype(BF16), lam_q1=lam_q1[i], lam_k1=lam_k1[i], lam_q2=lam_q2[i],
                       lam_k2=lam_k2[i], subln_g=subln_g[i], gmlp_ln_g=gmlp_ln_g[i], gmlp_ln_b=gmlp_ln_b[i],
                       gmlp_ws=gmlp_ws[i], gmlp_bs=gmlp_bs[i])
            out_c, out_x = _odd_layer(hx_in, hc_in, prm, lam_init, ctx_out)
            w_out = odd_w_out[i].astype(BF16)
        wr_t = jnp.transpose(w_router[l])
        w1, w3, w2 = _to_bf16(w_e1, l), _to_bf16(w_e3, l), _to_bf16(w_e2, l)
        x = _outproj(out_x[0], out_x[1], w_out, x, part(mx, 2), ln_g[l, 0], ln_b[l, 0])
        x = _moe(x, part(mx, 4), part(mx, 3), part(mx, 5), wr_t, w1, w3, w2, ln_g[l, 1], ln_b[l, 1])
        if ctx_out:
            ctx = _outproj(out_c[0], out_c[1], w_out, ctx, part(mc, 2), ln_g[l, 0], ln_b[l, 0])
            ctx = _moe(ctx, part(mc, 4), part(mc, 3), part(mc, 5), wr_t, w1, w3, w2, ln_g[l, 1], ln_b[l, 1])
    return x
```

```python
import functools
import math

import jax
import jax.numpy as jnp
from jax import lax
from jax.experimental import pallas as pl
from jax.experimental.pallas import tpu as pltpu

F32 = jnp.float32
BF16 = jnp.bfloat16
HI = lax.Precision.HIGHEST

D_MODEL = 1024
DEPTH = 4
GRID_W = 64
D_CONV = 256
RWKV_HEADS = 12
RWKV_HD = 64
D_RWKV = RWKV_HEADS * RWKV_HD
LORA = 64
LORA_G = 128
RWKV_COLS = 3 * D_RWKV + 4 * LORA + LORA_G
DECAY_SCALE = math.exp(-0.5)
GN_EPS = 64e-5
N_PAIRS = RWKV_HEADS // 2
SCAN_CHUNK = 64
DIFF_HEADS = 6
DIFF_HD = 64
DIFF_VD = 2 * DIFF_HD
D_DIFF = DIFF_HEADS * DIFF_VD
AXIS_DIM = DIFF_HD // 2
ROPE_BASE = 10000.0
D_GMLP = 256
GMLP_CHUNK = 128
N_EXPERTS = 16
EC_FACTOR = 2
D_EXPERT = 2048
ALPHA = (2.0 * DEPTH) ** 0.25
LN_EPS = 1e-5
RMS_EPS = 1e-5

LANES = 128
SUBLANES = 8
VMEM_LIMIT = 56 * 1024 * 1024


def _cparams(sem):
    return pltpu.CompilerParams(dimension_semantics=sem, vmem_limit_bytes=VMEM_LIMIT)


def _bdot(a, b):
    return jnp.dot(a.astype(BF16), b.astype(BF16), preferred_element_type=F32)


def _hdot(a, b):
    return jnp.dot(a, b, preferred_element_type=F32, precision=HI)


def _half_mask(shape, dtype=F32):
    lane = lax.broadcasted_iota(jnp.int32, shape, len(shape) - 1)
    return (lane < 64).astype(dtype)


def _seg_ones():
    i = lax.broadcasted_iota(jnp.int32, (LANES, LANES), 0)
    j = lax.broadcasted_iota(jnp.int32, (LANES, LANES), 1)
    return ((i // 64) == (j // 64)).astype(F32)


def _layer_norm_rows(z, g, b, eps):
    mu = jnp.mean(z, axis=-1, keepdims=True)
    zc = z - mu
    var = jnp.mean(zc * zc, axis=-1, keepdims=True)
    return zc * lax.rsqrt(var + eps) * g + b


def _mod_kernel(c_ref, w_ref, b_ref, o_ref):
    a = c_ref[...]
    a = a * jax.nn.sigmoid(a)
    o_ref[0] = _hdot(a, w_ref[0]) + b_ref[0]


def _modulation(c8, w_mod, b_mod):
    tn = 1536
    nl, d, n6 = w_mod.shape
    return pl.pallas_call(
        _mod_kernel,
        grid=(nl, n6 // tn),
        in_specs=[pl.BlockSpec((8, d), lambda l, j: (0, 0)),
                  pl.BlockSpec((1, d, tn), lambda l, j: (l, 0, j)),
                  pl.BlockSpec((1, 1, tn), lambda l, j: (l, 0, j))],
        out_specs=pl.BlockSpec((1, 8, tn), lambda l, j: (l, 0, j)),
        out_shape=jax.ShapeDtypeStruct((nl, 8, n6), F32),
        compiler_params=_cparams(("parallel", "parallel")),
        name="modulation",
    )(c8, w_mod, b_mod.reshape(nl, 1, n6))


def _inproj_kernel(x_ref, sc_ref, sh_ref, w_ref, *o_refs, splits):
    h = x_ref[0] * (1.0 + sc_ref[0]) + sh_ref[0]
    y = _bdot(h, w_ref[...])
    off = 0
    for o, s in zip(o_refs, splits):
        o[0] = y[:, off:off + s].astype(o.dtype)
        off += s


def _inproj(x, sc, sh, w_bf16, splits, tm=256):
    b, t, d = x.shape
    tm = min(tm, t)
    dout = w_bf16.shape[1]
    return pl.pallas_call(
        functools.partial(_inproj_kernel, splits=splits),
        grid=(b, t // tm),
        in_specs=[pl.BlockSpec((1, tm, d), lambda i, j: (i, j, 0)),
                  pl.BlockSpec((1, 1, d), lambda i, j: (i, 0, 0)),
                  pl.BlockSpec((1, 1, d), lambda i, j: (i, 0, 0)),
                  pl.BlockSpec((d, dout), lambda i, j: (0, 0))],
        out_specs=[pl.BlockSpec((1, tm, s), lambda i, j: (i, j, 0)) for s in splits],
        out_shape=[jax.ShapeDtypeStruct((b, t, s), F32) for s in splits],
        compiler_params=_cparams(("parallel", "parallel")),
        name="inproj",
    )(x, sc, sh, w_bf16)


def _shifted(p, prev_row, next_row):
    tb = p.shape[0]
    row = lax.broadcasted_iota(jnp.int32, p.shape, 0)
    p_prev = jnp.where(row == 0, prev_row, pltpu.roll(p, 1, axis=0))
    p_next = jnp.where(row == tb - 1, next_row, pltpu.roll(p, tb - 1, axis=0))
    return p_prev, p_next


def _edge_rows(pp_ref, pn_ref):
    i = pl.program_id(1)
    nb = pl.num_programs(1)
    prev_row = jnp.where(i > 0, pp_ref[0, SUBLANES - 1:SUBLANES, :], 0.0)
    next_row = jnp.where(i < nb - 1, pn_ref[0, 0:1, :], 0.0)
    return prev_row, next_row


def _feat_kernel(p_ref, pp_ref, pn_ref, mu_ref, dup_ref, d0_ref, aup_ref, a0_ref, gup_ref, kxi_ref, kal_ref,
                 rb_ref, r_o, kk_o, v_o, kr0_o, kr1_o, lw0_o, lw1_o, b0_o, b1_o, g_o, gbv_o):
    p = p_ref[0]
    prev_row, next_row = _edge_rows(pp_ref, pn_ref)
    p_prev, p_next = _shifted(p, prev_row, next_row)
    p = p + mu_ref[...] * (0.5 * (p_prev + p_next) - p)
    dr = D_RWKV
    r = p[:, 0:dr]
    k = p[:, dr:2 * dr]
    v = p[:, 2 * dr:3 * dr]
    c0 = 3 * dr
    dd = jnp.tanh(p[:, c0:c0 + 2 * LORA])
    da = p[:, c0 + 2 * LORA:c0 + 4 * LORA]
    dg = jax.nn.sigmoid(p[:, c0 + 4 * LORA:c0 + 4 * LORA + LORA_G])
    g = _hdot(dg, gup_ref[...])
    lw, a = [], []
    for d in range(2):
        lw.append(-DECAY_SCALE * jax.nn.sigmoid(d0_ref[d:d + 1, :] + _hdot(dd[:, d * LORA:(d + 1) * LORA], dup_ref[d])))
        a.append(jax.nn.sigmoid(a0_ref[d:d + 1, :] + _hdot(da[:, d * LORA:(d + 1) * LORA], aup_ref[d])))
    kx = k * kxi_ref[...]
    kal = kal_ref[...]
    kr = [k * (1.0 + (a[d] - 1.0) * kal) for d in range(2)]
    bon = r * (0.5 * (kr[0] + kr[1])) * rb_ref[...]
    ones = _seg_ones()
    g_o[0] = g
    for j in range(N_PAIRS):
        sl = slice(j * LANES, (j + 1) * LANES)
        kxj = kx[:, sl]
        kk = kxj * lax.rsqrt(_hdot(kxj * kxj, ones) + 1e-12)
        r_o[0, j] = r[:, sl]
        kk_o[0, j] = kk
        v_o[0, j] = v[:, sl]
        kr0_o[0, j] = kr[0][:, sl]
        kr1_o[0, j] = kr[1][:, sl]
        lw0_o[0, j] = lw[0][:, sl]
        lw1_o[0, j] = lw[1][:, sl]
        b0_o[0, j] = kk * a[0][:, sl]
        b1_o[0, j] = kk * a[1][:, sl]
        gbv_o[0, :, sl] = g[:, sl] * _hdot(bon[:, sl], ones) * v[:, sl]


def _rwkv_features(p, prm, tb=256):
    b, t, cols = p.shape
    tb = min(tb, t)
    nb8 = t // SUBLANES
    r8 = tb // SUBLANES
    full = lambda a: pl.BlockSpec(a.shape, lambda i, j: (0,) * a.ndim)
    params = [prm["shift_mu"].reshape(1, cols), prm["decay_up"], prm["decay_0"], prm["a_up"], prm["a_0"],
              prm["g_up"], prm["k_xi"].reshape(1, -1), prm["k_alpha"].reshape(1, -1), prm["r_bonus"].reshape(1, -1)]
    packed = jax.ShapeDtypeStruct((b, N_PAIRS, t, LANES), F32)
    flat = jax.ShapeDtypeStruct((b, t, D_RWKV), F32)
    pspec = pl.BlockSpec((1, N_PAIRS, tb, LANES), lambda i, j: (i, 0, j, 0))
    fspec = pl.BlockSpec((1, tb, D_RWKV), lambda i, j: (i, j, 0))
    return pl.pallas_call(
        _feat_kernel,
        grid=(b, t // tb),
        in_specs=[pl.BlockSpec((1, tb, cols), lambda i, j: (i, j, 0)),
                  pl.BlockSpec((1, SUBLANES, cols), lambda i, j: (i, jnp.maximum(j * r8 - 1, 0), 0)),
                  pl.BlockSpec((1, SUBLANES, cols), lambda i, j: (i, jnp.minimum((j + 1) * r8, nb8 - 1), 0))]
                 + [full(a) for a in params],
        out_specs=[pspec] * 9 + [fspec] * 2,
        out_shape=[packed] * 9 + [flat] * 2,
        compiler_params=_cparams(("parallel", "parallel")),
        name="rwkv_features",
    )(p, p, p, *params)


def _scan_chunks(feats, states, rev):
    n = SCAN_CHUNK
    m0 = _half_mask((n, LANES))
    m1 = 1.0 - m0
    ti = lax.broadcasted_iota(jnp.int32, (n, n), 0)
    tj = lax.broadcasted_iota(jnp.int32, (n, n), 1)
    tri = ((ti <= tj) if rev else (ti >= tj)).astype(F32)
    si = lax.broadcasted_iota(jnp.int32, (2 * n, 2 * n), 0)
    sj = lax.broadcasted_iota(jnp.int32, (2 * n, 2 * n), 1)
    same = (si // n) == (sj // n)
    ri, rj = si % n, sj % n
    strict = same & ((ri < rj) if rev else (ri > rj))
    incl = same & ((ri <= rj) if rev else (ri >= rj))
    eye = (si == sj).astype(F32)

    def stack(x):
        return jnp.concatenate([x * m0, x * m1], axis=0).astype(BF16)

    pre = []
    for r, lw, kap, bb, kr, v in feats:
        cs = _hdot(tri, lw)
        tot = jnp.sum(lw, axis=0, keepdims=True)
        e_neg = jnp.exp(-cs)
        e_tot = jnp.exp(tot - cs)
        rh = r * jnp.exp(cs)
        pre.append(dict(kh=stack(kap * jnp.exp(cs - lw)), bh=stack(bb * e_neg), kq=stack(kr * e_neg), rh=rh,
                        rhs=stack(rh), kt=stack(kr * e_tot), bt=stack(bb * e_tot), v=stack(v), gl=jnp.exp(tot)))
    nt = (((1,), (1,)), ((), ()))
    amats = [lax.dot_general(jnp.concatenate([p["kh"], p["rhs"]], axis=0), jnp.concatenate([p["bh"], p["kq"]], axis=0),
                             nt, preferred_element_type=F32) for p in pre]
    a1 = [jnp.where(strict, a[:2 * n, :2 * n], 0.0) for a in amats]
    a2 = [jnp.where(strict, a[:2 * n, 2 * n:], 0.0).astype(BF16) for a in amats]
    a4 = [jnp.where(incl, a[2 * n:, :2 * n], 0.0).astype(BF16) for a in amats]
    a3 = [jnp.where(incl, a[2 * n:, 2 * n:], 0.0).astype(BF16) for a in amats]
    tinv = [eye - a for a in a1]
    pw = a1
    for _ in range(5):
        pw = [_bdot(p, p) for p in pw]
        tinv = [t + _bdot(t, p) for t, p in zip(tinv, pw)]
    w_s = [_bdot(a, p["v"]) for a, p in zip(a2, pre)]
    mm = [_bdot(t, jnp.concatenate([p["kh"], w.astype(BF16)], axis=1)).astype(BF16)
          for t, p, w in zip(tinv, pre, w_s)]
    gmat = [_bdot(p["bt"].astype(F32).T, m) for p, m in zip(pre, mm)]
    phi_t = [eye * p["gl"] - g[:, :LANES] for p, g in zip(pre, gmat)]
    psi_t = [_bdot(p["kt"].astype(F32).T, p["v"]) - g[:, LANES:] for p, g in zip(pre, gmat)]
    qy = [_bdot(a, m) for a, m in zip(a4, mm)]
    y0_s = [_bdot(a, p["v"]) - q[:, LANES:] for a, p, q in zip(a3, pre, qy)]
    ys, sts = [], []
    for p, q, y0, ph, ps, st in zip(pre, qy, y0_s, phi_t, psi_t, states):
        qm = p["rh"] - (q[:n, :LANES] + q[n:, :LANES])
        ys.append(_bdot(qm, st) + (y0[:n] + y0[n:]))
        sts.append(_bdot(ph, st) + ps)
    return ys, sts


def _scan_kernel(r_ref, lw_ref, kap_ref, b_ref, kr_ref, v_ref, s0_ref, y_ref, sT_ref, st_scr, *, rev):
    c = pl.program_id(1)

    @pl.when(c == 0)
    def _():
        st_scr[...] = s0_ref[0]

    feats = [(r_ref[0, hp], lw_ref[0, hp], kap_ref[0, hp], b_ref[0, hp], kr_ref[0, hp], v_ref[0, hp])
             for hp in range(N_PAIRS)]
    ys, sts = _scan_chunks(feats, [st_scr[hp] for hp in range(N_PAIRS)], rev)
    for hp in range(N_PAIRS):
        y_ref[0, hp] = ys[hp]
        st_scr[hp] = sts[hp]

    @pl.when(c == pl.num_programs(1) - 1)
    def _():
        sT_ref[0] = st_scr[...]


def _rwkv_scan(r, lw, kap, bb, kr, v, s0, rev):
    b, _, t, _ = r.shape
    n = SCAN_CHUNK
    nc = t // n
    cidx = (lambda c: nc - 1 - c) if rev else (lambda c: c)
    fspec = pl.BlockSpec((1, N_PAIRS, n, LANES), lambda i, c: (i, 0, cidx(c), 0))
    sspec = pl.BlockSpec((1, N_PAIRS, LANES, LANES), lambda i, c: (i, 0, 0, 0))
    return pl.pallas_call(
        functools.partial(_scan_kernel, rev=rev),
        grid=(b, nc),
        in_specs=[fspec] * 6 + [sspec],
        out_specs=[fspec, sspec],
        out_shape=[jax.ShapeDtypeStruct(r.shape, F32), jax.ShapeDtypeStruct(s0.shape, F32)],
        scratch_shapes=[pltpu.VMEM((N_PAIRS, LANES, LANES), F32)],
        compiler_params=_cparams(("parallel", "arbitrary")),
        name="rwkv_scan_rev" if rev else "rwkv_scan_fwd",
    )(r, lw, kap, bb, kr, v, s0)


def _evenpost_kernel(y0_ref, y1_ref, g_ref, gbv_ref, pc_ref, pcp_ref, pcn_ref, cw_ref, gng_ref, gnb_ref,
                     conv_o, rw_o):
    pc = pc_ref[0]
    prev_row, next_row = _edge_rows(pcp_ref, pcn_ref)
    dc = D_CONV

    def gated(z):
        return z[:, dc:2 * dc] * z[:, 2 * dc:3 * dc]

    u = gated(pc)
    u_prev, u_next = _shifted(u, gated(prev_row), gated(next_row))
    cw = cw_ref[...]
    conv_o[0] = (pc[:, 0:dc] * (cw[0:1] * u_prev + cw[1:2] * u + cw[2:3] * u_next)).astype(conv_o.dtype)
    ones = _seg_ones() * (1.0 / RWKV_HD)
    for j in range(N_PAIRS):
        sl = slice(j * LANES, (j + 1) * LANES)
        y = y0_ref[0, j] + y1_ref[0, j]
        mu = _hdot(y, ones)
        yc = y - mu
        var = _hdot(yc * yc, ones)
        yn = yc * lax.rsqrt(var + GN_EPS) * gng_ref[:, sl] + gnb_ref[:, sl]
        rw_o[0, :, sl] = (g_ref[0, :, sl] * yn + gbv_ref[0, :, sl]).astype(rw_o.dtype)


def _even_post(y0, y1, g, gbv, pc, conv_w, gn_g, gn_b, tb=256):
    b, _, t, _ = y0.shape
    tb = min(tb, t)
    nb8 = t // SUBLANES
    r8 = tb // SUBLANES
    c3 = pc.shape[-1]
    pspec = pl.BlockSpec((1, N_PAIRS, tb, LANES), lambda i, j: (i, 0, j, 0))
    fspec = pl.BlockSpec((1, tb, D_RWKV), lambda i, j: (i, j, 0))
    full = lambda a: pl.BlockSpec(a.shape, lambda i, j: (0,) * a.ndim)
    gn_g = gn_g.reshape(1, -1)
    gn_b = gn_b.reshape(1, -1)
    return pl.pallas_call(
        _evenpost_kernel,
        grid=(b, t // tb),
        in_specs=[pspec, pspec, fspec, fspec,
                  pl.BlockSpec((1, tb, c3), lambda i, j: (i, j, 0)),
                  pl.BlockSpec((1, SUBLANES, c3), lambda i, j: (i, jnp.maximum(j * r8 - 1, 0), 0)),
                  pl.BlockSpec((1, SUBLANES, c3), lambda i, j: (i, jnp.minimum((j + 1) * r8, nb8 - 1), 0)),
                  full(conv_w), full(gn_g), full(gn_b)],
        out_specs=[pl.BlockSpec((1, tb, D_CONV), lambda i, j: (i, j, 0)), fspec],
        out_shape=[jax.ShapeDtypeStruct((b, t, D_CONV), BF16), jax.ShapeDtypeStruct((b, t, D_RWKV), BF16)],
        compiler_params=_cparams(("parallel", "parallel")),
        name="even_post",
    )(y0, y1, g, gbv, pc, pc, pc, conv_w, gn_g, gn_b)


def _outproj_kernel(a_ref, b_ref, wa_ref, wb_ref, x_ref, gm_ref, lg_ref, lb_ref, o_ref):
    yx = _bdot(a_ref[0], wa_ref[...]) + _bdot(b_ref[0], wb_ref[...])
    z = ALPHA * x_ref[0] + gm_ref[0] * yx
    o_ref[0] = _layer_norm_rows(z, lg_ref[...], lb_ref[...], LN_EPS)


def _outproj(ma, mb, w_out_bf16, x, gm, ln_g, ln_b, tm=512):
    b, t, d = x.shape
    tm = min(tm, t)
    da, db = ma.shape[-1], mb.shape[-1]
    wa, wb = w_out_bf16[:da], w_out_bf16[da:]
    row = lambda w: pl.BlockSpec((1, tm, w), lambda i, j: (i, j, 0))
    full = lambda a: pl.BlockSpec(a.shape, lambda i, j: (0,) * a.ndim)
    ln_g = ln_g.reshape(1, d)
    ln_b = ln_b.reshape(1, d)
    return pl.pallas_call(
        _outproj_kernel,
        grid=(b, t // tm),
        in_specs=[row(da), row(db), full(wa), full(wb), row(d),
                  pl.BlockSpec((1, 1, d), lambda i, j: (i, 0, 0)), full(ln_g), full(ln_b)],
        out_specs=row(d),
        out_shape=jax.ShapeDtypeStruct((b, t, d), F32),
        compiler_params=_cparams(("parallel", "parallel")),
        name="outproj_norm",
    )(ma, mb, wa, wb, x, gm, ln_g, ln_b)


def _rope_kernel(x_ref, cos_ref, sin_ref, o_ref, *, scale):
    cos = cos_ref[...]
    sin = sin_ref[...]
    lane = lax.broadcasted_iota(jnp.int32, cos.shape, 1)
    first = (lane % (2 * (AXIS_DIM // 2))) < (AXIS_DIM // 2)
    half = AXIS_DIM // 2
    for j in range(DIFF_HEADS):
        sl = slice(j * LANES, (j + 1) * LANES)
        xs = x_ref[0, :, sl]
        partner = jnp.where(first, -pltpu.roll(xs, LANES - half, axis=1), pltpu.roll(xs, half, axis=1))
        o_ref[0, :, sl] = ((xs * cos + partner * sin) * scale).astype(o_ref.dtype)


def _rope(x, cos, sin, scale, tb=512):
    b, t, w = x.shape
    tb = min(tb, t)
    return pl.pallas_call(
        functools.partial(_rope_kernel, scale=scale),
        grid=(b, t // tb),
        in_specs=[pl.BlockSpec((1, tb, w), lambda i, j: (i, j, 0)),
                  pl.BlockSpec((tb, LANES), lambda i, j: (j, 0)),
                  pl.BlockSpec((tb, LANES), lambda i, j: (j, 0))],
        out_specs=pl.BlockSpec((1, tb, w), lambda i, j: (i, j, 0)),
        out_shape=jax.ShapeDtypeStruct((b, t, w), BF16),
        compiler_params=_cparams(("parallel", "parallel")),
        name="axial_rope",
    )(x, cos, sin)


def _rope_tables(t):
    rows = t // GRID_W
    row = jnp.repeat(jnp.arange(rows), GRID_W).astype(F32)
    col = jnp.tile(jnp.arange(GRID_W), rows).astype(F32)
    inv = ROPE_BASE ** (-jnp.arange(0, AXIS_DIM, 2, dtype=F32) / AXIS_DIM)
    ang_r = row[:, None] * inv
    ang_c = col[:, None] * inv
    ang = jnp.concatenate([ang_r, ang_r, ang_c, ang_c], axis=-1)
    ang = jnp.concatenate([ang, ang], axis=-1)
    return jnp.cos(ang), jnp.sin(ang)


def _attn_kernel(lam_ref, q_ref, k_ref, v_ref, g_ref, o_ref, *, tk, out_scale):
    q = q_ref[0]
    tq = q.shape[0]
    hm = _half_mask(q.shape, BF16)
    q1 = q * hm
    qs = (q1, q - q1)
    nk = k_ref.shape[1] // tk
    nsl = tk // LANES

    def scores(qh, j):
        kb = k_ref[0, pl.ds(pl.multiple_of(j * tk, tk), tk), :]
        return lax.dot_general(qh, kb, (((1,), (1,)), ((), ())), preferred_element_type=F32)

    def col(s, c):
        return s[:, c * LANES:(c + 1) * LANES]

    def body(j, carry):
        vb = v_ref[0, pl.ds(pl.multiple_of(j * tk, tk), tk), :]
        ss = [scores(qh, j) for qh in qs]
        out = []
        for s, (m, ls, acc) in zip(ss, (carry[:3], carry[3:])):
            mx = col(s, 0)
            for c in range(1, nsl):
                mx = jnp.maximum(mx, col(s, c))
            m_new = jnp.maximum(m, jnp.broadcast_to(jnp.max(mx, axis=-1, keepdims=True), (tq, LANES)))
            corr = jnp.exp2(m - m_new)
            ps = [jnp.exp2(col(s, c) - m_new) for c in range(nsl)]
            ls = corr * ls
            for pc in ps:
                ls = ls + pc
            p = jnp.concatenate([pc.astype(BF16) for pc in ps], axis=1)
            acc = corr * acc + jnp.dot(p, vb, preferred_element_type=F32)
            out += [m_new, ls, acc]
        return tuple(out)

    neg = jnp.full((tq, LANES), -jnp.inf, F32)
    zero = jnp.zeros((tq, LANES), F32)
    _, ls1, acc1, _, ls2, acc2 = lax.fori_loop(0, nk, body, (neg, zero, zero, neg, zero, zero))
    l1 = jnp.sum(ls1, axis=-1, keepdims=True)
    l2 = jnp.sum(ls2, axis=-1, keepdims=True)
    o = acc1 / l1 - lam_ref[0] * (acc2 / l2)
    o = o * lax.rsqrt(jnp.mean(o * o, axis=-1, keepdims=True) + RMS_EPS) * g_ref[...] * out_scale
    o_ref[0] = o.astype(o_ref.dtype)


def _diff_attention(q, k, v, lam, subln_g, lam_init, tq=512, tk=2816):
    b, t, w = q.shape
    tkk = k.shape[1]
    tq = min(tq, t)
    tk = math.gcd(tk, tkk)
    assert tk % LANES == 0 and t % tq == 0
    subln_g = subln_g.reshape(1, DIFF_VD)
    return pl.pallas_call(
        functools.partial(_attn_kernel, tk=tk, out_scale=1.0 - lam_init),
        grid=(b, DIFF_HEADS, t // tq),
        in_specs=[pl.BlockSpec(memory_space=pltpu.SMEM),
                  pl.BlockSpec((1, tq, LANES), lambda i, h, j: (i, j, h)),
                  pl.BlockSpec((1, tkk, LANES), lambda i, h, j: (i, 0, h)),
                  pl.BlockSpec((1, tkk, LANES), lambda i, h, j: (i, 0, h)),
                  pl.BlockSpec((1, DIFF_VD), lambda i, h, j: (0, 0))],
        out_specs=pl.BlockSpec((1, tq, LANES), lambda i, h, j: (i, j, h)),
        out_shape=jax.ShapeDtypeStruct((b, t, w), BF16),
        compiler_params=_cparams(("parallel", "parallel", "parallel")),
        name="diff_attention",
    )(lam, q, k, v, subln_g)


def _gmlp_kernel(p_ref, lg_ref, lb_ref, ws_ref, bs_ref, o_ref):
    p = p_ref[0]
    tb = p.shape[0]
    ge = 0.5 * p * (1.0 + lax.erf(p * (2.0 ** -0.5)))
    u = ge[:, :D_GMLP]
    v = _layer_norm_rows(ge[:, D_GMLP:], lg_ref[...], lb_ref[...], LN_EPS)
    hm = _half_mask((GMLP_CHUNK, LANES))
    for c in range(tb // GMLP_CHUNK):
        rs = slice(c * GMLP_CHUNK, (c + 1) * GMLP_CHUNK)
        for j in range(D_GMLP // LANES):
            sl = slice(j * LANES, (j + 1) * LANES)
            vc = v[rs, sl]
            va = vc * hm
            mixed = _bdot(ws_ref[2 * j], va) + _bdot(ws_ref[2 * j + 1], vc - va) + bs_ref[:, sl]
            o_ref[0, rs, sl] = (u[rs, sl] * mixed).astype(o_ref.dtype)


def _chunk_gmlp(p, ln_g, ln_b, ws, bs, tb=256):
    b, t, w = p.shape
    tb = min(tb, t)
    ln_g = ln_g.reshape(1, -1)
    ln_b = ln_b.reshape(1, -1)
    bs_t = jnp.repeat(jnp.transpose(bs), D_GMLP // ws.shape[0], axis=1)
    full = lambda a: pl.BlockSpec(a.shape, lambda i, j: (0,) * a.ndim)
    return pl.pallas_call(
        _gmlp_kernel,
        grid=(b, t // tb),
        in_specs=[pl.BlockSpec((1, tb, w), lambda i, j: (i, j, 0)), full(ln_g), full(ln_b), full(ws), full(bs_t)],
        out_specs=pl.BlockSpec((1, tb, D_GMLP), lambda i, j: (i, j, 0)),
        out_shape=jax.ShapeDtypeStruct((b, t, D_GMLP), BF16),
        compiler_params=_cparams(("parallel", "parallel")),
        name="chunk_gmlp",
    )(p, ln_g, ln_b, ws, bs_t)


def _router_kernel(x_ref, sc_ref, sh_ref, wr_ref, h_o, aff_o):
    h = x_ref[0] * (1.0 + sc_ref[0]) + sh_ref[0]
    h_o[0] = h.astype(h_o.dtype)
    logits = lax.dot_general(wr_ref[...], h, (((1,), (1,)), ((), ())), preferred_element_type=F32, precision=HI)
    m = jnp.max(logits, axis=0, keepdims=True)
    e = jnp.exp(logits - m)
    aff_o[0] = e / jnp.sum(e, axis=0, keepdims=True)


def _router(x, sc, sh, w_router_t, tm=512):
    b, t, d = x.shape
    tm = min(tm, t)
    e = w_router_t.shape[0]
    return pl.pallas_call(
        _router_kernel,
        grid=(b, t // tm),
        in_specs=[pl.BlockSpec((1, tm, d), lambda i, j: (i, j, 0)),
                  pl.BlockSpec((1, 1, d), lambda i, j: (i, 0, 0)),
                  pl.BlockSpec((1, 1, d), lambda i, j: (i, 0, 0)),
                  pl.BlockSpec((e, d), lambda i, j: (0, 0))],
        out_specs=[pl.BlockSpec((1, tm, d), lambda i, j: (i, j, 0)),
                   pl.BlockSpec((1, e, tm), lambda i, j: (i, 0, j))],
        out_shape=[jax.ShapeDtypeStruct((b, t, d), BF16), jax.ShapeDtypeStruct((b, e, t), F32)],
        compiler_params=_cparams(("parallel", "parallel")),
        name="router",
    )(x, sc, sh, w_router_t)


ROUTE_BLOCK = 256
STARTS_PAD = LANES


def _lane_cumsum(mask):
    e, t = mask.shape
    i = lax.broadcasted_iota(jnp.int32, (LANES, LANES), 0)
    j = lax.broadcasted_iota(jnp.int32, (LANES, LANES), 1)
    upper = (i <= j).astype(BF16)
    x = jnp.where(mask, 1.0, 0.0).astype(BF16)
    off = jnp.zeros((e, 1), F32)
    out = []
    for c in range(t // LANES):
        blk = jnp.dot(x[:, c * LANES:(c + 1) * LANES], upper, preferred_element_type=F32) + off
        out.append(blk)
        off = blk[:, LANES - 1:LANES]
    return jnp.concatenate(out, axis=1)


def _select_kernel(aff_ref, pos_ref, starts_ref, *, cap):
    a = aff_ref[0]
    e, t = a.shape
    bits = pltpu.bitcast(a, jnp.int32)

    def search(i, thr):
        cand = thr | jnp.left_shift(jnp.int32(1), 30 - i)
        cnt = jnp.sum(jnp.where(bits >= cand, 1.0, 0.0), axis=1, keepdims=True)
        return jnp.where(cnt >= cap, cand, thr)

    thr = lax.fori_loop(0, 31, search, jnp.zeros((e, 1), jnp.int32))
    gt = bits > thr
    eq = bits == thr
    need = cap - jnp.sum(jnp.where(gt, 1.0, 0.0), axis=1, keepdims=True)
    sel = gt | (eq & (_lane_cumsum(eq) <= need))
    csel = _lane_cumsum(sel)
    pos_ref[0] = jnp.where(sel, csel - 1.0, -1.0).astype(jnp.int32)
    ti = lax.broadcasted_iota(jnp.int32, (t, STARTS_PAD), 0)
    ki = lax.broadcasted_iota(jnp.int32, (t, STARTS_PAD), 1)
    pick = jnp.where(ti == ki * ROUTE_BLOCK - 1, 1.0, 0.0)
    starts_ref[0] = (_hdot(csel, pick) + 0.5).astype(jnp.int32)


def _select(aff, cap):
    b, e, t = aff.shape
    assert t % ROUTE_BLOCK == 0 and t // ROUTE_BLOCK <= STARTS_PAD
    return pl.pallas_call(
        functools.partial(_select_kernel, cap=cap),
        grid=(b,),
        in_specs=[pl.BlockSpec((1, e, t), lambda i: (i, 0, 0))],
        out_specs=[pl.BlockSpec((1, e, t), lambda i: (i, 0, 0)), pl.BlockSpec((1, e, STARTS_PAD), lambda i: (i, 0, 0))],
        out_shape=[jax.ShapeDtypeStruct((b, e, t), jnp.int32), jax.ShapeDtypeStruct((b, e, STARTS_PAD), jnp.int32)],
        compiler_params=_cparams(("parallel",)),
        name="expert_select",
    )(aff)


def _window_start(start, align, cap, width):
    s = lax.shift_left(lax.shift_right_logical(start, align.bit_length() - 1), align.bit_length() - 1)
    return pl.multiple_of(jnp.minimum(s, cap - width), align)


def _gather_ffn_kernel(starts_ref, pos_ref, h_ref, w1_ref, w3_ref, w2_ref, y_ref, xs_scr, *, width, rows):
    ei, bi, k = pl.program_id(0), pl.program_id(1), pl.program_id(2)
    ne = pl.num_programs(0)
    cap = xs_scr.shape[0]
    nsub = h_ref.shape[1] // ROUTE_BLOCK

    @pl.when(k == 0)
    def _():
        xs_scr[...] = jnp.zeros_like(xs_scr)

    jrow = lax.broadcasted_iota(jnp.int32, (width, ROUTE_BLOCK), 0)
    for sb in range(nsub):
        start = starts_ref[(bi * ne + ei) * STARTS_PAD + k * nsub + sb]
        s8 = _window_start(start, SUBLANES, cap, width)
        prow = pos_ref[0, :, sb * ROUTE_BLOCK:(sb + 1) * ROUTE_BLOCK]
        onehot = jnp.where(prow - s8 == jrow, 1.0, 0.0).astype(BF16)
        xs_scr[pl.ds(s8, width), :] += jnp.dot(onehot, h_ref[0, sb * ROUTE_BLOCK:(sb + 1) * ROUTE_BLOCK, :],
                                               preferred_element_type=F32)

    @pl.when(k == pl.num_programs(2) - 1)
    def _():
        for r in range(cap // rows):
            x = xs_scr[r * rows:(r + 1) * rows, :].astype(BF16)
            h1 = jnp.dot(x, w1_ref[0], preferred_element_type=F32)
            h3 = jnp.dot(x, w3_ref[0], preferred_element_type=F32)
            hid = (h1 * jax.nn.sigmoid(h1)) * h3
            y = jnp.dot(hid.astype(BF16), w2_ref[0], preferred_element_type=F32)
            y_ref[0, 0, r * rows:(r + 1) * rows, :] = y.astype(y_ref.dtype)


def _gather_ffn(h, pos, starts, w1, w3, w2, cap, tch=2048, rows=256):
    b, t, d = h.shape
    e, _, f = w1.shape
    tch = min(tch, t)
    rows = min(rows, cap)
    width = min(ROUTE_BLOCK + SUBLANES, cap)
    grid_spec = pltpu.PrefetchScalarGridSpec(
        num_scalar_prefetch=1,
        grid=(e, b, t // tch),
        in_specs=[pl.BlockSpec((1, 1, tch), lambda ei, bi, k, s: (bi * e + ei, 0, k)),
                  pl.BlockSpec((1, tch, d), lambda ei, bi, k, s: (bi, k, 0)),
                  pl.BlockSpec((1, d, f), lambda ei, bi, k, s: (ei, 0, 0)),
                  pl.BlockSpec((1, d, f), lambda ei, bi, k, s: (ei, 0, 0)),
                  pl.BlockSpec((1, f, d), lambda ei, bi, k, s: (ei, 0, 0))],
        out_specs=pl.BlockSpec((1, 1, cap, d), lambda ei, bi, k, s: (bi, ei, 0, 0)),
        scratch_shapes=[pltpu.VMEM((cap, d), F32)],
    )
    return pl.pallas_call(
        functools.partial(_gather_ffn_kernel, width=width, rows=rows),
        grid_spec=grid_spec,
        out_shape=jax.ShapeDtypeStruct((b, e, cap, d), BF16),
        compiler_params=_cparams(("parallel", "parallel", "arbitrary")),
        name="expert_gather_ffn",
    )(starts.reshape(-1), pos.reshape(b * e, 1, t), h, w1, w3, w2)


def _combine_kernel(starts_ref, post_ref, afft_ref, y_ref, x_ref, gf_ref, lg_ref, lb_ref, o_ref, acc_scr, *, width):
    bi, i, ei = pl.program_id(0), pl.program_id(1), pl.program_id(2)
    ne = pl.num_programs(2)
    cap = y_ref.shape[2]
    nsub = x_ref.shape[1] // ROUTE_BLOCK

    @pl.when(ei == 0)
    def _():
        acc_scr[...] = jnp.zeros_like(acc_scr)

    elane = lax.broadcasted_iota(jnp.int32, (ROUTE_BLOCK, ne), 1) == ei
    jcol = lax.broadcasted_iota(jnp.int32, (ROUTE_BLOCK, width), 1)
    for sb in range(nsub):
        rs = slice(sb * ROUTE_BLOCK, (sb + 1) * ROUTE_BLOCK)
        start = starts_ref[(bi * ne + ei) * STARTS_PAD + i * nsub + sb]
        s16 = _window_start(start, 2 * SUBLANES, cap, width)
        pcol = jnp.sum(jnp.where(elane, post_ref[0, rs, :].astype(F32), 0.0), axis=1, keepdims=True)
        gcol = jnp.sum(jnp.where(elane, afft_ref[0, rs, :], 0.0), axis=1, keepdims=True)
        onehot = jnp.where(pcol.astype(jnp.int32) - s16 == jcol, 1.0, 0.0).astype(BF16)
        acc_scr[rs, :] += gcol * jnp.dot(onehot, y_ref[0, 0, pl.ds(s16, width), :], preferred_element_type=F32)

    @pl.when(ei == ne - 1)
    def _():
        z = ALPHA * x_ref[0] + gf_ref[0] * acc_scr[...]
        o_ref[0] = _layer_norm_rows(z, lg_ref[...], lb_ref[...], LN_EPS)


def _combine(x, y, pos_t, aff_t, starts, gf, ln_g, ln_b, tbo=1024):
    b, t, d = x.shape
    _, e, cap, _ = y.shape
    tbo = min(tbo, t)
    width = min(ROUTE_BLOCK + 2 * SUBLANES, cap)
    grid_spec = pltpu.PrefetchScalarGridSpec(
        num_scalar_prefetch=1,
        grid=(b, t // tbo, e),
        in_specs=[pl.BlockSpec((1, tbo, e), lambda bi, i, ei, s: (bi, i, 0)),
                  pl.BlockSpec((1, tbo, e), lambda bi, i, ei, s: (bi, i, 0)),
                  pl.BlockSpec((1, 1, cap, d), lambda bi, i, ei, s: (bi, ei, 0, 0)),
                  pl.BlockSpec((1, tbo, d), lambda bi, i, ei, s: (bi, i, 0)),
                  pl.BlockSpec((1, 1, d), lambda bi, i, ei, s: (bi, 0, 0)),
                  pl.BlockSpec((1, d), lambda bi, i, ei, s: (0, 0)),
                  pl.BlockSpec((1, d), lambda bi, i, ei, s: (0, 0))],
        out_specs=pl.BlockSpec((1, tbo, d), lambda bi, i, ei, s: (bi, i, 0)),
        scratch_shapes=[pltpu.VMEM((tbo, d), F32)],
    )
    return pl.pallas_call(
        functools.partial(_combine_kernel, width=width),
        grid_spec=grid_spec,
        out_shape=jax.ShapeDtypeStruct((b, t, d), F32),
        compiler_params=_cparams(("parallel", "parallel", "arbitrary")),
        name="expert_combine_norm",
    )(starts.reshape(-1), pos_t, aff_t, y, x, gf, ln_g.reshape(1, d), ln_b.reshape(1, d))


def _cast_kernel(x_ref, o_ref):
    o_ref[...] = x_ref[...].astype(o_ref.dtype)


def _to_bf16(w, l):
    _, e, r, c = w.shape
    return pl.pallas_call(
        _cast_kernel,
        grid=(e,),
        in_specs=[pl.BlockSpec((None, 1, r, c), lambda i: (l, i, 0, 0))],
        out_specs=pl.BlockSpec((1, r, c), lambda i: (i, 0, 0)),
        out_shape=jax.ShapeDtypeStruct((e, r, c), BF16),
        compiler_params=_cparams(("parallel",)),
        name="cast_bf16",
    )(w)


def _moe(x, sc, sh, gf, w_router_t, w1, w3, w2, ln_g, ln_b):
    t = x.shape[1]
    cap = EC_FACTOR * t // N_EXPERTS
    h, aff = _router(x, sc, sh, w_router_t)
    pos, starts = _select(aff, cap)
    y = _gather_ffn(h, pos, starts, w1, w3, w2, cap)
    to_token_major = lambda z: jnp.transpose(z, (0, 2, 1))
    return _combine(x, y, to_token_major(pos), to_token_major(aff), starts, gf, ln_g, ln_b)


def _even_layer(hx_in, hc_in, prm, ctx_out):
    w_in = prm["w_in"]
    splits = (3 * D_CONV, RWKV_COLS)
    pcx, prx = _inproj(*hx_in, w_in, splits)
    pcc, prc = _inproj(*hc_in, w_in, splits)
    fx = _rwkv_features(prx, prm)
    fc = _rwkv_features(prc, prm)
    b = prx.shape[0]
    zero = jnp.zeros((b, N_PAIRS, LANES, LANES), F32)
    ys_x, ys_c = [], []
    for d, rev in ((0, False), (1, True)):
        def args(f):
            r, kk, v, kr0, kr1, lw0, lw1, b0, b1 = f[:9]
            return (r, (lw0, lw1)[d], kk, (b0, b1)[d], (kr0, kr1)[d], v)
        yc, s_ctx = _rwkv_scan(*args(fc), zero, rev)
        yx, _ = _rwkv_scan(*args(fx), s_ctx, rev)
        ys_x.append(yx)
        ys_c.append(yc)
    out_x = _even_post(ys_x[0], ys_x[1], fx[9], fx[10], pcx, prm["conv_w"], prm["gn_g"], prm["gn_b"])
    out_c = None
    if ctx_out:
        out_c = _even_post(ys_c[0], ys_c[1], fc[9], fc[10], pcc, prm["conv_w"], prm["gn_g"], prm["gn_b"])
    return out_c, out_x


def _odd_layer(hx_in, hc_in, prm, lam_init, ctx_out):
    w_in = prm["w_in"]
    splits = (D_DIFF, D_DIFF, D_DIFF, 2 * D_GMLP)
    qx, kx, vx, gx = _inproj(*hx_in, w_in, splits)
    qc, kc, vc, gc = _inproj(*hc_in, w_in, splits)
    t = qx.shape[1]
    cos, sin = _rope_tables(t)
    qscale = DIFF_HD ** -0.5 * math.log2(math.e)
    q_rot = _rope(qx, cos, sin, qscale)
    k_rot = _rope(kx, cos, sin, 1.0)
    k_all = jnp.concatenate([kc.astype(BF16), k_rot], axis=1)
    v_all = jnp.concatenate([vc.astype(BF16), vx.astype(BF16)], axis=1)
    lam = (jnp.exp(jnp.sum(prm["lam_q1"] * prm["lam_k1"])) - jnp.exp(jnp.sum(prm["lam_q2"] * prm["lam_k2"]))
           + lam_init).reshape(1).astype(F32)
    att_x = _diff_attention(q_rot, k_all, v_all, lam, prm["subln_g"], lam_init)
    gm_x = _chunk_gmlp(gx, prm["gmlp_ln_g"], prm["gmlp_ln_b"], prm["gmlp_ws"], prm["gmlp_bs"])
    out_c = None
    if ctx_out:
        att_c = _diff_attention((qc * qscale).astype(BF16), kc.astype(BF16), vc.astype(BF16), lam,
                                prm["subln_g"], lam_init)
        gm_c = _chunk_gmlp(gc, prm["gmlp_ln_g"], prm["gmlp_ln_b"], prm["gmlp_ws"], prm["gmlp_bs"])
        out_c = (att_c, gm_c)
    return out_c, (att_x, gm_x)


def kernel(x, c, ctx, c_ctx, w_mod, b_mod, ln_g, ln_b, even_w_in, even_w_out, conv_w, shift_mu, decay_up, decay_0, a_up, a_0, g_up, k_xi, k_alpha, r_bonus, gn_g, gn_b, odd_w_in, odd_w_out, lam_q1, lam_k1, lam_q2, lam_k2, subln_g, gmlp_ln_g, gmlp_ln_b, gmlp_ws, gmlp_bs, w_router, w_e1, w_e3, w_e2):
    bsz, _, d = x.shape
    assert bsz <= SUBLANES - 1
    c8 = jnp.zeros((SUBLANES, d), F32).at[:bsz].set(c).at[bsz].set(c_ctx)
    mod = _modulation(c8, w_mod, b_mod)
    for l in range(DEPTH):
        ctx_out = l < DEPTH - 1
        i = l // 2
        mx = mod[l, :bsz].reshape(bsz, 1, 6, d)
        mc = jnp.broadcast_to(mod[l, bsz].reshape(1, 1, 6, d), (bsz, 1, 6, d))
        part = lambda m, n: m[:, :, n]
        hx_in = (x, part(mx, 1), part(mx, 0))
        hc_in = (ctx, part(mc, 1), part(mc, 0))
        if l % 2 == 0:
            prm = dict(w_in=even_w_in[i].astype(BF16), conv_w=conv_w[i], shift_mu=shift_mu[i],
                       decay_up=decay_up[i], decay_0=decay_0[i], a_up=a_up[i], a_0=a_0[i], g_up=g_up[i],
                       k_xi=k_xi[i], k_alpha=k_alpha[i], r_bonus=r_bonus[i], gn_g=gn_g[i], gn_b=gn_b[i])
            out_c, out_x = _even_layer(hx_in, hc_in, prm, ctx_out)
            w_out = even_w_out[i].astype(BF16)
        else:
            lam_init = 0.8 - 0.6 * math.exp(-0.3 * l)
            prm = dict(w_in=odd_w_in[i].astype(BF16), lam_q1=lam_q1[i], lam_k1=lam_k1[i], lam_q2=lam_q2[i],
                       lam_k2=lam_k2[i], subln_g=subln_g[i], gmlp_ln_g=gmlp_ln_g[i], gmlp_ln_b=gmlp_ln_b[i],
                       gmlp_ws=gmlp_ws[i], gmlp_bs=gmlp_bs[i])
            out_c, out_x = _odd_layer(hx_in, hc_in, prm, lam_init, ctx_out)
            w_out = odd_w_out[i].astype(BF16)
        wr_t = jnp.transpose(w_router[l])
        w1, w3, w2 = _to_bf16(w_e1, l), _to_bf16(w_e3, l), _to_bf16(w_e2, l)
        x = _outproj(out_x[0], out_x[1], w_out, x, part(mx, 2), ln_g[l, 0], ln_b[l, 0])
        x = _moe(x, part(mx, 4), part(mx, 3), part(mx, 5), wr_t, w1, w3, w2, ln_g[l, 1], ln_b[l, 1])
        if ctx_out:
            ctx = _outproj(out_c[0], out_c[1], w_out, ctx, part(mc, 2), ln_g[l, 0], ln_b[l, 0])
            ctx = _moe(ctx, part(mc, 4), part(mc, 3), part(mc, 5), wr_t, w1, w3, w2, ln_g[l, 1], ln_b[l, 1])
    return x
```

```python
import functools
import math

import jax
import jax.numpy as jnp
from jax import lax
from jax.experimental import pallas as pl
from jax.experimental.pallas import tpu as pltpu

F32 = jnp.float32
BF16 = jnp.bfloat16
HI = lax.Precision.HIGHEST

D_MODEL = 1024
DEPTH = 4
GRID_W = 64
D_CONV = 256
RWKV_HEADS = 12
RWKV_HD = 64
D_RWKV = RWKV_HEADS * RWKV_HD
LORA = 64
LORA_G = 128
RWKV_COLS = 3 * D_RWKV + 4 * LORA + LORA_G
DECAY_SCALE = math.exp(-0.5)
GN_EPS = 64e-5
N_PAIRS = RWKV_HEADS // 2
SCAN_CHUNK = 64
DIFF_HEADS = 6
DIFF_HD = 64
DIFF_VD = 2 * DIFF_HD
D_DIFF = DIFF_HEADS * DIFF_VD
AXIS_DIM = DIFF_HD // 2
ROPE_BASE = 10000.0
D_GMLP = 256
GMLP_CHUNK = 128
N_EXPERTS = 16
EC_FACTOR = 2
D_EXPERT = 2048
ALPHA = (2.0 * DEPTH) ** 0.25
LN_EPS = 1e-5
RMS_EPS = 1e-5

LANES = 128
SUBLANES = 8
VMEM_LIMIT = 56 * 1024 * 1024


def _cparams(sem):
    return pltpu.CompilerParams(dimension_semantics=sem, vmem_limit_bytes=VMEM_LIMIT)


def _bdot(a, b):
    return jnp.dot(a.astype(BF16), b.astype(BF16), preferred_element_type=F32)


def _hdot(a, b):
    return jnp.dot(a, b, preferred_element_type=F32, precision=HI)


def _half_mask(shape, dtype=F32):
    lane = lax.broadcasted_iota(jnp.int32, shape, len(shape) - 1)
    return (lane < 64).astype(dtype)


def _seg_ones():
    i = lax.broadcasted_iota(jnp.int32, (LANES, LANES), 0)
    j = lax.broadcasted_iota(jnp.int32, (LANES, LANES), 1)
    return ((i // 64) == (j // 64)).astype(F32)


def _layer_norm_rows(z, g, b, eps):
    mu = jnp.mean(z, axis=-1, keepdims=True)
    zc = z - mu
    var = jnp.mean(zc * zc, axis=-1, keepdims=True)
    return zc * lax.rsqrt(var + eps) * g + b


def _mod_kernel(c_ref, w_ref, b_ref, o_ref):
    a = c_ref[...]
    a = a * jax.nn.sigmoid(a)
    o_ref[0] = _hdot(a, w_ref[0]) + b_ref[0]


def _modulation(c8, w_mod, b_mod):
    tn = 1536
    nl, d, n6 = w_mod.shape
    return pl.pallas_call(
        _mod_kernel,
        grid=(nl, n6 // tn),
        in_specs=[pl.BlockSpec((8, d), lambda l, j: (0, 0)),
                  pl.BlockSpec((1, d, tn), lambda l, j: (l, 0, j)),
                  pl.BlockSpec((1, 1, tn), lambda l, j: (l, 0, j))],
        out_specs=pl.BlockSpec((1, 8, tn), lambda l, j: (l, 0, j)),
        out_shape=jax.ShapeDtypeStruct((nl, 8, n6), F32),
        compiler_params=_cparams(("parallel", "parallel")),
        name="modulation",
    )(c8, w_mod, b_mod.reshape(nl, 1, n6))


def _inproj_kernel(x_ref, sc_ref, sh_ref, w_ref, *o_refs, splits):
    h = x_ref[0] * (1.0 + sc_ref[0]) + sh_ref[0]
    y = _bdot(h, w_ref[...])
    off = 0
    for o, s in zip(o_refs, splits):
        o[0] = y[:, off:off + s].astype(o.dtype)
        off += s


def _inproj(x, sc, sh, w_bf16, splits, tm=256):
    b, t, d = x.shape
    tm = min(tm, t)
    dout = w_bf16.shape[1]
    return pl.pallas_call(
        functools.partial(_inproj_kernel, splits=splits),
        grid=(b, t // tm),
        in_specs=[pl.BlockSpec((1, tm, d), lambda i, j: (i, j, 0)),
                  pl.BlockSpec((1, 1, d), lambda i, j: (i, 0, 0)),
                  pl.BlockSpec((1, 1, d), lambda i, j: (i, 0, 0)),
                  pl.BlockSpec((d, dout), lambda i, j: (0, 0))],
        out_specs=[pl.BlockSpec((1, tm, s), lambda i, j: (i, j, 0)) for s in splits],
        out_shape=[jax.ShapeDtypeStruct((b, t, s), F32) for s in splits],
        compiler_params=_cparams(("parallel", "parallel")),
        name="inproj",
    )(x, sc, sh, w_bf16)


def _shifted(p, prev_row, next_row):
    tb = p.shape[0]
    row = lax.broadcasted_iota(jnp.int32, p.shape, 0)
    p_prev = jnp.where(row == 0, prev_row, pltpu.roll(p, 1, axis=0))
    p_next = jnp.where(row == tb - 1, next_row, pltpu.roll(p, tb - 1, axis=0))
    return p_prev, p_next


def _edge_rows(pp_ref, pn_ref):
    i = pl.program_id(1)
    nb = pl.num_programs(1)
    prev_row = jnp.where(i > 0, pp_ref[0, SUBLANES - 1:SUBLANES, :], 0.0)
    next_row = jnp.where(i < nb - 1, pn_ref[0, 0:1, :], 0.0)
    return prev_row, next_row


def _feat_kernel(p_ref, pp_ref, pn_ref, mu_ref, dup_ref, d0_ref, aup_ref, a0_ref, gup_ref, kxi_ref, kal_ref,
                 rb_ref, r_o, kk_o, v_o, kr0_o, kr1_o, lw0_o, lw1_o, b0_o, b1_o, g_o, gbv_o):
    p = p_ref[0]
    prev_row, next_row = _edge_rows(pp_ref, pn_ref)
    p_prev, p_next = _shifted(p, prev_row, next_row)
    p = p + mu_ref[...] * (0.5 * (p_prev + p_next) - p)
    dr = D_RWKV
    r = p[:, 0:dr]
    k = p[:, dr:2 * dr]
    v = p[:, 2 * dr:3 * dr]
    c0 = 3 * dr
    dd = jnp.tanh(p[:, c0:c0 + 2 * LORA])
    da = p[:, c0 + 2 * LORA:c0 + 4 * LORA]
    dg = jax.nn.sigmoid(p[:, c0 + 4 * LORA:c0 + 4 * LORA + LORA_G])
    g = _hdot(dg, gup_ref[...])
    lw, a = [], []
    for d in range(2):
        lw.append(-DECAY_SCALE * jax.nn.sigmoid(d0_ref[d:d + 1, :] + _hdot(dd[:, d * LORA:(d + 1) * LORA], dup_ref[d])))
        a.append(jax.nn.sigmoid(a0_ref[d:d + 1, :] + _hdot(da[:, d * LORA:(d + 1) * LORA], aup_ref[d])))
    kx = k * kxi_ref[...]
    kal = kal_ref[...]
    kr = [k * (1.0 + (a[d] - 1.0) * kal) for d in range(2)]
    bon = r * (0.5 * (kr[0] + kr[1])) * rb_ref[...]
    ones = _seg_ones()
    g_o[0] = g
    for j in range(N_PAIRS):
        sl = slice(j * LANES, (j + 1) * LANES)
        kxj = kx[:, sl]
        kk = kxj * lax.rsqrt(_hdot(kxj * kxj, ones) + 1e-12)
        r_o[0, j] = r[:, sl]
        kk_o[0, j] = kk
        v_o[0, j] = v[:, sl]
        kr0_o[0, j] = kr[0][:, sl]
        kr1_o[0, j] = kr[1][:, sl]
        lw0_o[0, j] = lw[0][:, sl]
        lw1_o[0, j] = lw[1][:, sl]
        b0_o[0, j] = kk * a[0][:, sl]
        b1_o[0, j] = kk * a[1][:, sl]
        gbv_o[0, :, sl] = g[:, sl] * _hdot(bon[:, sl], ones) * v[:, sl]


def _rwkv_features(p, prm, tb=256):
    b, t, cols = p.shape
    tb = min(tb, t)
    nb8 = t // SUBLANES
    r8 = tb // SUBLANES
    full = lambda a: pl.BlockSpec(a.shape, lambda i, j: (0,) * a.ndim)
    params = [prm["shift_mu"].reshape(1, cols), prm["decay_up"], prm["decay_0"], prm["a_up"], prm["a_0"],
              prm["g_up"], prm["k_xi"].reshape(1, -1), prm["k_alpha"].reshape(1, -1), prm["r_bonus"].reshape(1, -1)]
    packed = jax.ShapeDtypeStruct((b, N_PAIRS, t, LANES), F32)
    flat = jax.ShapeDtypeStruct((b, t, D_RWKV), F32)
    pspec = pl.BlockSpec((1, N_PAIRS, tb, LANES), lambda i, j: (i, 0, j, 0))
    fspec = pl.BlockSpec((1, tb, D_RWKV), lambda i, j: (i, j, 0))
    return pl.pallas_call(
        _feat_kernel,
        grid=(b, t // tb),
        in_specs=[pl.BlockSpec((1, tb, cols), lambda i, j: (i, j, 0)),
                  pl.BlockSpec((1, SUBLANES, cols), lambda i, j: (i, jnp.maximum(j * r8 - 1, 0), 0)),
                  pl.BlockSpec((1, SUBLANES, cols), lambda i, j: (i, jnp.minimum((j + 1) * r8, nb8 - 1), 0))]
                 + [full(a) for a in params],
        out_specs=[pspec] * 9 + [fspec] * 2,
        out_shape=[packed] * 9 + [flat] * 2,
        compiler_params=_cparams(("parallel", "parallel")),
        name="rwkv_features",
    )(p, p, p, *params)


def _scan_chunks(feats, rev):
    n = SCAN_CHUNK
    m0 = _half_mask((n, LANES))
    m1 = 1.0 - m0
    ti = lax.broadcasted_iota(jnp.int32, (n, n), 0)
    tj = lax.broadcasted_iota(jnp.int32, (n, n), 1)
    tri = ((ti <= tj) if rev else (ti >= tj)).astype(F32)
    si = lax.broadcasted_iota(jnp.int32, (2 * n, 2 * n), 0)
    sj = lax.broadcasted_iota(jnp.int32, (2 * n, 2 * n), 1)
    same = (si // n) == (sj // n)
    ri, rj = si % n, sj % n
    strict = same & ((ri < rj) if rev else (ri > rj))
    incl = same & ((ri <= rj) if rev else (ri >= rj))
    eye = (si == sj).astype(F32)

    def stack(x):
        return jnp.concatenate([x * m0, x * m1], axis=0).astype(BF16)

    pre = []
    for r, lw, kap, bb, kr, v in feats:
        cs = _hdot(tri, lw)
        tot = jnp.sum(lw, axis=0, keepdims=True)
        e_neg = jnp.exp(-cs)
        e_tot = jnp.exp(tot - cs)
        rh = r * jnp.exp(cs)
        pre.append(dict(kh=stack(kap * jnp.exp(cs - lw)), bh=stack(bb * e_neg), kq=stack(kr * e_neg), rh=rh,
                        rhs=stack(rh), kt=stack(kr * e_tot), bt=stack(bb * e_tot), v=stack(v), gl=jnp.exp(tot)))
    nt = (((1,), (1,)), ((), ()))
    amats = [lax.dot_general(jnp.concatenate([p["kh"], p["rhs"]], axis=0), jnp.concatenate([p["bh"], p["kq"]], axis=0),
                             nt, preferred_element_type=F32) for p in pre]
    a1 = [jnp.where(strict, a[:2 * n, :2 * n], 0.0) for a in amats]
    a2 = [jnp.where(strict, a[:2 * n, 2 * n:], 0.0).astype(BF16) for a in amats]
    a4 = [jnp.where(incl, a[2 * n:, :2 * n], 0.0).astype(BF16) for a in amats]
    a3 = [jnp.where(incl, a[2 * n:, 2 * n:], 0.0).astype(BF16) for a in amats]
    tinv = [eye - a for a in a1]
    pw = a1
    for _ in range(5):
        pw = [_bdot(p, p) for p in pw]
        tinv = [t + _bdot(t, p) for t, p in zip(tinv, pw)]
    w_s = [_bdot(a, p["v"]) for a, p in zip(a2, pre)]
    mm = [_bdot(t, jnp.concatenate([p["kh"], w.astype(BF16)], axis=1)).astype(BF16)
          for t, p, w in zip(tinv, pre, w_s)]
    gmat = [_bdot(p["bt"].astype(F32).T, m) for p, m in zip(pre, mm)]
    phi_t = [eye * p["gl"] - g[:, :LANES] for p, g in zip(pre, gmat)]
    psi_t = [_bdot(p["kt"].astype(F32).T, p["v"]) - g[:, LANES:] for p, g in zip(pre, gmat)]
    qy = [_bdot(a, m) for a, m in zip(a4, mm)]
    y0_s = [_bdot(a, p["v"]) - q[:, LANES:] for a, p, q in zip(a3, pre, qy)]
    return [(p["rh"] - (q[:n, :LANES] + q[n:, :LANES]), y0[:n] + y0[n:], ph, ps)
            for p, q, y0, ph, ps in zip(pre, qy, y0_s, phi_t, psi_t)]


def _scan_kernel(r_ref, lw_ref, kap_ref, b_ref, kr_ref, v_ref, s0_ref, y_ref, sT_ref, st_scr, *, rev):
    c = pl.program_id(1)

    @pl.when(c == 0)
    def _():
        st_scr[...] = s0_ref[0]

    n = SCAN_CHUNK
    nsub = r_ref.shape[2] // n
    order = range(nsub - 1, -1, -1) if rev else range(nsub)
    rows = lambda ref, hp, k: ref[0, hp, k * n:(k + 1) * n, :]
    items = [(k, hp) for k in order for hp in range(N_PAIRS)]
    terms = _scan_chunks([tuple(rows(ref, hp, k) for ref in (r_ref, lw_ref, kap_ref, b_ref, kr_ref, v_ref))
                          for k, hp in items], rev)
    states = [st_scr[hp] for hp in range(N_PAIRS)]
    for (k, hp), (qm, y0, phi_t, psi_t) in zip(items, terms):
        y_ref[0, hp, k * n:(k + 1) * n, :] = _bdot(qm, states[hp]) + y0
        states[hp] = _bdot(phi_t, states[hp]) + psi_t
    for hp in range(N_PAIRS):
        st_scr[hp] = states[hp]

    @pl.when(c == pl.num_programs(1) - 1)
    def _():
        sT_ref[0] = st_scr[...]


def _rwkv_scan(r, lw, kap, bb, kr, v, s0, rev, chunks_per_step=4):
    b, _, t, _ = r.shape
    n = SCAN_CHUNK * math.gcd(chunks_per_step, t // SCAN_CHUNK)
    nc = t // n
    cidx = (lambda c: nc - 1 - c) if rev else (lambda c: c)
    fspec = pl.BlockSpec((1, N_PAIRS, n, LANES), lambda i, c: (i, 0, cidx(c), 0))
    sspec = pl.BlockSpec((1, N_PAIRS, LANES, LANES), lambda i, c: (i, 0, 0, 0))
    return pl.pallas_call(
        functools.partial(_scan_kernel, rev=rev),
        grid=(b, nc),
        in_specs=[fspec] * 6 + [sspec],
        out_specs=[fspec, sspec],
        out_shape=[jax.ShapeDtypeStruct(r.shape, F32), jax.ShapeDtypeStruct(s0.shape, F32)],
        scratch_shapes=[pltpu.VMEM((N_PAIRS, LANES, LANES), F32)],
        compiler_params=_cparams(("parallel", "arbitrary")),
        name="rwkv_scan_rev" if rev else "rwkv_scan_fwd",
    )(r, lw, kap, bb, kr, v, s0)


def _evenpost_kernel(y0_ref, y1_ref, g_ref, gbv_ref, pc_ref, pcp_ref, pcn_ref, cw_ref, gng_ref, gnb_ref,
                     conv_o, rw_o):
    pc = pc_ref[0]
    prev_row, next_row = _edge_rows(pcp_ref, pcn_ref)
    dc = D_CONV

    def gated(z):
        return z[:, dc:2 * dc] * z[:, 2 * dc:3 * dc]

    u = gated(pc)
    u_prev, u_next = _shifted(u, gated(prev_row), gated(next_row))
    cw = cw_ref[...]
    conv_o[0] = (pc[:, 0:dc] * (cw[0:1] * u_prev + cw[1:2] * u + cw[2:3] * u_next)).astype(conv_o.dtype)
    ones = _seg_ones() * (1.0 / RWKV_HD)
    for j in range(N_PAIRS):
        sl = slice(j * LANES, (j + 1) * LANES)
        y = y0_ref[0, j] + y1_ref[0, j]
        mu = _hdot(y, ones)
        yc = y - mu
        var = _hdot(yc * yc, ones)
        yn = yc * lax.rsqrt(var + GN_EPS) * gng_ref[:, sl] + gnb_ref[:, sl]
        rw_o[0, :, sl] = (g_ref[0, :, sl] * yn + gbv_ref[0, :, sl]).astype(rw_o.dtype)


def _even_post(y0, y1, g, gbv, pc, conv_w, gn_g, gn_b, tb=256):
    b, _, t, _ = y0.shape
    tb = min(tb, t)
    nb8 = t // SUBLANES
    r8 = tb // SUBLANES
    c3 = pc.shape[-1]
    pspec = pl.BlockSpec((1, N_PAIRS, tb, LANES), lambda i, j: (i, 0, j, 0))
    fspec = pl.BlockSpec((1, tb, D_RWKV), lambda i, j: (i, j, 0))
    full = lambda a: pl.BlockSpec(a.shape, lambda i, j: (0,) * a.ndim)
    gn_g = gn_g.reshape(1, -1)
    gn_b = gn_b.reshape(1, -1)
    return pl.pallas_call(
        _evenpost_kernel,
        grid=(b, t // tb),
        in_specs=[pspec, pspec, fspec, fspec,
                  pl.BlockSpec((1, tb, c3), lambda i, j: (i, j, 0)),
                  pl.BlockSpec((1, SUBLANES, c3), lambda i, j: (i, jnp.maximum(j * r8 - 1, 0), 0)),
                  pl.BlockSpec((1, SUBLANES, c3), lambda i, j: (i, jnp.minimum((j + 1) * r8, nb8 - 1), 0)),
                  full(conv_w), full(gn_g), full(gn_b)],
        out_specs=[pl.BlockSpec((1, tb, D_CONV), lambda i, j: (i, j, 0)), fspec],
        out_shape=[jax.ShapeDtypeStruct((b, t, D_CONV), BF16), jax.ShapeDtypeStruct((b, t, D_RWKV), BF16)],
        compiler_params=_cparams(("parallel", "parallel")),
        name="even_post",
    )(y0, y1, g, gbv, pc, pc, pc, conv_w, gn_g, gn_b)


def _outproj_kernel(a_ref, b_ref, wa_ref, wb_ref, x_ref, gm_ref, lg_ref, lb_ref, o_ref):
    yx = _bdot(a_ref[0], wa_ref[...]) + _bdot(b_ref[0], wb_ref[...])
    z = ALPHA * x_ref[0] + gm_ref[0] * yx
    o_ref[0] = _layer_norm_rows(z, lg_ref[...], lb_ref[...], LN_EPS)


def _outproj(ma, mb, w_out_bf16, x, gm, ln_g, ln_b, tm=512):
    b, t, d = x.shape
    tm = min(tm, t)
    da, db = ma.shape[-1], mb.shape[-1]
    wa, wb = w_out_bf16[:da], w_out_bf16[da:]
    row = lambda w: pl.BlockSpec((1, tm, w), lambda i, j: (i, j, 0))
    full = lambda a: pl.BlockSpec(a.shape, lambda i, j: (0,) * a.ndim)
    ln_g = ln_g.reshape(1, d)
    ln_b = ln_b.reshape(1, d)
    return pl.pallas_call(
        _outproj_kernel,
        grid=(b, t // tm),
        in_specs=[row(da), row(db), full(wa), full(wb), row(d),
                  pl.BlockSpec((1, 1, d), lambda i, j: (i, 0, 0)), full(ln_g), full(ln_b)],
        out_specs=row(d),
        out_shape=jax.ShapeDtypeStruct((b, t, d), F32),
        compiler_params=_cparams(("parallel", "parallel")),
        name="outproj_norm",
    )(ma, mb, wa, wb, x, gm, ln_g, ln_b)


def _rope_kernel(x_ref, cos_ref, sin_ref, o_ref, *, scale):
    cos = cos_ref[...]
    sin = sin_ref[...]
    lane = lax.broadcasted_iota(jnp.int32, cos.shape, 1)
    first = (lane % (2 * (AXIS_DIM // 2))) < (AXIS_DIM // 2)
    half = AXIS_DIM // 2
    for j in range(DIFF_HEADS):
        sl = slice(j * LANES, (j + 1) * LANES)
        xs = x_ref[0, :, sl]
        partner = jnp.where(first, -pltpu.roll(xs, LANES - half, axis=1), pltpu.roll(xs, half, axis=1))
        o_ref[0, :, sl] = ((xs * cos + partner * sin) * scale).astype(o_ref.dtype)


def _rope(x, cos, sin, scale, tb=512):
    b, t, w = x.shape
    tb = min(tb, t)
    return pl.pallas_call(
        functools.partial(_rope_kernel, scale=scale),
        grid=(b, t // tb),
        in_specs=[pl.BlockSpec((1, tb, w), lambda i, j: (i, j, 0)),
                  pl.BlockSpec((tb, LANES), lambda i, j: (j, 0)),
                  pl.BlockSpec((tb, LANES), lambda i, j: (j, 0))],
        out_specs=pl.BlockSpec((1, tb, w), lambda i, j: (i, j, 0)),
        out_shape=jax.ShapeDtypeStruct((b, t, w), BF16),
        compiler_params=_cparams(("parallel", "parallel")),
        name="axial_rope",
    )(x, cos, sin)


def _rope_tables(t):
    rows = t // GRID_W
    row = jnp.repeat(jnp.arange(rows), GRID_W).astype(F32)
    col = jnp.tile(jnp.arange(GRID_W), rows).astype(F32)
    inv = ROPE_BASE ** (-jnp.arange(0, AXIS_DIM, 2, dtype=F32) / AXIS_DIM)
    ang_r = row[:, None] * inv
    ang_c = col[:, None] * inv
    ang = jnp.concatenate([ang_r, ang_r, ang_c, ang_c], axis=-1)
    ang = jnp.concatenate([ang, ang], axis=-1)
    return jnp.cos(ang), jnp.sin(ang)


def _attn_kernel(lam_ref, q_ref, k_ref, v_ref, g_ref, o_ref, *, tk, out_scale):
    q = q_ref[0]
    tq = q.shape[0]
    hm = _half_mask(q.shape, BF16)
    q1 = q * hm
    qs = (q1, q - q1)
    nk = k_ref.shape[1] // tk
    nsl = tk // LANES

    def scores(qh, j):
        kb = k_ref[0, pl.ds(pl.multiple_of(j * tk, tk), tk), :]
        return lax.dot_general(qh, kb, (((1,), (1,)), ((), ())), preferred_element_type=F32)

    def col(s, c):
        return s[:, c * LANES:(c + 1) * LANES]

    def body(j, carry):
        vb = v_ref[0, pl.ds(pl.multiple_of(j * tk, tk), tk), :]
        ss = [scores(qh, j) for qh in qs]
        out = []
        for s, (m, ls, acc) in zip(ss, (carry[:3], carry[3:])):
            mx = col(s, 0)
            for c in range(1, nsl):
                mx = jnp.maximum(mx, col(s, c))
            m_new = jnp.maximum(m, jnp.broadcast_to(jnp.max(mx, axis=-1, keepdims=True), (tq, LANES)))
            corr = jnp.exp2(m - m_new)
            ps = [jnp.exp2(col(s, c) - m_new) for c in range(nsl)]
            ls = corr * ls
            for pc in ps:
                ls = ls + pc
            p = jnp.concatenate([pc.astype(BF16) for pc in ps], axis=1)
            acc = corr * acc + jnp.dot(p, vb, preferred_element_type=F32)
            out += [m_new, ls, acc]
        return tuple(out)

    neg = jnp.full((tq, LANES), -jnp.inf, F32)
    zero = jnp.zeros((tq, LANES), F32)
    _, ls1, acc1, _, ls2, acc2 = lax.fori_loop(0, nk, body, (neg, zero, zero, neg, zero, zero), unroll=True)
    l1 = jnp.sum(ls1, axis=-1, keepdims=True)
    l2 = jnp.sum(ls2, axis=-1, keepdims=True)
    o = acc1 / l1 - lam_ref[0] * (acc2 / l2)
    o = o * lax.rsqrt(jnp.mean(o * o, axis=-1, keepdims=True) + RMS_EPS) * g_ref[...] * out_scale
    o_ref[0] = o.astype(o_ref.dtype)


def _diff_attention(q, k, v, lam, subln_g, lam_init, tq=512, tk=1408):
    b, t, w = q.shape
    tkk = k.shape[1]
    tq = min(tq, t)
    tk = math.gcd(tk, tkk)
    assert tk % LANES == 0 and t % tq == 0
    subln_g = subln_g.reshape(1, DIFF_VD)
    return pl.pallas_call(
        functools.partial(_attn_kernel, tk=tk, out_scale=1.0 - lam_init),
        grid=(b, DIFF_HEADS, t // tq),
        in_specs=[pl.BlockSpec(memory_space=pltpu.SMEM),
                  pl.BlockSpec((1, tq, LANES), lambda i, h, j: (i, j, h)),
                  pl.BlockSpec((1, tkk, LANES), lambda i, h, j: (i, 0, h)),
                  pl.BlockSpec((1, tkk, LANES), lambda i, h, j: (i, 0, h)),
                  pl.BlockSpec((1, DIFF_VD), lambda i, h, j: (0, 0))],
        out_specs=pl.BlockSpec((1, tq, LANES), lambda i, h, j: (i, j, h)),
        out_shape=jax.ShapeDtypeStruct((b, t, w), BF16),
        compiler_params=_cparams(("parallel", "parallel", "parallel")),
        name="diff_attention",
    )(lam, q, k, v, subln_g)


def _gmlp_kernel(p_ref, lg_ref, lb_ref, ws_ref, bs_ref, o_ref):
    p = p_ref[0]
    tb = p.shape[0]
    ge = 0.5 * p * (1.0 + lax.erf(p * (2.0 ** -0.5)))
    u = ge[:, :D_GMLP]
    v = _layer_norm_rows(ge[:, D_GMLP:], lg_ref[...], lb_ref[...], LN_EPS)
    hm = _half_mask((GMLP_CHUNK, LANES))
    for c in range(tb // GMLP_CHUNK):
        rs = slice(c * GMLP_CHUNK, (c + 1) * GMLP_CHUNK)
        for j in range(D_GMLP // LANES):
            sl = slice(j * LANES, (j + 1) * LANES)
            vc = v[rs, sl]
            va = vc * hm
            mixed = _bdot(ws_ref[2 * j], va) + _bdot(ws_ref[2 * j + 1], vc - va) + bs_ref[:, sl]
            o_ref[0, rs, sl] = (u[rs, sl] * mixed).astype(o_ref.dtype)


def _chunk_gmlp(p, ln_g, ln_b, ws, bs, tb=256):
    b, t, w = p.shape
    tb = min(tb, t)
    ln_g = ln_g.reshape(1, -1)
    ln_b = ln_b.reshape(1, -1)
    bs_t = jnp.repeat(jnp.transpose(bs), D_GMLP // ws.shape[0], axis=1)
    full = lambda a: pl.BlockSpec(a.shape, lambda i, j: (0,) * a.ndim)
    return pl.pallas_call(
        _gmlp_kernel,
        grid=(b, t // tb),
        in_specs=[pl.BlockSpec((1, tb, w), lambda i, j: (i, j, 0)), full(ln_g), full(ln_b), full(ws), full(bs_t)],
        out_specs=pl.BlockSpec((1, tb, D_GMLP), lambda i, j: (i, j, 0)),
        out_shape=jax.ShapeDtypeStruct((b, t, D_GMLP), BF16),
        compiler_params=_cparams(("parallel", "parallel")),
        name="chunk_gmlp",
    )(p, ln_g, ln_b, ws, bs_t)


def _router_kernel(x_ref, sc_ref, sh_ref, wr_ref, h_o, aff_o):
    h = x_ref[0] * (1.0 + sc_ref[0]) + sh_ref[0]
    h_o[0] = h.astype(h_o.dtype)
    logits = lax.dot_general(wr_ref[...], h, (((1,), (1,)), ((), ())), preferred_element_type=F32, precision=HI)
    m = jnp.max(logits, axis=0, keepdims=True)
    e = jnp.exp(logits - m)
    aff_o[0] = e / jnp.sum(e, axis=0, keepdims=True)


def _router(x, sc, sh, w_router_t, tm=512):
    b, t, d = x.shape
    tm = min(tm, t)
    e = w_router_t.shape[0]
    return pl.pallas_call(
        _router_kernel,
        grid=(b, t // tm),
        in_specs=[pl.BlockSpec((1, tm, d), lambda i, j: (i, j, 0)),
                  pl.BlockSpec((1, 1, d), lambda i, j: (i, 0, 0)),
                  pl.BlockSpec((1, 1, d), lambda i, j: (i, 0, 0)),
                  pl.BlockSpec((e, d), lambda i, j: (0, 0))],
        out_specs=[pl.BlockSpec((1, tm, d), lambda i, j: (i, j, 0)),
                   pl.BlockSpec((1, e, tm), lambda i, j: (i, 0, j))],
        out_shape=[jax.ShapeDtypeStruct((b, t, d), BF16), jax.ShapeDtypeStruct((b, e, t), F32)],
        compiler_params=_cparams(("parallel", "parallel")),
        name="router",
    )(x, sc, sh, w_router_t)


ROUTE_BLOCK = 256
STARTS_PAD = LANES


def _lane_cumsum(mask):
    e, t = mask.shape
    i = lax.broadcasted_iota(jnp.int32, (LANES, LANES), 0)
    j = lax.broadcasted_iota(jnp.int32, (LANES, LANES), 1)
    upper = (i <= j).astype(BF16)
    x = jnp.where(mask, 1.0, 0.0).astype(BF16)
    off = jnp.zeros((e, 1), F32)
    out = []
    for c in range(t // LANES):
        blk = jnp.dot(x[:, c * LANES:(c + 1) * LANES], upper, preferred_element_type=F32) + off
        out.append(blk)
        off = blk[:, LANES - 1:LANES]
    return jnp.concatenate(out, axis=1)


def _select_kernel(aff_ref, pos_ref, starts_ref, *, cap):
    a = aff_ref[0]
    e, t = a.shape
    bits = pltpu.bitcast(a, jnp.int32)

    def search(i, thr):
        cand = thr | jnp.left_shift(jnp.int32(1), 30 - i)
        cnt = jnp.sum(jnp.where(bits >= cand, 1.0, 0.0), axis=1, keepdims=True)
        return jnp.where(cnt >= cap, cand, thr)

    thr = lax.fori_loop(0, 31, search, jnp.zeros((e, 1), jnp.int32))
    gt = bits > thr
    eq = bits == thr
    need = cap - jnp.sum(jnp.where(gt, 1.0, 0.0), axis=1, keepdims=True)
    sel = gt | (eq & (_lane_cumsum(eq) <= need))
    csel = _lane_cumsum(sel)
    pos_ref[0] = jnp.where(sel, csel - 1.0, -1.0).astype(jnp.int32)
    ti = lax.broadcasted_iota(jnp.int32, (t, STARTS_PAD), 0)
    ki = lax.broadcasted_iota(jnp.int32, (t, STARTS_PAD), 1)
    pick = jnp.where(ti == ki * ROUTE_BLOCK - 1, 1.0, 0.0)
    starts_ref[0] = (_hdot(csel, pick) + 0.5).astype(jnp.int32)


def _select(aff, cap):
    b, e, t = aff.shape
    assert t % ROUTE_BLOCK == 0 and t // ROUTE_BLOCK <= STARTS_PAD
    return pl.pallas_call(
        functools.partial(_select_kernel, cap=cap),
        grid=(b,),
        in_specs=[pl.BlockSpec((1, e, t), lambda i: (i, 0, 0))],
        out_specs=[pl.BlockSpec((1, e, t), lambda i: (i, 0, 0)), pl.BlockSpec((1, e, STARTS_PAD), lambda i: (i, 0, 0))],
        out_shape=[jax.ShapeDtypeStruct((b, e, t), jnp.int32), jax.ShapeDtypeStruct((b, e, STARTS_PAD), jnp.int32)],
        compiler_params=_cparams(("parallel",)),
        name="expert_select",
    )(aff)


def _window_start(start, align, cap, width):
    s = lax.shift_left(lax.shift_right_logical(start, align.bit_length() - 1), align.bit_length() - 1)
    return pl.multiple_of(jnp.minimum(s, cap - width), align)


def _gather_ffn_kernel(starts_ref, pos_ref, h_ref, w1_ref, w3_ref, w2_ref, y_ref, xs_scr, *, width, rows):
    ei, bi, k = pl.program_id(0), pl.program_id(1), pl.program_id(2)
    ne = pl.num_programs(0)
    cap = xs_scr.shape[0]
    nsub = h_ref.shape[1] // ROUTE_BLOCK

    @pl.when(k == 0)
    def _():
        xs_scr[...] = jnp.zeros_like(xs_scr)

    jrow = lax.broadcasted_iota(jnp.int32, (width, ROUTE_BLOCK), 0)
    for sb in range(nsub):
        start = starts_ref[(bi * ne + ei) * STARTS_PAD + k * nsub + sb]
        s8 = _window_start(start, SUBLANES, cap, width)
        prow = pos_ref[0, :, sb * ROUTE_BLOCK:(sb + 1) * ROUTE_BLOCK]
        onehot = jnp.where(prow - s8 == jrow, 1.0, 0.0).astype(BF16)
        xs_scr[pl.ds(s8, width), :] += jnp.dot(onehot, h_ref[0, sb * ROUTE_BLOCK:(sb + 1) * ROUTE_BLOCK, :],
                                               preferred_element_type=F32)

    @pl.when(k == pl.num_programs(2) - 1)
    def _():
        for r in range(cap // rows):
            x = xs_scr[r * rows:(r + 1) * rows, :].astype(BF16)
            h1 = jnp.dot(x, w1_ref[0], preferred_element_type=F32)
            h3 = jnp.dot(x, w3_ref[0], preferred_element_type=F32)
            hid = (h1 * jax.nn.sigmoid(h1)) * h3
            y = jnp.dot(hid.astype(BF16), w2_ref[0], preferred_element_type=F32)
            y_ref[0, 0, r * rows:(r + 1) * rows, :] = y.astype(y_ref.dtype)


def _gather_ffn(h, pos, starts, w1, w3, w2, cap, tch=2048, rows=256):
    b, t, d = h.shape
    e, _, f = w1.shape
    tch = min(tch, t)
    rows = min(rows, cap)
    width = min(ROUTE_BLOCK + SUBLANES, cap)
    grid_spec = pltpu.PrefetchScalarGridSpec(
        num_scalar_prefetch=1,
        grid=(e, b, t // tch),
        in_specs=[pl.BlockSpec((1, 1, tch), lambda ei, bi, k, s: (bi * e + ei, 0, k)),
                  pl.BlockSpec((1, tch, d), lambda ei, bi, k, s: (bi, k, 0)),
                  pl.BlockSpec((1, d, f), lambda ei, bi, k, s: (ei, 0, 0)),
                  pl.BlockSpec((1, d, f), lambda ei, bi, k, s: (ei, 0, 0)),
                  pl.BlockSpec((1, f, d), lambda ei, bi, k, s: (ei, 0, 0))],
        out_specs=pl.BlockSpec((1, 1, cap, d), lambda ei, bi, k, s: (bi, ei, 0, 0)),
        scratch_shapes=[pltpu.VMEM((cap, d), F32)],
    )
    return pl.pallas_call(
        functools.partial(_gather_ffn_kernel, width=width, rows=rows),
        grid_spec=grid_spec,
        out_shape=jax.ShapeDtypeStruct((b, e, cap, d), BF16),
        compiler_params=_cparams(("parallel", "parallel", "arbitrary")),
        name="expert_gather_ffn",
    )(starts.reshape(-1), pos.reshape(b * e, 1, t), h, w1, w3, w2)


def _combine_kernel(starts_ref, post_ref, afft_ref, y_ref, x_ref, gf_ref, lg_ref, lb_ref, o_ref, acc_scr, *, width):
    bi, i, ei = pl.program_id(0), pl.program_id(1), pl.program_id(2)
    ne = pl.num_programs(2)
    cap = y_ref.shape[2]
    nsub = x_ref.shape[1] // ROUTE_BLOCK

    @pl.when(ei == 0)
    def _():
        acc_scr[...] = jnp.zeros_like(acc_scr)

    elane = lax.broadcasted_iota(jnp.int32, (ROUTE_BLOCK, ne), 1) == ei
    jcol = lax.broadcasted_iota(jnp.int32, (ROUTE_BLOCK, width), 1)
    for sb in range(nsub):
        rs = slice(sb * ROUTE_BLOCK, (sb + 1) * ROUTE_BLOCK)
        start = starts_ref[(bi * ne + ei) * STARTS_PAD + i * nsub + sb]
        s16 = _window_start(start, 2 * SUBLANES, cap, width)
        pcol = jnp.sum(jnp.where(elane, post_ref[0, rs, :].astype(F32), 0.0), axis=1, keepdims=True)
        gcol = jnp.sum(jnp.where(elane, afft_ref[0, rs, :], 0.0), axis=1, keepdims=True)
        onehot = jnp.where(pcol.astype(jnp.int32) - s16 == jcol, 1.0, 0.0).astype(BF16)
        acc_scr[rs, :] += gcol * jnp.dot(onehot, y_ref[0, 0, pl.ds(s16, width), :], preferred_element_type=F32)

    @pl.when(ei == ne - 1)
    def _():
        z = ALPHA * x_ref[0] + gf_ref[0] * acc_scr[...]
        o_ref[0] = _layer_norm_rows(z, lg_ref[...], lb_ref[...], LN_EPS)


def _combine(x, y, pos_t, aff_t, starts, gf, ln_g, ln_b, tbo=1024):
    b, t, d = x.shape
    _, e, cap, _ = y.shape
    tbo = min(tbo, t)
    width = min(ROUTE_BLOCK + 2 * SUBLANES, cap)
    grid_spec = pltpu.PrefetchScalarGridSpec(
        num_scalar_prefetch=1,
        grid=(b, t // tbo, e),
        in_specs=[pl.BlockSpec((1, tbo, e), lambda bi, i, ei, s: (bi, i, 0)),
                  pl.BlockSpec((1, tbo, e), lambda bi, i, ei, s: (bi, i, 0)),
                  pl.BlockSpec((1, 1, cap, d), lambda bi, i, ei, s: (bi, ei, 0, 0)),
                  pl.BlockSpec((1, tbo, d), lambda bi, i, ei, s: (bi, i, 0)),
                  pl.BlockSpec((1, 1, d), lambda bi, i, ei, s: (bi, 0, 0)),
                  pl.BlockSpec((1, d), lambda bi, i, ei, s: (0, 0)),
                  pl.BlockSpec((1, d), lambda bi, i, ei, s: (0, 0))],
        out_specs=pl.BlockSpec((1, tbo, d), lambda bi, i, ei, s: (bi, i, 0)),
        scratch_shapes=[pltpu.VMEM((tbo, d), F32)],
    )
    return pl.pallas_call(
        functools.partial(_combine_kernel, width=width),
        grid_spec=grid_spec,
        out_shape=jax.ShapeDtypeStruct((b, t, d), F32),
        compiler_params=_cparams(("parallel", "parallel", "arbitrary")),
        name="expert_combine_norm",
    )(starts.reshape(-1), pos_t, aff_t, y, x, gf, ln_g.reshape(1, d), ln_b.reshape(1, d))


def _cast_kernel(x_ref, o_ref):
    o_ref[...] = x_ref[...].astype(o_ref.dtype)


def _to_bf16(w, l):
    _, e, r, c = w.shape
    return pl.pallas_call(
        _cast_kernel,
        grid=(e,),
        in_specs=[pl.BlockSpec((None, 1, r, c), lambda i: (l, i, 0, 0))],
        out_specs=pl.BlockSpec((1, r, c), lambda i: (i, 0, 0)),
        out_shape=jax.ShapeDtypeStruct((e, r, c), BF16),
        compiler_params=_cparams(("parallel",)),
        name="cast_bf16",
    )(w)


def _moe(x, sc, sh, gf, w_router_t, w1, w3, w2, ln_g, ln_b):
    t = x.shape[1]
    cap = EC_FACTOR * t // N_EXPERTS
    h, aff = _router(x, sc, sh, w_router_t)
    pos, starts = _select(aff, cap)
    y = _gather_ffn(h, pos, starts, w1, w3, w2, cap)
    to_token_major = lambda z: jnp.transpose(z, (0, 2, 1))
    return _combine(x, y, to_token_major(pos), to_token_major(aff), starts, gf, ln_g, ln_b)


def _even_layer(hx_in, hc_in, prm, ctx_out):
    w_in = prm["w_in"]
    splits = (3 * D_CONV, RWKV_COLS)
    pcx, prx = _inproj(*hx_in, w_in, splits)
    pcc, prc = _inproj(*hc_in, w_in, splits)
    fx = _rwkv_features(prx, prm)
    fc = _rwkv_features(prc, prm)
    b = prx.shape[0]
    zero = jnp.zeros((b, N_PAIRS, LANES, LANES), F32)
    ys_x, ys_c = [], []
    for d, rev in ((0, False), (1, True)):
        def args(f):
            r, kk, v, kr0, kr1, lw0, lw1, b0, b1 = f[:9]
            return (r, (lw0, lw1)[d], kk, (b0, b1)[d], (kr0, kr1)[d], v)
        yc, s_ctx = _rwkv_scan(*args(fc), zero, rev)
        yx, _ = _rwkv_scan(*args(fx), s_ctx, rev)
        ys_x.append(yx)
        ys_c.append(yc)
    out_x = _even_post(ys_x[0], ys_x[1], fx[9], fx[10], pcx, prm["conv_w"], prm["gn_g"], prm["gn_b"])
    out_c = None
    if ctx_out:
        out_c = _even_post(ys_c[0], ys_c[1], fc[9], fc[10], pcc, prm["conv_w"], prm["gn_g"], prm["gn_b"])
    return out_c, out_x


def _odd_layer(hx_in, hc_in, prm, lam_init, ctx_out):
    w_in = prm["w_in"]
    splits = (D_DIFF, D_DIFF, D_DIFF, 2 * D_GMLP)
    qx, kx, vx, gx = _inproj(*hx_in, w_in, splits)
    qc, kc, vc, gc = _inproj(*hc_in, w_in, splits)
    t = qx.shape[1]
    cos, sin = _rope_tables(t)
    qscale = DIFF_HD ** -0.5 * math.log2(math.e)
    q_rot = _rope(qx, cos, sin, qscale)
    k_rot = _rope(kx, cos, sin, 1.0)
    k_all = jnp.concatenate([kc.astype(BF16), k_rot], axis=1)
    v_all = jnp.concatenate([vc.astype(BF16), vx.astype(BF16)], axis=1)
    lam = (jnp.exp(jnp.sum(prm["lam_q1"] * prm["lam_k1"])) - jnp.exp(jnp.sum(prm["lam_q2"] * prm["lam_k2"]))
           + lam_init).reshape(1).astype(F32)
    att_x = _diff_attention(q_rot, k_all, v_all, lam, prm["subln_g"], lam_init)
    gm_x = _chunk_gmlp(gx, prm["gmlp_ln_g"], prm["gmlp_ln_b"], prm["gmlp_ws"], prm["gmlp_bs"])
    out_c = None
    if ctx_out:
        att_c = _diff_attention((qc * qscale).astype(BF16), kc.astype(BF16), vc.astype(BF16), lam,
                                prm["subln_g"], lam_init)
        gm_c = _chunk_gmlp(gc, prm["gmlp_ln_g"], prm["gmlp_ln_b"], prm["gmlp_ws"], prm["gmlp_bs"])
        out_c = (att_c, gm_c)
    return out_c, (att_x, gm_x)


def kernel(x, c, ctx, c_ctx, w_mod, b_mod, ln_g, ln_b, even_w_in, even_w_out, conv_w, shift_mu, decay_up, decay_0, a_up, a_0, g_up, k_xi, k_alpha, r_bonus, gn_g, gn_b, odd_w_in, odd_w_out, lam_q1, lam_k1, lam_q2, lam_k2, subln_g, gmlp_ln_g, gmlp_ln_b, gmlp_ws, gmlp_bs, w_router, w_e1, w_e3, w_e2):
    bsz, _, d = x.shape
    assert bsz <= SUBLANES - 1
    c8 = jnp.zeros((SUBLANES, d), F32).at[:bsz].set(c).at[bsz].set(c_ctx)
    mod = _modulation(c8, w_mod, b_mod)
    for l in range(DEPTH):
        ctx_out = l < DEPTH - 1
        i = l // 2
        mx = mod[l, :bsz].reshape(bsz, 1, 6, d)
        mc = jnp.broadcast_to(mod[l, bsz].reshape(1, 1, 6, d), (bsz, 1, 6, d))
        part = lambda m, n: m[:, :, n]
        hx_in = (x, part(mx, 1), part(mx, 0))
        hc_in = (ctx, part(mc, 1), part(mc, 0))
        if l % 2 == 0:
            prm = dict(w_in=even_w_in[i].astype(BF16), conv_w=conv_w[i], shift_mu=shift_mu[i],
                       decay_up=decay_up[i], decay_0=decay_0[i], a_up=a_up[i], a_0=a_0[i], g_up=g_up[i],
                       k_xi=k_xi[i], k_alpha=k_alpha[i], r_bonus=r_bonus[i], gn_g=gn_g[i], gn_b=gn_b[i])
            out_c, out_x = _even_layer(hx_in, hc_in, prm, ctx_out)
            w_out = even_w_out[i].astype(BF16)
        else:
            lam_init = 0.8 - 0.6 * math.exp(-0.3 * l)
            prm = dict(w_in=odd_w_in[i].astype(BF16), lam_q1=lam_q1[i], lam_k1=lam_k1[i], lam_q2=lam_q2[i],
                       lam_k2=lam_k2[i], subln_g=subln_g[i], gmlp_ln_g=gmlp_ln_g[i], gmlp_ln_b=gmlp_ln_b[i],
                       gmlp_ws=gmlp_ws[i], gmlp_bs=gmlp_bs[i])
            out_c, out_x = _odd_layer(hx_in, hc_in, prm, lam_init, ctx_out)
            w_out = odd_w_out[i].astype(BF16)
        wr_t = jnp.transpose(w_router[l])
        w1, w3, w2 = _to_bf16(w_e1, l), _to_bf16(w_e3, l), _to_bf16(w_e2, l)
        x = _outproj(out_x[0], out_x[1], w_out, x, part(mx, 2), ln_g[l, 0], ln_b[l, 0])
        x = _moe(x, part(mx, 4), part(mx, 3), part(mx, 5), wr_t, w1, w3, w2, ln_g[l, 1], ln_b[l, 1])
        if ctx_out:
            ctx = _outproj(out_c[0], out_c[1], w_out, ctx, part(mc, 2), ln_g[l, 0], ln_b[l, 0])
            ctx = _moe(ctx, part(mc, 4), part(mc, 3), part(mc, 5), wr_t, w1, w3, w2, ln_g[l, 1], ln_b[l, 1])
    return x
```

```python
import functools
import math

import jax
import jax.numpy as jnp
from jax import lax
from jax.experimental import pallas as pl
from jax.experimental.pallas import tpu as pltpu

F32 = jnp.float32
BF16 = jnp.bfloat16
HI = lax.Precision.HIGHEST

D_MODEL = 1024
DEPTH = 4
GRID_W = 64
D_CONV = 256
RWKV_HEADS = 12
RWKV_HD = 64
D_RWKV = RWKV_HEADS * RWKV_HD
LORA = 64
LORA_G = 128
RWKV_COLS = 3 * D_RWKV + 4 * LORA + LORA_G
DECAY_SCALE = math.exp(-0.5)
GN_EPS = 64e-5
N_PAIRS = RWKV_HEADS // 2
SCAN_CHUNK = 64
DIFF_HEADS = 6
DIFF_HD = 64
DIFF_VD = 2 * DIFF_HD
D_DIFF = DIFF_HEADS * DIFF_VD
AXIS_DIM = DIFF_HD // 2
ROPE_BASE = 10000.0
D_GMLP = 256
GMLP_CHUNK = 128
N_EXPERTS = 16
EC_FACTOR = 2
D_EXPERT = 2048
ALPHA = (2.0 * DEPTH) ** 0.25
LN_EPS = 1e-5
RMS_EPS = 1e-5

LANES = 128
SUBLANES = 8
VMEM_LIMIT = 56 * 1024 * 1024


def _cparams(sem):
    return pltpu.CompilerParams(dimension_semantics=sem, vmem_limit_bytes=VMEM_LIMIT)


def _bdot(a, b):
    return jnp.dot(a.astype(BF16), b.astype(BF16), preferred_element_type=F32)


def _hdot(a, b):
    return jnp.dot(a, b, preferred_element_type=F32, precision=HI)


def _split_bf16(x, terms):
    parts = []
    for _ in range(terms):
        p = x.astype(BF16)
        parts.append(p)
        x = x - p.astype(F32)
    return parts


def _dot3(a, b):
    a_hi, a_lo = _split_bf16(a, 2)
    b_hi, b_lo = _split_bf16(b, 2)
    d = lambda x, y: jnp.dot(x, y, preferred_element_type=F32)
    return d(a_hi, b_hi) + (d(a_hi, b_lo) + d(a_lo, b_hi))


def _dot_sel(a, b, terms, exact="rhs"):
    d = lambda x, y: jnp.dot(x, y, preferred_element_type=F32)
    if exact == "rhs":
        bb = b.astype(BF16)
        outs = [d(p, bb) for p in _split_bf16(a, terms)]
    else:
        ab = a.astype(BF16)
        outs = [d(ab, p) for p in _split_bf16(b, terms)]
    out = outs[-1]
    for o in reversed(outs[:-1]):
        out = out + o
    return out


def _half_mask(shape, dtype=F32):
    lane = lax.broadcasted_iota(jnp.int32, shape, len(shape) - 1)
    return (lane < 64).astype(dtype)


def _seg_ones():
    i = lax.broadcasted_iota(jnp.int32, (LANES, LANES), 0)
    j = lax.broadcasted_iota(jnp.int32, (LANES, LANES), 1)
    return ((i // 64) == (j // 64)).astype(F32)


def _layer_norm_rows(z, g, b, eps):
    mu = jnp.mean(z, axis=-1, keepdims=True)
    zc = z - mu
    var = jnp.mean(zc * zc, axis=-1, keepdims=True)
    return zc * lax.rsqrt(var + eps) * g + b


def _mod_kernel(c_ref, w_ref, b_ref, o_ref):
    a = c_ref[...]
    a = a * jax.nn.sigmoid(a)
    o_ref[0] = _hdot(a, w_ref[0]) + b_ref[0]


def _modulation(c8, w_mod, b_mod):
    tn = 1536
    nl, d, n6 = w_mod.shape
    return pl.pallas_call(
        _mod_kernel,
        grid=(nl, n6 // tn),
        in_specs=[pl.BlockSpec((8, d), lambda l, j: (0, 0)),
                  pl.BlockSpec((1, d, tn), lambda l, j: (l, 0, j)),
                  pl.BlockSpec((1, 1, tn), lambda l, j: (l, 0, j))],
        out_specs=pl.BlockSpec((1, 8, tn), lambda l, j: (l, 0, j)),
        out_shape=jax.ShapeDtypeStruct((nl, 8, n6), F32),
        compiler_params=_cparams(("parallel", "parallel")),
        name="modulation",
    )(c8, w_mod, b_mod.reshape(nl, 1, n6))


def _inproj_kernel(x_ref, sc_ref, sh_ref, w_ref, *o_refs, splits):
    h = x_ref[0] * (1.0 + sc_ref[0]) + sh_ref[0]
    y = _bdot(h, w_ref[...])
    off = 0
    for o, s in zip(o_refs, splits):
        o[0] = y[:, off:off + s].astype(o.dtype)
        off += s


def _inproj(x, sc, sh, w_bf16, splits, tm=512):
    b, t, d = x.shape
    tm = min(tm, t)
    dout = w_bf16.shape[1]
    return pl.pallas_call(
        functools.partial(_inproj_kernel, splits=splits),
        grid=(b, t // tm),
        in_specs=[pl.BlockSpec((1, tm, d), lambda i, j: (i, j, 0)),
                  pl.BlockSpec((1, 1, d), lambda i, j: (i, 0, 0)),
                  pl.BlockSpec((1, 1, d), lambda i, j: (i, 0, 0)),
                  pl.BlockSpec((d, dout), lambda i, j: (0, 0))],
        out_specs=[pl.BlockSpec((1, tm, s), lambda i, j: (i, j, 0)) for s in splits],
        out_shape=[jax.ShapeDtypeStruct((b, t, s), F32) for s in splits],
        compiler_params=_cparams(("parallel", "parallel")),
        name="inproj",
    )(x, sc, sh, w_bf16)


def _shifted(p, prev_row, next_row):
    tb = p.shape[0]
    row = lax.broadcasted_iota(jnp.int32, p.shape, 0)
    p_prev = jnp.where(row == 0, prev_row, pltpu.roll(p, 1, axis=0))
    p_next = jnp.where(row == tb - 1, next_row, pltpu.roll(p, tb - 1, axis=0))
    return p_prev, p_next


def _edge_rows(pp_ref, pn_ref):
    i = pl.program_id(1)
    nb = pl.num_programs(1)
    prev_row = jnp.where(i > 0, pp_ref[0, SUBLANES - 1:SUBLANES, :], 0.0)
    next_row = jnp.where(i < nb - 1, pn_ref[0, 0:1, :], 0.0)
    return prev_row, next_row


def _feat_kernel(p_ref, pp_ref, pn_ref, mu_ref, dup_ref, d0_ref, aup_ref, a0_ref, gup_ref, kxi_ref, kal_ref,
                 rb_ref, r_o, kk_o, v_o, kr0_o, kr1_o, lw0_o, lw1_o, b0_o, b1_o, g_o, gbv_o):
    p = p_ref[0]
    prev_row, next_row = _edge_rows(pp_ref, pn_ref)
    p_prev, p_next = _shifted(p, prev_row, next_row)
    p = p + mu_ref[...] * (0.5 * (p_prev + p_next) - p)
    dr = D_RWKV
    r = p[:, 0:dr]
    k = p[:, dr:2 * dr]
    v = p[:, 2 * dr:3 * dr]
    c0 = 3 * dr
    dd = jnp.tanh(p[:, c0:c0 + 2 * LORA])
    da = p[:, c0 + 2 * LORA:c0 + 4 * LORA]
    dg = jax.nn.sigmoid(p[:, c0 + 4 * LORA:c0 + 4 * LORA + LORA_G])
    g = _dot3(dg, gup_ref[...])
    lw, a = [], []
    for d in range(2):
        lw.append(-DECAY_SCALE * jax.nn.sigmoid(d0_ref[d:d + 1, :] + _dot3(dd[:, d * LORA:(d + 1) * LORA], dup_ref[d])))
        a.append(jax.nn.sigmoid(a0_ref[d:d + 1, :] + _dot3(da[:, d * LORA:(d + 1) * LORA], aup_ref[d])))
    kx = k * kxi_ref[...]
    kal = kal_ref[...]
    kr = [k * (1.0 + (a[d] - 1.0) * kal) for d in range(2)]
    bon = r * (0.5 * (kr[0] + kr[1])) * rb_ref[...]
    ones = _seg_ones()
    g_o[0] = g
    for j in range(N_PAIRS):
        sl = slice(j * LANES, (j + 1) * LANES)
        kxj = kx[:, sl]
        kk = kxj * lax.rsqrt(_dot_sel(kxj * kxj, ones, 2) + 1e-12)
        r_o[0, j] = r[:, sl]
        kk_o[0, j] = kk
        v_o[0, j] = v[:, sl]
        kr0_o[0, j] = kr[0][:, sl]
        kr1_o[0, j] = kr[1][:, sl]
        lw0_o[0, j] = lw[0][:, sl]
        lw1_o[0, j] = lw[1][:, sl]
        b0_o[0, j] = kk * a[0][:, sl]
        b1_o[0, j] = kk * a[1][:, sl]
        gbv_o[0, :, sl] = g[:, sl] * _dot_sel(bon[:, sl], ones, 2) * v[:, sl]


def _rwkv_features(p, prm, tb=256):
    b, t, cols = p.shape
    tb = min(tb, t)
    nb8 = t // SUBLANES
    r8 = tb // SUBLANES
    full = lambda a: pl.BlockSpec(a.shape, lambda i, j: (0,) * a.ndim)
    params = [prm["shift_mu"].reshape(1, cols), prm["decay_up"], prm["decay_0"], prm["a_up"], prm["a_0"],
              prm["g_up"], prm["k_xi"].reshape(1, -1), prm["k_alpha"].reshape(1, -1), prm["r_bonus"].reshape(1, -1)]
    packed = jax.ShapeDtypeStruct((b, N_PAIRS, t, LANES), F32)
    flat = jax.ShapeDtypeStruct((b, t, D_RWKV), F32)
    pspec = pl.BlockSpec((1, N_PAIRS, tb, LANES), lambda i, j: (i, 0, j, 0))
    fspec = pl.BlockSpec((1, tb, D_RWKV), lambda i, j: (i, j, 0))
    return pl.pallas_call(
        _feat_kernel,
        grid=(b, t // tb),
        in_specs=[pl.BlockSpec((1, tb, cols), lambda i, j: (i, j, 0)),
                  pl.BlockSpec((1, SUBLANES, cols), lambda i, j: (i, jnp.maximum(j * r8 - 1, 0), 0)),
                  pl.BlockSpec((1, SUBLANES, cols), lambda i, j: (i, jnp.minimum((j + 1) * r8, nb8 - 1), 0))]
                 + [full(a) for a in params],
        out_specs=[pspec] * 9 + [fspec] * 2,
        out_shape=[packed] * 9 + [flat] * 2,
        compiler_params=_cparams(("parallel", "parallel")),
        name="rwkv_features",
    )(p, p, p, *params)


def _scan_chunks(feats, rev):
    n = SCAN_CHUNK
    m0 = _half_mask((n, LANES))
    m1 = 1.0 - m0
    ti = lax.broadcasted_iota(jnp.int32, (n, n), 0)
    tj = lax.broadcasted_iota(jnp.int32, (n, n), 1)
    tri = ((ti <= tj) if rev else (ti >= tj)).astype(F32)
    si = lax.broadcasted_iota(jnp.int32, (2 * n, 2 * n), 0)
    sj = lax.broadcasted_iota(jnp.int32, (2 * n, 2 * n), 1)
    same = (si // n) == (sj // n)
    ri, rj = si % n, sj % n
    strict = same & ((ri < rj) if rev else (ri > rj))
    incl = same & ((ri <= rj) if rev else (ri >= rj))
    eye = (si == sj).astype(F32)

    def stack(x):
        return jnp.concatenate([x * m0, x * m1], axis=0).astype(BF16)

    pre = []
    for r, lw, kap, bb, kr, v in feats:
        cs = _dot_sel(tri, lw, 3, exact="lhs")
        tot = jnp.sum(lw, axis=0, keepdims=True)
        e_neg = jnp.exp(-cs)
        e_tot = jnp.exp(tot - cs)
        rh = r * jnp.exp(cs)
        pre.append(dict(kh=stack(kap * jnp.exp(cs - lw)), bh=stack(bb * e_neg), kq=stack(kr * e_neg), rh=rh,
                        rhs=stack(rh), kt=stack(kr * e_tot), bt=stack(bb * e_tot), v=stack(v), gl=jnp.exp(tot)))
    nt = (((1,), (1,)), ((), ()))
    amats = [lax.dot_general(jnp.concatenate([p["kh"], p["rhs"]], axis=0), jnp.concatenate([p["bh"], p["kq"]], axis=0),
                             nt, preferred_element_type=F32) for p in pre]
    a1 = [jnp.where(strict, a[:2 * n, :2 * n], 0.0) for a in amats]
    a2 = [jnp.where(strict, a[:2 * n, 2 * n:], 0.0).astype(BF16) for a in amats]
    a4 = [jnp.where(incl, a[2 * n:, :2 * n], 0.0).astype(BF16) for a in amats]
    a3 = [jnp.where(incl, a[2 * n:, 2 * n:], 0.0).astype(BF16) for a in amats]
    tinv = [eye - a for a in a1]
    pw = a1
    for _ in range(5):
        pw = [_bdot(p, p) for p in pw]
        tinv = [t + _bdot(t, p) for t, p in zip(tinv, pw)]
    w_s = [_bdot(a, p["v"]) for a, p in zip(a2, pre)]
    mm = [_bdot(t, jnp.concatenate([p["kh"], w.astype(BF16)], axis=1)).astype(BF16)
          for t, p, w in zip(tinv, pre, w_s)]
    gmat = [_bdot(p["bt"].astype(F32).T, m) for p, m in zip(pre, mm)]
    phi_t = [eye * p["gl"] - g[:, :LANES] for p, g in zip(pre, gmat)]
    psi_t = [_bdot(p["kt"].astype(F32).T, p["v"]) - g[:, LANES:] for p, g in zip(pre, gmat)]
    qy = [_bdot(a, m) for a, m in zip(a4, mm)]
    y0_s = [_bdot(a, p["v"]) - q[:, LANES:] for a, p, q in zip(a3, pre, qy)]
    return [(p["rh"] - (q[:n, :LANES] + q[n:, :LANES]), y0[:n] + y0[n:], ph, ps)
            for p, q, y0, ph, ps in zip(pre, qy, y0_s, phi_t, psi_t)]


def _scan_kernel(r_ref, lw_ref, kap_ref, b_ref, kr_ref, v_ref, s0_ref, y_ref, sT_ref, st_scr, *, rev):
    c = pl.program_id(1)

    @pl.when(c == 0)
    def _():
        st_scr[...] = s0_ref[0]

    n = SCAN_CHUNK
    nsub = r_ref.shape[2] // n
    order = range(nsub - 1, -1, -1) if rev else range(nsub)
    rows = lambda ref, hp, k: ref[0, hp, k * n:(k + 1) * n, :]
    items = [(k, hp) for k in order for hp in range(N_PAIRS)]
    terms = _scan_chunks([tuple(rows(ref, hp, k) for ref in (r_ref, lw_ref, kap_ref, b_ref, kr_ref, v_ref))
                          for k, hp in items], rev)
    states = [st_scr[hp] for hp in range(N_PAIRS)]
    for (k, hp), (qm, y0, phi_t, psi_t) in zip(items, terms):
        y_ref[0, hp, k * n:(k + 1) * n, :] = _bdot(qm, states[hp]) + y0
        states[hp] = _bdot(phi_t, states[hp]) + psi_t
    for hp in range(N_PAIRS):
        st_scr[hp] = states[hp]

    @pl.when(c == pl.num_programs(1) - 1)
    def _():
        sT_ref[0] = st_scr[...]


def _rwkv_scan(r, lw, kap, bb, kr, v, s0, rev, chunks_per_step=4):
    b, _, t, _ = r.shape
    n = SCAN_CHUNK * math.gcd(chunks_per_step, t // SCAN_CHUNK)
    nc = t // n
    cidx = (lambda c: nc - 1 - c) if rev else (lambda c: c)
    fspec = pl.BlockSpec((1, N_PAIRS, n, LANES), lambda i, c: (i, 0, cidx(c), 0))
    sspec = pl.BlockSpec((1, N_PAIRS, LANES, LANES), lambda i, c: (i, 0, 0, 0))
    return pl.pallas_call(
        functools.partial(_scan_kernel, rev=rev),
        grid=(b, nc),
        in_specs=[fspec] * 6 + [sspec],
        out_specs=[fspec, sspec],
        out_shape=[jax.ShapeDtypeStruct(r.shape, F32), jax.ShapeDtypeStruct(s0.shape, F32)],
        scratch_shapes=[pltpu.VMEM((N_PAIRS, LANES, LANES), F32)],
        compiler_params=_cparams(("parallel", "arbitrary")),
        name="rwkv_scan_rev" if rev else "rwkv_scan_fwd",
    )(r, lw, kap, bb, kr, v, s0)


def _evenpost_kernel(y0_ref, y1_ref, g_ref, gbv_ref, pc_ref, pcp_ref, pcn_ref, cw_ref, gng_ref, gnb_ref,
                     conv_o, rw_o):
    pc = pc_ref[0]
    prev_row, next_row = _edge_rows(pcp_ref, pcn_ref)
    dc = D_CONV

    def gated(z):
        return z[:, dc:2 * dc] * z[:, 2 * dc:3 * dc]

    u = gated(pc)
    u_prev, u_next = _shifted(u, gated(prev_row), gated(next_row))
    cw = cw_ref[...]
    conv_o[0] = (pc[:, 0:dc] * (cw[0:1] * u_prev + cw[1:2] * u + cw[2:3] * u_next)).astype(conv_o.dtype)
    ones = _seg_ones() * (1.0 / RWKV_HD)
    for j in range(N_PAIRS):
        sl = slice(j * LANES, (j + 1) * LANES)
        y = y0_ref[0, j] + y1_ref[0, j]
        mu = _dot_sel(y, ones, 2)
        yc = y - mu
        var = _dot_sel(yc * yc, ones, 2)
        yn = yc * lax.rsqrt(var + GN_EPS) * gng_ref[:, sl] + gnb_ref[:, sl]
        rw_o[0, :, sl] = (g_ref[0, :, sl] * yn + gbv_ref[0, :, sl]).astype(rw_o.dtype)


def _even_post(y0, y1, g, gbv, pc, conv_w, gn_g, gn_b, tb=256):
    b, _, t, _ = y0.shape
    tb = min(tb, t)
    nb8 = t // SUBLANES
    r8 = tb // SUBLANES
    c3 = pc.shape[-1]
    pspec = pl.BlockSpec((1, N_PAIRS, tb, LANES), lambda i, j: (i, 0, j, 0))
    fspec = pl.BlockSpec((1, tb, D_RWKV), lambda i, j: (i, j, 0))
    full = lambda a: pl.BlockSpec(a.shape, lambda i, j: (0,) * a.ndim)
    gn_g = gn_g.reshape(1, -1)
    gn_b = gn_b.reshape(1, -1)
    return pl.pallas_call(
        _evenpost_kernel,
        grid=(b, t // tb),
        in_specs=[pspec, pspec, fspec, fspec,
                  pl.BlockSpec((1, tb, c3), lambda i, j: (i, j, 0)),
                  pl.BlockSpec((1, SUBLANES, c3), lambda i, j: (i, jnp.maximum(j * r8 - 1, 0), 0)),
                  pl.BlockSpec((1, SUBLANES, c3), lambda i, j: (i, jnp.minimum((j + 1) * r8, nb8 - 1), 0)),
                  full(conv_w), full(gn_g), full(gn_b)],
        out_specs=[pl.BlockSpec((1, tb, D_CONV), lambda i, j: (i, j, 0)), fspec],
        out_shape=[jax.ShapeDtypeStruct((b, t, D_CONV), BF16), jax.ShapeDtypeStruct((b, t, D_RWKV), BF16)],
        compiler_params=_cparams(("parallel", "parallel")),
        name="even_post",
    )(y0, y1, g, gbv, pc, pc, pc, conv_w, gn_g, gn_b)


def _outproj_kernel(a_ref, b_ref, wa_ref, wb_ref, x_ref, gm_ref, lg_ref, lb_ref, o_ref):
    yx = _bdot(a_ref[0], wa_ref[...]) + _bdot(b_ref[0], wb_ref[...])
    z = ALPHA * x_ref[0] + gm_ref[0] * yx
    o_ref[0] = _layer_norm_rows(z, lg_ref[...], lb_ref[...], LN_EPS)


def _outproj(ma, mb, w_out_bf16, x, gm, ln_g, ln_b, tm=512):
    b, t, d = x.shape
    tm = min(tm, t)
    da, db = ma.shape[-1], mb.shape[-1]
    wa, wb = w_out_bf16[:da], w_out_bf16[da:]
    row = lambda w: pl.BlockSpec((1, tm, w), lambda i, j: (i, j, 0))
    full = lambda a: pl.BlockSpec(a.shape, lambda i, j: (0,) * a.ndim)
    ln_g = ln_g.reshape(1, d)
    ln_b = ln_b.reshape(1, d)
    return pl.pallas_call(
        _outproj_kernel,
        grid=(b, t // tm),
        in_specs=[row(da), row(db), full(wa), full(wb), row(d),
                  pl.BlockSpec((1, 1, d), lambda i, j: (i, 0, 0)), full(ln_g), full(ln_b)],
        out_specs=row(d),
        out_shape=jax.ShapeDtypeStruct((b, t, d), F32),
        compiler_params=_cparams(("parallel", "parallel")),
        name="outproj_norm",
    )(ma, mb, wa, wb, x, gm, ln_g, ln_b)


def _rope_kernel(x_ref, cos_ref, sin_ref, o_ref, *, scale):
    cos = cos_ref[...]
    sin = sin_ref[...]
    lane = lax.broadcasted_iota(jnp.int32, cos.shape, 1)
    first = (lane % (2 * (AXIS_DIM // 2))) < (AXIS_DIM // 2)
    half = AXIS_DIM // 2
    for j in range(DIFF_HEADS):
        sl = slice(j * LANES, (j + 1) * LANES)
        xs = x_ref[0, :, sl]
        partner = jnp.where(first, -pltpu.roll(xs, LANES - half, axis=1), pltpu.roll(xs, half, axis=1))
        o_ref[0, :, sl] = ((xs * cos + partner * sin) * scale).astype(o_ref.dtype)


def _rope(x, cos, sin, scale, tb=512):
    b, t, w = x.shape
    tb = min(tb, t)
    return pl.pallas_call(
        functools.partial(_rope_kernel, scale=scale),
        grid=(b, t // tb),
        in_specs=[pl.BlockSpec((1, tb, w), lambda i, j: (i, j, 0)),
                  pl.BlockSpec((tb, LANES), lambda i, j: (j, 0)),
                  pl.BlockSpec((tb, LANES), lambda i, j: (j, 0))],
        out_specs=pl.BlockSpec((1, tb, w), lambda i, j: (i, j, 0)),
        out_shape=jax.ShapeDtypeStruct((b, t, w), BF16),
        compiler_params=_cparams(("parallel", "parallel")),
        name="axial_rope",
    )(x, cos, sin)


def _rope_tables(t):
    rows = t // GRID_W
    row = jnp.repeat(jnp.arange(rows), GRID_W).astype(F32)
    col = jnp.tile(jnp.arange(GRID_W), rows).astype(F32)
    inv = ROPE_BASE ** (-jnp.arange(0, AXIS_DIM, 2, dtype=F32) / AXIS_DIM)
    ang_r = row[:, None] * inv
    ang_c = col[:, None] * inv
    ang = jnp.concatenate([ang_r, ang_r, ang_c, ang_c], axis=-1)
    ang = jnp.concatenate([ang, ang], axis=-1)
    return jnp.cos(ang), jnp.sin(ang)


def _attn_kernel(lam_ref, q_ref, k_ref, v_ref, g_ref, o_ref, *, tk, out_scale):
    q = q_ref[0]
    tq = q.shape[0]
    hm = _half_mask(q.shape, BF16)
    q1 = q * hm
    qs = (q1, q - q1)
    nk = k_ref.shape[1] // tk
    nsl = tk // LANES

    def scores(qh, j):
        kb = k_ref[0, pl.ds(pl.multiple_of(j * tk, tk), tk), :]
        return lax.dot_general(qh, kb, (((1,), (1,)), ((), ())), preferred_element_type=F32)

    def col(s, c):
        return s[:, c * LANES:(c + 1) * LANES]

    def body(j, carry):
        vb = v_ref[0, pl.ds(pl.multiple_of(j * tk, tk), tk), :]
        ss = [scores(qh, j) for qh in qs]
        out = []
        for s, (m, ls, acc) in zip(ss, (carry[:3], carry[3:])):
            mx = col(s, 0)
            for c in range(1, nsl):
                mx = jnp.maximum(mx, col(s, c))
            m_new = jnp.maximum(m, jnp.broadcast_to(jnp.max(mx, axis=-1, keepdims=True), (tq, LANES)))
            corr = jnp.exp2(m - m_new)
            ps = [jnp.exp2(col(s, c) - m_new) for c in range(nsl)]
            ls = corr * ls
            for pc in ps:
                ls = ls + pc
            p = jnp.concatenate([pc.astype(BF16) for pc in ps], axis=1)
            acc = corr * acc + jnp.dot(p, vb, preferred_element_type=F32)
            out += [m_new, ls, acc]
        return tuple(out)

    neg = jnp.full((tq, LANES), -jnp.inf, F32)
    zero = jnp.zeros((tq, LANES), F32)
    _, ls1, acc1, _, ls2, acc2 = lax.fori_loop(0, nk, body, (neg, zero, zero, neg, zero, zero), unroll=True)
    l1 = jnp.sum(ls1, axis=-1, keepdims=True)
    l2 = jnp.sum(ls2, axis=-1, keepdims=True)
    o = acc1 / l1 - lam_ref[0] * (acc2 / l2)
    o = o * lax.rsqrt(jnp.mean(o * o, axis=-1, keepdims=True) + RMS_EPS) * g_ref[...] * out_scale
    o_ref[0] = o.astype(o_ref.dtype)


def _diff_attention(q, k, v, lam, subln_g, lam_init, tq=512, tk=1408):
    b, t, w = q.shape
    tkk = k.shape[1]
    tq = min(tq, t)
    tk = math.gcd(tk, tkk)
    assert tk % LANES == 0 and t % tq == 0
    subln_g = subln_g.reshape(1, DIFF_VD)
    return pl.pallas_call(
        functools.partial(_attn_kernel, tk=tk, out_scale=1.0 - lam_init),
        grid=(b, DIFF_HEADS, t // tq),
        in_specs=[pl.BlockSpec(memory_space=pltpu.SMEM),
                  pl.BlockSpec((1, tq, LANES), lambda i, h, j: (i, j, h)),
                  pl.BlockSpec((1, tkk, LANES), lambda i, h, j: (i, 0, h)),
                  pl.BlockSpec((1, tkk, LANES), lambda i, h, j: (i, 0, h)),
                  pl.BlockSpec((1, DIFF_VD), lambda i, h, j: (0, 0))],
        out_specs=pl.BlockSpec((1, tq, LANES), lambda i, h, j: (i, j, h)),
        out_shape=jax.ShapeDtypeStruct((b, t, w), BF16),
        compiler_params=_cparams(("parallel", "parallel", "parallel")),
        name="diff_attention",
    )(lam, q, k, v, subln_g)


def _gmlp_kernel(p_ref, lg_ref, lb_ref, ws_ref, bs_ref, o_ref):
    p = p_ref[0]
    tb = p.shape[0]
    ge = 0.5 * p * (1.0 + lax.erf(p * (2.0 ** -0.5)))
    u = ge[:, :D_GMLP]
    v = _layer_norm_rows(ge[:, D_GMLP:], lg_ref[...], lb_ref[...], LN_EPS)
    hm = _half_mask((GMLP_CHUNK, LANES))
    for c in range(tb // GMLP_CHUNK):
        rs = slice(c * GMLP_CHUNK, (c + 1) * GMLP_CHUNK)
        for j in range(D_GMLP // LANES):
            sl = slice(j * LANES, (j + 1) * LANES)
            vc = v[rs, sl]
            va = vc * hm
            mixed = _bdot(ws_ref[2 * j], va) + _bdot(ws_ref[2 * j + 1], vc - va) + bs_ref[:, sl]
            o_ref[0, rs, sl] = (u[rs, sl] * mixed).astype(o_ref.dtype)


def _chunk_gmlp(p, ln_g, ln_b, ws, bs, tb=256):
    b, t, w = p.shape
    tb = min(tb, t)
    ln_g = ln_g.reshape(1, -1)
    ln_b = ln_b.reshape(1, -1)
    bs_t = jnp.repeat(jnp.transpose(bs), D_GMLP // ws.shape[0], axis=1)
    full = lambda a: pl.BlockSpec(a.shape, lambda i, j: (0,) * a.ndim)
    return pl.pallas_call(
        _gmlp_kernel,
        grid=(b, t // tb),
        in_specs=[pl.BlockSpec((1, tb, w), lambda i, j: (i, j, 0)), full(ln_g), full(ln_b), full(ws), full(bs_t)],
        out_specs=pl.BlockSpec((1, tb, D_GMLP), lambda i, j: (i, j, 0)),
        out_shape=jax.ShapeDtypeStruct((b, t, D_GMLP), BF16),
        compiler_params=_cparams(("parallel", "parallel")),
        name="chunk_gmlp",
    )(p, ln_g, ln_b, ws, bs_t)


def _router_kernel(x_ref, sc_ref, sh_ref, wr_ref, h_o, aff_o):
    h = x_ref[0] * (1.0 + sc_ref[0]) + sh_ref[0]
    h_o[0] = h.astype(h_o.dtype)
    logits = lax.dot_general(wr_ref[...], h, (((1,), (1,)), ((), ())), preferred_element_type=F32, precision=HI)
    m = jnp.max(logits, axis=0, keepdims=True)
    e = jnp.exp(logits - m)
    aff_o[0] = e / jnp.sum(e, axis=0, keepdims=True)


def _router(x, sc, sh, w_router_t, tm=512):
    b, t, d = x.shape
    tm = min(tm, t)
    e = w_router_t.shape[0]
    return pl.pallas_call(
        _router_kernel,
        grid=(b, t // tm),
        in_specs=[pl.BlockSpec((1, tm, d), lambda i, j: (i, j, 0)),
                  pl.BlockSpec((1, 1, d), lambda i, j: (i, 0, 0)),
                  pl.BlockSpec((1, 1, d), lambda i, j: (i, 0, 0)),
                  pl.BlockSpec((e, d), lambda i, j: (0, 0))],
        out_specs=[pl.BlockSpec((1, tm, d), lambda i, j: (i, j, 0)),
                   pl.BlockSpec((1, e, tm), lambda i, j: (i, 0, j))],
        out_shape=[jax.ShapeDtypeStruct((b, t, d), BF16), jax.ShapeDtypeStruct((b, e, t), F32)],
        compiler_params=_cparams(("parallel", "parallel")),
        name="router",
    )(x, sc, sh, w_router_t)


ROUTE_BLOCK = 128
STARTS_PAD = LANES


def _lane_cumsum(mask):
    e, t = mask.shape
    i = lax.broadcasted_iota(jnp.int32, (LANES, LANES), 0)
    j = lax.broadcasted_iota(jnp.int32, (LANES, LANES), 1)
    upper = (i <= j).astype(BF16)
    x = jnp.where(mask, 1.0, 0.0).astype(BF16)
    off = jnp.zeros((e, 1), F32)
    out = []
    for c in range(t // LANES):
        blk = jnp.dot(x[:, c * LANES:(c + 1) * LANES], upper, preferred_element_type=F32) + off
        out.append(blk)
        off = blk[:, LANES - 1:LANES]
    return jnp.concatenate(out, axis=1)


def _select_kernel(aff_ref, pos_ref, starts_ref, *, cap):
    a = aff_ref[0]
    e, t = a.shape
    bits = pltpu.bitcast(a, jnp.int32)

    def search(i, thr):
        cand = thr | jnp.left_shift(jnp.int32(1), 30 - i)
        cnt = jnp.sum(jnp.where(bits >= cand, 1.0, 0.0), axis=1, keepdims=True)
        return jnp.where(cnt >= cap, cand, thr)

    thr = lax.fori_loop(0, 31, search, jnp.zeros((e, 1), jnp.int32))
    gt = bits > thr
    eq = bits == thr
    need = cap - jnp.sum(jnp.where(gt, 1.0, 0.0), axis=1, keepdims=True)
    sel = gt | (eq & (_lane_cumsum(eq) <= need))
    csel = _lane_cumsum(sel)
    pos_ref[0] = jnp.where(sel, csel - 1.0, -1.0).astype(jnp.int32)
    ti = lax.broadcasted_iota(jnp.int32, (t, STARTS_PAD), 0)
    ki = lax.broadcasted_iota(jnp.int32, (t, STARTS_PAD), 1)
    pick = jnp.where(ti == ki * ROUTE_BLOCK - 1, 1.0, 0.0)
    starts_ref[0] = (_dot_sel(csel, pick, 2) + 0.5).astype(jnp.int32)


def _select(aff, cap):
    b, e, t = aff.shape
    assert t % ROUTE_BLOCK == 0 and t // ROUTE_BLOCK <= STARTS_PAD
    return pl.pallas_call(
        functools.partial(_select_kernel, cap=cap),
        grid=(b,),
        in_specs=[pl.BlockSpec((1, e, t), lambda i: (i, 0, 0))],
        out_specs=[pl.BlockSpec((1, e, t), lambda i: (i, 0, 0)), pl.BlockSpec((1, e, STARTS_PAD), lambda i: (i, 0, 0))],
        out_shape=[jax.ShapeDtypeStruct((b, e, t), jnp.int32), jax.ShapeDtypeStruct((b, e, STARTS_PAD), jnp.int32)],
        compiler_params=_cparams(("parallel",)),
        name="expert_select",
    )(aff)


def _window_start(start, align, cap, width):
    s = lax.shift_left(lax.shift_right_logical(start, align.bit_length() - 1), align.bit_length() - 1)
    return pl.multiple_of(jnp.minimum(s, cap - width), align)


def _gather_ffn_kernel(starts_ref, pos_ref, h_ref, w1_ref, w3_ref, w2_ref, y_ref, xs_scr, *, width, rows, cap):
    ei, bi, k = pl.program_id(0), pl.program_id(1), pl.program_id(2)
    ne, nb = pl.num_programs(0), pl.num_programs(1)
    nsub = h_ref.shape[1] // ROUTE_BLOCK
    merged = xs_scr.shape[0] > cap
    base = pl.multiple_of(bi * cap, SUBLANES) if merged else 0

    @pl.when(k == 0)
    def _():
        xs_scr[pl.ds(base, cap), :] = jnp.zeros((cap, xs_scr.shape[1]), F32)

    jrow = lax.broadcasted_iota(jnp.int32, (width, ROUTE_BLOCK), 0)
    for sb in range(nsub):
        start = starts_ref[(bi * ne + ei) * STARTS_PAD + k * nsub + sb]
        s8 = _window_start(start, SUBLANES, cap, width)
        prow = pos_ref[0, :, sb * ROUTE_BLOCK:(sb + 1) * ROUTE_BLOCK]
        onehot = jnp.where(prow - s8 == jrow, 1.0, 0.0).astype(BF16)
        xs_scr[pl.ds(base + s8, width), :] += jnp.dot(onehot, h_ref[0, sb * ROUTE_BLOCK:(sb + 1) * ROUTE_BLOCK, :],
                                                      preferred_element_type=F32)

    last = k == pl.num_programs(2) - 1

    @pl.when(last & (bi == nb - 1) if merged else last)
    def _():
        for r in range(xs_scr.shape[0] // rows):
            x = xs_scr[r * rows:(r + 1) * rows, :].astype(BF16)
            h1 = jnp.dot(x, w1_ref[0], preferred_element_type=F32)
            h3 = jnp.dot(x, w3_ref[0], preferred_element_type=F32)
            hid = (h1 * jax.nn.sigmoid(h1)) * h3
            y = jnp.dot(hid.astype(BF16), w2_ref[0], preferred_element_type=F32).astype(y_ref.dtype)
            for s in range(max(rows // cap, 1)):
                n = min(rows, cap)
                row0 = r * rows + s * cap
                y_ref[row0 // cap, 0, row0 % cap:row0 % cap + n, :] = y[s * cap:s * cap + n]


def _gather_ffn(h, pos, starts, w1, w3, w2, cap, tch=2048, rows=256):
    b, t, d = h.shape
    e, _, f = w1.shape
    tch = min(tch, t)
    merged = b * cap <= rows
    slots = b if merged else 1
    rows = min(rows, slots * cap)
    assert (slots * cap) % rows == 0 and (rows % cap == 0 or cap % rows == 0)
    width = min(ROUTE_BLOCK + SUBLANES, cap)
    grid_spec = pltpu.PrefetchScalarGridSpec(
        num_scalar_prefetch=1,
        grid=(e, b, t // tch),
        in_specs=[pl.BlockSpec((1, 1, tch), lambda ei, bi, k, s: (bi * e + ei, 0, k)),
                  pl.BlockSpec((1, tch, d), lambda ei, bi, k, s: (bi, k, 0)),
                  pl.BlockSpec((1, d, f), lambda ei, bi, k, s: (ei, 0, 0)),
                  pl.BlockSpec((1, d, f), lambda ei, bi, k, s: (ei, 0, 0)),
                  pl.BlockSpec((1, f, d), lambda ei, bi, k, s: (ei, 0, 0))],
        out_specs=pl.BlockSpec((slots, 1, cap, d), (lambda ei, bi, k, s: (0, ei, 0, 0)) if merged
                               else (lambda ei, bi, k, s: (bi, ei, 0, 0))),
        scratch_shapes=[pltpu.VMEM((slots * cap, d), F32)],
    )
    return pl.pallas_call(
        functools.partial(_gather_ffn_kernel, width=width, rows=rows, cap=cap),
        grid_spec=grid_spec,
        out_shape=jax.ShapeDtypeStruct((b, e, cap, d), BF16),
        compiler_params=_cparams(("parallel", "arbitrary", "arbitrary") if merged
                                 else ("parallel", "parallel", "arbitrary")),
        name="expert_gather_ffn",
    )(starts.reshape(-1), pos.reshape(b * e, 1, t), h, w1, w3, w2)


def _combine_kernel(starts_ref, post_ref, afft_ref, y_ref, x_ref, gf_ref, lg_ref, lb_ref, o_ref, acc_scr, *, width):
    bi, i, ei = pl.program_id(0), pl.program_id(1), pl.program_id(2)
    ne = pl.num_programs(2)
    cap = y_ref.shape[2]
    nsub = x_ref.shape[1] // ROUTE_BLOCK

    @pl.when(ei == 0)
    def _():
        acc_scr[...] = jnp.zeros_like(acc_scr)

    elane = lax.broadcasted_iota(jnp.int32, (ROUTE_BLOCK, ne), 1) == ei
    jcol = lax.broadcasted_iota(jnp.int32, (ROUTE_BLOCK, width), 1)
    for sb in range(nsub):
        rs = slice(sb * ROUTE_BLOCK, (sb + 1) * ROUTE_BLOCK)
        start = starts_ref[(bi * ne + ei) * STARTS_PAD + i * nsub + sb]
        s16 = _window_start(start, 2 * SUBLANES, cap, width)
        pcol = jnp.sum(jnp.where(elane, post_ref[0, rs, :].astype(F32), 0.0), axis=1, keepdims=True)
        gcol = jnp.sum(jnp.where(elane, afft_ref[0, rs, :], 0.0), axis=1, keepdims=True)
        onehot = jnp.where(pcol.astype(jnp.int32) - s16 == jcol, 1.0, 0.0).astype(BF16)
        acc_scr[rs, :] += gcol * jnp.dot(onehot, y_ref[0, 0, pl.ds(s16, width), :], preferred_element_type=F32)

    @pl.when(ei == ne - 1)
    def _():
        z = ALPHA * x_ref[0] + gf_ref[0] * acc_scr[...]
        o_ref[0] = _layer_norm_rows(z, lg_ref[...], lb_ref[...], LN_EPS)


def _combine(x, y, pos_t, aff_t, starts, gf, ln_g, ln_b, tbo=1024):
    b, t, d = x.shape
    _, e, cap, _ = y.shape
    tbo = min(tbo, t)
    width = min(ROUTE_BLOCK + 2 * SUBLANES, cap)
    grid_spec = pltpu.PrefetchScalarGridSpec(
        num_scalar_prefetch=1,
        grid=(b, t // tbo, e),
        in_specs=[pl.BlockSpec((1, tbo, e), lambda bi, i, ei, s: (bi, i, 0)),
                  pl.BlockSpec((1, tbo, e), lambda bi, i, ei, s: (bi, i, 0)),
                  pl.BlockSpec((1, 1, cap, d), lambda bi, i, ei, s: (bi, ei, 0, 0)),
                  pl.BlockSpec((1, tbo, d), lambda bi, i, ei, s: (bi, i, 0)),
                  pl.BlockSpec((1, 1, d), lambda bi, i, ei, s: (bi, 0, 0)),
                  pl.BlockSpec((1, d), lambda bi, i, ei, s: (0, 0)),
                  pl.BlockSpec((1, d), lambda bi, i, ei, s: (0, 0))],
        out_specs=pl.BlockSpec((1, tbo, d), lambda bi, i, ei, s: (bi, i, 0)),
        scratch_shapes=[pltpu.VMEM((tbo, d), F32)],
    )
    return pl.pallas_call(
        functools.partial(_combine_kernel, width=width),
        grid_spec=grid_spec,
        out_shape=jax.ShapeDtypeStruct((b, t, d), F32),
        compiler_params=_cparams(("parallel", "parallel", "arbitrary")),
        name="expert_combine_norm",
    )(starts.reshape(-1), pos_t, aff_t, y, x, gf, ln_g.reshape(1, d), ln_b.reshape(1, d))


def _cast_kernel(x_ref, o_ref):
    o_ref[...] = x_ref[...].astype(o_ref.dtype)


def _to_bf16(w, l):
    _, e, r, c = w.shape
    return pl.pallas_call(
        _cast_kernel,
        grid=(e,),
        in_specs=[pl.BlockSpec((None, 1, r, c), lambda i: (l, i, 0, 0))],
        out_specs=pl.BlockSpec((1, r, c), lambda i: (i, 0, 0)),
        out_shape=jax.ShapeDtypeStruct((e, r, c), BF16),
        compiler_params=_cparams(("parallel",)),
        name="cast_bf16",
    )(w)


def _moe(x, sc, sh, gf, w_router_t, w1, w3, w2, ln_g, ln_b):
    t = x.shape[1]
    cap = EC_FACTOR * t // N_EXPERTS
    h, aff = _router(x, sc, sh, w_router_t)
    pos, starts = _select(aff, cap)
    y = _gather_ffn(h, pos, starts, w1, w3, w2, cap)
    to_token_major = lambda z: jnp.transpose(z, (0, 2, 1))
    return _combine(x, y, to_token_major(pos), to_token_major(aff), starts, gf, ln_g, ln_b)


def _even_layer(hx_in, hc_in, prm, ctx_out):
    w_in = prm["w_in"]
    splits = (3 * D_CONV, RWKV_COLS)
    pcx, prx = _inproj(*hx_in, w_in, splits)
    pcc, prc = _inproj(*hc_in, w_in, splits)
    fx = _rwkv_features(prx, prm)
    fc = _rwkv_features(prc, prm)
    b = prx.shape[0]
    zero = jnp.zeros((b, N_PAIRS, LANES, LANES), F32)
    ys_x, ys_c = [], []
    for d, rev in ((0, False), (1, True)):
        def args(f):
            r, kk, v, kr0, kr1, lw0, lw1, b0, b1 = f[:9]
            return (r, (lw0, lw1)[d], kk, (b0, b1)[d], (kr0, kr1)[d], v)
        yc, s_ctx = _rwkv_scan(*args(fc), zero, rev)
        yx, _ = _rwkv_scan(*args(fx), s_ctx, rev)
        ys_x.append(yx)
        ys_c.append(yc)
    out_x = _even_post(ys_x[0], ys_x[1], fx[9], fx[10], pcx, prm["conv_w"], prm["gn_g"], prm["gn_b"])
    out_c = None
    if ctx_out:
        out_c = _even_post(ys_c[0], ys_c[1], fc[9], fc[10], pcc, prm["conv_w"], prm["gn_g"], prm["gn_b"])
    return out_c, out_x


def _odd_layer(hx_in, hc_in, prm, lam_init, ctx_out):
    w_in = prm["w_in"]
    splits = (D_DIFF, D_DIFF, D_DIFF, 2 * D_GMLP)
    qx, kx, vx, gx = _inproj(*hx_in, w_in, splits)
    qc, kc, vc, gc = _inproj(*hc_in, w_in, splits)
    t = qx.shape[1]
    cos, sin = _rope_tables(t)
    qscale = DIFF_HD ** -0.5 * math.log2(math.e)
    q_rot = _rope(qx, cos, sin, qscale)
    k_rot = _rope(kx, cos, sin, 1.0)
    k_all = jnp.concatenate([kc.astype(BF16), k_rot], axis=1)
    v_all = jnp.concatenate([vc.astype(BF16), vx.astype(BF16)], axis=1)
    lam = (jnp.exp(jnp.sum(prm["lam_q1"] * prm["lam_k1"])) - jnp.exp(jnp.sum(prm["lam_q2"] * prm["lam_k2"]))
           + lam_init).reshape(1).astype(F32)
    att_x = _diff_attention(q_rot, k_all, v_all, lam, prm["subln_g"], lam_init)
    gm_x = _chunk_gmlp(gx, prm["gmlp_ln_g"], prm["gmlp_ln_b"], prm["gmlp_ws"], prm["gmlp_bs"])
    out_c = None
    if ctx_out:
        att_c = _diff_attention((qc * qscale).astype(BF16), kc.astype(BF16), vc.astype(BF16), lam,
                                prm["subln_g"], lam_init)
        gm_c = _chunk_gmlp(gc, prm["gmlp_ln_g"], prm["gmlp_ln_b"], prm["gmlp_ws"], prm["gmlp_bs"])
        out_c = (att_c, gm_c)
    return out_c, (att_x, gm_x)


def kernel(x, c, ctx, c_ctx, w_mod, b_mod, ln_g, ln_b, even_w_in, even_w_out, conv_w, shift_mu, decay_up, decay_0, a_up, a_0, g_up, k_xi, k_alpha, r_bonus, gn_g, gn_b, odd_w_in, odd_w_out, lam_q1, lam_k1, lam_q2, lam_k2, subln_g, gmlp_ln_g, gmlp_ln_b, gmlp_ws, gmlp_bs, w_router, w_e1, w_e3, w_e2):
    bsz, _, d = x.shape
    assert bsz <= SUBLANES - 1
    c8 = jnp.zeros((SUBLANES, d), F32).at[:bsz].set(c).at[bsz].set(c_ctx)
    mod = _modulation(c8, w_mod, b_mod)
    for l in range(DEPTH):
        ctx_out = l < DEPTH - 1
        i = l // 2
        mx = mod[l, :bsz].reshape(bsz, 1, 6, d)
        mc = jnp.broadcast_to(mod[l, bsz].reshape(1, 1, 6, d), (bsz, 1, 6, d))
        part = lambda m, n: m[:, :, n]
        hx_in = (x, part(mx, 1), part(mx, 0))
        hc_in = (ctx, part(mc, 1), part(mc, 0))
        if l % 2 == 0:
            prm = dict(w_in=even_w_in[i].astype(BF16), conv_w=conv_w[i], shift_mu=shift_mu[i],
                       decay_up=decay_up[i], decay_0=decay_0[i], a_up=a_up[i], a_0=a_0[i], g_up=g_up[i],
                       k_xi=k_xi[i], k_alpha=k_alpha[i], r_bonus=r_bonus[i], gn_g=gn_g[i], gn_b=gn_b[i])
            out_c, out_x = _even_layer(hx_in, hc_in, prm, ctx_out)
            w_out = even_w_out[i].astype(BF16)
        else:
            lam_init = 0.8 - 0.6 * math.exp(-0.3 * l)
            prm = dict(w_in=odd_w_in[i].astype(BF16), lam_q1=lam_q1[i], lam_k1=lam_k1[i], lam_q2=lam_q2[i],
                       lam_k2=lam_k2[i], subln_g=subln_g[i], gmlp_ln_g=gmlp_ln_g[i], gmlp_ln_b=gmlp_ln_b[i],
                       gmlp_ws=gmlp_ws[i], gmlp_bs=gmlp_bs[i])
            out_c, out_x = _odd_layer(hx_in, hc_in, prm, lam_init, ctx_out)
            w_out = odd_w_out[i].astype(BF16)
        wr_t = jnp.transpose(w_router[l])
        w1, w3, w2 = _to_bf16(w_e1, l), _to_bf16(w_e3, l), _to_bf16(w_e2, l)
        x = _outproj(out_x[0], out_x[1], w_out, x, part(mx, 2), ln_g[l, 0], ln_b[l, 0])
        x = _moe(x, part(mx, 4), part(mx, 3), part(mx, 5), wr_t, w1, w3, w2, ln_g[l, 1], ln_b[l, 1])
        if ctx_out:
            ctx = _outproj(out_c[0], out_c[1], w_out, ctx, part(mc, 2), ln_g[l, 0], ln_b[l, 0])
            ctx = _moe(ctx, part(mc, 4), part(mc, 3), part(mc, 5), wr_t, w1, w3, w2, ln_g[l, 1], ln_b[l, 1])
    return x
```

```python
import functools
import math

import jax
import jax.numpy as jnp
from jax import lax
from jax.experimental import pallas as pl
from jax.experimental.pallas import tpu as pltpu

F32 = jnp.float32
BF16 = jnp.bfloat16
HI = lax.Precision.HIGHEST

D_MODEL = 1024
DEPTH = 4
GRID_W = 64
D_CONV = 256
RWKV_HEADS = 12
RWKV_HD = 64
D_RWKV = RWKV_HEADS * RWKV_HD
LORA = 64
LORA_G = 128
RWKV_COLS = 3 * D_RWKV + 4 * LORA + LORA_G
DECAY_SCALE = math.exp(-0.5)
GN_EPS = 64e-5
N_PAIRS = RWKV_HEADS // 2
SCAN_CHUNK = 64
DIFF_HEADS = 6
DIFF_HD = 64
DIFF_VD = 2 * DIFF_HD
D_DIFF = DIFF_HEADS * DIFF_VD
AXIS_DIM = DIFF_HD // 2
ROPE_BASE = 10000.0
D_GMLP = 256
GMLP_CHUNK = 128
N_EXPERTS = 16
EC_FACTOR = 2
D_EXPERT = 2048
ALPHA = (2.0 * DEPTH) ** 0.25
LN_EPS = 1e-5
RMS_EPS = 1e-5

LANES = 128
SUBLANES = 8
VMEM_LIMIT = 56 * 1024 * 1024


def _cparams(sem):
    return pltpu.CompilerParams(dimension_semantics=sem, vmem_limit_bytes=VMEM_LIMIT)


def _bdot(a, b):
    return jnp.dot(a.astype(BF16), b.astype(BF16), preferred_element_type=F32)


def _hdot(a, b):
    return jnp.dot(a, b, preferred_element_type=F32, precision=HI)


def _split_bf16(x, terms):
    parts = []
    for _ in range(terms):
        p = x.astype(BF16)
        parts.append(p)
        x = x - p.astype(F32)
    return parts


def _dot3(a, b):
    a_hi, a_lo = _split_bf16(a, 2)
    b_hi, b_lo = _split_bf16(b, 2)
    d = lambda x, y: jnp.dot(x, y, preferred_element_type=F32)
    return d(a_hi, b_hi) + (d(a_hi, b_lo) + d(a_lo, b_hi))


def _dot_sel(a, b, terms, exact="rhs"):
    d = lambda x, y: jnp.dot(x, y, preferred_element_type=F32)
    if exact == "rhs":
        bb = b.astype(BF16)
        outs = [d(p, bb) for p in _split_bf16(a, terms)]
    else:
        ab = a.astype(BF16)
        outs = [d(ab, p) for p in _split_bf16(b, terms)]
    out = outs[-1]
    for o in reversed(outs[:-1]):
        out = out + o
    return out


def _half_mask(shape, dtype=F32):
    lane = lax.broadcasted_iota(jnp.int32, shape, len(shape) - 1)
    return (lane < 64).astype(dtype)


def _seg_ones():
    i = lax.broadcasted_iota(jnp.int32, (LANES, LANES), 0)
    j = lax.broadcasted_iota(jnp.int32, (LANES, LANES), 1)
    return ((i // 64) == (j // 64)).astype(F32)


def _layer_norm_rows(z, g, b, eps):
    mu = jnp.mean(z, axis=-1, keepdims=True)
    zc = z - mu
    var = jnp.mean(zc * zc, axis=-1, keepdims=True)
    return zc * lax.rsqrt(var + eps) * g + b


def _mod_kernel(c_ref, w_ref, b_ref, o_ref):
    a = c_ref[...]
    a = a * jax.nn.sigmoid(a)
    o_ref[0] = _hdot(a, w_ref[0]) + b_ref[0]


def _modulation(c8, w_mod, b_mod):
    tn = 1536
    nl, d, n6 = w_mod.shape
    return pl.pallas_call(
        _mod_kernel,
        grid=(nl, n6 // tn),
        in_specs=[pl.BlockSpec((8, d), lambda l, j: (0, 0)),
                  pl.BlockSpec((1, d, tn), lambda l, j: (l, 0, j)),
                  pl.BlockSpec((1, 1, tn), lambda l, j: (l, 0, j))],
        out_specs=pl.BlockSpec((1, 8, tn), lambda l, j: (l, 0, j)),
        out_shape=jax.ShapeDtypeStruct((nl, 8, n6), F32),
        compiler_params=_cparams(("parallel", "parallel")),
        name="modulation",
    )(c8, w_mod, b_mod.reshape(nl, 1, n6))


def _inproj_kernel(x_ref, sc_ref, sh_ref, w_ref, *o_refs, splits):
    h = x_ref[0] * (1.0 + sc_ref[0]) + sh_ref[0]
    y = _bdot(h, w_ref[...])
    off = 0
    for o, s in zip(o_refs, splits):
        o[0] = y[:, off:off + s].astype(o.dtype)
        off += s


def _inproj(x, sc, sh, w_bf16, splits, tm=512):
    b, t, d = x.shape
    tm = min(tm, t)
    dout = w_bf16.shape[1]
    return pl.pallas_call(
        functools.partial(_inproj_kernel, splits=splits),
        grid=(b, t // tm),
        in_specs=[pl.BlockSpec((1, tm, d), lambda i, j: (i, j, 0)),
                  pl.BlockSpec((1, 1, d), lambda i, j: (i, 0, 0)),
                  pl.BlockSpec((1, 1, d), lambda i, j: (i, 0, 0)),
                  pl.BlockSpec((d, dout), lambda i, j: (0, 0))],
        out_specs=[pl.BlockSpec((1, tm, s), lambda i, j: (i, j, 0)) for s in splits],
        out_shape=[jax.ShapeDtypeStruct((b, t, s), F32) for s in splits],
        compiler_params=_cparams(("parallel", "parallel")),
        name="inproj",
    )(x, sc, sh, w_bf16)


def _shifted(p, prev_row, next_row):
    tb = p.shape[0]
    row = lax.broadcasted_iota(jnp.int32, p.shape, 0)
    p_prev = jnp.where(row == 0, prev_row, pltpu.roll(p, 1, axis=0))
    p_next = jnp.where(row == tb - 1, next_row, pltpu.roll(p, tb - 1, axis=0))
    return p_prev, p_next


def _edge_rows(pp_ref, pn_ref):
    i = pl.program_id(1)
    nb = pl.num_programs(1)
    prev_row = jnp.where(i > 0, pp_ref[0, SUBLANES - 1:SUBLANES, :], 0.0)
    next_row = jnp.where(i < nb - 1, pn_ref[0, 0:1, :], 0.0)
    return prev_row, next_row


def _feat_kernel(p_ref, pp_ref, pn_ref, mu_ref, dup_ref, d0_ref, aup_ref, a0_ref, gup_ref, kxi_ref, kal_ref,
                 rb_ref, r_o, kk_o, v_o, kr0_o, kr1_o, lw0_o, lw1_o, b0_o, b1_o, g_o, gbv_o):
    p = p_ref[0]
    prev_row, next_row = _edge_rows(pp_ref, pn_ref)
    p_prev, p_next = _shifted(p, prev_row, next_row)
    p = p + mu_ref[...] * (0.5 * (p_prev + p_next) - p)
    dr = D_RWKV
    r = p[:, 0:dr]
    k = p[:, dr:2 * dr]
    v = p[:, 2 * dr:3 * dr]
    c0 = 3 * dr
    dd = jnp.tanh(p[:, c0:c0 + 2 * LORA])
    da = p[:, c0 + 2 * LORA:c0 + 4 * LORA]
    dg = jax.nn.sigmoid(p[:, c0 + 4 * LORA:c0 + 4 * LORA + LORA_G])
    g = _dot3(dg, gup_ref[...])
    lw, a = [], []
    for d in range(2):
        lw.append(-DECAY_SCALE * jax.nn.sigmoid(d0_ref[d:d + 1, :] + _dot3(dd[:, d * LORA:(d + 1) * LORA], dup_ref[d])))
        a.append(jax.nn.sigmoid(a0_ref[d:d + 1, :] + _dot3(da[:, d * LORA:(d + 1) * LORA], aup_ref[d])))
    kx = k * kxi_ref[...]
    kal = kal_ref[...]
    kr = [k * (1.0 + (a[d] - 1.0) * kal) for d in range(2)]
    bon = r * (0.5 * (kr[0] + kr[1])) * rb_ref[...]
    ones = _seg_ones()
    g_o[0] = g
    for j in range(N_PAIRS):
        sl = slice(j * LANES, (j + 1) * LANES)
        kxj = kx[:, sl]
        kk = kxj * lax.rsqrt(_dot_sel(kxj * kxj, ones, 2) + 1e-12)
        r_o[0, j] = r[:, sl]
        kk_o[0, j] = kk
        v_o[0, j] = v[:, sl]
        kr0_o[0, j] = kr[0][:, sl]
        kr1_o[0, j] = kr[1][:, sl]
        lw0_o[0, j] = lw[0][:, sl]
        lw1_o[0, j] = lw[1][:, sl]
        b0_o[0, j] = kk * a[0][:, sl]
        b1_o[0, j] = kk * a[1][:, sl]
        gbv_o[0, :, sl] = g[:, sl] * _dot_sel(bon[:, sl], ones, 2) * v[:, sl]


def _rwkv_features(p, prm, tb=256):
    b, t, cols = p.shape
    tb = min(tb, t)
    nb8 = t // SUBLANES
    r8 = tb // SUBLANES
    full = lambda a: pl.BlockSpec(a.shape, lambda i, j: (0,) * a.ndim)
    params = [prm["shift_mu"].reshape(1, cols), prm["decay_up"], prm["decay_0"], prm["a_up"], prm["a_0"],
              prm["g_up"], prm["k_xi"].reshape(1, -1), prm["k_alpha"].reshape(1, -1), prm["r_bonus"].reshape(1, -1)]
    packed = jax.ShapeDtypeStruct((b, N_PAIRS, t, LANES), F32)
    flat = jax.ShapeDtypeStruct((b, t, D_RWKV), F32)
    pspec = pl.BlockSpec((1, N_PAIRS, tb, LANES), lambda i, j: (i, 0, j, 0))
    fspec = pl.BlockSpec((1, tb, D_RWKV), lambda i, j: (i, j, 0))
    return pl.pallas_call(
        _feat_kernel,
        grid=(b, t // tb),
        in_specs=[pl.BlockSpec((1, tb, cols), lambda i, j: (i, j, 0)),
                  pl.BlockSpec((1, SUBLANES, cols), lambda i, j: (i, jnp.maximum(j * r8 - 1, 0), 0)),
                  pl.BlockSpec((1, SUBLANES, cols), lambda i, j: (i, jnp.minimum((j + 1) * r8, nb8 - 1), 0))]
                 + [full(a) for a in params],
        out_specs=[pspec] * 9 + [fspec] * 2,
        out_shape=[packed] * 9 + [flat] * 2,
        compiler_params=_cparams(("parallel", "parallel")),
        name="rwkv_features",
    )(p, p, p, *params)


def _scan_chunks(feats, rev):
    n = SCAN_CHUNK
    m0 = _half_mask((n, LANES))
    m1 = 1.0 - m0
    ti = lax.broadcasted_iota(jnp.int32, (n, n), 0)
    tj = lax.broadcasted_iota(jnp.int32, (n, n), 1)
    tri = ((ti <= tj) if rev else (ti >= tj)).astype(F32)
    si = lax.broadcasted_iota(jnp.int32, (2 * n, 2 * n), 0)
    sj = lax.broadcasted_iota(jnp.int32, (2 * n, 2 * n), 1)
    same = (si // n) == (sj // n)
    ri, rj = si % n, sj % n
    strict = same & ((ri < rj) if rev else (ri > rj))
    incl = same & ((ri <= rj) if rev else (ri >= rj))
    eye = (si == sj).astype(F32)

    def stack(x):
        return jnp.concatenate([x * m0, x * m1], axis=0).astype(BF16)

    pre = []
    for r, lw, kap, bb, kr, v in feats:
        cs = _dot_sel(tri, lw, 3, exact="lhs")
        tot = jnp.sum(lw, axis=0, keepdims=True)
        e_neg = jnp.exp(-cs)
        e_tot = jnp.exp(tot - cs)
        rh = r * jnp.exp(cs)
        pre.append(dict(kh=stack(kap * jnp.exp(cs - lw)), bh=stack(bb * e_neg), kq=stack(kr * e_neg), rh=rh,
                        rhs=stack(rh), kt=stack(kr * e_tot), bt=stack(bb * e_tot), v=stack(v), gl=jnp.exp(tot)))
    nt = (((1,), (1,)), ((), ()))
    amats = [lax.dot_general(jnp.concatenate([p["kh"], p["rhs"]], axis=0), jnp.concatenate([p["bh"], p["kq"]], axis=0),
                             nt, preferred_element_type=F32) for p in pre]
    a1 = [jnp.where(strict, a[:2 * n, :2 * n], 0.0) for a in amats]
    a2 = [jnp.where(strict, a[:2 * n, 2 * n:], 0.0).astype(BF16) for a in amats]
    a4 = [jnp.where(incl, a[2 * n:, :2 * n], 0.0).astype(BF16) for a in amats]
    a3 = [jnp.where(incl, a[2 * n:, 2 * n:], 0.0).astype(BF16) for a in amats]
    tinv = [eye - a for a in a1]
    pw = a1
    for _ in range(5):
        pw = [_bdot(p, p) for p in pw]
        tinv = [t + _bdot(t, p) for t, p in zip(tinv, pw)]
    w_s = [_bdot(a, p["v"]) for a, p in zip(a2, pre)]
    mm = [_bdot(t, jnp.concatenate([p["kh"], w.astype(BF16)], axis=1)).astype(BF16)
          for t, p, w in zip(tinv, pre, w_s)]
    gmat = [_bdot(p["bt"].astype(F32).T, m) for p, m in zip(pre, mm)]
    phi_t = [eye * p["gl"] - g[:, :LANES] for p, g in zip(pre, gmat)]
    psi_t = [_bdot(p["kt"].astype(F32).T, p["v"]) - g[:, LANES:] for p, g in zip(pre, gmat)]
    qy = [_bdot(a, m) for a, m in zip(a4, mm)]
    y0_s = [_bdot(a, p["v"]) - q[:, LANES:] for a, p, q in zip(a3, pre, qy)]
    return [(p["rh"] - (q[:n, :LANES] + q[n:, :LANES]), y0[:n] + y0[n:], ph, ps)
            for p, q, y0, ph, ps in zip(pre, qy, y0_s, phi_t, psi_t)]


def _scan_kernel(r_ref, lw_ref, kap_ref, b_ref, kr_ref, v_ref, s0_ref, y_ref, sT_ref, st_scr, *, rev):
    c = pl.program_id(1)

    @pl.when(c == 0)
    def _():
        st_scr[...] = s0_ref[0]

    n = SCAN_CHUNK
    nsub = r_ref.shape[2] // n
    order = range(nsub - 1, -1, -1) if rev else range(nsub)
    rows = lambda ref, hp, k: ref[0, hp, k * n:(k + 1) * n, :]
    items = [(k, hp) for k in order for hp in range(N_PAIRS)]
    terms = _scan_chunks([tuple(rows(ref, hp, k) for ref in (r_ref, lw_ref, kap_ref, b_ref, kr_ref, v_ref))
                          for k, hp in items], rev)
    states = [st_scr[hp] for hp in range(N_PAIRS)]
    for (k, hp), (qm, y0, phi_t, psi_t) in zip(items, terms):
        y_ref[0, hp, k * n:(k + 1) * n, :] = _bdot(qm, states[hp]) + y0
        states[hp] = _bdot(phi_t, states[hp]) + psi_t
    for hp in range(N_PAIRS):
        st_scr[hp] = states[hp]

    @pl.when(c == pl.num_programs(1) - 1)
    def _():
        sT_ref[0] = st_scr[...]


def _rwkv_scan(r, lw, kap, bb, kr, v, s0, rev, chunks_per_step=4):
    b, _, t, _ = r.shape
    n = SCAN_CHUNK * math.gcd(chunks_per_step, t // SCAN_CHUNK)
    nc = t // n
    cidx = (lambda c: nc - 1 - c) if rev else (lambda c: c)
    fspec = pl.BlockSpec((1, N_PAIRS, n, LANES), lambda i, c: (i, 0, cidx(c), 0))
    sspec = pl.BlockSpec((1, N_PAIRS, LANES, LANES), lambda i, c: (i, 0, 0, 0))
    return pl.pallas_call(
        functools.partial(_scan_kernel, rev=rev),
        grid=(b, nc),
        in_specs=[fspec] * 6 + [sspec],
        out_specs=[fspec, sspec],
        out_shape=[jax.ShapeDtypeStruct(r.shape, F32), jax.ShapeDtypeStruct(s0.shape, F32)],
        scratch_shapes=[pltpu.VMEM((N_PAIRS, LANES, LANES), F32)],
        compiler_params=_cparams(("parallel", "arbitrary")),
        name="rwkv_scan_rev" if rev else "rwkv_scan_fwd",
    )(r, lw, kap, bb, kr, v, s0)


def _evenpost_kernel(y0_ref, y1_ref, g_ref, gbv_ref, pc_ref, pcp_ref, pcn_ref, cw_ref, gng_ref, gnb_ref,
                     conv_o, rw_o):
    pc = pc_ref[0]
    prev_row, next_row = _edge_rows(pcp_ref, pcn_ref)
    dc = D_CONV

    def gated(z):
        return z[:, dc:2 * dc] * z[:, 2 * dc:3 * dc]

    u = gated(pc)
    u_prev, u_next = _shifted(u, gated(prev_row), gated(next_row))
    cw = cw_ref[...]
    conv_o[0] = (pc[:, 0:dc] * (cw[0:1] * u_prev + cw[1:2] * u + cw[2:3] * u_next)).astype(conv_o.dtype)
    ones = _seg_ones() * (1.0 / RWKV_HD)
    for j in range(N_PAIRS):
        sl = slice(j * LANES, (j + 1) * LANES)
        y = y0_ref[0, j] + y1_ref[0, j]
        mu = _dot_sel(y, ones, 2)
        yc = y - mu
        var = _dot_sel(yc * yc, ones, 2)
        yn = yc * lax.rsqrt(var + GN_EPS) * gng_ref[:, sl] + gnb_ref[:, sl]
        rw_o[0, :, sl] = (g_ref[0, :, sl] * yn + gbv_ref[0, :, sl]).astype(rw_o.dtype)


def _even_post(y0, y1, g, gbv, pc, conv_w, gn_g, gn_b, tb=256):
    b, _, t, _ = y0.shape
    tb = min(tb, t)
    nb8 = t // SUBLANES
    r8 = tb // SUBLANES
    c3 = pc.shape[-1]
    pspec = pl.BlockSpec((1, N_PAIRS, tb, LANES), lambda i, j: (i, 0, j, 0))
    fspec = pl.BlockSpec((1, tb, D_RWKV), lambda i, j: (i, j, 0))
    full = lambda a: pl.BlockSpec(a.shape, lambda i, j: (0,) * a.ndim)
    gn_g = gn_g.reshape(1, -1)
    gn_b = gn_b.reshape(1, -1)
    return pl.pallas_call(
        _evenpost_kernel,
        grid=(b, t // tb),
        in_specs=[pspec, pspec, fspec, fspec,
                  pl.BlockSpec((1, tb, c3), lambda i, j: (i, j, 0)),
                  pl.BlockSpec((1, SUBLANES, c3), lambda i, j: (i, jnp.maximum(j * r8 - 1, 0), 0)),
                  pl.BlockSpec((1, SUBLANES, c3), lambda i, j: (i, jnp.minimum((j + 1) * r8, nb8 - 1), 0)),
                  full(conv_w), full(gn_g), full(gn_b)],
        out_specs=[pl.BlockSpec((1, tb, D_CONV), lambda i, j: (i, j, 0)), fspec],
        out_shape=[jax.ShapeDtypeStruct((b, t, D_CONV), BF16), jax.ShapeDtypeStruct((b, t, D_RWKV), BF16)],
        compiler_params=_cparams(("parallel", "parallel")),
        name="even_post",
    )(y0, y1, g, gbv, pc, pc, pc, conv_w, gn_g, gn_b)


def _outproj_kernel(a_ref, b_ref, wa_ref, wb_ref, x_ref, gm_ref, lg_ref, lb_ref, o_ref):
    yx = _bdot(a_ref[0], wa_ref[...]) + _bdot(b_ref[0], wb_ref[...])
    z = ALPHA * x_ref[0] + gm_ref[0] * yx
    o_ref[0] = _layer_norm_rows(z, lg_ref[...], lb_ref[...], LN_EPS)


def _outproj(ma, mb, w_out_bf16, x, gm, ln_g, ln_b, tm=512):
    b, t, d = x.shape
    tm = min(tm, t)
    da, db = ma.shape[-1], mb.shape[-1]
    wa, wb = w_out_bf16[:da], w_out_bf16[da:]
    row = lambda w: pl.BlockSpec((1, tm, w), lambda i, j: (i, j, 0))
    full = lambda a: pl.BlockSpec(a.shape, lambda i, j: (0,) * a.ndim)
    ln_g = ln_g.reshape(1, d)
    ln_b = ln_b.reshape(1, d)
    return pl.pallas_call(
        _outproj_kernel,
        grid=(b, t // tm),
        in_specs=[row(da), row(db), full(wa), full(wb), row(d),
                  pl.BlockSpec((1, 1, d), lambda i, j: (i, 0, 0)), full(ln_g), full(ln_b)],
        out_specs=row(d),
        out_shape=jax.ShapeDtypeStruct((b, t, d), F32),
        compiler_params=_cparams(("parallel", "parallel")),
        name="outproj_norm",
    )(ma, mb, wa, wb, x, gm, ln_g, ln_b)


def _rope_kernel(x_ref, cos_ref, sin_ref, o_ref, *, scale):
    cos = cos_ref[...]
    sin = sin_ref[...]
    lane = lax.broadcasted_iota(jnp.int32, cos.shape, 1)
    first = (lane % (2 * (AXIS_DIM // 2))) < (AXIS_DIM // 2)
    half = AXIS_DIM // 2
    for j in range(DIFF_HEADS):
        sl = slice(j * LANES, (j + 1) * LANES)
        xs = x_ref[0, :, sl]
        partner = jnp.where(first, -pltpu.roll(xs, LANES - half, axis=1), pltpu.roll(xs, half, axis=1))
        o_ref[0, :, sl] = ((xs * cos + partner * sin) * scale).astype(o_ref.dtype)


def _rope(x, cos, sin, scale, tb=512):
    b, t, w = x.shape
    tb = min(tb, t)
    return pl.pallas_call(
        functools.partial(_rope_kernel, scale=scale),
        grid=(b, t // tb),
        in_specs=[pl.BlockSpec((1, tb, w), lambda i, j: (i, j, 0)),
                  pl.BlockSpec((tb, LANES), lambda i, j: (j, 0)),
                  pl.BlockSpec((tb, LANES), lambda i, j: (j, 0))],
        out_specs=pl.BlockSpec((1, tb, w), lambda i, j: (i, j, 0)),
        out_shape=jax.ShapeDtypeStruct((b, t, w), BF16),
        compiler_params=_cparams(("parallel", "parallel")),
        name="axial_rope",
    )(x, cos, sin)


def _rope_tables(t):
    rows = t // GRID_W
    row = jnp.repeat(jnp.arange(rows), GRID_W).astype(F32)
    col = jnp.tile(jnp.arange(GRID_W), rows).astype(F32)
    inv = ROPE_BASE ** (-jnp.arange(0, AXIS_DIM, 2, dtype=F32) / AXIS_DIM)
    ang_r = row[:, None] * inv
    ang_c = col[:, None] * inv
    ang = jnp.concatenate([ang_r, ang_r, ang_c, ang_c], axis=-1)
    ang = jnp.concatenate([ang, ang], axis=-1)
    return jnp.cos(ang), jnp.sin(ang)


def _attn_kernel(lam_ref, q_ref, k_ref, v_ref, g_ref, o_ref, *, tk, out_scale):
    q = q_ref[0]
    tq = q.shape[0]
    hm = _half_mask(q.shape, BF16)
    q1 = q * hm
    qs = (q1, q - q1)
    nk = k_ref.shape[1] // tk
    nsl = tk // LANES

    def scores(qh, j):
        kb = k_ref[0, pl.ds(pl.multiple_of(j * tk, tk), tk), :]
        return lax.dot_general(qh, kb, (((1,), (1,)), ((), ())), preferred_element_type=F32)

    def col(s, c):
        return s[:, c * LANES:(c + 1) * LANES]

    def body(j, carry):
        vb = v_ref[0, pl.ds(pl.multiple_of(j * tk, tk), tk), :]
        ss = [scores(qh, j) for qh in qs]
        out = []
        for s, (m, ls, acc) in zip(ss, (carry[:3], carry[3:])):
            mx = col(s, 0)
            for c in range(1, nsl):
                mx = jnp.maximum(mx, col(s, c))
            m_new = jnp.maximum(m, jnp.broadcast_to(jnp.max(mx, axis=-1, keepdims=True), (tq, LANES)))
            corr = jnp.exp2(m - m_new)
            ps = [jnp.exp2(col(s, c) - m_new) for c in range(nsl)]
            ls = corr * ls
            for pc in ps:
                ls = ls + pc
            p = jnp.concatenate([pc.astype(BF16) for pc in ps], axis=1)
            acc = corr * acc + jnp.dot(p, vb, preferred_element_type=F32)
            out += [m_new, ls, acc]
        return tuple(out)

    neg = jnp.full((tq, LANES), -jnp.inf, F32)
    zero = jnp.zeros((tq, LANES), F32)
    _, ls1, acc1, _, ls2, acc2 = lax.fori_loop(0, nk, body, (neg, zero, zero, neg, zero, zero), unroll=True)
    l1 = jnp.sum(ls1, axis=-1, keepdims=True)
    l2 = jnp.sum(ls2, axis=-1, keepdims=True)
    o = acc1 / l1 - lam_ref[0] * (acc2 / l2)
    o = o * lax.rsqrt(jnp.mean(o * o, axis=-1, keepdims=True) + RMS_EPS) * g_ref[...] * out_scale
    o_ref[0] = o.astype(o_ref.dtype)


def _diff_attention(q, k, v, lam, subln_g, lam_init, tq=512, tk=1408):
    b, t, w = q.shape
    tkk = k.shape[1]
    tq = min(tq, t)
    tk = math.gcd(tk, tkk)
    assert tk % LANES == 0 and t % tq == 0
    subln_g = subln_g.reshape(1, DIFF_VD)
    return pl.pallas_call(
        functools.partial(_attn_kernel, tk=tk, out_scale=1.0 - lam_init),
        grid=(b, DIFF_HEADS, t // tq),
        in_specs=[pl.BlockSpec(memory_space=pltpu.SMEM),
                  pl.BlockSpec((1, tq, LANES), lambda i, h, j: (i, j, h)),
                  pl.BlockSpec((1, tkk, LANES), lambda i, h, j: (i, 0, h)),
                  pl.BlockSpec((1, tkk, LANES), lambda i, h, j: (i, 0, h)),
                  pl.BlockSpec((1, DIFF_VD), lambda i, h, j: (0, 0))],
        out_specs=pl.BlockSpec((1, tq, LANES), lambda i, h, j: (i, j, h)),
        out_shape=jax.ShapeDtypeStruct((b, t, w), BF16),
        compiler_params=_cparams(("parallel", "parallel", "parallel")),
        name="diff_attention",
    )(lam, q, k, v, subln_g)


def _gmlp_kernel(p_ref, lg_ref, lb_ref, ws_ref, bs_ref, o_ref):
    p = p_ref[0]
    tb = p.shape[0]
    ge = 0.5 * p * (1.0 + lax.erf(p * (2.0 ** -0.5)))
    u = ge[:, :D_GMLP]
    v = _layer_norm_rows(ge[:, D_GMLP:], lg_ref[...], lb_ref[...], LN_EPS)
    hm = _half_mask((GMLP_CHUNK, LANES))
    for c in range(tb // GMLP_CHUNK):
        rs = slice(c * GMLP_CHUNK, (c + 1) * GMLP_CHUNK)
        for j in range(D_GMLP // LANES):
            sl = slice(j * LANES, (j + 1) * LANES)
            vc = v[rs, sl]
            va = vc * hm
            mixed = _bdot(ws_ref[2 * j], va) + _bdot(ws_ref[2 * j + 1], vc - va) + bs_ref[:, sl]
            o_ref[0, rs, sl] = (u[rs, sl] * mixed).astype(o_ref.dtype)


def _chunk_gmlp(p, ln_g, ln_b, ws, bs, tb=256):
    b, t, w = p.shape
    tb = min(tb, t)
    ln_g = ln_g.reshape(1, -1)
    ln_b = ln_b.reshape(1, -1)
    bs_t = jnp.repeat(jnp.transpose(bs), D_GMLP // ws.shape[0], axis=1)
    full = lambda a: pl.BlockSpec(a.shape, lambda i, j: (0,) * a.ndim)
    return pl.pallas_call(
        _gmlp_kernel,
        grid=(b, t // tb),
        in_specs=[pl.BlockSpec((1, tb, w), lambda i, j: (i, j, 0)), full(ln_g), full(ln_b), full(ws), full(bs_t)],
        out_specs=pl.BlockSpec((1, tb, D_GMLP), lambda i, j: (i, j, 0)),
        out_shape=jax.ShapeDtypeStruct((b, t, D_GMLP), BF16),
        compiler_params=_cparams(("parallel", "parallel")),
        name="chunk_gmlp",
    )(p, ln_g, ln_b, ws, bs_t)


def _router_kernel(x_ref, sc_ref, sh_ref, wr_ref, h_o, aff_o):
    h = x_ref[0] * (1.0 + sc_ref[0]) + sh_ref[0]
    h_o[0] = h.astype(h_o.dtype)
    logits = lax.dot_general(wr_ref[...], h, (((1,), (1,)), ((), ())), preferred_element_type=F32, precision=HI)
    m = jnp.max(logits, axis=0, keepdims=True)
    e = jnp.exp(logits - m)
    aff_o[0] = e / jnp.sum(e, axis=0, keepdims=True)


def _router(x, sc, sh, w_router_t, tm=512):
    b, t, d = x.shape
    tm = min(tm, t)
    e = w_router_t.shape[0]
    return pl.pallas_call(
        _router_kernel,
        grid=(b, t // tm),
        in_specs=[pl.BlockSpec((1, tm, d), lambda i, j: (i, j, 0)),
                  pl.BlockSpec((1, 1, d), lambda i, j: (i, 0, 0)),
                  pl.BlockSpec((1, 1, d), lambda i, j: (i, 0, 0)),
                  pl.BlockSpec((e, d), lambda i, j: (0, 0))],
        out_specs=[pl.BlockSpec((1, tm, d), lambda i, j: (i, j, 0)),
                   pl.BlockSpec((1, e, tm), lambda i, j: (i, 0, j))],
        out_shape=[jax.ShapeDtypeStruct((b, t, d), BF16), jax.ShapeDtypeStruct((b, e, t), F32)],
        compiler_params=_cparams(("parallel", "parallel")),
        name="router",
    )(x, sc, sh, w_router_t)


ROUTE_BLOCK = 128
STARTS_PAD = LANES


def _lane_cumsum(mask):
    e, t = mask.shape
    i = lax.broadcasted_iota(jnp.int32, (LANES, LANES), 0)
    j = lax.broadcasted_iota(jnp.int32, (LANES, LANES), 1)
    upper = (i <= j).astype(BF16)
    x = jnp.where(mask, 1.0, 0.0).astype(BF16)
    off = jnp.zeros((e, 1), F32)
    out = []
    for c in range(t // LANES):
        blk = jnp.dot(x[:, c * LANES:(c + 1) * LANES], upper, preferred_element_type=F32) + off
        out.append(blk)
        off = blk[:, LANES - 1:LANES]
    return jnp.concatenate(out, axis=1)


def _select_kernel(aff_ref, pos_ref, starts_ref, *, cap):
    a = aff_ref[0]
    e, t = a.shape
    bits = pltpu.bitcast(a, jnp.int32)

    def search(i, thr):
        cand = thr | jnp.left_shift(jnp.int32(1), 30 - i)
        cnt = jnp.sum(jnp.where(bits >= cand, 1.0, 0.0), axis=1, keepdims=True)
        return jnp.where(cnt >= cap, cand, thr)

    thr = lax.fori_loop(0, 31, search, jnp.zeros((e, 1), jnp.int32))
    gt = bits > thr
    eq = bits == thr
    need = cap - jnp.sum(jnp.where(gt, 1.0, 0.0), axis=1, keepdims=True)
    sel = gt | (eq & (_lane_cumsum(eq) <= need))
    csel = _lane_cumsum(sel)
    pos_ref[0] = jnp.where(sel, csel - 1.0, -1.0).astype(jnp.int32)
    ti = lax.broadcasted_iota(jnp.int32, (t, STARTS_PAD), 0)
    ki = lax.broadcasted_iota(jnp.int32, (t, STARTS_PAD), 1)
    pick = jnp.where(ti == ki * ROUTE_BLOCK - 1, 1.0, 0.0)
    starts_ref[0] = (_dot_sel(csel, pick, 2) + 0.5).astype(jnp.int32)


def _select(aff, cap):
    b, e, t = aff.shape
    assert t % ROUTE_BLOCK == 0 and t // ROUTE_BLOCK < STARTS_PAD
    return pl.pallas_call(
        functools.partial(_select_kernel, cap=cap),
        grid=(b,),
        in_specs=[pl.BlockSpec((1, e, t), lambda i: (i, 0, 0))],
        out_specs=[pl.BlockSpec((1, e, t), lambda i: (i, 0, 0)), pl.BlockSpec((1, e, STARTS_PAD), lambda i: (i, 0, 0))],
        out_shape=[jax.ShapeDtypeStruct((b, e, t), jnp.int32), jax.ShapeDtypeStruct((b, e, STARTS_PAD), jnp.int32)],
        compiler_params=_cparams(("parallel",)),
        name="expert_select",
    )(aff)


def _window_start(start, align, cap, width):
    s = lax.shift_left(lax.shift_right_logical(start, align.bit_length() - 1), align.bit_length() - 1)
    return pl.multiple_of(jnp.minimum(s, cap - width), align)


def _gather_ffn_kernel(starts_ref, pos_ref, aff_ref, h_ref, w1_ref, w3_ref, w2_ref, y_ref, xs_scr, gate_scr, *,
                       width, rows, cap):
    ei, bi, k = pl.program_id(0), pl.program_id(1), pl.program_id(2)
    ne, nb = pl.num_programs(0), pl.num_programs(1)
    nsub = h_ref.shape[1] // ROUTE_BLOCK
    merged = xs_scr.shape[0] > cap
    base = pl.multiple_of(bi * cap, SUBLANES) if merged else 0

    @pl.when(k == 0)
    def _():
        xs_scr[pl.ds(base, cap), :] = jnp.zeros((cap, xs_scr.shape[1]), F32)
        gate_scr[pl.ds(base, cap), :] = jnp.zeros((cap, LANES), F32)

    def copy_rows(sb, start, w):
        s8 = _window_start(start, SUBLANES, cap, w)
        cols = slice(sb * ROUTE_BLOCK, (sb + 1) * ROUTE_BLOCK)
        jrow = lax.broadcasted_iota(jnp.int32, (w, ROUTE_BLOCK), 0)
        hit = pos_ref[0, :, cols] - s8 == jrow
        xs_scr[pl.ds(base + s8, w), :] += jnp.dot(jnp.where(hit, 1.0, 0.0).astype(BF16), h_ref[0, cols, :],
                                                  preferred_element_type=F32)
        gate = jnp.sum(jnp.where(hit, aff_ref[0, :, cols], 0.0), axis=1, keepdims=True)
        gate_scr[pl.ds(base + s8, w), :] += jnp.broadcast_to(gate, (w, LANES))

    for sb in range(nsub):
        copy_rows(sb, starts_ref[(bi * ne + ei) * STARTS_PAD + k * nsub + sb], width)

    last = k == pl.num_programs(2) - 1

    @pl.when(last & (bi == nb - 1) if merged else last)
    def _():
        for r in range(xs_scr.shape[0] // rows):
            x = xs_scr[r * rows:(r + 1) * rows, :].astype(BF16)
            h1 = jnp.dot(x, w1_ref[0], preferred_element_type=F32)
            h3 = jnp.dot(x, w3_ref[0], preferred_element_type=F32)
            hid = (h1 * jax.nn.sigmoid(h1)) * h3
            y = jnp.dot(hid.astype(BF16), w2_ref[0], preferred_element_type=F32)
            y = (y * gate_scr[r * rows:(r + 1) * rows, 0:1]).astype(y_ref.dtype)
            for s in range(max(rows // cap, 1)):
                n = min(rows, cap)
                row0 = r * rows + s * cap
                y_ref[row0 // cap, 0, row0 % cap:row0 % cap + n, :] = y[s * cap:s * cap + n]


def _gather_ffn(h, pos, aff, starts, w1, w3, w2, cap, tch=2048, rows=256):
    b, t, d = h.shape
    e, _, f = w1.shape
    tch = min(tch, t)
    merged = b * cap <= rows
    slots = b if merged else 1
    rows = min(rows, slots * cap)
    assert (slots * cap) % rows == 0 and (rows % cap == 0 or cap % rows == 0)
    width = min(ROUTE_BLOCK + SUBLANES, cap)
    grid_spec = pltpu.PrefetchScalarGridSpec(
        num_scalar_prefetch=1,
        grid=(e, b, t // tch),
        in_specs=[pl.BlockSpec((1, 1, tch), lambda ei, bi, k, s: (bi * e + ei, 0, k)),
                  pl.BlockSpec((1, 1, tch), lambda ei, bi, k, s: (bi * e + ei, 0, k)),
                  pl.BlockSpec((1, tch, d), lambda ei, bi, k, s: (bi, k, 0)),
                  pl.BlockSpec((1, d, f), lambda ei, bi, k, s: (ei, 0, 0)),
                  pl.BlockSpec((1, d, f), lambda ei, bi, k, s: (ei, 0, 0)),
                  pl.BlockSpec((1, f, d), lambda ei, bi, k, s: (ei, 0, 0))],
        out_specs=pl.BlockSpec((slots, 1, cap, d), (lambda ei, bi, k, s: (0, ei, 0, 0)) if merged
                               else (lambda ei, bi, k, s: (bi, ei, 0, 0))),
        scratch_shapes=[pltpu.VMEM((slots * cap, d), F32), pltpu.VMEM((slots * cap, LANES), F32)],
    )
    return pl.pallas_call(
        functools.partial(_gather_ffn_kernel, width=width, rows=rows, cap=cap),
        grid_spec=grid_spec,
        out_shape=jax.ShapeDtypeStruct((b, e, cap, d), BF16),
        compiler_params=_cparams(("parallel", "arbitrary", "arbitrary") if merged
                                 else ("parallel", "parallel", "arbitrary")),
        name="expert_gather_ffn",
    )(starts.reshape(-1), pos.reshape(b * e, 1, t), aff.reshape(b * e, 1, t), h, w1, w3, w2)


def _combine_kernel(starts_ref, post_ref, y_ref, x_ref, gf_ref, lg_ref, lb_ref, o_ref, acc_scr, *, width, group):
    bi, i, eg = pl.program_id(0), pl.program_id(1), pl.program_id(2)
    ne = pl.num_programs(2) * group
    cap = y_ref.shape[2]
    nsub = x_ref.shape[1] // ROUTE_BLOCK

    @pl.when(eg == 0)
    def _():
        acc_scr[...] = jnp.zeros_like(acc_scr)

    lane = lax.broadcasted_iota(jnp.int32, (ROUTE_BLOCK, ne), 1)
    jcol = lax.broadcasted_iota(jnp.int32, (ROUTE_BLOCK, width), 1)
    for sb in range(nsub):
        rs = slice(sb * ROUTE_BLOCK, (sb + 1) * ROUTE_BLOCK)
        pblk = post_ref[0, rs, :].astype(F32)
        total = None
        for j in range(group):
            ei = eg * group + j
            start = starts_ref[(bi * ne + ei) * STARTS_PAD + i * nsub + sb]
            s16 = _window_start(start, 2 * SUBLANES, cap, width)
            pcol = jnp.sum(jnp.where(lane == ei, pblk, 0.0), axis=1, keepdims=True)
            onehot = jnp.where(pcol.astype(jnp.int32) - s16 == jcol, 1.0, 0.0).astype(BF16)
            part = jnp.dot(onehot, y_ref[0, j, pl.ds(s16, width), :], preferred_element_type=F32)
            total = part if total is None else total + part
        acc_scr[rs, :] += total

    @pl.when(eg == pl.num_programs(2) - 1)
    def _():
        z = ALPHA * x_ref[0] + gf_ref[0] * acc_scr[...]
        o_ref[0] = _layer_norm_rows(z, lg_ref[...], lb_ref[...], LN_EPS)


def _combine(x, y, pos_t, starts, gf, ln_g, ln_b, tbo=1024, group=4):
    b, t, d = x.shape
    _, e, cap, _ = y.shape
    tbo = min(tbo, t)
    width = min(ROUTE_BLOCK + 2 * SUBLANES, cap)
    grid_spec = pltpu.PrefetchScalarGridSpec(
        num_scalar_prefetch=1,
        grid=(b, t // tbo, e // group),
        in_specs=[pl.BlockSpec((1, tbo, e), lambda bi, i, eg, s: (bi, i, 0)),
                  pl.BlockSpec((1, group, cap, d), lambda bi, i, eg, s: (bi, eg, 0, 0)),
                  pl.BlockSpec((1, tbo, d), lambda bi, i, eg, s: (bi, i, 0)),
                  pl.BlockSpec((1, 1, d), lambda bi, i, eg, s: (bi, 0, 0)),
                  pl.BlockSpec((1, d), lambda bi, i, eg, s: (0, 0)),
                  pl.BlockSpec((1, d), lambda bi, i, eg, s: (0, 0))],
        out_specs=pl.BlockSpec((1, tbo, d), lambda bi, i, eg, s: (bi, i, 0)),
        scratch_shapes=[pltpu.VMEM((tbo, d), F32)],
    )
    return pl.pallas_call(
        functools.partial(_combine_kernel, width=width, group=group),
        grid_spec=grid_spec,
        out_shape=jax.ShapeDtypeStruct((b, t, d), F32),
        compiler_params=_cparams(("parallel", "parallel", "arbitrary")),
        name="expert_combine_norm",
    )(starts.reshape(-1), pos_t, y, x, gf, ln_g.reshape(1, d), ln_b.reshape(1, d))


def _cast_kernel(x_ref, o_ref):
    o_ref[...] = x_ref[...].astype(o_ref.dtype)


def _to_bf16(w, l):
    _, e, r, c = w.shape
    return pl.pallas_call(
        _cast_kernel,
        grid=(e,),
        in_specs=[pl.BlockSpec((None, 1, r, c), lambda i: (l, i, 0, 0))],
        out_specs=pl.BlockSpec((1, r, c), lambda i: (i, 0, 0)),
        out_shape=jax.ShapeDtypeStruct((e, r, c), BF16),
        compiler_params=_cparams(("parallel",)),
        name="cast_bf16",
    )(w)


def _moe(x, sc, sh, gf, w_router_t, w1, w3, w2, ln_g, ln_b):
    t = x.shape[1]
    cap = EC_FACTOR * t // N_EXPERTS
    h, aff = _router(x, sc, sh, w_router_t)
    pos, starts = _select(aff, cap)
    y = _gather_ffn(h, pos, aff, starts, w1, w3, w2, cap)
    return _combine(x, y, jnp.transpose(pos, (0, 2, 1)), starts, gf, ln_g, ln_b)


def _even_layer(hx_in, hc_in, prm, ctx_out):
    w_in = prm["w_in"]
    splits = (3 * D_CONV, RWKV_COLS)
    pcx, prx = _inproj(*hx_in, w_in, splits)
    pcc, prc = _inproj(*hc_in, w_in, splits)
    fx = _rwkv_features(prx, prm)
    fc = _rwkv_features(prc, prm)
    b = prx.shape[0]
    zero = jnp.zeros((b, N_PAIRS, LANES, LANES), F32)
    ys_x, ys_c = [], []
    for d, rev in ((0, False), (1, True)):
        def args(f):
            r, kk, v, kr0, kr1, lw0, lw1, b0, b1 = f[:9]
            return (r, (lw0, lw1)[d], kk, (b0, b1)[d], (kr0, kr1)[d], v)
        yc, s_ctx = _rwkv_scan(*args(fc), zero, rev)
        yx, _ = _rwkv_scan(*args(fx), s_ctx, rev)
        ys_x.append(yx)
        ys_c.append(yc)
    out_x = _even_post(ys_x[0], ys_x[1], fx[9], fx[10], pcx, prm["conv_w"], prm["gn_g"], prm["gn_b"])
    out_c = None
    if ctx_out:
        out_c = _even_post(ys_c[0], ys_c[1], fc[9], fc[10], pcc, prm["conv_w"], prm["gn_g"], prm["gn_b"])
    return out_c, out_x


def _odd_layer(hx_in, hc_in, prm, lam_init, ctx_out):
    w_in = prm["w_in"]
    splits = (D_DIFF, D_DIFF, D_DIFF, 2 * D_GMLP)
    qx, kx, vx, gx = _inproj(*hx_in, w_in, splits)
    qc, kc, vc, gc = _inproj(*hc_in, w_in, splits)
    t = qx.shape[1]
    cos, sin = _rope_tables(t)
    qscale = DIFF_HD ** -0.5 * math.log2(math.e)
    q_rot = _rope(qx, cos, sin, qscale)
    k_rot = _rope(kx, cos, sin, 1.0)
    k_all = jnp.concatenate([kc.astype(BF16), k_rot], axis=1)
    v_all = jnp.concatenate([vc.astype(BF16), vx.astype(BF16)], axis=1)
    lam = (jnp.exp(jnp.sum(prm["lam_q1"] * prm["lam_k1"])) - jnp.exp(jnp.sum(prm["lam_q2"] * prm["lam_k2"]))
           + lam_init).reshape(1).astype(F32)
    att_x = _diff_attention(q_rot, k_all, v_all, lam, prm["subln_g"], lam_init)
    gm_x = _chunk_gmlp(gx, prm["gmlp_ln_g"], prm["gmlp_ln_b"], prm["gmlp_ws"], prm["gmlp_bs"])
    out_c = None
    if ctx_out:
        att_c = _diff_attention((qc * qscale).astype(BF16), kc.astype(BF16), vc.astype(BF16), lam,
                                prm["subln_g"], lam_init)
        gm_c = _chunk_gmlp(gc, prm["gmlp_ln_g"], prm["gmlp_ln_b"], prm["gmlp_ws"], prm["gmlp_bs"])
        out_c = (att_c, gm_c)
    return out_c, (att_x, gm_x)


def kernel(x, c, ctx, c_ctx, w_mod, b_mod, ln_g, ln_b, even_w_in, even_w_out, conv_w, shift_mu, decay_up, decay_0, a_up, a_0, g_up, k_xi, k_alpha, r_bonus, gn_g, gn_b, odd_w_in, odd_w_out, lam_q1, lam_k1, lam_q2, lam_k2, subln_g, gmlp_ln_g, gmlp_ln_b, gmlp_ws, gmlp_bs, w_router, w_e1, w_e3, w_e2):
    bsz, _, d = x.shape
    assert bsz <= SUBLANES - 1
    c8 = jnp.zeros((SUBLANES, d), F32).at[:bsz].set(c).at[bsz].set(c_ctx)
    mod = _modulation(c8, w_mod, b_mod)
    for l in range(DEPTH):
        ctx_out = l < DEPTH - 1
        i = l // 2
        mx = mod[l, :bsz].reshape(bsz, 1, 6, d)
        mc = jnp.broadcast_to(mod[l, bsz].reshape(1, 1, 6, d), (bsz, 1, 6, d))
        part = lambda m, n: m[:, :, n]
        hx_in = (x, part(mx, 1), part(mx, 0))
        hc_in = (ctx, part(mc, 1), part(mc, 0))
        if l % 2 == 0:
            prm = dict(w_in=even_w_in[i].astype(BF16), conv_w=conv_w[i], shift_mu=shift_mu[i],
                       decay_up=decay_up[i], decay_0=decay_0[i], a_up=a_up[i], a_0=a_0[i], g_up=g_up[i],
                       k_xi=k_xi[i], k_alpha=k_alpha[i], r_bonus=r_bonus[i], gn_g=gn_g[i], gn_b=gn_b[i])
            out_c, out_x = _even_layer(hx_in, hc_in, prm, ctx_out)
            w_out = even_w_out[i].astype(BF16)
        else:
            lam_init = 0.8 - 0.6 * math.exp(-0.3 * l)
            prm = dict(w_in=odd_w_in[i].astype(BF16), lam_q1=lam_q1[i], lam_k1=lam_k1[i], lam_q2=lam_q2[i],
                       lam_k2=lam_k2[i], subln_g=subln_g[i], gmlp_ln_g=gmlp_ln_g[i], gmlp_ln_b=gmlp_ln_b[i],
                       gmlp_ws=gmlp_ws[i], gmlp_bs=gmlp_bs[i])
            out_c, out_x = _odd_layer(hx_in, hc_in, prm, lam_init, ctx_out)
            w_out = odd_w_out[i].astype(BF16)
        wr_t = jnp.transpose(w_router[l])
        w1, w3, w2 = _to_bf16(w_e1, l), _to_bf16(w_e3, l), _to_bf16(w_e2, l)
        x = _outproj(out_x[0], out_x[1], w_out, x, part(mx, 2), ln_g[l, 0], ln_b[l, 0])
        x = _moe(x, part(mx, 4), part(mx, 3), part(mx, 5), wr_t, w1, w3, w2, ln_g[l, 1], ln_b[l, 1])
        if ctx_out:
            ctx = _outproj(out_c[0], out_c[1], w_out, ctx, part(mc, 2), ln_g[l, 0], ln_b[l, 0])
            ctx = _moe(ctx, part(mc, 4), part(mc, 3), part(mc, 5), wr_t, w1, w3, w2, ln_g[l, 1], ln_b[l, 1])
    return x
```

```python
import functools
import math

import jax
import jax.numpy as jnp
from jax import lax
from jax.experimental import pallas as pl
from jax.experimental.pallas import tpu as pltpu

F32 = jnp.float32
BF16 = jnp.bfloat16
HI = lax.Precision.HIGHEST

D_MODEL = 1024
DEPTH = 4
GRID_W = 64
D_CONV = 256
RWKV_HEADS = 12
RWKV_HD = 64
D_RWKV = RWKV_HEADS * RWKV_HD
LORA = 64
LORA_G = 128
RWKV_COLS = 3 * D_RWKV + 4 * LORA + LORA_G
DECAY_SCALE = math.exp(-0.5)
GN_EPS = 64e-5
N_PAIRS = RWKV_HEADS // 2
SCAN_CHUNK = 64
DIFF_HEADS = 6
DIFF_HD = 64
DIFF_VD = 2 * DIFF_HD
D_DIFF = DIFF_HEADS * DIFF_VD
AXIS_DIM = DIFF_HD // 2
ROPE_BASE = 10000.0
D_GMLP = 256
GMLP_CHUNK = 128
N_EXPERTS = 16
EC_FACTOR = 2
D_EXPERT = 2048
ALPHA = (2.0 * DEPTH) ** 0.25
LN_EPS = 1e-5
RMS_EPS = 1e-5

LANES = 128
SUBLANES = 8
VMEM_LIMIT = 56 * 1024 * 1024


def _cparams(sem):
    return pltpu.CompilerParams(dimension_semantics=sem, vmem_limit_bytes=VMEM_LIMIT)


def _bdot(a, b):
    return jnp.dot(a.astype(BF16), b.astype(BF16), preferred_element_type=F32)


def _hdot(a, b):
    return jnp.dot(a, b, preferred_element_type=F32, precision=HI)


def _split_bf16(x, terms):
    parts = []
    for _ in range(terms):
        p = x.astype(BF16)
        parts.append(p)
        x = x - p.astype(F32)
    return parts


def _dot3(a, b):
    a_hi, a_lo = _split_bf16(a, 2)
    b_hi, b_lo = _split_bf16(b, 2)
    d = lambda x, y: jnp.dot(x, y, preferred_element_type=F32)
    return d(a_hi, b_hi) + (d(a_hi, b_lo) + d(a_lo, b_hi))


def _dot_sel(a, b, terms, exact="rhs"):
    d = lambda x, y: jnp.dot(x, y, preferred_element_type=F32)
    if exact == "rhs":
        bb = b.astype(BF16)
        outs = [d(p, bb) for p in _split_bf16(a, terms)]
    else:
        ab = a.astype(BF16)
        outs = [d(ab, p) for p in _split_bf16(b, terms)]
    out = outs[-1]
    for o in reversed(outs[:-1]):
        out = out + o
    return out


def _half_mask(shape, dtype=F32):
    lane = lax.broadcasted_iota(jnp.int32, shape, len(shape) - 1)
    return (lane < 64).astype(dtype)


def _seg_ones():
    i = lax.broadcasted_iota(jnp.int32, (LANES, LANES), 0)
    j = lax.broadcasted_iota(jnp.int32, (LANES, LANES), 1)
    return ((i // 64) == (j // 64)).astype(F32)


def _layer_norm_rows(z, g, b, eps):
    mu = jnp.mean(z, axis=-1, keepdims=True)
    zc = z - mu
    var = jnp.mean(zc * zc, axis=-1, keepdims=True)
    return zc * lax.rsqrt(var + eps) * g + b


def _mod_kernel(c_ref, w_ref, b_ref, o_ref):
    a = c_ref[...]
    a = a * jax.nn.sigmoid(a)
    o_ref[0] = _hdot(a, w_ref[0]) + b_ref[0]


def _modulation(c8, w_mod, b_mod):
    tn = 1536
    nl, d, n6 = w_mod.shape
    return pl.pallas_call(
        _mod_kernel,
        grid=(nl, n6 // tn),
        in_specs=[pl.BlockSpec((8, d), lambda l, j: (0, 0)),
                  pl.BlockSpec((1, d, tn), lambda l, j: (l, 0, j)),
                  pl.BlockSpec((1, 1, tn), lambda l, j: (l, 0, j))],
        out_specs=pl.BlockSpec((1, 8, tn), lambda l, j: (l, 0, j)),
        out_shape=jax.ShapeDtypeStruct((nl, 8, n6), F32),
        compiler_params=_cparams(("parallel", "parallel")),
        name="modulation",
    )(c8, w_mod, b_mod.reshape(nl, 1, n6))


def _inproj_kernel(x_ref, sc_ref, sh_ref, w_ref, *o_refs, splits):
    h = x_ref[0] * (1.0 + sc_ref[0]) + sh_ref[0]
    y = _bdot(h, w_ref[...])
    off = 0
    for o, s in zip(o_refs, splits):
        o[0] = y[:, off:off + s].astype(o.dtype)
        off += s


def _inproj(x, sc, sh, w_bf16, splits, tm=512):
    b, t, d = x.shape
    tm = min(tm, t)
    dout = w_bf16.shape[1]
    return pl.pallas_call(
        functools.partial(_inproj_kernel, splits=splits),
        grid=(b, t // tm),
        in_specs=[pl.BlockSpec((1, tm, d), lambda i, j: (i, j, 0)),
                  pl.BlockSpec((1, 1, d), lambda i, j: (i, 0, 0)),
                  pl.BlockSpec((1, 1, d), lambda i, j: (i, 0, 0)),
                  pl.BlockSpec((d, dout), lambda i, j: (0, 0))],
        out_specs=[pl.BlockSpec((1, tm, s), lambda i, j: (i, j, 0)) for s in splits],
        out_shape=[jax.ShapeDtypeStruct((b, t, s), F32) for s in splits],
        compiler_params=_cparams(("parallel", "parallel")),
        name="inproj",
    )(x, sc, sh, w_bf16)


def _shifted(p, prev_row, next_row):
    tb = p.shape[0]
    row = lax.broadcasted_iota(jnp.int32, p.shape, 0)
    p_prev = jnp.where(row == 0, prev_row, pltpu.roll(p, 1, axis=0))
    p_next = jnp.where(row == tb - 1, next_row, pltpu.roll(p, tb - 1, axis=0))
    return p_prev, p_next


def _edge_rows(pp_ref, pn_ref):
    i = pl.program_id(1)
    nb = pl.num_programs(1)
    prev_row = jnp.where(i > 0, pp_ref[0, SUBLANES - 1:SUBLANES, :], 0.0)
    next_row = jnp.where(i < nb - 1, pn_ref[0, 0:1, :], 0.0)
    return prev_row, next_row


def _feat_kernel(p_ref, pp_ref, pn_ref, mu_ref, dup_ref, d0_ref, aup_ref, a0_ref, gup_ref, kxi_ref, kal_ref,
                 rb_ref, r_o, kk_o, v_o, kr0_o, kr1_o, lw0_o, lw1_o, b0_o, b1_o, g_o, gbv_o):
    p = p_ref[0]
    prev_row, next_row = _edge_rows(pp_ref, pn_ref)
    p_prev, p_next = _shifted(p, prev_row, next_row)
    p = p + mu_ref[...] * (0.5 * (p_prev + p_next) - p)
    dr = D_RWKV
    r = p[:, 0:dr]
    k = p[:, dr:2 * dr]
    v = p[:, 2 * dr:3 * dr]
    c0 = 3 * dr
    dd = jnp.tanh(p[:, c0:c0 + 2 * LORA])
    da = p[:, c0 + 2 * LORA:c0 + 4 * LORA]
    dg = jax.nn.sigmoid(p[:, c0 + 4 * LORA:c0 + 4 * LORA + LORA_G])
    g = _dot3(dg, gup_ref[...])
    lw, a = [], []
    for d in range(2):
        lw.append(-DECAY_SCALE * jax.nn.sigmoid(d0_ref[d:d + 1, :] + _dot3(dd[:, d * LORA:(d + 1) * LORA], dup_ref[d])))
        a.append(jax.nn.sigmoid(a0_ref[d:d + 1, :] + _dot3(da[:, d * LORA:(d + 1) * LORA], aup_ref[d])))
    kx = k * kxi_ref[...]
    kal = kal_ref[...]
    kr = [k * (1.0 + (a[d] - 1.0) * kal) for d in range(2)]
    bon = r * (0.5 * (kr[0] + kr[1])) * rb_ref[...]
    ones = _seg_ones()
    g_o[0] = g
    for j in range(N_PAIRS):
        sl = slice(j * LANES, (j + 1) * LANES)
        kxj = kx[:, sl]
        kk = kxj * lax.rsqrt(_dot_sel(kxj * kxj, ones, 2) + 1e-12)
        r_o[0, j] = r[:, sl]
        kk_o[0, j] = kk
        v_o[0, j] = v[:, sl]
        kr0_o[0, j] = kr[0][:, sl]
        kr1_o[0, j] = kr[1][:, sl]
        lw0_o[0, j] = lw[0][:, sl]
        lw1_o[0, j] = lw[1][:, sl]
        b0_o[0, j] = kk * a[0][:, sl]
        b1_o[0, j] = kk * a[1][:, sl]
        gbv_o[0, :, sl] = g[:, sl] * _dot_sel(bon[:, sl], ones, 2) * v[:, sl]


def _rwkv_features(p, prm, tb=256):
    b, t, cols = p.shape
    tb = min(tb, t)
    nb8 = t // SUBLANES
    r8 = tb // SUBLANES
    full = lambda a: pl.BlockSpec(a.shape, lambda i, j: (0,) * a.ndim)
    params = [prm["shift_mu"].reshape(1, cols), prm["decay_up"], prm["decay_0"], prm["a_up"], prm["a_0"],
              prm["g_up"], prm["k_xi"].reshape(1, -1), prm["k_alpha"].reshape(1, -1), prm["r_bonus"].reshape(1, -1)]
    packed = jax.ShapeDtypeStruct((b, N_PAIRS, t, LANES), F32)
    flat = jax.ShapeDtypeStruct((b, t, D_RWKV), F32)
    pspec = pl.BlockSpec((1, N_PAIRS, tb, LANES), lambda i, j: (i, 0, j, 0))
    fspec = pl.BlockSpec((1, tb, D_RWKV), lambda i, j: (i, j, 0))
    return pl.pallas_call(
        _feat_kernel,
        grid=(b, t // tb),
        in_specs=[pl.BlockSpec((1, tb, cols), lambda i, j: (i, j, 0)),
                  pl.BlockSpec((1, SUBLANES, cols), lambda i, j: (i, jnp.maximum(j * r8 - 1, 0), 0)),
                  pl.BlockSpec((1, SUBLANES, cols), lambda i, j: (i, jnp.minimum((j + 1) * r8, nb8 - 1), 0))]
                 + [full(a) for a in params],
        out_specs=[pspec] * 9 + [fspec] * 2,
        out_shape=[packed] * 9 + [flat] * 2,
        compiler_params=_cparams(("parallel", "parallel")),
        name="rwkv_features",
    )(p, p, p, *params)


def _scan_chunks(feats, rev):
    n = SCAN_CHUNK
    m0 = _half_mask((n, LANES))
    m1 = 1.0 - m0
    ti = lax.broadcasted_iota(jnp.int32, (n, n), 0)
    tj = lax.broadcasted_iota(jnp.int32, (n, n), 1)
    tri = ((ti <= tj) if rev else (ti >= tj)).astype(F32)
    si = lax.broadcasted_iota(jnp.int32, (2 * n, 2 * n), 0)
    sj = lax.broadcasted_iota(jnp.int32, (2 * n, 2 * n), 1)
    same = (si // n) == (sj // n)
    ri, rj = si % n, sj % n
    strict = same & ((ri < rj) if rev else (ri > rj))
    incl = same & ((ri <= rj) if rev else (ri >= rj))
    eye = (si == sj).astype(F32)

    def stack(x):
        return jnp.concatenate([x * m0, x * m1], axis=0).astype(BF16)

    pre = []
    for r, lw, kap, bb, kr, v in feats:
        cs = _dot_sel(tri, lw, 3, exact="lhs")
        tot = jnp.sum(lw, axis=0, keepdims=True)
        e_neg = jnp.exp(-cs)
        e_tot = jnp.exp(tot - cs)
        rh = r * jnp.exp(cs)
        pre.append(dict(kh=stack(kap * jnp.exp(cs - lw)), bh=stack(bb * e_neg), kq=stack(kr * e_neg), rh=rh,
                        rhs=stack(rh), kt=stack(kr * e_tot), bt=stack(bb * e_tot), v=stack(v), gl=jnp.exp(tot)))
    nt = (((1,), (1,)), ((), ()))
    amats = [lax.dot_general(jnp.concatenate([p["kh"], p["rhs"]], axis=0), jnp.concatenate([p["bh"], p["kq"]], axis=0),
                             nt, preferred_element_type=F32) for p in pre]
    a1 = [jnp.where(strict, a[:2 * n, :2 * n], 0.0) for a in amats]
    a2 = [jnp.where(strict, a[:2 * n, 2 * n:], 0.0).astype(BF16) for a in amats]
    a4 = [jnp.where(incl, a[2 * n:, :2 * n], 0.0).astype(BF16) for a in amats]
    a3 = [jnp.where(incl, a[2 * n:, 2 * n:], 0.0).astype(BF16) for a in amats]
    tinv = [eye - a for a in a1]
    pw = a1
    for _ in range(5):
        pw = [_bdot(p, p) for p in pw]
        tinv = [t + _bdot(t, p) for t, p in zip(tinv, pw)]
    w_s = [_bdot(a, p["v"]) for a, p in zip(a2, pre)]
    mm = [_bdot(t, jnp.concatenate([p["kh"], w.astype(BF16)], axis=1)).astype(BF16)
          for t, p, w in zip(tinv, pre, w_s)]
    gmat = [_bdot(p["bt"].astype(F32).T, m) for p, m in zip(pre, mm)]
    phi_t = [eye * p["gl"] - g[:, :LANES] for p, g in zip(pre, gmat)]
    psi_t = [_bdot(p["kt"].astype(F32).T, p["v"]) - g[:, LANES:] for p, g in zip(pre, gmat)]
    qy = [_bdot(a, m) for a, m in zip(a4, mm)]
    y0_s = [_bdot(a, p["v"]) - q[:, LANES:] for a, p, q in zip(a3, pre, qy)]
    return [(p["rh"] - (q[:n, :LANES] + q[n:, :LANES]), y0[:n] + y0[n:], ph, ps)
            for p, q, y0, ph, ps in zip(pre, qy, y0_s, phi_t, psi_t)]


def _scan_kernel(r_ref, lw_ref, kap_ref, b_ref, kr_ref, v_ref, s0_ref, y_ref, sT_ref, st_scr, *, rev):
    c = pl.program_id(1)

    @pl.when(c == 0)
    def _():
        st_scr[...] = s0_ref[0]

    n = SCAN_CHUNK
    nsub = r_ref.shape[2] // n
    order = range(nsub - 1, -1, -1) if rev else range(nsub)
    rows = lambda ref, hp, k: ref[0, hp, k * n:(k + 1) * n, :]
    items = [(k, hp) for k in order for hp in range(N_PAIRS)]
    terms = _scan_chunks([tuple(rows(ref, hp, k) for ref in (r_ref, lw_ref, kap_ref, b_ref, kr_ref, v_ref))
                          for k, hp in items], rev)
    states = [st_scr[hp] for hp in range(N_PAIRS)]
    for (k, hp), (qm, y0, phi_t, psi_t) in zip(items, terms):
        y_ref[0, hp, k * n:(k + 1) * n, :] = _bdot(qm, states[hp]) + y0
        states[hp] = _bdot(phi_t, states[hp]) + psi_t
    for hp in range(N_PAIRS):
        st_scr[hp] = states[hp]

    @pl.when(c == pl.num_programs(1) - 1)
    def _():
        sT_ref[0] = st_scr[...]


def _rwkv_scan(r, lw, kap, bb, kr, v, s0, rev, chunks_per_step=4):
    b, _, t, _ = r.shape
    n = SCAN_CHUNK * math.gcd(chunks_per_step, t // SCAN_CHUNK)
    nc = t // n
    cidx = (lambda c: nc - 1 - c) if rev else (lambda c: c)
    fspec = pl.BlockSpec((1, N_PAIRS, n, LANES), lambda i, c: (i, 0, cidx(c), 0))
    sspec = pl.BlockSpec((1, N_PAIRS, LANES, LANES), lambda i, c: (i, 0, 0, 0))
    return pl.pallas_call(
        functools.partial(_scan_kernel, rev=rev),
        grid=(b, nc),
        in_specs=[fspec] * 6 + [sspec],
        out_specs=[fspec, sspec],
        out_shape=[jax.ShapeDtypeStruct(r.shape, F32), jax.ShapeDtypeStruct(s0.shape, F32)],
        scratch_shapes=[pltpu.VMEM((N_PAIRS, LANES, LANES), F32)],
        compiler_params=_cparams(("parallel", "arbitrary")),
        name="rwkv_scan_rev" if rev else "rwkv_scan_fwd",
    )(r, lw, kap, bb, kr, v, s0)


def _evenpost_kernel(y0_ref, y1_ref, g_ref, gbv_ref, pc_ref, pcp_ref, pcn_ref, cw_ref, gng_ref, gnb_ref,
                     conv_o, rw_o):
    pc = pc_ref[0]
    prev_row, next_row = _edge_rows(pcp_ref, pcn_ref)
    dc = D_CONV

    def gated(z):
        return z[:, dc:2 * dc] * z[:, 2 * dc:3 * dc]

    u = gated(pc)
    u_prev, u_next = _shifted(u, gated(prev_row), gated(next_row))
    cw = cw_ref[...]
    conv_o[0] = (pc[:, 0:dc] * (cw[0:1] * u_prev + cw[1:2] * u + cw[2:3] * u_next)).astype(conv_o.dtype)
    ones = _seg_ones() * (1.0 / RWKV_HD)
    for j in range(N_PAIRS):
        sl = slice(j * LANES, (j + 1) * LANES)
        y = y0_ref[0, j] + y1_ref[0, j]
        mu = _dot_sel(y, ones, 2)
        yc = y - mu
        var = _dot_sel(yc * yc, ones, 2)
        yn = yc * lax.rsqrt(var + GN_EPS) * gng_ref[:, sl] + gnb_ref[:, sl]
        rw_o[0, :, sl] = (g_ref[0, :, sl] * yn + gbv_ref[0, :, sl]).astype(rw_o.dtype)


def _even_post(y0, y1, g, gbv, pc, conv_w, gn_g, gn_b, tb=256):
    b, _, t, _ = y0.shape
    tb = min(tb, t)
    nb8 = t // SUBLANES
    r8 = tb // SUBLANES
    c3 = pc.shape[-1]
    pspec = pl.BlockSpec((1, N_PAIRS, tb, LANES), lambda i, j: (i, 0, j, 0))
    fspec = pl.BlockSpec((1, tb, D_RWKV), lambda i, j: (i, j, 0))
    full = lambda a: pl.BlockSpec(a.shape, lambda i, j: (0,) * a.ndim)
    gn_g = gn_g.reshape(1, -1)
    gn_b = gn_b.reshape(1, -1)
    return pl.pallas_call(
        _evenpost_kernel,
        grid=(b, t // tb),
        in_specs=[pspec, pspec, fspec, fspec,
                  pl.BlockSpec((1, tb, c3), lambda i, j: (i, j, 0)),
                  pl.BlockSpec((1, SUBLANES, c3), lambda i, j: (i, jnp.maximum(j * r8 - 1, 0), 0)),
                  pl.BlockSpec((1, SUBLANES, c3), lambda i, j: (i, jnp.minimum((j + 1) * r8, nb8 - 1), 0)),
                  full(conv_w), full(gn_g), full(gn_b)],
        out_specs=[pl.BlockSpec((1, tb, D_CONV), lambda i, j: (i, j, 0)), fspec],
        out_shape=[jax.ShapeDtypeStruct((b, t, D_CONV), BF16), jax.ShapeDtypeStruct((b, t, D_RWKV), BF16)],
        compiler_params=_cparams(("parallel", "parallel")),
        name="even_post",
    )(y0, y1, g, gbv, pc, pc, pc, conv_w, gn_g, gn_b)


def _outproj_kernel(a_ref, b_ref, wa_ref, wb_ref, x_ref, gm_ref, lg_ref, lb_ref, sc_ref, sh_ref, wr_ref,
                    o_ref, h_o, aff_o):
    yx = _bdot(a_ref[0], wa_ref[...]) + _bdot(b_ref[0], wb_ref[...])
    z = ALPHA * x_ref[0] + gm_ref[0] * yx
    xn = _layer_norm_rows(z, lg_ref[...], lb_ref[...], LN_EPS)
    o_ref[0] = xn
    h = xn * (1.0 + sc_ref[0]) + sh_ref[0]
    h_o[0] = h.astype(h_o.dtype)
    logits = lax.dot_general(wr_ref[...], h, (((1,), (1,)), ((), ())), preferred_element_type=F32, precision=HI)
    m = jnp.max(logits, axis=0, keepdims=True)
    e = jnp.exp(logits - m)
    aff_o[0] = e / jnp.sum(e, axis=0, keepdims=True)


def _outproj(ma, mb, w_out_bf16, x, gm, ln_g, ln_b, sc_f, sh_f, w_router_t, tm=512):
    b, t, d = x.shape
    tm = min(tm, t)
    e = w_router_t.shape[0]
    da, db = ma.shape[-1], mb.shape[-1]
    wa, wb = w_out_bf16[:da], w_out_bf16[da:]
    row = lambda w: pl.BlockSpec((1, tm, w), lambda i, j: (i, j, 0))
    full = lambda a: pl.BlockSpec(a.shape, lambda i, j: (0,) * a.ndim)
    vec = pl.BlockSpec((1, 1, d), lambda i, j: (i, 0, 0))
    ln_g = ln_g.reshape(1, d)
    ln_b = ln_b.reshape(1, d)
    return pl.pallas_call(
        _outproj_kernel,
        grid=(b, t // tm),
        in_specs=[row(da), row(db), full(wa), full(wb), row(d), vec, full(ln_g), full(ln_b), vec, vec,
                  full(w_router_t)],
        out_specs=[row(d), row(d), pl.BlockSpec((1, e, tm), lambda i, j: (i, 0, j))],
        out_shape=[jax.ShapeDtypeStruct((b, t, d), F32), jax.ShapeDtypeStruct((b, t, d), BF16),
                   jax.ShapeDtypeStruct((b, e, t), F32)],
        compiler_params=_cparams(("parallel", "parallel")),
        name="outproj_norm_router",
    )(ma, mb, wa, wb, x, gm, ln_g, ln_b, sc_f, sh_f, w_router_t)


def _inproj_attn_kernel(x_ref, sc_ref, sh_ref, w_ref, *refs, rope, qscale):
    if rope:
        cos_ref, sin_ref, q_o, k_o, v_o, g_o = refs
        cos, sin = cos_ref[...], sin_ref[...]
        lane = lax.broadcasted_iota(jnp.int32, cos.shape, 1)
        half = AXIS_DIM // 2
        first = (lane % (2 * half)) < half

        def rot(z):
            partner = jnp.where(first, -pltpu.roll(z, LANES - half, axis=1), pltpu.roll(z, half, axis=1))
            return z * cos + partner * sin
    else:
        q_o, k_o, v_o, g_o = refs
        rot = lambda z: z
    h = x_ref[0] * (1.0 + sc_ref[0]) + sh_ref[0]
    y = _bdot(h, w_ref[...])
    for j in range(DIFF_HEADS):
        sl = slice(j * LANES, (j + 1) * LANES)
        q_o[0, :, sl] = (rot(y[:, sl]) * qscale).astype(q_o.dtype)
        k_o[0, :, sl] = rot(y[:, D_DIFF + j * LANES:D_DIFF + (j + 1) * LANES]).astype(k_o.dtype)
    v_o[0] = y[:, 2 * D_DIFF:3 * D_DIFF].astype(v_o.dtype)
    g_o[0] = y[:, 3 * D_DIFF:]


def _inproj_attn(x, sc, sh, w_bf16, tables, qscale, tm=512):
    b, t, d = x.shape
    tm = min(tm, t)
    dout = w_bf16.shape[1]
    rope = tables is not None
    row = lambda w: pl.BlockSpec((1, tm, w), lambda i, j: (i, j, 0))
    vec = pl.BlockSpec((1, 1, d), lambda i, j: (i, 0, 0))
    tab = [pl.BlockSpec((tm, LANES), lambda i, j: (j, 0))] * 2 if rope else []
    return pl.pallas_call(
        functools.partial(_inproj_attn_kernel, rope=rope, qscale=qscale),
        grid=(b, t // tm),
        in_specs=[row(d), vec, vec, pl.BlockSpec((d, dout), lambda i, j: (0, 0))] + tab,
        out_specs=[row(D_DIFF)] * 3 + [row(dout - 3 * D_DIFF)],
        out_shape=[jax.ShapeDtypeStruct((b, t, D_DIFF), BF16)] * 3
                  + [jax.ShapeDtypeStruct((b, t, dout - 3 * D_DIFF), F32)],
        compiler_params=_cparams(("parallel", "parallel")),
        name="inproj_attn",
    )(x, sc, sh, w_bf16, *(tables if rope else ()))


def _rope_tables(t):
    rows = t // GRID_W
    row = jnp.repeat(jnp.arange(rows), GRID_W).astype(F32)
    col = jnp.tile(jnp.arange(GRID_W), rows).astype(F32)
    inv = ROPE_BASE ** (-jnp.arange(0, AXIS_DIM, 2, dtype=F32) / AXIS_DIM)
    ang_r = row[:, None] * inv
    ang_c = col[:, None] * inv
    ang = jnp.concatenate([ang_r, ang_r, ang_c, ang_c], axis=-1)
    ang = jnp.concatenate([ang, ang], axis=-1)
    return jnp.cos(ang), jnp.sin(ang)


def _attn_kernel(lam_ref, q_ref, k_ref, v_ref, g_ref, o_ref, *, tk, out_scale):
    q = q_ref[0]
    tq = q.shape[0]
    hm = _half_mask(q.shape, BF16)
    q1 = q * hm
    qs = (q1, q - q1)
    nk = k_ref.shape[1] // tk
    nsl = tk // LANES

    def scores(qh, j):
        kb = k_ref[0, pl.ds(pl.multiple_of(j * tk, tk), tk), :]
        return lax.dot_general(qh, kb, (((1,), (1,)), ((), ())), preferred_element_type=F32)

    def col(s, c):
        return s[:, c * LANES:(c + 1) * LANES]

    def body(j, carry):
        vb = v_ref[0, pl.ds(pl.multiple_of(j * tk, tk), tk), :]
        ss = [scores(qh, j) for qh in qs]
        out = []
        for s, (m, ls, acc) in zip(ss, (carry[:3], carry[3:])):
            mx = col(s, 0)
            for c in range(1, nsl):
                mx = jnp.maximum(mx, col(s, c))
            m_new = jnp.maximum(m, jnp.broadcast_to(jnp.max(mx, axis=-1, keepdims=True), (tq, LANES)))
            corr = jnp.exp2(m - m_new)
            ps = [jnp.exp2(col(s, c) - m_new) for c in range(nsl)]
            ls = corr * ls
            for pc in ps:
                ls = ls + pc
            p = jnp.concatenate([pc.astype(BF16) for pc in ps], axis=1)
            acc = corr * acc + jnp.dot(p, vb, preferred_element_type=F32)
            out += [m_new, ls, acc]
        return tuple(out)

    neg = jnp.full((tq, LANES), -jnp.inf, F32)
    zero = jnp.zeros((tq, LANES), F32)
    _, ls1, acc1, _, ls2, acc2 = lax.fori_loop(0, nk, body, (neg, zero, zero, neg, zero, zero), unroll=True)
    l1 = jnp.sum(ls1, axis=-1, keepdims=True)
    l2 = jnp.sum(ls2, axis=-1, keepdims=True)
    o = acc1 / l1 - lam_ref[0] * (acc2 / l2)
    o = o * lax.rsqrt(jnp.mean(o * o, axis=-1, keepdims=True) + RMS_EPS) * g_ref[...] * out_scale
    o_ref[0] = o.astype(o_ref.dtype)


def _diff_attention(q, k, v, lam, subln_g, lam_init, tq=512, tk=1408):
    b, t, w = q.shape
    tkk = k.shape[1]
    tq = min(tq, t)
    tk = math.gcd(tk, tkk)
    assert tk % LANES == 0 and t % tq == 0
    subln_g = subln_g.reshape(1, DIFF_VD)
    return pl.pallas_call(
        functools.partial(_attn_kernel, tk=tk, out_scale=1.0 - lam_init),
        grid=(b, DIFF_HEADS, t // tq),
        in_specs=[pl.BlockSpec(memory_space=pltpu.SMEM),
                  pl.BlockSpec((1, tq, LANES), lambda i, h, j: (i, j, h)),
                  pl.BlockSpec((1, tkk, LANES), lambda i, h, j: (i, 0, h)),
                  pl.BlockSpec((1, tkk, LANES), lambda i, h, j: (i, 0, h)),
                  pl.BlockSpec((1, DIFF_VD), lambda i, h, j: (0, 0))],
        out_specs=pl.BlockSpec((1, tq, LANES), lambda i, h, j: (i, j, h)),
        out_shape=jax.ShapeDtypeStruct((b, t, w), BF16),
        compiler_params=_cparams(("parallel", "parallel", "parallel")),
        name="diff_attention",
    )(lam, q, k, v, subln_g)


def _gmlp_kernel(p_ref, lg_ref, lb_ref, ws_ref, bs_ref, o_ref):
    p = p_ref[0]
    tb = p.shape[0]
    ge = 0.5 * p * (1.0 + lax.erf(p * (2.0 ** -0.5)))
    u = ge[:, :D_GMLP]
    v = _layer_norm_rows(ge[:, D_GMLP:], lg_ref[...], lb_ref[...], LN_EPS)
    hm = _half_mask((GMLP_CHUNK, LANES))
    for c in range(tb // GMLP_CHUNK):
        rs = slice(c * GMLP_CHUNK, (c + 1) * GMLP_CHUNK)
        for j in range(D_GMLP // LANES):
            sl = slice(j * LANES, (j + 1) * LANES)
            vc = v[rs, sl]
            va = vc * hm
            mixed = _bdot(ws_ref[2 * j], va) + _bdot(ws_ref[2 * j + 1], vc - va) + bs_ref[:, sl]
            o_ref[0, rs, sl] = (u[rs, sl] * mixed).astype(o_ref.dtype)


def _chunk_gmlp(p, ln_g, ln_b, ws, bs, tb=256):
    b, t, w = p.shape
    tb = min(tb, t)
    ln_g = ln_g.reshape(1, -1)
    ln_b = ln_b.reshape(1, -1)
    bs_t = jnp.repeat(jnp.transpose(bs), D_GMLP // ws.shape[0], axis=1)
    full = lambda a: pl.BlockSpec(a.shape, lambda i, j: (0,) * a.ndim)
    return pl.pallas_call(
        _gmlp_kernel,
        grid=(b, t // tb),
        in_specs=[pl.BlockSpec((1, tb, w), lambda i, j: (i, j, 0)), full(ln_g), full(ln_b), full(ws), full(bs_t)],
        out_specs=pl.BlockSpec((1, tb, D_GMLP), lambda i, j: (i, j, 0)),
        out_shape=jax.ShapeDtypeStruct((b, t, D_GMLP), BF16),
        compiler_params=_cparams(("parallel", "parallel")),
        name="chunk_gmlp",
    )(p, ln_g, ln_b, ws, bs_t)


ROUTE_BLOCK = 128
STARTS_PAD = LANES


def _lane_cumsum(mask):
    e, t = mask.shape
    i = lax.broadcasted_iota(jnp.int32, (LANES, LANES), 0)
    j = lax.broadcasted_iota(jnp.int32, (LANES, LANES), 1)
    upper = (i <= j).astype(BF16)
    x = jnp.where(mask, 1.0, 0.0).astype(BF16)
    off = jnp.zeros((e, 1), F32)
    out = []
    for c in range(t // LANES):
        blk = jnp.dot(x[:, c * LANES:(c + 1) * LANES], upper, preferred_element_type=F32) + off
        out.append(blk)
        off = blk[:, LANES - 1:LANES]
    return jnp.concatenate(out, axis=1)


def _select_kernel(aff_ref, pos_ref, starts_ref, *, cap):
    a = aff_ref[0]
    e, t = a.shape
    bits = pltpu.bitcast(a, jnp.int32)

    def search(i, thr):
        cand = thr | jnp.left_shift(jnp.int32(1), 30 - i)
        cnt = jnp.sum(jnp.where(bits >= cand, 1.0, 0.0), axis=1, keepdims=True)
        return jnp.where(cnt >= cap, cand, thr)

    thr = lax.fori_loop(0, 31, search, jnp.zeros((e, 1), jnp.int32))
    gt = bits > thr
    eq = bits == thr
    need = cap - jnp.sum(jnp.where(gt, 1.0, 0.0), axis=1, keepdims=True)
    sel = gt | (eq & (_lane_cumsum(eq) <= need))
    csel = _lane_cumsum(sel)
    pos_ref[0] = jnp.where(sel, csel - 1.0, -1.0).astype(jnp.int32)
    ti = lax.broadcasted_iota(jnp.int32, (t, STARTS_PAD), 0)
    ki = lax.broadcasted_iota(jnp.int32, (t, STARTS_PAD), 1)
    pick = jnp.where(ti == ki * ROUTE_BLOCK - 1, 1.0, 0.0)
    starts_ref[0] = (_dot_sel(csel, pick, 2) + 0.5).astype(jnp.int32)


def _select(aff, cap):
    b, e, t = aff.shape
    assert t % ROUTE_BLOCK == 0 and t // ROUTE_BLOCK < STARTS_PAD
    return pl.pallas_call(
        functools.partial(_select_kernel, cap=cap),
        grid=(b,),
        in_specs=[pl.BlockSpec((1, e, t), lambda i: (i, 0, 0))],
        out_specs=[pl.BlockSpec((1, e, t), lambda i: (i, 0, 0)), pl.BlockSpec((1, e, STARTS_PAD), lambda i: (i, 0, 0))],
        out_shape=[jax.ShapeDtypeStruct((b, e, t), jnp.int32), jax.ShapeDtypeStruct((b, e, STARTS_PAD), jnp.int32)],
        compiler_params=_cparams(("parallel",)),
        name="expert_select",
    )(aff)


def _window_start(start, align, cap, width):
    s = lax.shift_left(lax.shift_right_logical(start, align.bit_length() - 1), align.bit_length() - 1)
    return pl.multiple_of(jnp.minimum(s, cap - width), align)


def _gather_ffn_kernel(starts_ref, pos_ref, aff_ref, h_ref, w1_ref, w3_ref, w2_ref, y_ref, xs_scr, gate_scr, *,
                       width, rows, cap):
    ei, bi, k = pl.program_id(0), pl.program_id(1), pl.program_id(2)
    ne, nb = pl.num_programs(0), pl.num_programs(1)
    nsub = h_ref.shape[1] // ROUTE_BLOCK
    merged = xs_scr.shape[0] > cap
    base = pl.multiple_of(bi * cap, SUBLANES) if merged else 0

    @pl.when(k == 0)
    def _():
        xs_scr[pl.ds(base, cap), :] = jnp.zeros((cap, xs_scr.shape[1]), F32)
        gate_scr[pl.ds(base, cap), :] = jnp.zeros((cap, LANES), F32)

    def copy_rows(sb, start, w):
        s8 = _window_start(start, SUBLANES, cap, w)
        cols = slice(sb * ROUTE_BLOCK, (sb + 1) * ROUTE_BLOCK)
        jrow = lax.broadcasted_iota(jnp.int32, (w, ROUTE_BLOCK), 0)
        hit = pos_ref[0, :, cols] - s8 == jrow
        xs_scr[pl.ds(base + s8, w), :] += jnp.dot(jnp.where(hit, 1.0, 0.0).astype(BF16), h_ref[0, cols, :],
                                                  preferred_element_type=F32)
        gate = jnp.sum(jnp.where(hit, aff_ref[0, :, cols], 0.0), axis=1, keepdims=True)
        gate_scr[pl.ds(base + s8, w), :] += jnp.broadcast_to(gate, (w, LANES))

    for sb in range(nsub):
        copy_rows(sb, starts_ref[(bi * ne + ei) * STARTS_PAD + k * nsub + sb], width)

    last = k == pl.num_programs(2) - 1

    @pl.when(last & (bi == nb - 1) if merged else last)
    def _():
        for r in range(xs_scr.shape[0] // rows):
            x = xs_scr[r * rows:(r + 1) * rows, :].astype(BF16)
            h1 = jnp.dot(x, w1_ref[0], preferred_element_type=F32)
            h3 = jnp.dot(x, w3_ref[0], preferred_element_type=F32)
            hid = (h1 * jax.nn.sigmoid(h1)) * h3
            y = jnp.dot(hid.astype(BF16), w2_ref[0], preferred_element_type=F32)
            y = (y * gate_scr[r * rows:(r + 1) * rows, 0:1]).astype(y_ref.dtype)
            for s in range(max(rows // cap, 1)):
                n = min(rows, cap)
                row0 = r * rows + s * cap
                y_ref[row0 // cap, 0, row0 % cap:row0 % cap + n, :] = y[s * cap:s * cap + n]


def _gather_ffn(h, pos, aff, starts, w1, w3, w2, cap, tch=2048, rows=256):
    b, t, d = h.shape
    e, _, f = w1.shape
    tch = min(tch, t)
    merged = b * cap <= rows
    slots = b if merged else 1
    rows = min(rows, slots * cap)
    assert (slots * cap) % rows == 0 and (rows % cap == 0 or cap % rows == 0)
    width = min(ROUTE_BLOCK + SUBLANES, cap)
    grid_spec = pltpu.PrefetchScalarGridSpec(
        num_scalar_prefetch=1,
        grid=(e, b, t // tch),
        in_specs=[pl.BlockSpec((1, 1, tch), lambda ei, bi, k, s: (bi * e + ei, 0, k)),
                  pl.BlockSpec((1, 1, tch), lambda ei, bi, k, s: (bi * e + ei, 0, k)),
                  pl.BlockSpec((1, tch, d), lambda ei, bi, k, s: (bi, k, 0)),
                  pl.BlockSpec((1, d, f), lambda ei, bi, k, s: (ei, 0, 0)),
                  pl.BlockSpec((1, d, f), lambda ei, bi, k, s: (ei, 0, 0)),
                  pl.BlockSpec((1, f, d), lambda ei, bi, k, s: (ei, 0, 0))],
        out_specs=pl.BlockSpec((slots, 1, cap, d), (lambda ei, bi, k, s: (0, ei, 0, 0)) if merged
                               else (lambda ei, bi, k, s: (bi, ei, 0, 0))),
        scratch_shapes=[pltpu.VMEM((slots * cap, d), F32), pltpu.VMEM((slots * cap, LANES), F32)],
    )
    return pl.pallas_call(
        functools.partial(_gather_ffn_kernel, width=width, rows=rows, cap=cap),
        grid_spec=grid_spec,
        out_shape=jax.ShapeDtypeStruct((b, e, cap, d), BF16),
        compiler_params=_cparams(("parallel", "arbitrary", "arbitrary") if merged
                                 else ("parallel", "parallel", "arbitrary")),
        name="expert_gather_ffn",
    )(starts.reshape(-1), pos.reshape(b * e, 1, t), aff.reshape(b * e, 1, t), h, w1, w3, w2)


def _combine_kernel(starts_ref, post_ref, y_ref, x_ref, gf_ref, lg_ref, lb_ref, o_ref, acc_scr, *, width, group):
    bi, i, eg = pl.program_id(0), pl.program_id(1), pl.program_id(2)
    ne = pl.num_programs(2) * group
    cap = y_ref.shape[2]
    nsub = x_ref.shape[1] // ROUTE_BLOCK

    @pl.when(eg == 0)
    def _():
        acc_scr[...] = jnp.zeros_like(acc_scr)

    lane = lax.broadcasted_iota(jnp.int32, (ROUTE_BLOCK, ne), 1)
    jcol = lax.broadcasted_iota(jnp.int32, (ROUTE_BLOCK, width), 1)
    for sb in range(nsub):
        rs = slice(sb * ROUTE_BLOCK, (sb + 1) * ROUTE_BLOCK)
        pblk = post_ref[0, rs, :].astype(F32)
        total = None
        for j in range(group):
            ei = eg * group + j
            start = starts_ref[(bi * ne + ei) * STARTS_PAD + i * nsub + sb]
            s16 = _window_start(start, 2 * SUBLANES, cap, width)
            pcol = jnp.sum(jnp.where(lane == ei, pblk, 0.0), axis=1, keepdims=True)
            onehot = jnp.where(pcol.astype(jnp.int32) - s16 == jcol, 1.0, 0.0).astype(BF16)
            part = jnp.dot(onehot, y_ref[0, j, pl.ds(s16, width), :], preferred_element_type=F32)
            total = part if total is None else total + part
        acc_scr[rs, :] += total

    @pl.when(eg == pl.num_programs(2) - 1)
    def _():
        z = ALPHA * x_ref[0] + gf_ref[0] * acc_scr[...]
        o_ref[0] = _layer_norm_rows(z, lg_ref[...], lb_ref[...], LN_EPS)


def _combine(x, y, pos_t, starts, gf, ln_g, ln_b, tbo=1024, group=4):
    b, t, d = x.shape
    _, e, cap, _ = y.shape
    tbo = min(tbo, t)
    width = min(ROUTE_BLOCK + 2 * SUBLANES, cap)
    grid_spec = pltpu.PrefetchScalarGridSpec(
        num_scalar_prefetch=1,
        grid=(b, t // tbo, e // group),
        in_specs=[pl.BlockSpec((1, tbo, e), lambda bi, i, eg, s: (bi, i, 0)),
                  pl.BlockSpec((1, group, cap, d), lambda bi, i, eg, s: (bi, eg, 0, 0)),
                  pl.BlockSpec((1, tbo, d), lambda bi, i, eg, s: (bi, i, 0)),
                  pl.BlockSpec((1, 1, d), lambda bi, i, eg, s: (bi, 0, 0)),
                  pl.BlockSpec((1, d), lambda bi, i, eg, s: (0, 0)),
                  pl.BlockSpec((1, d), lambda bi, i, eg, s: (0, 0))],
        out_specs=pl.BlockSpec((1, tbo, d), lambda bi, i, eg, s: (bi, i, 0)),
        scratch_shapes=[pltpu.VMEM((tbo, d), F32)],
    )
    return pl.pallas_call(
        functools.partial(_combine_kernel, width=width, group=group),
        grid_spec=grid_spec,
        out_shape=jax.ShapeDtypeStruct((b, t, d), F32),
        compiler_params=_cparams(("parallel", "parallel", "arbitrary")),
        name="expert_combine_norm",
    )(starts.reshape(-1), pos_t, y, x, gf, ln_g.reshape(1, d), ln_b.reshape(1, d))


def _cast_kernel(x_ref, o_ref):
    o_ref[...] = x_ref[...].astype(o_ref.dtype)


def _to_bf16(w, l):
    _, e, r, c = w.shape
    return pl.pallas_call(
        _cast_kernel,
        grid=(e,),
        in_specs=[pl.BlockSpec((None, 1, r, c), lambda i: (l, i, 0, 0))],
        out_specs=pl.BlockSpec((1, r, c), lambda i: (i, 0, 0)),
        out_shape=jax.ShapeDtypeStruct((e, r, c), BF16),
        compiler_params=_cparams(("parallel",)),
        name="cast_bf16",
    )(w)


def _moe(x, h, aff, gf, w1, w3, w2, ln_g, ln_b):
    t = x.shape[1]
    cap = EC_FACTOR * t // N_EXPERTS
    pos, starts = _select(aff, cap)
    y = _gather_ffn(h, pos, aff, starts, w1, w3, w2, cap)
    return _combine(x, y, jnp.transpose(pos, (0, 2, 1)), starts, gf, ln_g, ln_b)


def _even_layer(hx_in, hc_in, prm, ctx_out):
    w_in = prm["w_in"]
    splits = (3 * D_CONV, RWKV_COLS)
    pcx, prx = _inproj(*hx_in, w_in, splits)
    pcc, prc = _inproj(*hc_in, w_in, splits)
    fx = _rwkv_features(prx, prm)
    fc = _rwkv_features(prc, prm)
    b = prx.shape[0]
    zero = jnp.zeros((b, N_PAIRS, LANES, LANES), F32)
    ys_x, ys_c = [], []
    for d, rev in ((0, False), (1, True)):
        def args(f):
            r, kk, v, kr0, kr1, lw0, lw1, b0, b1 = f[:9]
            return (r, (lw0, lw1)[d], kk, (b0, b1)[d], (kr0, kr1)[d], v)
        yc, s_ctx = _rwkv_scan(*args(fc), zero, rev)
        yx, _ = _rwkv_scan(*args(fx), s_ctx, rev)
        ys_x.append(yx)
        ys_c.append(yc)
    out_x = _even_post(ys_x[0], ys_x[1], fx[9], fx[10], pcx, prm["conv_w"], prm["gn_g"], prm["gn_b"])
    out_c = None
    if ctx_out:
        out_c = _even_post(ys_c[0], ys_c[1], fc[9], fc[10], pcc, prm["conv_w"], prm["gn_g"], prm["gn_b"])
    return out_c, out_x


def _odd_layer(hx_in, hc_in, prm, lam_init, ctx_out):
    w_in = prm["w_in"]
    qscale = DIFF_HD ** -0.5 * math.log2(math.e)
    qx, kx, vx, gx = _inproj_attn(*hx_in, w_in, _rope_tables(hx_in[0].shape[1]), qscale)
    qc, kc, vc, gc = _inproj_attn(*hc_in, w_in, None, qscale)
    k_all = jnp.concatenate([kc, kx], axis=1)
    v_all = jnp.concatenate([vc, vx], axis=1)
    lam = (jnp.exp(jnp.sum(prm["lam_q1"] * prm["lam_k1"])) - jnp.exp(jnp.sum(prm["lam_q2"] * prm["lam_k2"]))
           + lam_init).reshape(1).astype(F32)
    att_x = _diff_attention(qx, k_all, v_all, lam, prm["subln_g"], lam_init)
    gm_x = _chunk_gmlp(gx, prm["gmlp_ln_g"], prm["gmlp_ln_b"], prm["gmlp_ws"], prm["gmlp_bs"])
    out_c = None
    if ctx_out:
        att_c = _diff_attention(qc, kc, vc, lam, prm["subln_g"], lam_init)
        gm_c = _chunk_gmlp(gc, prm["gmlp_ln_g"], prm["gmlp_ln_b"], prm["gmlp_ws"], prm["gmlp_bs"])
        out_c = (att_c, gm_c)
    return out_c, (att_x, gm_x)


def kernel(x, c, ctx, c_ctx, w_mod, b_mod, ln_g, ln_b, even_w_in, even_w_out, conv_w, shift_mu, decay_up, decay_0, a_up, a_0, g_up, k_xi, k_alpha, r_bonus, gn_g, gn_b, odd_w_in, odd_w_out, lam_q1, lam_k1, lam_q2, lam_k2, subln_g, gmlp_ln_g, gmlp_ln_b, gmlp_ws, gmlp_bs, w_router, w_e1, w_e3, w_e2):
    bsz, _, d = x.shape
    assert bsz <= SUBLANES - 1
    c8 = jnp.zeros((SUBLANES, d), F32).at[:bsz].set(c).at[bsz].set(c_ctx)
    mod = _modulation(c8, w_mod, b_mod)
    for l in range(DEPTH):
        ctx_out = l < DEPTH - 1
        i = l // 2
        mx = mod[l, :bsz].reshape(bsz, 1, 6, d)
        mc = jnp.broadcast_to(mod[l, bsz].reshape(1, 1, 6, d), (bsz, 1, 6, d))
        part = lambda m, n: m[:, :, n]
        hx_in = (x, part(mx, 1), part(mx, 0))
        hc_in = (ctx, part(mc, 1), part(mc, 0))
        if l % 2 == 0:
            prm = dict(w_in=even_w_in[i].astype(BF16), conv_w=conv_w[i], shift_mu=shift_mu[i],
                       decay_up=decay_up[i], decay_0=decay_0[i], a_up=a_up[i], a_0=a_0[i], g_up=g_up[i],
                       k_xi=k_xi[i], k_alpha=k_alpha[i], r_bonus=r_bonus[i], gn_g=gn_g[i], gn_b=gn_b[i])
            out_c, out_x = _even_layer(hx_in, hc_in, prm, ctx_out)
            w_out = even_w_out[i].astype(BF16)
        else:
            lam_init = 0.8 - 0.6 * math.exp(-0.3 * l)
            prm = dict(w_in=odd_w_in[i].astype(BF16), lam_q1=lam_q1[i], lam_k1=lam_k1[i], lam_q2=lam_q2[i],
                       lam_k2=lam_k2[i], subln_g=subln_g[i], gmlp_ln_g=gmlp_ln_g[i], gmlp_ln_b=gmlp_ln_b[i],
                       gmlp_ws=gmlp_ws[i], gmlp_bs=gmlp_bs[i])
            out_c, out_x = _odd_layer(hx_in, hc_in, prm, lam_init, ctx_out)
            w_out = odd_w_out[i].astype(BF16)
        wr_t = jnp.transpose(w_router[l])
        w1, w3, w2 = _to_bf16(w_e1, l), _to_bf16(w_e3, l), _to_bf16(w_e2, l)
        x, h, aff = _outproj(out_x[0], out_x[1], w_out, x, part(mx, 2), ln_g[l, 0], ln_b[l, 0],
                             part(mx, 4), part(mx, 3), wr_t)
        x = _moe(x, h, aff, part(mx, 5), w1, w3, w2, ln_g[l, 1], ln_b[l, 1])
        if ctx_out:
            ctx, h, aff = _outproj(out_c[0], out_c[1], w_out, ctx, part(mc, 2), ln_g[l, 0], ln_b[l, 0],
                                   part(mc, 4), part(mc, 3), wr_t)
            ctx = _moe(ctx, h, aff, part(mc, 5), w1, w3, w2, ln_g[l, 1], ln_b[l, 1])
    return x
```

```python
import functools
import math

import jax
import jax.numpy as jnp
from jax import lax
from jax.experimental import pallas as pl
from jax.experimental.pallas import tpu as pltpu

F32 = jnp.float32
BF16 = jnp.bfloat16
HI = lax.Precision.HIGHEST

D_MODEL = 1024
DEPTH = 4
GRID_W = 64
D_CONV = 256
RWKV_HEADS = 12
RWKV_HD = 64
D_RWKV = RWKV_HEADS * RWKV_HD
LORA = 64
LORA_G = 128
RWKV_COLS = 3 * D_RWKV + 4 * LORA + LORA_G
DECAY_SCALE = math.exp(-0.5)
GN_EPS = 64e-5
N_PAIRS = RWKV_HEADS // 2
SCAN_CHUNK = 64
DIFF_HEADS = 6
DIFF_HD = 64
DIFF_VD = 2 * DIFF_HD
D_DIFF = DIFF_HEADS * DIFF_VD
AXIS_DIM = DIFF_HD // 2
ROPE_BASE = 10000.0
D_GMLP = 256
GMLP_CHUNK = 128
N_EXPERTS = 16
EC_FACTOR = 2
D_EXPERT = 2048
ALPHA = (2.0 * DEPTH) ** 0.25
LN_EPS = 1e-5
RMS_EPS = 1e-5

LANES = 128
SUBLANES = 8
VMEM_LIMIT = 56 * 1024 * 1024


def _cparams(sem):
    return pltpu.CompilerParams(dimension_semantics=sem, vmem_limit_bytes=VMEM_LIMIT)


def _bdot(a, b):
    return jnp.dot(a.astype(BF16), b.astype(BF16), preferred_element_type=F32)


def _hdot(a, b):
    return jnp.dot(a, b, preferred_element_type=F32, precision=HI)


def _split_bf16(x, terms):
    parts = []
    for _ in range(terms):
        p = x.astype(BF16)
        parts.append(p)
        x = x - p.astype(F32)
    return parts


def _dot3(a, b):
    a_hi, a_lo = _split_bf16(a, 2)
    b_hi, b_lo = _split_bf16(b, 2)
    d = lambda x, y: jnp.dot(x, y, preferred_element_type=F32)
    return d(a_hi, b_hi) + (d(a_hi, b_lo) + d(a_lo, b_hi))


def _dot_sel(a, b, terms, exact="rhs"):
    d = lambda x, y: jnp.dot(x, y, preferred_element_type=F32)
    if exact == "rhs":
        bb = b.astype(BF16)
        outs = [d(p, bb) for p in _split_bf16(a, terms)]
    else:
        ab = a.astype(BF16)
        outs = [d(ab, p) for p in _split_bf16(b, terms)]
    out = outs[-1]
    for o in reversed(outs[:-1]):
        out = out + o
    return out


def _half_mask(shape, dtype=F32):
    lane = lax.broadcasted_iota(jnp.int32, shape, len(shape) - 1)
    return (lane < 64).astype(dtype)


def _seg_ones():
    i = lax.broadcasted_iota(jnp.int32, (LANES, LANES), 0)
    j = lax.broadcasted_iota(jnp.int32, (LANES, LANES), 1)
    return ((i // 64) == (j // 64)).astype(F32)


def _layer_norm_rows(z, g, b, eps):
    mu = jnp.mean(z, axis=-1, keepdims=True)
    zc = z - mu
    var = jnp.mean(zc * zc, axis=-1, keepdims=True)
    return zc * lax.rsqrt(var + eps) * g + b


def _mod_kernel(c_ref, w_ref, b_ref, o_ref):
    a = c_ref[...]
    a = a * jax.nn.sigmoid(a)
    o_ref[0] = _hdot(a, w_ref[0]) + b_ref[0]


def _modulation(c8, w_mod, b_mod):
    tn = 1536
    nl, d, n6 = w_mod.shape
    return pl.pallas_call(
        _mod_kernel,
        grid=(nl, n6 // tn),
        in_specs=[pl.BlockSpec((8, d), lambda l, j: (0, 0)),
                  pl.BlockSpec((1, d, tn), lambda l, j: (l, 0, j)),
                  pl.BlockSpec((1, 1, tn), lambda l, j: (l, 0, j))],
        out_specs=pl.BlockSpec((1, 8, tn), lambda l, j: (l, 0, j)),
        out_shape=jax.ShapeDtypeStruct((nl, 8, n6), F32),
        compiler_params=_cparams(("parallel", "parallel")),
        name="modulation",
    )(c8, w_mod, b_mod.reshape(nl, 1, n6))


def _inproj_kernel(x_ref, sc_ref, sh_ref, w_ref, *o_refs, splits):
    h = x_ref[0] * (1.0 + sc_ref[0]) + sh_ref[0]
    y = _bdot(h, w_ref[...])
    off = 0
    for o, s in zip(o_refs, splits):
        o[0] = y[:, off:off + s].astype(o.dtype)
        off += s


def _inproj(x, sc, sh, w_bf16, splits, tm=512):
    b, t, d = x.shape
    tm = min(tm, t)
    dout = w_bf16.shape[1]
    return pl.pallas_call(
        functools.partial(_inproj_kernel, splits=splits),
        grid=(b, t // tm),
        in_specs=[pl.BlockSpec((1, tm, d), lambda i, j: (i, j, 0)),
                  pl.BlockSpec((1, 1, d), lambda i, j: (i, 0, 0)),
                  pl.BlockSpec((1, 1, d), lambda i, j: (i, 0, 0)),
                  pl.BlockSpec((d, dout), lambda i, j: (0, 0))],
        out_specs=[pl.BlockSpec((1, tm, s), lambda i, j: (i, j, 0)) for s in splits],
        out_shape=[jax.ShapeDtypeStruct((b, t, s), F32) for s in splits],
        compiler_params=_cparams(("parallel", "parallel")),
        name="inproj",
    )(x, sc, sh, w_bf16)


def _shifted(p, prev_row, next_row):
    tb = p.shape[0]
    row = lax.broadcasted_iota(jnp.int32, p.shape, 0)
    p_prev = jnp.where(row == 0, prev_row, pltpu.roll(p, 1, axis=0))
    p_next = jnp.where(row == tb - 1, next_row, pltpu.roll(p, tb - 1, axis=0))
    return p_prev, p_next


def _edge_rows(pp_ref, pn_ref):
    i = pl.program_id(1)
    nb = pl.num_programs(1)
    prev_row = jnp.where(i > 0, pp_ref[0, SUBLANES - 1:SUBLANES, :], 0.0)
    next_row = jnp.where(i < nb - 1, pn_ref[0, 0:1, :], 0.0)
    return prev_row, next_row


def _feat_kernel(p_ref, pp_ref, pn_ref, mu_ref, dup_ref, d0_ref, aup_ref, a0_ref, gup_ref, kxi_ref, kal_ref,
                 rb_ref, r_o, kk_o, v_o, kr0_o, kr1_o, lw0_o, lw1_o, b0_o, b1_o, g_o, gbv_o):
    p = p_ref[0]
    prev_row, next_row = _edge_rows(pp_ref, pn_ref)
    p_prev, p_next = _shifted(p, prev_row, next_row)
    p = p + mu_ref[...] * (0.5 * (p_prev + p_next) - p)
    dr = D_RWKV
    r = p[:, 0:dr]
    k = p[:, dr:2 * dr]
    v = p[:, 2 * dr:3 * dr]
    c0 = 3 * dr
    dd = jnp.tanh(p[:, c0:c0 + 2 * LORA])
    da = p[:, c0 + 2 * LORA:c0 + 4 * LORA]
    dg = jax.nn.sigmoid(p[:, c0 + 4 * LORA:c0 + 4 * LORA + LORA_G])
    g = _dot3(dg, gup_ref[...])
    lw, a = [], []
    for d in range(2):
        lw.append(-DECAY_SCALE * jax.nn.sigmoid(d0_ref[d:d + 1, :] + _dot3(dd[:, d * LORA:(d + 1) * LORA], dup_ref[d])))
        a.append(jax.nn.sigmoid(a0_ref[d:d + 1, :] + _dot3(da[:, d * LORA:(d + 1) * LORA], aup_ref[d])))
    kx = k * kxi_ref[...]
    kal = kal_ref[...]
    kr = [k * (1.0 + (a[d] - 1.0) * kal) for d in range(2)]
    bon = r * (0.5 * (kr[0] + kr[1])) * rb_ref[...]
    ones = _seg_ones()
    g_o[0] = g
    for j in range(N_PAIRS):
        sl = slice(j * LANES, (j + 1) * LANES)
        kxj = kx[:, sl]
        kk = kxj * lax.rsqrt(_dot_sel(kxj * kxj, ones, 2) + 1e-12)
        r_o[0, j] = r[:, sl].astype(r_o.dtype)
        kk_o[0, j] = kk.astype(kk_o.dtype)
        v_o[0, j] = v[:, sl].astype(v_o.dtype)
        kr0_o[0, j] = kr[0][:, sl].astype(kr0_o.dtype)
        kr1_o[0, j] = kr[1][:, sl].astype(kr1_o.dtype)
        lw0_o[0, j] = lw[0][:, sl]
        lw1_o[0, j] = lw[1][:, sl]
        b0_o[0, j] = (kk * a[0][:, sl]).astype(b0_o.dtype)
        b1_o[0, j] = (kk * a[1][:, sl]).astype(b1_o.dtype)
        gbv_o[0, :, sl] = g[:, sl] * _dot_sel(bon[:, sl], ones, 2) * v[:, sl]


def _rwkv_features(p, prm, tb=256):
    b, t, cols = p.shape
    tb = min(tb, t)
    nb8 = t // SUBLANES
    r8 = tb // SUBLANES
    full = lambda a: pl.BlockSpec(a.shape, lambda i, j: (0,) * a.ndim)
    params = [prm["shift_mu"].reshape(1, cols), prm["decay_up"], prm["decay_0"], prm["a_up"], prm["a_0"],
              prm["g_up"], prm["k_xi"].reshape(1, -1), prm["k_alpha"].reshape(1, -1), prm["r_bonus"].reshape(1, -1)]
    packed = lambda dt: jax.ShapeDtypeStruct((b, N_PAIRS, t, LANES), dt)
    flat = jax.ShapeDtypeStruct((b, t, D_RWKV), F32)
    pspec = pl.BlockSpec((1, N_PAIRS, tb, LANES), lambda i, j: (i, 0, j, 0))
    fspec = pl.BlockSpec((1, tb, D_RWKV), lambda i, j: (i, j, 0))
    return pl.pallas_call(
        _feat_kernel,
        grid=(b, t // tb),
        in_specs=[pl.BlockSpec((1, tb, cols), lambda i, j: (i, j, 0)),
                  pl.BlockSpec((1, SUBLANES, cols), lambda i, j: (i, jnp.maximum(j * r8 - 1, 0), 0)),
                  pl.BlockSpec((1, SUBLANES, cols), lambda i, j: (i, jnp.minimum((j + 1) * r8, nb8 - 1), 0))]
                 + [full(a) for a in params],
        out_specs=[pspec] * 9 + [fspec] * 2,
        out_shape=[packed(BF16)] * 5 + [packed(F32)] * 2 + [packed(BF16)] * 2 + [flat] * 2,
        compiler_params=_cparams(("parallel", "parallel")),
        name="rwkv_features",
    )(p, p, p, *params)


def _scan_chunks(feats, rev):
    n = SCAN_CHUNK
    m0 = _half_mask((n, LANES))
    m1 = 1.0 - m0
    ti = lax.broadcasted_iota(jnp.int32, (n, n), 0)
    tj = lax.broadcasted_iota(jnp.int32, (n, n), 1)
    tri = ((ti <= tj) if rev else (ti >= tj)).astype(F32)
    si = lax.broadcasted_iota(jnp.int32, (2 * n, 2 * n), 0)
    sj = lax.broadcasted_iota(jnp.int32, (2 * n, 2 * n), 1)
    same = (si // n) == (sj // n)
    ri, rj = si % n, sj % n
    strict = same & ((ri < rj) if rev else (ri > rj))
    incl = same & ((ri <= rj) if rev else (ri >= rj))
    eye = (si == sj).astype(F32)

    def stack(x):
        return jnp.concatenate([x * m0, x * m1], axis=0).astype(BF16)

    pre = []
    for r, lw, kap, bb, kr, v in feats:
        cs = _dot_sel(tri, lw, 3, exact="lhs")
        tot = jnp.sum(lw, axis=0, keepdims=True)
        e_neg = jnp.exp(-cs)
        e_tot = jnp.exp(tot - cs)
        rh = r * jnp.exp(cs)
        pre.append(dict(kh=stack(kap * jnp.exp(cs - lw)), bh=stack(bb * e_neg), kq=stack(kr * e_neg), rh=rh,
                        rhs=stack(rh), kt=stack(kr * e_tot), bt=stack(bb * e_tot), v=stack(v), gl=jnp.exp(tot)))
    nt = (((1,), (1,)), ((), ()))
    amats = [lax.dot_general(jnp.concatenate([p["kh"], p["rhs"]], axis=0), jnp.concatenate([p["bh"], p["kq"]], axis=0),
                             nt, preferred_element_type=F32) for p in pre]
    a1 = [jnp.where(strict, a[:2 * n, :2 * n], 0.0) for a in amats]
    a2 = [jnp.where(strict, a[:2 * n, 2 * n:], 0.0).astype(BF16) for a in amats]
    a4 = [jnp.where(incl, a[2 * n:, :2 * n], 0.0).astype(BF16) for a in amats]
    a3 = [jnp.where(incl, a[2 * n:, 2 * n:], 0.0).astype(BF16) for a in amats]
    tinv = [eye - a for a in a1]
    pw = a1
    for _ in range(5):
        pw = [_bdot(p, p) for p in pw]
        tinv = [t + _bdot(t, p) for t, p in zip(tinv, pw)]
    w_s = [_bdot(a, p["v"]) for a, p in zip(a2, pre)]
    mm = [_bdot(t, jnp.concatenate([p["kh"], w.astype(BF16)], axis=1)).astype(BF16)
          for t, p, w in zip(tinv, pre, w_s)]
    gmat = [_bdot(p["bt"].astype(F32).T, m) for p, m in zip(pre, mm)]
    phi_t = [eye * p["gl"] - g[:, :LANES] for p, g in zip(pre, gmat)]
    psi_t = [_bdot(p["kt"].astype(F32).T, p["v"]) - g[:, LANES:] for p, g in zip(pre, gmat)]
    qy = [_bdot(a, m) for a, m in zip(a4, mm)]
    y0_s = [_bdot(a, p["v"]) - q[:, LANES:] for a, p, q in zip(a3, pre, qy)]
    return [(p["rh"] - (q[:n, :LANES] + q[n:, :LANES]), y0[:n] + y0[n:], ph, ps)
            for p, q, y0, ph, ps in zip(pre, qy, y0_s, phi_t, psi_t)]


def _scan_kernel(r_ref, lw_ref, kap_ref, b_ref, kr_ref, v_ref, s0_ref, y_ref, sT_ref, st_scr, *, rev):
    c = pl.program_id(1)

    @pl.when(c == 0)
    def _():
        st_scr[...] = s0_ref[0]

    n = SCAN_CHUNK
    nsub = r_ref.shape[2] // n
    order = range(nsub - 1, -1, -1) if rev else range(nsub)
    rows = lambda ref, hp, k: ref[0, hp, k * n:(k + 1) * n, :]
    items = [(k, hp) for k in order for hp in range(N_PAIRS)]
    terms = _scan_chunks([tuple(rows(ref, hp, k) for ref in (r_ref, lw_ref, kap_ref, b_ref, kr_ref, v_ref))
                          for k, hp in items], rev)
    states = [st_scr[hp] for hp in range(N_PAIRS)]
    for (k, hp), (qm, y0, phi_t, psi_t) in zip(items, terms):
        y_ref[0, hp, k * n:(k + 1) * n, :] = _bdot(qm, states[hp]) + y0
        states[hp] = _bdot(phi_t, states[hp]) + psi_t
    for hp in range(N_PAIRS):
        st_scr[hp] = states[hp]

    @pl.when(c == pl.num_programs(1) - 1)
    def _():
        sT_ref[0] = st_scr[...]


def _rwkv_scan(r, lw, kap, bb, kr, v, s0, rev, chunks_per_step=4):
    b, _, t, _ = r.shape
    n = SCAN_CHUNK * math.gcd(chunks_per_step, t // SCAN_CHUNK)
    nc = t // n
    cidx = (lambda c: nc - 1 - c) if rev else (lambda c: c)
    fspec = pl.BlockSpec((1, N_PAIRS, n, LANES), lambda i, c: (i, 0, cidx(c), 0))
    sspec = pl.BlockSpec((1, N_PAIRS, LANES, LANES), lambda i, c: (i, 0, 0, 0))
    return pl.pallas_call(
        functools.partial(_scan_kernel, rev=rev),
        grid=(b, nc),
        in_specs=[fspec] * 6 + [sspec],
        out_specs=[fspec, sspec],
        out_shape=[jax.ShapeDtypeStruct(r.shape, F32), jax.ShapeDtypeStruct(s0.shape, F32)],
        scratch_shapes=[pltpu.VMEM((N_PAIRS, LANES, LANES), F32)],
        compiler_params=_cparams(("parallel", "arbitrary")),
        name="rwkv_scan_rev" if rev else "rwkv_scan_fwd",
    )(r, lw, kap, bb, kr, v, s0)


def _evenpost_kernel(y0_ref, y1_ref, g_ref, gbv_ref, pc_ref, pcp_ref, pcn_ref, cw_ref, gng_ref, gnb_ref,
                     conv_o, rw_o):
    pc = pc_ref[0]
    prev_row, next_row = _edge_rows(pcp_ref, pcn_ref)
    dc = D_CONV

    def gated(z):
        return z[:, dc:2 * dc] * z[:, 2 * dc:3 * dc]

    u = gated(pc)
    u_prev, u_next = _shifted(u, gated(prev_row), gated(next_row))
    cw = cw_ref[...]
    conv_o[0] = (pc[:, 0:dc] * (cw[0:1] * u_prev + cw[1:2] * u + cw[2:3] * u_next)).astype(conv_o.dtype)
    ones = _seg_ones() * (1.0 / RWKV_HD)
    for j in range(N_PAIRS):
        sl = slice(j * LANES, (j + 1) * LANES)
        y = y0_ref[0, j] + y1_ref[0, j]
        mu = _dot_sel(y, ones, 2)
        yc = y - mu
        var = _dot_sel(yc * yc, ones, 2)
        yn = yc * lax.rsqrt(var + GN_EPS) * gng_ref[:, sl] + gnb_ref[:, sl]
        rw_o[0, :, sl] = (g_ref[0, :, sl] * yn + gbv_ref[0, :, sl]).astype(rw_o.dtype)


def _even_post(y0, y1, g, gbv, pc, conv_w, gn_g, gn_b, tb=256):
    b, _, t, _ = y0.shape
    tb = min(tb, t)
    nb8 = t // SUBLANES
    r8 = tb // SUBLANES
    c3 = pc.shape[-1]
    pspec = pl.BlockSpec((1, N_PAIRS, tb, LANES), lambda i, j: (i, 0, j, 0))
    fspec = pl.BlockSpec((1, tb, D_RWKV), lambda i, j: (i, j, 0))
    full = lambda a: pl.BlockSpec(a.shape, lambda i, j: (0,) * a.ndim)
    gn_g = gn_g.reshape(1, -1)
    gn_b = gn_b.reshape(1, -1)
    return pl.pallas_call(
        _evenpost_kernel,
        grid=(b, t // tb),
        in_specs=[pspec, pspec, fspec, fspec,
                  pl.BlockSpec((1, tb, c3), lambda i, j: (i, j, 0)),
                  pl.BlockSpec((1, SUBLANES, c3), lambda i, j: (i, jnp.maximum(j * r8 - 1, 0), 0)),
                  pl.BlockSpec((1, SUBLANES, c3), lambda i, j: (i, jnp.minimum((j + 1) * r8, nb8 - 1), 0)),
                  full(conv_w), full(gn_g), full(gn_b)],
        out_specs=[pl.BlockSpec((1, tb, D_CONV), lambda i, j: (i, j, 0)), fspec],
        out_shape=[jax.ShapeDtypeStruct((b, t, D_CONV), BF16), jax.ShapeDtypeStruct((b, t, D_RWKV), BF16)],
        compiler_params=_cparams(("parallel", "parallel")),
        name="even_post",
    )(y0, y1, g, gbv, pc, pc, pc, conv_w, gn_g, gn_b)


def _outproj_kernel(a_ref, b_ref, wa_ref, wb_ref, x_ref, gm_ref, lg_ref, lb_ref, sc_ref, sh_ref, wr_ref,
                    o_ref, h_o, aff_o):
    yx = _bdot(a_ref[0], wa_ref[...]) + _bdot(b_ref[0], wb_ref[...])
    z = ALPHA * x_ref[0] + gm_ref[0] * yx
    xn = _layer_norm_rows(z, lg_ref[...], lb_ref[...], LN_EPS)
    o_ref[0] = xn
    h = xn * (1.0 + sc_ref[0]) + sh_ref[0]
    h_o[0] = h.astype(h_o.dtype)
    logits = lax.dot_general(wr_ref[...], h, (((1,), (1,)), ((), ())), preferred_element_type=F32, precision=HI)
    m = jnp.max(logits, axis=0, keepdims=True)
    e = jnp.exp(logits - m)
    aff_o[0] = e / jnp.sum(e, axis=0, keepdims=True)


def _outproj(ma, mb, w_out_bf16, x, gm, ln_g, ln_b, sc_f, sh_f, w_router_t, tm=512):
    b, t, d = x.shape
    tm = min(tm, t)
    e = w_router_t.shape[0]
    da, db = ma.shape[-1], mb.shape[-1]
    wa, wb = w_out_bf16[:da], w_out_bf16[da:]
    row = lambda w: pl.BlockSpec((1, tm, w), lambda i, j: (i, j, 0))
    full = lambda a: pl.BlockSpec(a.shape, lambda i, j: (0,) * a.ndim)
    vec = pl.BlockSpec((1, 1, d), lambda i, j: (i, 0, 0))
    ln_g = ln_g.reshape(1, d)
    ln_b = ln_b.reshape(1, d)
    return pl.pallas_call(
        _outproj_kernel,
        grid=(b, t // tm),
        in_specs=[row(da), row(db), full(wa), full(wb), row(d), vec, full(ln_g), full(ln_b), vec, vec,
                  full(w_router_t)],
        out_specs=[row(d), row(d), pl.BlockSpec((1, e, tm), lambda i, j: (i, 0, j))],
        out_shape=[jax.ShapeDtypeStruct((b, t, d), F32), jax.ShapeDtypeStruct((b, t, d), BF16),
                   jax.ShapeDtypeStruct((b, e, t), F32)],
        compiler_params=_cparams(("parallel", "parallel")),
        name="outproj_norm_router",
    )(ma, mb, wa, wb, x, gm, ln_g, ln_b, sc_f, sh_f, w_router_t)


def _inproj_attn_kernel(x_ref, sc_ref, sh_ref, w_ref, *refs, rope, qscale):
    if rope:
        cos_ref, sin_ref, q_o, k_o, v_o, g_o = refs
        cos, sin = cos_ref[...], sin_ref[...]
        lane = lax.broadcasted_iota(jnp.int32, cos.shape, 1)
        half = AXIS_DIM // 2
        first = (lane % (2 * half)) < half

        def rot(z):
            partner = jnp.where(first, -pltpu.roll(z, LANES - half, axis=1), pltpu.roll(z, half, axis=1))
            return z * cos + partner * sin
    else:
        q_o, k_o, v_o, g_o = refs
        rot = lambda z: z
    h = x_ref[0] * (1.0 + sc_ref[0]) + sh_ref[0]
    y = _bdot(h, w_ref[...])
    for j in range(DIFF_HEADS):
        sl = slice(j * LANES, (j + 1) * LANES)
        q_o[0, :, sl] = (rot(y[:, sl]) * qscale).astype(q_o.dtype)
        k_o[0, :, sl] = rot(y[:, D_DIFF + j * LANES:D_DIFF + (j + 1) * LANES]).astype(k_o.dtype)
    v_o[0] = y[:, 2 * D_DIFF:3 * D_DIFF].astype(v_o.dtype)
    g_o[0] = y[:, 3 * D_DIFF:]


def _inproj_attn(x, sc, sh, w_bf16, tables, qscale, tm=512):
    b, t, d = x.shape
    tm = min(tm, t)
    dout = w_bf16.shape[1]
    rope = tables is not None
    row = lambda w: pl.BlockSpec((1, tm, w), lambda i, j: (i, j, 0))
    vec = pl.BlockSpec((1, 1, d), lambda i, j: (i, 0, 0))
    tab = [pl.BlockSpec((tm, LANES), lambda i, j: (j, 0))] * 2 if rope else []
    return pl.pallas_call(
        functools.partial(_inproj_attn_kernel, rope=rope, qscale=qscale),
        grid=(b, t // tm),
        in_specs=[row(d), vec, vec, pl.BlockSpec((d, dout), lambda i, j: (0, 0))] + tab,
        out_specs=[row(D_DIFF)] * 3 + [row(dout - 3 * D_DIFF)],
        out_shape=[jax.ShapeDtypeStruct((b, t, D_DIFF), BF16)] * 3
                  + [jax.ShapeDtypeStruct((b, t, dout - 3 * D_DIFF), F32)],
        compiler_params=_cparams(("parallel", "parallel")),
        name="inproj_attn",
    )(x, sc, sh, w_bf16, *(tables if rope else ()))


def _rope_tables(t):
    rows = t // GRID_W
    row = jnp.repeat(jnp.arange(rows), GRID_W).astype(F32)
    col = jnp.tile(jnp.arange(GRID_W), rows).astype(F32)
    inv = ROPE_BASE ** (-jnp.arange(0, AXIS_DIM, 2, dtype=F32) / AXIS_DIM)
    ang_r = row[:, None] * inv
    ang_c = col[:, None] * inv
    ang = jnp.concatenate([ang_r, ang_r, ang_c, ang_c], axis=-1)
    ang = jnp.concatenate([ang, ang], axis=-1)
    return jnp.cos(ang), jnp.sin(ang)


def _attn_kernel(lam_ref, q_ref, k_ref, v_ref, g_ref, o_ref, *, tk, out_scale):
    q = q_ref[0]
    tq = q.shape[0]
    hm = _half_mask(q.shape, BF16)
    q1 = q * hm
    qs = (q1, q - q1)
    nk = k_ref.shape[1] // tk
    nsl = tk // LANES

    def scores(qh, j):
        kb = k_ref[0, pl.ds(pl.multiple_of(j * tk, tk), tk), :]
        return lax.dot_general(qh, kb, (((1,), (1,)), ((), ())), preferred_element_type=F32)

    def col(s, c):
        return s[:, c * LANES:(c + 1) * LANES]

    def body(j, carry):
        vb = v_ref[0, pl.ds(pl.multiple_of(j * tk, tk), tk), :]
        ss = [scores(qh, j) for qh in qs]
        out = []
        for s, (m, ls, acc) in zip(ss, (carry[:3], carry[3:])):
            mx = col(s, 0)
            for c in range(1, nsl):
                mx = jnp.maximum(mx, col(s, c))
            m_new = jnp.maximum(m, jnp.broadcast_to(jnp.max(mx, axis=-1, keepdims=True), (tq, LANES)))
            corr = jnp.exp2(m - m_new)
            ps = [jnp.exp2(col(s, c) - m_new) for c in range(nsl)]
            ls = corr * ls
            for pc in ps:
                ls = ls + pc
            p = jnp.concatenate([pc.astype(BF16) for pc in ps], axis=1)
            acc = corr * acc + jnp.dot(p, vb, preferred_element_type=F32)
            out += [m_new, ls, acc]
        return tuple(out)

    neg = jnp.full((tq, LANES), -jnp.inf, F32)
    zero = jnp.zeros((tq, LANES), F32)
    _, ls1, acc1, _, ls2, acc2 = lax.fori_loop(0, nk, body, (neg, zero, zero, neg, zero, zero), unroll=True)
    l1 = jnp.sum(ls1, axis=-1, keepdims=True)
    l2 = jnp.sum(ls2, axis=-1, keepdims=True)
    o = acc1 / l1 - lam_ref[0] * (acc2 / l2)
    o = o * lax.rsqrt(jnp.mean(o * o, axis=-1, keepdims=True) + RMS_EPS) * g_ref[...] * out_scale
    o_ref[0] = o.astype(o_ref.dtype)


def _diff_attention(q, k, v, lam, subln_g, lam_init, tq=512, tk=1408):
    b, t, w = q.shape
    tkk = k.shape[1]
    tq = min(tq, t)
    tk = math.gcd(tk, tkk)
    assert tk % LANES == 0 and t % tq == 0
    subln_g = subln_g.reshape(1, DIFF_VD)
    return pl.pallas_call(
        functools.partial(_attn_kernel, tk=tk, out_scale=1.0 - lam_init),
        grid=(b, DIFF_HEADS, t // tq),
        in_specs=[pl.BlockSpec(memory_space=pltpu.SMEM),
                  pl.BlockSpec((1, tq, LANES), lambda i, h, j: (i, j, h)),
                  pl.BlockSpec((1, tkk, LANES), lambda i, h, j: (i, 0, h)),
                  pl.BlockSpec((1, tkk, LANES), lambda i, h, j: (i, 0, h)),
                  pl.BlockSpec((1, DIFF_VD), lambda i, h, j: (0, 0))],
        out_specs=pl.BlockSpec((1, tq, LANES), lambda i, h, j: (i, j, h)),
        out_shape=jax.ShapeDtypeStruct((b, t, w), BF16),
        compiler_params=_cparams(("parallel", "parallel", "parallel")),
        name="diff_attention",
    )(lam, q, k, v, subln_g)


def _gmlp_kernel(p_ref, lg_ref, lb_ref, ws_ref, bs_ref, o_ref):
    p = p_ref[0]
    tb = p.shape[0]
    ge = 0.5 * p * (1.0 + lax.erf(p * (2.0 ** -0.5)))
    u = ge[:, :D_GMLP]
    v = _layer_norm_rows(ge[:, D_GMLP:], lg_ref[...], lb_ref[...], LN_EPS)
    hm = _half_mask((GMLP_CHUNK, LANES))
    for c in range(tb // GMLP_CHUNK):
        rs = slice(c * GMLP_CHUNK, (c + 1) * GMLP_CHUNK)
        for j in range(D_GMLP // LANES):
            sl = slice(j * LANES, (j + 1) * LANES)
            vc = v[rs, sl]
            va = vc * hm
            mixed = _bdot(ws_ref[2 * j], va) + _bdot(ws_ref[2 * j + 1], vc - va) + bs_ref[:, sl]
            o_ref[0, rs, sl] = (u[rs, sl] * mixed).astype(o_ref.dtype)


def _chunk_gmlp(p, ln_g, ln_b, ws, bs, tb=256):
    b, t, w = p.shape
    tb = min(tb, t)
    ln_g = ln_g.reshape(1, -1)
    ln_b = ln_b.reshape(1, -1)
    bs_t = jnp.repeat(jnp.transpose(bs), D_GMLP // ws.shape[0], axis=1)
    full = lambda a: pl.BlockSpec(a.shape, lambda i, j: (0,) * a.ndim)
    return pl.pallas_call(
        _gmlp_kernel,
        grid=(b, t // tb),
        in_specs=[pl.BlockSpec((1, tb, w), lambda i, j: (i, j, 0)), full(ln_g), full(ln_b), full(ws), full(bs_t)],
        out_specs=pl.BlockSpec((1, tb, D_GMLP), lambda i, j: (i, j, 0)),
        out_shape=jax.ShapeDtypeStruct((b, t, D_GMLP), BF16),
        compiler_params=_cparams(("parallel", "parallel")),
        name="chunk_gmlp",
    )(p, ln_g, ln_b, ws, bs_t)


ROUTE_BLOCK = 128
STARTS_PAD = LANES


def _lane_cumsum(mask):
    e, t = mask.shape
    i = lax.broadcasted_iota(jnp.int32, (LANES, LANES), 0)
    j = lax.broadcasted_iota(jnp.int32, (LANES, LANES), 1)
    upper = (i <= j).astype(BF16)
    x = jnp.where(mask, 1.0, 0.0).astype(BF16)
    off = jnp.zeros((e, 1), F32)
    out = []
    for c in range(t // LANES):
        blk = jnp.dot(x[:, c * LANES:(c + 1) * LANES], upper, preferred_element_type=F32) + off
        out.append(blk)
        off = blk[:, LANES - 1:LANES]
    return jnp.concatenate(out, axis=1)


def _select_kernel(aff_ref, pos_ref, starts_ref, *, cap):
    a = aff_ref[0]
    e, t = a.shape
    bits = pltpu.bitcast(a, jnp.int32)

    def search(i, thr):
        cand = thr | jnp.left_shift(jnp.int32(1), 30 - i)
        cnt = jnp.sum(jnp.where(bits >= cand, 1.0, 0.0), axis=1, keepdims=True)
        return jnp.where(cnt >= cap, cand, thr)

    thr = lax.fori_loop(0, 31, search, jnp.zeros((e, 1), jnp.int32))
    gt = bits > thr
    eq = bits == thr
    need = cap - jnp.sum(jnp.where(gt, 1.0, 0.0), axis=1, keepdims=True)
    sel = gt | (eq & (_lane_cumsum(eq) <= need))
    csel = _lane_cumsum(sel)
    pos_ref[0] = jnp.where(sel, csel - 1.0, -1.0).astype(jnp.int32)
    ti = lax.broadcasted_iota(jnp.int32, (t, STARTS_PAD), 0)
    ki = lax.broadcasted_iota(jnp.int32, (t, STARTS_PAD), 1)
    pick = jnp.where(ti == ki * ROUTE_BLOCK - 1, 1.0, 0.0)
    starts_ref[0] = (_dot_sel(csel, pick, 2) + 0.5).astype(jnp.int32)


def _select(aff, cap):
    b, e, t = aff.shape
    assert t % ROUTE_BLOCK == 0 and t // ROUTE_BLOCK < STARTS_PAD
    return pl.pallas_call(
        functools.partial(_select_kernel, cap=cap),
        grid=(b,),
        in_specs=[pl.BlockSpec((1, e, t), lambda i: (i, 0, 0))],
        out_specs=[pl.BlockSpec((1, e, t), lambda i: (i, 0, 0)), pl.BlockSpec((1, e, STARTS_PAD), lambda i: (i, 0, 0))],
        out_shape=[jax.ShapeDtypeStruct((b, e, t), jnp.int32), jax.ShapeDtypeStruct((b, e, STARTS_PAD), jnp.int32)],
        compiler_params=_cparams(("parallel",)),
        name="expert_select",
    )(aff)


def _window_start(start, align, cap, width):
    s = lax.shift_left(lax.shift_right_logical(start, align.bit_length() - 1), align.bit_length() - 1)
    return pl.multiple_of(jnp.minimum(s, cap - width), align)


def _gather_ffn_kernel(starts_ref, pos_ref, aff_ref, h_ref, w1_ref, w3_ref, w2_ref, y_ref, xs_scr, gate_scr, *,
                       width, rows, cap):
    ei, bi, k = pl.program_id(0), pl.program_id(1), pl.program_id(2)
    ne, nb = pl.num_programs(0), pl.num_programs(1)
    nsub = h_ref.shape[1] // ROUTE_BLOCK
    merged = xs_scr.shape[0] > cap
    base = pl.multiple_of(bi * cap, SUBLANES) if merged else 0

    @pl.when(k == 0)
    def _():
        xs_scr[pl.ds(base, cap), :] = jnp.zeros((cap, xs_scr.shape[1]), F32)
        gate_scr[pl.ds(base, cap), :] = jnp.zeros((cap, LANES), F32)

    def copy_rows(sb, start, w):
        s8 = _window_start(start, SUBLANES, cap, w)
        cols = slice(sb * ROUTE_BLOCK, (sb + 1) * ROUTE_BLOCK)
        jrow = lax.broadcasted_iota(jnp.int32, (w, ROUTE_BLOCK), 0)
        hit = pos_ref[0, :, cols] - s8 == jrow
        xs_scr[pl.ds(base + s8, w), :] += jnp.dot(jnp.where(hit, 1.0, 0.0).astype(BF16), h_ref[0, cols, :],
                                                  preferred_element_type=F32)
        gate = jnp.sum(jnp.where(hit, aff_ref[0, :, cols], 0.0), axis=1, keepdims=True)
        gate_scr[pl.ds(base + s8, w), :] += jnp.broadcast_to(gate, (w, LANES))

    for sb in range(nsub):
        copy_rows(sb, starts_ref[(bi * ne + ei) * STARTS_PAD + k * nsub + sb], width)

    last = k == pl.num_programs(2) - 1

    @pl.when(last & (bi == nb - 1) if merged else last)
    def _():
        for r in range(xs_scr.shape[0] // rows):
            x = xs_scr[r * rows:(r + 1) * rows, :].astype(BF16)
            h1 = jnp.dot(x, w1_ref[0], preferred_element_type=F32)
            h3 = jnp.dot(x, w3_ref[0], preferred_element_type=F32)
            hid = (h1 * jax.nn.sigmoid(h1)) * h3
            y = jnp.dot(hid.astype(BF16), w2_ref[0], preferred_element_type=F32)
            y = (y * gate_scr[r * rows:(r + 1) * rows, 0:1]).astype(y_ref.dtype)
            for s in range(max(rows // cap, 1)):
                n = min(rows, cap)
                row0 = r * rows + s * cap
                y_ref[row0 // cap, 0, row0 % cap:row0 % cap + n, :] = y[s * cap:s * cap + n]


def _gather_ffn(h, pos, aff, starts, w1, w3, w2, cap, tch=4096, rows=256):
    b, t, d = h.shape
    e, _, f = w1.shape
    tch = min(tch, t)
    merged = b * cap <= rows
    slots = b if merged else 1
    rows = min(rows, slots * cap)
    assert (slots * cap) % rows == 0 and (rows % cap == 0 or cap % rows == 0)
    width = min(ROUTE_BLOCK + SUBLANES, cap)
    grid_spec = pltpu.PrefetchScalarGridSpec(
        num_scalar_prefetch=1,
        grid=(e, b, t // tch),
        in_specs=[pl.BlockSpec((1, 1, tch), lambda ei, bi, k, s: (bi * e + ei, 0, k)),
                  pl.BlockSpec((1, 1, tch), lambda ei, bi, k, s: (bi * e + ei, 0, k)),
                  pl.BlockSpec((1, tch, d), lambda ei, bi, k, s: (bi, k, 0)),
                  pl.BlockSpec((1, d, f), lambda ei, bi, k, s: (ei, 0, 0)),
                  pl.BlockSpec((1, d, f), lambda ei, bi, k, s: (ei, 0, 0)),
                  pl.BlockSpec((1, f, d), lambda ei, bi, k, s: (ei, 0, 0))],
        out_specs=pl.BlockSpec((slots, 1, cap, d), (lambda ei, bi, k, s: (0, ei, 0, 0)) if merged
                               else (lambda ei, bi, k, s: (bi, ei, 0, 0))),
        scratch_shapes=[pltpu.VMEM((slots * cap, d), F32), pltpu.VMEM((slots * cap, LANES), F32)],
    )
    return pl.pallas_call(
        functools.partial(_gather_ffn_kernel, width=width, rows=rows, cap=cap),
        grid_spec=grid_spec,
        out_shape=jax.ShapeDtypeStruct((b, e, cap, d), BF16),
        compiler_params=_cparams(("parallel", "arbitrary", "arbitrary") if merged
                                 else ("parallel", "parallel", "arbitrary")),
        name="expert_gather_ffn",
    )(starts.reshape(-1), pos.reshape(b * e, 1, t), aff.reshape(b * e, 1, t), h, w1, w3, w2)


def _combine_kernel(starts_ref, post_ref, y_ref, x_ref, gf_ref, lg_ref, lb_ref, o_ref, acc_scr, *, width, group):
    bi, i, eg = pl.program_id(0), pl.program_id(1), pl.program_id(2)
    ne = pl.num_programs(2) * group
    cap = y_ref.shape[2]
    nsub = x_ref.shape[1] // ROUTE_BLOCK

    @pl.when(eg == 0)
    def _():
        acc_scr[...] = jnp.zeros_like(acc_scr)

    lane = lax.broadcasted_iota(jnp.int32, (ROUTE_BLOCK, ne), 1)
    jcol = lax.broadcasted_iota(jnp.int32, (ROUTE_BLOCK, width), 1)
    for sb in range(nsub):
        rs = slice(sb * ROUTE_BLOCK, (sb + 1) * ROUTE_BLOCK)
        pblk = post_ref[0, rs, :].astype(F32)
        total = None
        for j in range(group):
            ei = eg * group + j
            start = starts_ref[(bi * ne + ei) * STARTS_PAD + i * nsub + sb]
            s16 = _window_start(start, 2 * SUBLANES, cap, width)
            pcol = jnp.sum(jnp.where(lane == ei, pblk, 0.0), axis=1, keepdims=True)
            onehot = jnp.where(pcol.astype(jnp.int32) - s16 == jcol, 1.0, 0.0).astype(BF16)
            part = jnp.dot(onehot, y_ref[0, j, pl.ds(s16, width), :], preferred_element_type=F32)
            total = part if total is None else total + part
        acc_scr[rs, :] += total

    @pl.when(eg == pl.num_programs(2) - 1)
    def _():
        z = ALPHA * x_ref[0] + gf_ref[0] * acc_scr[...]
        o_ref[0] = _layer_norm_rows(z, lg_ref[...], lb_ref[...], LN_EPS)


def _combine(x, y, pos_t, starts, gf, ln_g, ln_b, tbo=1024, group=4):
    b, t, d = x.shape
    _, e, cap, _ = y.shape
    tbo = min(tbo, t)
    width = min(ROUTE_BLOCK + 2 * SUBLANES, cap)
    grid_spec = pltpu.PrefetchScalarGridSpec(
        num_scalar_prefetch=1,
        grid=(b, t // tbo, e // group),
        in_specs=[pl.BlockSpec((1, tbo, e), lambda bi, i, eg, s: (bi, i, 0)),
                  pl.BlockSpec((1, group, cap, d), lambda bi, i, eg, s: (bi, eg, 0, 0)),
                  pl.BlockSpec((1, tbo, d), lambda bi, i, eg, s: (bi, i, 0)),
                  pl.BlockSpec((1, 1, d), lambda bi, i, eg, s: (bi, 0, 0)),
                  pl.BlockSpec((1, d), lambda bi, i, eg, s: (0, 0)),
                  pl.BlockSpec((1, d), lambda bi, i, eg, s: (0, 0))],
        out_specs=pl.BlockSpec((1, tbo, d), lambda bi, i, eg, s: (bi, i, 0)),
        scratch_shapes=[pltpu.VMEM((tbo, d), F32)],
    )
    return pl.pallas_call(
        functools.partial(_combine_kernel, width=width, group=group),
        grid_spec=grid_spec,
        out_shape=jax.ShapeDtypeStruct((b, t, d), F32),
        compiler_params=_cparams(("parallel", "parallel", "arbitrary")),
        name="expert_combine_norm",
    )(starts.reshape(-1), pos_t, y, x, gf, ln_g.reshape(1, d), ln_b.reshape(1, d))


def _cast_kernel(x_ref, o_ref):
    o_ref[...] = x_ref[...].astype(o_ref.dtype)


def _to_bf16(w, l):
    _, e, r, c = w.shape
    return pl.pallas_call(
        _cast_kernel,
        grid=(e,),
        in_specs=[pl.BlockSpec((None, 1, r, c), lambda i: (l, i, 0, 0))],
        out_specs=pl.BlockSpec((1, r, c), lambda i: (i, 0, 0)),
        out_shape=jax.ShapeDtypeStruct((e, r, c), BF16),
        compiler_params=_cparams(("parallel",)),
        name="cast_bf16",
    )(w)


def _moe(x, h, aff, gf, w1, w3, w2, ln_g, ln_b):
    t = x.shape[1]
    cap = EC_FACTOR * t // N_EXPERTS
    pos, starts = _select(aff, cap)
    y = _gather_ffn(h, pos, aff, starts, w1, w3, w2, cap)
    return _combine(x, y, jnp.transpose(pos, (0, 2, 1)), starts, gf, ln_g, ln_b)


def _even_layer(hx_in, hc_in, prm, ctx_out):
    w_in = prm["w_in"]
    splits = (3 * D_CONV, RWKV_COLS)
    pcx, prx = _inproj(*hx_in, w_in, splits)
    pcc, prc = _inproj(*hc_in, w_in, splits)
    fx = _rwkv_features(prx, prm)
    fc = _rwkv_features(prc, prm)
    b = prx.shape[0]
    zero = jnp.zeros((b, N_PAIRS, LANES, LANES), F32)
    ys_x, ys_c = [], []
    for d, rev in ((0, False), (1, True)):
        def args(f):
            r, kk, v, kr0, kr1, lw0, lw1, b0, b1 = f[:9]
            return (r, (lw0, lw1)[d], kk, (b0, b1)[d], (kr0, kr1)[d], v)
        yc, s_ctx = _rwkv_scan(*args(fc), zero, rev)
        yx, _ = _rwkv_scan(*args(fx), s_ctx, rev)
        ys_x.append(yx)
        ys_c.append(yc)
    out_x = _even_post(ys_x[0], ys_x[1], fx[9], fx[10], pcx, prm["conv_w"], prm["gn_g"], prm["gn_b"])
    out_c = None
    if ctx_out:
        out_c = _even_post(ys_c[0], ys_c[1], fc[9], fc[10], pcc, prm["conv_w"], prm["gn_g"], prm["gn_b"])
    return out_c, out_x


def _odd_layer(hx_in, hc_in, prm, lam_init, ctx_out):
    w_in = prm["w_in"]
    qscale = DIFF_HD ** -0.5 * math.log2(math.e)
    qx, kx, vx, gx = _inproj_attn(*hx_in, w_in, _rope_tables(hx_in[0].shape[1]), qscale)
    qc, kc, vc, gc = _inproj_attn(*hc_in, w_in, None, qscale)
    k_all = jnp.concatenate([kc, kx], axis=1)
    v_all = jnp.concatenate([vc, vx], axis=1)
    lam = (jnp.exp(jnp.sum(prm["lam_q1"] * prm["lam_k1"])) - jnp.exp(jnp.sum(prm["lam_q2"] * prm["lam_k2"]))
           + lam_init).reshape(1).astype(F32)
    att_x = _diff_attention(qx, k_all, v_all, lam, prm["subln_g"], lam_init)
    gm_x = _chunk_gmlp(gx, prm["gmlp_ln_g"], prm["gmlp_ln_b"], prm["gmlp_ws"], prm["gmlp_bs"])
    out_c = None
    if ctx_out:
        att_c = _diff_attention(qc, kc, vc, lam, prm["subln_g"], lam_init)
        gm_c = _chunk_gmlp(gc, prm["gmlp_ln_g"], prm["gmlp_ln_b"], prm["gmlp_ws"], prm["gmlp_bs"])
        out_c = (att_c, gm_c)
    return out_c, (att_x, gm_x)


def kernel(x, c, ctx, c_ctx, w_mod, b_mod, ln_g, ln_b, even_w_in, even_w_out, conv_w, shift_mu, decay_up, decay_0, a_up, a_0, g_up, k_xi, k_alpha, r_bonus, gn_g, gn_b, odd_w_in, odd_w_out, lam_q1, lam_k1, lam_q2, lam_k2, subln_g, gmlp_ln_g, gmlp_ln_b, gmlp_ws, gmlp_bs, w_router, w_e1, w_e3, w_e2):
    bsz, _, d = x.shape
    assert bsz <= SUBLANES - 1
    c8 = jnp.zeros((SUBLANES, d), F32).at[:bsz].set(c).at[bsz].set(c_ctx)
    mod = _modulation(c8, w_mod, b_mod)
    for l in range(DEPTH):
        ctx_out = l < DEPTH - 1
        i = l // 2
        mx = mod[l, :bsz].reshape(bsz, 1, 6, d)
        mc = jnp.broadcast_to(mod[l, bsz].reshape(1, 1, 6, d), (bsz, 1, 6, d))
        part = lambda m, n: m[:, :, n]
        hx_in = (x, part(mx, 1), part(mx, 0))
        hc_in = (ctx, part(mc, 1), part(mc, 0))
        if l % 2 == 0:
            prm = dict(w_in=even_w_in[i].astype(BF16), conv_w=conv_w[i], shift_mu=shift_mu[i],
                       decay_up=decay_up[i], decay_0=decay_0[i], a_up=a_up[i], a_0=a_0[i], g_up=g_up[i],
                       k_xi=k_xi[i], k_alpha=k_alpha[i], r_bonus=r_bonus[i], gn_g=gn_g[i], gn_b=gn_b[i])
            out_c, out_x = _even_layer(hx_in, hc_in, prm, ctx_out)
            w_out = even_w_out[i].astype(BF16)
        else:
            lam_init = 0.8 - 0.6 * math.exp(-0.3 * l)
            prm = dict(w_in=odd_w_in[i].astype(BF16), lam_q1=lam_q1[i], lam_k1=lam_k1[i], lam_q2=lam_q2[i],
                       lam_k2=lam_k2[i], subln_g=subln_g[i], gmlp_ln_g=gmlp_ln_g[i], gmlp_ln_b=gmlp_ln_b[i],
                       gmlp_ws=gmlp_ws[i], gmlp_bs=gmlp_bs[i])
            out_c, out_x = _odd_layer(hx_in, hc_in, prm, lam_init, ctx_out)
            w_out = odd_w_out[i].astype(BF16)
        wr_t = jnp.transpose(w_router[l])
        w1, w3, w2 = _to_bf16(w_e1, l), _to_bf16(w_e3, l), _to_bf16(w_e2, l)
        x, h, aff = _outproj(out_x[0], out_x[1], w_out, x, part(mx, 2), ln_g[l, 0], ln_b[l, 0],
                             part(mx, 4), part(mx, 3), wr_t)
        x = _moe(x, h, aff, part(mx, 5), w1, w3, w2, ln_g[l, 1], ln_b[l, 1])
        if ctx_out:
            ctx, h, aff = _outproj(out_c[0], out_c[1], w_out, ctx, part(mc, 2), ln_g[l, 0], ln_b[l, 0],
                                   part(mc, 4), part(mc, 3), wr_t)
            ctx = _moe(ctx, h, aff, part(mc, 5), w1, w3, w2, ln_g[l, 1], ln_b[l, 1])
    return x
```

```python
import functools
import math

import jax
import jax.numpy as jnp
from jax import lax
from jax.experimental import pallas as pl
from jax.experimental.pallas import tpu as pltpu

F32 = jnp.float32
BF16 = jnp.bfloat16
HI = lax.Precision.HIGHEST

D_MODEL = 1024
DEPTH = 4
GRID_W = 64
D_CONV = 256
RWKV_HEADS = 12
RWKV_HD = 64
D_RWKV = RWKV_HEADS * RWKV_HD
LORA = 64
LORA_G = 128
RWKV_COLS = 3 * D_RWKV + 4 * LORA + LORA_G
DECAY_SCALE = math.exp(-0.5)
GN_EPS = 64e-5
N_PAIRS = RWKV_HEADS // 2
SCAN_CHUNK = 64
DIFF_HEADS = 6
DIFF_HD = 64
DIFF_VD = 2 * DIFF_HD
D_DIFF = DIFF_HEADS * DIFF_VD
AXIS_DIM = DIFF_HD // 2
ROPE_BASE = 10000.0
D_GMLP = 256
GMLP_CHUNK = 128
N_EXPERTS = 16
EC_FACTOR = 2
D_EXPERT = 2048
ALPHA = (2.0 * DEPTH) ** 0.25
LN_EPS = 1e-5
RMS_EPS = 1e-5

LANES = 128
SUBLANES = 8
VMEM_LIMIT = 56 * 1024 * 1024


def _cparams(sem):
    return pltpu.CompilerParams(dimension_semantics=sem, vmem_limit_bytes=VMEM_LIMIT)


def _bdot(a, b):
    return jnp.dot(a.astype(BF16), b.astype(BF16), preferred_element_type=F32)


def _hdot(a, b):
    return jnp.dot(a, b, preferred_element_type=F32, precision=HI)


def _split_bf16(x, terms):
    parts = []
    for _ in range(terms):
        p = x.astype(BF16)
        parts.append(p)
        x = x - p.astype(F32)
    return parts


def _dot3(a, b):
    a_hi, a_lo = _split_bf16(a, 2)
    b_hi, b_lo = _split_bf16(b, 2)
    d = lambda x, y: jnp.dot(x, y, preferred_element_type=F32)
    return d(a_hi, b_hi) + (d(a_hi, b_lo) + d(a_lo, b_hi))


def _dot_sel(a, b, terms, exact="rhs"):
    d = lambda x, y: jnp.dot(x, y, preferred_element_type=F32)
    if exact == "rhs":
        bb = b.astype(BF16)
        outs = [d(p, bb) for p in _split_bf16(a, terms)]
    else:
        ab = a.astype(BF16)
        outs = [d(ab, p) for p in _split_bf16(b, terms)]
    out = outs[-1]
    for o in reversed(outs[:-1]):
        out = out + o
    return out


def _half_mask(shape, dtype=F32):
    lane = lax.broadcasted_iota(jnp.int32, shape, len(shape) - 1)
    return (lane < 64).astype(dtype)


def _seg_ones():
    i = lax.broadcasted_iota(jnp.int32, (LANES, LANES), 0)
    j = lax.broadcasted_iota(jnp.int32, (LANES, LANES), 1)
    return ((i // 64) == (j // 64)).astype(F32)


def _layer_norm_rows(z, g, b, eps):
    mu = jnp.mean(z, axis=-1, keepdims=True)
    zc = z - mu
    var = jnp.mean(zc * zc, axis=-1, keepdims=True)
    return zc * lax.rsqrt(var + eps) * g + b


def _mod_kernel(c_ref, w_ref, b_ref, o_ref):
    a = c_ref[...]
    a = a * jax.nn.sigmoid(a)
    o_ref[0] = _hdot(a, w_ref[0]) + b_ref[0]


def _modulation(c8, w_mod, b_mod):
    tn = 1536
    nl, d, n6 = w_mod.shape
    return pl.pallas_call(
        _mod_kernel,
        grid=(nl, n6 // tn),
        in_specs=[pl.BlockSpec((8, d), lambda l, j: (0, 0)),
                  pl.BlockSpec((1, d, tn), lambda l, j: (l, 0, j)),
                  pl.BlockSpec((1, 1, tn), lambda l, j: (l, 0, j))],
        out_specs=pl.BlockSpec((1, 8, tn), lambda l, j: (l, 0, j)),
        out_shape=jax.ShapeDtypeStruct((nl, 8, n6), F32),
        compiler_params=_cparams(("parallel", "parallel")),
        name="modulation",
    )(c8, w_mod, b_mod.reshape(nl, 1, n6))


def _inproj_kernel(x_ref, sc_ref, sh_ref, w_ref, *o_refs, splits):
    h = x_ref[0] * (1.0 + sc_ref[0]) + sh_ref[0]
    y = _bdot(h, w_ref[...])
    off = 0
    for o, s in zip(o_refs, splits):
        o[0] = y[:, off:off + s].astype(o.dtype)
        off += s


def _inproj(x, sc, sh, w_bf16, splits, tm=512):
    b, t, d = x.shape
    tm = min(tm, t)
    dout = w_bf16.shape[1]
    return pl.pallas_call(
        functools.partial(_inproj_kernel, splits=splits),
        grid=(b, t // tm),
        in_specs=[pl.BlockSpec((1, tm, d), lambda i, j: (i, j, 0)),
                  pl.BlockSpec((1, 1, d), lambda i, j: (i, 0, 0)),
                  pl.BlockSpec((1, 1, d), lambda i, j: (i, 0, 0)),
                  pl.BlockSpec((d, dout), lambda i, j: (0, 0))],
        out_specs=[pl.BlockSpec((1, tm, s), lambda i, j: (i, j, 0)) for s in splits],
        out_shape=[jax.ShapeDtypeStruct((b, t, s), F32) for s in splits],
        compiler_params=_cparams(("parallel", "parallel")),
        name="inproj",
    )(x, sc, sh, w_bf16)


def _shifted(p, prev_row, next_row):
    tb = p.shape[0]
    row = lax.broadcasted_iota(jnp.int32, p.shape, 0)
    p_prev = jnp.where(row == 0, prev_row, pltpu.roll(p, 1, axis=0))
    p_next = jnp.where(row == tb - 1, next_row, pltpu.roll(p, tb - 1, axis=0))
    return p_prev, p_next


def _edge_rows(pp_ref, pn_ref):
    i = pl.program_id(1)
    nb = pl.num_programs(1)
    prev_row = jnp.where(i > 0, pp_ref[0, SUBLANES - 1:SUBLANES, :], 0.0)
    next_row = jnp.where(i < nb - 1, pn_ref[0, 0:1, :], 0.0)
    return prev_row, next_row


def _feat_kernel(p_ref, pp_ref, pn_ref, mu_ref, dup_ref, d0_ref, aup_ref, a0_ref, gup_ref, kxi_ref, kal_ref,
                 rb_ref, r_o, kk_o, v_o, kr0_o, kr1_o, lw0_o, lw1_o, b0_o, b1_o, g_o, gbv_o):
    p = p_ref[0]
    prev_row, next_row = _edge_rows(pp_ref, pn_ref)
    p_prev, p_next = _shifted(p, prev_row, next_row)
    p = p + mu_ref[...] * (0.5 * (p_prev + p_next) - p)
    dr = D_RWKV
    r = p[:, 0:dr]
    k = p[:, dr:2 * dr]
    v = p[:, 2 * dr:3 * dr]
    c0 = 3 * dr
    dd = jnp.tanh(p[:, c0:c0 + 2 * LORA])
    da = p[:, c0 + 2 * LORA:c0 + 4 * LORA]
    dg = jax.nn.sigmoid(p[:, c0 + 4 * LORA:c0 + 4 * LORA + LORA_G])
    g = _dot3(dg, gup_ref[...])
    lw, a = [], []
    for d in range(2):
        lw.append(-DECAY_SCALE * jax.nn.sigmoid(d0_ref[d:d + 1, :] + _dot3(dd[:, d * LORA:(d + 1) * LORA], dup_ref[d])))
        a.append(jax.nn.sigmoid(a0_ref[d:d + 1, :] + _dot3(da[:, d * LORA:(d + 1) * LORA], aup_ref[d])))
    kx = k * kxi_ref[...]
    kal = kal_ref[...]
    kr = [k * (1.0 + (a[d] - 1.0) * kal) for d in range(2)]
    bon = r * (0.5 * (kr[0] + kr[1])) * rb_ref[...]
    ones = _seg_ones()
    g_o[0] = g
    for j in range(N_PAIRS):
        sl = slice(j * LANES, (j + 1) * LANES)
        kxj = kx[:, sl]
        kk = kxj * lax.rsqrt(_dot_sel(kxj * kxj, ones, 2) + 1e-12)
        r_o[0, j] = r[:, sl].astype(r_o.dtype)
        kk_o[0, j] = kk.astype(kk_o.dtype)
        v_o[0, j] = v[:, sl].astype(v_o.dtype)
        kr0_o[0, j] = kr[0][:, sl].astype(kr0_o.dtype)
        kr1_o[0, j] = kr[1][:, sl].astype(kr1_o.dtype)
        lw0_o[0, j] = lw[0][:, sl]
        lw1_o[0, j] = lw[1][:, sl]
        b0_o[0, j] = (kk * a[0][:, sl]).astype(b0_o.dtype)
        b1_o[0, j] = (kk * a[1][:, sl]).astype(b1_o.dtype)
        gbv_o[0, :, sl] = g[:, sl] * _dot_sel(bon[:, sl], ones, 2) * v[:, sl]


def _rwkv_features(p, prm, tb=256):
    b, t, cols = p.shape
    tb = min(tb, t)
    nb8 = t // SUBLANES
    r8 = tb // SUBLANES
    full = lambda a: pl.BlockSpec(a.shape, lambda i, j: (0,) * a.ndim)
    params = [prm["shift_mu"].reshape(1, cols), prm["decay_up"], prm["decay_0"], prm["a_up"], prm["a_0"],
              prm["g_up"], prm["k_xi"].reshape(1, -1), prm["k_alpha"].reshape(1, -1), prm["r_bonus"].reshape(1, -1)]
    packed = lambda dt: jax.ShapeDtypeStruct((b, N_PAIRS, t, LANES), dt)
    flat = jax.ShapeDtypeStruct((b, t, D_RWKV), F32)
    pspec = pl.BlockSpec((1, N_PAIRS, tb, LANES), lambda i, j: (i, 0, j, 0))
    fspec = pl.BlockSpec((1, tb, D_RWKV), lambda i, j: (i, j, 0))
    return pl.pallas_call(
        _feat_kernel,
        grid=(b, t // tb),
        in_specs=[pl.BlockSpec((1, tb, cols), lambda i, j: (i, j, 0)),
                  pl.BlockSpec((1, SUBLANES, cols), lambda i, j: (i, jnp.maximum(j * r8 - 1, 0), 0)),
                  pl.BlockSpec((1, SUBLANES, cols), lambda i, j: (i, jnp.minimum((j + 1) * r8, nb8 - 1), 0))]
                 + [full(a) for a in params],
        out_specs=[pspec] * 9 + [fspec] * 2,
        out_shape=[packed(BF16)] * 5 + [packed(F32)] * 2 + [packed(BF16)] * 2 + [flat] * 2,
        compiler_params=_cparams(("parallel", "parallel")),
        name="rwkv_features",
    )(p, p, p, *params)


def _scan_chunks(feats, rev):
    n = SCAN_CHUNK
    m0 = _half_mask((n, LANES))
    m1 = 1.0 - m0
    ti = lax.broadcasted_iota(jnp.int32, (n, n), 0)
    tj = lax.broadcasted_iota(jnp.int32, (n, n), 1)
    tri = ((ti <= tj) if rev else (ti >= tj)).astype(F32)
    si = lax.broadcasted_iota(jnp.int32, (2 * n, 2 * n), 0)
    sj = lax.broadcasted_iota(jnp.int32, (2 * n, 2 * n), 1)
    same = (si // n) == (sj // n)
    ri, rj = si % n, sj % n
    strict = same & ((ri < rj) if rev else (ri > rj))
    incl = same & ((ri <= rj) if rev else (ri >= rj))
    eye = (si == sj).astype(F32)

    def stack(x):
        return jnp.concatenate([x * m0, x * m1], axis=0).astype(BF16)

    pre = []
    for r, lw, kap, bb, kr, v in feats:
        cs = _dot_sel(tri, lw, 3, exact="lhs")
        tot = jnp.sum(lw, axis=0, keepdims=True)
        e_neg = jnp.exp(-cs)
        e_tot = jnp.exp(tot - cs)
        rh = r * jnp.exp(cs)
        pre.append(dict(kh=stack(kap * jnp.exp(cs - lw)), bh=stack(bb * e_neg), kq=stack(kr * e_neg), rh=rh,
                        rhs=stack(rh), kt=stack(kr * e_tot), bt=stack(bb * e_tot), v=stack(v), gl=jnp.exp(tot)))
    nt = (((1,), (1,)), ((), ()))
    amats = [lax.dot_general(jnp.concatenate([p["kh"], p["rhs"]], axis=0), jnp.concatenate([p["bh"], p["kq"]], axis=0),
                             nt, preferred_element_type=F32) for p in pre]
    a1 = [jnp.where(strict, a[:2 * n, :2 * n], 0.0) for a in amats]
    a2 = [jnp.where(strict, a[:2 * n, 2 * n:], 0.0).astype(BF16) for a in amats]
    a4 = [jnp.where(incl, a[2 * n:, :2 * n], 0.0).astype(BF16) for a in amats]
    a3 = [jnp.where(incl, a[2 * n:, 2 * n:], 0.0).astype(BF16) for a in amats]
    tinv = [eye - a for a in a1]
    pw = a1
    for _ in range(5):
        pw = [_bdot(p, p) for p in pw]
        tinv = [t + _bdot(t, p) for t, p in zip(tinv, pw)]
    w_s = [_bdot(a, p["v"]) for a, p in zip(a2, pre)]
    mm = [_bdot(t, jnp.concatenate([p["kh"], w.astype(BF16)], axis=1)).astype(BF16)
          for t, p, w in zip(tinv, pre, w_s)]
    gmat = [_bdot(p["bt"].astype(F32).T, m) for p, m in zip(pre, mm)]
    phi_t = [eye * p["gl"] - g[:, :LANES] for p, g in zip(pre, gmat)]
    psi_t = [_bdot(p["kt"].astype(F32).T, p["v"]) - g[:, LANES:] for p, g in zip(pre, gmat)]
    qy = [_bdot(a, m) for a, m in zip(a4, mm)]
    y0_s = [_bdot(a, p["v"]) - q[:, LANES:] for a, p, q in zip(a3, pre, qy)]
    return [(p["rh"] - (q[:n, :LANES] + q[n:, :LANES]), y0[:n] + y0[n:], ph, ps)
            for p, q, y0, ph, ps in zip(pre, qy, y0_s, phi_t, psi_t)]


def _scan_kernel(r_ref, lw_ref, kap_ref, b_ref, kr_ref, v_ref, s0_ref, y_ref, sT_ref, st_scr, *, rev):
    c = pl.program_id(1)

    @pl.when(c == 0)
    def _():
        st_scr[...] = s0_ref[0]

    n = SCAN_CHUNK
    nsub = r_ref.shape[2] // n
    order = range(nsub - 1, -1, -1) if rev else range(nsub)
    rows = lambda ref, hp, k: ref[0, hp, k * n:(k + 1) * n, :]
    items = [(k, hp) for k in order for hp in range(N_PAIRS)]
    terms = _scan_chunks([tuple(rows(ref, hp, k) for ref in (r_ref, lw_ref, kap_ref, b_ref, kr_ref, v_ref))
                          for k, hp in items], rev)
    states = [st_scr[hp] for hp in range(N_PAIRS)]
    for (k, hp), (qm, y0, phi_t, psi_t) in zip(items, terms):
        y_ref[0, hp, k * n:(k + 1) * n, :] = _bdot(qm, states[hp]) + y0
        states[hp] = _bdot(phi_t, states[hp]) + psi_t
    for hp in range(N_PAIRS):
        st_scr[hp] = states[hp]

    @pl.when(c == pl.num_programs(1) - 1)
    def _():
        sT_ref[0] = st_scr[...]


def _rwkv_scan(r, lw, kap, bb, kr, v, s0, rev, chunks_per_step=4):
    b, _, t, _ = r.shape
    n = SCAN_CHUNK * math.gcd(chunks_per_step, t // SCAN_CHUNK)
    nc = t // n
    cidx = (lambda c: nc - 1 - c) if rev else (lambda c: c)
    fspec = pl.BlockSpec((1, N_PAIRS, n, LANES), lambda i, c: (i, 0, cidx(c), 0))
    sspec = pl.BlockSpec((1, N_PAIRS, LANES, LANES), lambda i, c: (i, 0, 0, 0))
    return pl.pallas_call(
        functools.partial(_scan_kernel, rev=rev),
        grid=(b, nc),
        in_specs=[fspec] * 6 + [sspec],
        out_specs=[fspec, sspec],
        out_shape=[jax.ShapeDtypeStruct(r.shape, F32), jax.ShapeDtypeStruct(s0.shape, F32)],
        scratch_shapes=[pltpu.VMEM((N_PAIRS, LANES, LANES), F32)],
        compiler_params=_cparams(("parallel", "arbitrary")),
        name="rwkv_scan_rev" if rev else "rwkv_scan_fwd",
    )(r, lw, kap, bb, kr, v, s0)


def _evenpost_kernel(y0_ref, y1_ref, g_ref, gbv_ref, pc_ref, pcp_ref, pcn_ref, cw_ref, gng_ref, gnb_ref,
                     conv_o, rw_o):
    pc = pc_ref[0]
    prev_row, next_row = _edge_rows(pcp_ref, pcn_ref)
    dc = D_CONV

    def gated(z):
        return z[:, dc:2 * dc] * z[:, 2 * dc:3 * dc]

    u = gated(pc)
    u_prev, u_next = _shifted(u, gated(prev_row), gated(next_row))
    cw = cw_ref[...]
    conv_o[0] = (pc[:, 0:dc] * (cw[0:1] * u_prev + cw[1:2] * u + cw[2:3] * u_next)).astype(conv_o.dtype)
    ones = _seg_ones() * (1.0 / RWKV_HD)
    for j in range(N_PAIRS):
        sl = slice(j * LANES, (j + 1) * LANES)
        y = y0_ref[0, j] + y1_ref[0, j]
        mu = _dot_sel(y, ones, 2)
        yc = y - mu
        var = _dot_sel(yc * yc, ones, 2)
        yn = yc * lax.rsqrt(var + GN_EPS) * gng_ref[:, sl] + gnb_ref[:, sl]
        rw_o[0, :, sl] = (g_ref[0, :, sl] * yn + gbv_ref[0, :, sl]).astype(rw_o.dtype)


def _even_post(y0, y1, g, gbv, pc, conv_w, gn_g, gn_b, tb=512):
    b, _, t, _ = y0.shape
    tb = min(tb, t)
    nb8 = t // SUBLANES
    r8 = tb // SUBLANES
    c3 = pc.shape[-1]
    pspec = pl.BlockSpec((1, N_PAIRS, tb, LANES), lambda i, j: (i, 0, j, 0))
    fspec = pl.BlockSpec((1, tb, D_RWKV), lambda i, j: (i, j, 0))
    full = lambda a: pl.BlockSpec(a.shape, lambda i, j: (0,) * a.ndim)
    gn_g = gn_g.reshape(1, -1)
    gn_b = gn_b.reshape(1, -1)
    return pl.pallas_call(
        _evenpost_kernel,
        grid=(b, t // tb),
        in_specs=[pspec, pspec, fspec, fspec,
                  pl.BlockSpec((1, tb, c3), lambda i, j: (i, j, 0)),
                  pl.BlockSpec((1, SUBLANES, c3), lambda i, j: (i, jnp.maximum(j * r8 - 1, 0), 0)),
                  pl.BlockSpec((1, SUBLANES, c3), lambda i, j: (i, jnp.minimum((j + 1) * r8, nb8 - 1), 0)),
                  full(conv_w), full(gn_g), full(gn_b)],
        out_specs=[pl.BlockSpec((1, tb, D_CONV), lambda i, j: (i, j, 0)), fspec],
        out_shape=[jax.ShapeDtypeStruct((b, t, D_CONV), BF16), jax.ShapeDtypeStruct((b, t, D_RWKV), BF16)],
        compiler_params=_cparams(("parallel", "parallel")),
        name="even_post",
    )(y0, y1, g, gbv, pc, pc, pc, conv_w, gn_g, gn_b)


def _outproj_kernel(a_ref, b_ref, wa_ref, wb_ref, x_ref, gm_ref, lg_ref, lb_ref, sc_ref, sh_ref, wr_ref,
                    o_ref, h_o, aff_o):
    yx = _bdot(a_ref[0], wa_ref[...]) + _bdot(b_ref[0], wb_ref[...])
    z = ALPHA * x_ref[0] + gm_ref[0] * yx
    xn = _layer_norm_rows(z, lg_ref[...], lb_ref[...], LN_EPS)
    o_ref[0] = xn
    h = xn * (1.0 + sc_ref[0]) + sh_ref[0]
    h_o[0] = h.astype(h_o.dtype)
    logits = lax.dot_general(wr_ref[...], h, (((1,), (1,)), ((), ())), preferred_element_type=F32, precision=HI)
    m = jnp.max(logits, axis=0, keepdims=True)
    e = jnp.exp(logits - m)
    aff_o[0] = e / jnp.sum(e, axis=0, keepdims=True)


def _outproj(ma, mb, w_out_bf16, x, gm, ln_g, ln_b, sc_f, sh_f, w_router_t, tm=512):
    b, t, d = x.shape
    tm = min(tm, t)
    e = w_router_t.shape[0]
    da, db = ma.shape[-1], mb.shape[-1]
    wa, wb = w_out_bf16[:da], w_out_bf16[da:]
    row = lambda w: pl.BlockSpec((1, tm, w), lambda i, j: (i, j, 0))
    full = lambda a: pl.BlockSpec(a.shape, lambda i, j: (0,) * a.ndim)
    vec = pl.BlockSpec((1, 1, d), lambda i, j: (i, 0, 0))
    ln_g = ln_g.reshape(1, d)
    ln_b = ln_b.reshape(1, d)
    return pl.pallas_call(
        _outproj_kernel,
        grid=(b, t // tm),
        in_specs=[row(da), row(db), full(wa), full(wb), row(d), vec, full(ln_g), full(ln_b), vec, vec,
                  full(w_router_t)],
        out_specs=[row(d), row(d), pl.BlockSpec((1, e, tm), lambda i, j: (i, 0, j))],
        out_shape=[jax.ShapeDtypeStruct((b, t, d), F32), jax.ShapeDtypeStruct((b, t, d), BF16),
                   jax.ShapeDtypeStruct((b, e, t), F32)],
        compiler_params=_cparams(("parallel", "parallel")),
        name="outproj_norm_router",
    )(ma, mb, wa, wb, x, gm, ln_g, ln_b, sc_f, sh_f, w_router_t)


def _inproj_attn_kernel(x_ref, sc_ref, sh_ref, w_ref, *refs, rope, qscale):
    if rope:
        cos_ref, sin_ref, q_o, k_o, v_o, g_o = refs
        cos, sin = cos_ref[...], sin_ref[...]
        lane = lax.broadcasted_iota(jnp.int32, cos.shape, 1)
        half = AXIS_DIM // 2
        first = (lane % (2 * half)) < half

        def rot(z):
            partner = jnp.where(first, -pltpu.roll(z, LANES - half, axis=1), pltpu.roll(z, half, axis=1))
            return z * cos + partner * sin
    else:
        q_o, k_o, v_o, g_o = refs
        rot = lambda z: z
    h = x_ref[0] * (1.0 + sc_ref[0]) + sh_ref[0]
    y = _bdot(h, w_ref[...])
    for j in range(DIFF_HEADS):
        sl = slice(j * LANES, (j + 1) * LANES)
        q_o[0, :, sl] = (rot(y[:, sl]) * qscale).astype(q_o.dtype)
        k_o[0, :, sl] = rot(y[:, D_DIFF + j * LANES:D_DIFF + (j + 1) * LANES]).astype(k_o.dtype)
    v_o[0] = y[:, 2 * D_DIFF:3 * D_DIFF].astype(v_o.dtype)
    g_o[0] = y[:, 3 * D_DIFF:]


def _inproj_attn(x, sc, sh, w_bf16, tables, qscale, tm=512):
    b, t, d = x.shape
    tm = min(tm, t)
    dout = w_bf16.shape[1]
    rope = tables is not None
    row = lambda w: pl.BlockSpec((1, tm, w), lambda i, j: (i, j, 0))
    vec = pl.BlockSpec((1, 1, d), lambda i, j: (i, 0, 0))
    tab = [pl.BlockSpec((tm, LANES), lambda i, j: (j, 0))] * 2 if rope else []
    return pl.pallas_call(
        functools.partial(_inproj_attn_kernel, rope=rope, qscale=qscale),
        grid=(b, t // tm),
        in_specs=[row(d), vec, vec, pl.BlockSpec((d, dout), lambda i, j: (0, 0))] + tab,
        out_specs=[row(D_DIFF)] * 3 + [row(dout - 3 * D_DIFF)],
        out_shape=[jax.ShapeDtypeStruct((b, t, D_DIFF), BF16)] * 3
                  + [jax.ShapeDtypeStruct((b, t, dout - 3 * D_DIFF), F32)],
        compiler_params=_cparams(("parallel", "parallel")),
        name="inproj_attn",
    )(x, sc, sh, w_bf16, *(tables if rope else ()))


def _rope_tables(t):
    rows = t // GRID_W
    row = jnp.repeat(jnp.arange(rows), GRID_W).astype(F32)
    col = jnp.tile(jnp.arange(GRID_W), rows).astype(F32)
    inv = ROPE_BASE ** (-jnp.arange(0, AXIS_DIM, 2, dtype=F32) / AXIS_DIM)
    ang_r = row[:, None] * inv
    ang_c = col[:, None] * inv
    ang = jnp.concatenate([ang_r, ang_r, ang_c, ang_c], axis=-1)
    ang = jnp.concatenate([ang, ang], axis=-1)
    return jnp.cos(ang), jnp.sin(ang)


def _attn_kernel(lam_ref, q_ref, k_ref, v_ref, g_ref, o_ref, *, tk, out_scale):
    q = q_ref[0]
    tq = q.shape[0]
    hm = _half_mask(q.shape, BF16)
    q1 = q * hm
    qs = (q1, q - q1)
    nk = k_ref.shape[1] // tk
    nsl = tk // LANES

    def scores(qh, j):
        kb = k_ref[0, pl.ds(pl.multiple_of(j * tk, tk), tk), :]
        return lax.dot_general(qh, kb, (((1,), (1,)), ((), ())), preferred_element_type=F32)

    def col(s, c):
        return s[:, c * LANES:(c + 1) * LANES]

    def body(j, carry):
        vb = v_ref[0, pl.ds(pl.multiple_of(j * tk, tk), tk), :]
        ss = [scores(qh, j) for qh in qs]
        out = []
        for s, (m, ls, acc) in zip(ss, (carry[:3], carry[3:])):
            mx = col(s, 0)
            for c in range(1, nsl):
                mx = jnp.maximum(mx, col(s, c))
            m_new = jnp.maximum(m, jnp.broadcast_to(jnp.max(mx, axis=-1, keepdims=True), (tq, LANES)))
            corr = jnp.exp2(m - m_new)
            ps = [jnp.exp2(col(s, c) - m_new) for c in range(nsl)]
            ls = corr * ls
            for pc in ps:
                ls = ls + pc
            p = jnp.concatenate([pc.astype(BF16) for pc in ps], axis=1)
            acc = corr * acc + jnp.dot(p, vb, preferred_element_type=F32)
            out += [m_new, ls, acc]
        return tuple(out)

    neg = jnp.full((tq, LANES), -jnp.inf, F32)
    zero = jnp.zeros((tq, LANES), F32)
    _, ls1, acc1, _, ls2, acc2 = lax.fori_loop(0, nk, body, (neg, zero, zero, neg, zero, zero), unroll=True)
    l1 = jnp.sum(ls1, axis=-1, keepdims=True)
    l2 = jnp.sum(ls2, axis=-1, keepdims=True)
    o = acc1 / l1 - lam_ref[0] * (acc2 / l2)
    o = o * lax.rsqrt(jnp.mean(o * o, axis=-1, keepdims=True) + RMS_EPS) * g_ref[...] * out_scale
    o_ref[0] = o.astype(o_ref.dtype)


def _diff_attention(q, k, v, lam, subln_g, lam_init, tq=512, tk=1408):
    b, t, w = q.shape
    tkk = k.shape[1]
    tq = min(tq, t)
    tk = math.gcd(tk, tkk)
    assert tk % LANES == 0 and t % tq == 0
    subln_g = subln_g.reshape(1, DIFF_VD)
    return pl.pallas_call(
        functools.partial(_attn_kernel, tk=tk, out_scale=1.0 - lam_init),
        grid=(b, DIFF_HEADS, t // tq),
        in_specs=[pl.BlockSpec(memory_space=pltpu.SMEM),
                  pl.BlockSpec((1, tq, LANES), lambda i, h, j: (i, j, h)),
                  pl.BlockSpec((1, tkk, LANES), lambda i, h, j: (i, 0, h)),
                  pl.BlockSpec((1, tkk, LANES), lambda i, h, j: (i, 0, h)),
                  pl.BlockSpec((1, DIFF_VD), lambda i, h, j: (0, 0))],
        out_specs=pl.BlockSpec((1, tq, LANES), lambda i, h, j: (i, j, h)),
        out_shape=jax.ShapeDtypeStruct((b, t, w), BF16),
        compiler_params=_cparams(("parallel", "parallel", "parallel")),
        name="diff_attention",
    )(lam, q, k, v, subln_g)


def _gmlp_kernel(p_ref, lg_ref, lb_ref, ws_ref, bs_ref, o_ref):
    p = p_ref[0]
    tb = p.shape[0]
    ge = 0.5 * p * (1.0 + lax.erf(p * (2.0 ** -0.5)))
    u = ge[:, :D_GMLP]
    v = _layer_norm_rows(ge[:, D_GMLP:], lg_ref[...], lb_ref[...], LN_EPS)
    hm = _half_mask((GMLP_CHUNK, LANES))
    for c in range(tb // GMLP_CHUNK):
        rs = slice(c * GMLP_CHUNK, (c + 1) * GMLP_CHUNK)
        for j in range(D_GMLP // LANES):
            sl = slice(j * LANES, (j + 1) * LANES)
            vc = v[rs, sl]
            va = vc * hm
            mixed = _bdot(ws_ref[2 * j], va) + _bdot(ws_ref[2 * j + 1], vc - va) + bs_ref[:, sl]
            o_ref[0, rs, sl] = (u[rs, sl] * mixed).astype(o_ref.dtype)


def _chunk_gmlp(p, ln_g, ln_b, ws, bs, tb=256):
    b, t, w = p.shape
    tb = min(tb, t)
    ln_g = ln_g.reshape(1, -1)
    ln_b = ln_b.reshape(1, -1)
    bs_t = jnp.repeat(jnp.transpose(bs), D_GMLP // ws.shape[0], axis=1)
    full = lambda a: pl.BlockSpec(a.shape, lambda i, j: (0,) * a.ndim)
    return pl.pallas_call(
        _gmlp_kernel,
        grid=(b, t // tb),
        in_specs=[pl.BlockSpec((1, tb, w), lambda i, j: (i, j, 0)), full(ln_g), full(ln_b), full(ws), full(bs_t)],
        out_specs=pl.BlockSpec((1, tb, D_GMLP), lambda i, j: (i, j, 0)),
        out_shape=jax.ShapeDtypeStruct((b, t, D_GMLP), BF16),
        compiler_params=_cparams(("parallel", "parallel")),
        name="chunk_gmlp",
    )(p, ln_g, ln_b, ws, bs_t)


ROUTE_BLOCK = 128
STARTS_PAD = LANES


def _lane_cumsum(mask):
    e, t = mask.shape
    i = lax.broadcasted_iota(jnp.int32, (LANES, LANES), 0)
    j = lax.broadcasted_iota(jnp.int32, (LANES, LANES), 1)
    upper = (i <= j).astype(BF16)
    x = jnp.where(mask, 1.0, 0.0).astype(BF16)
    off = jnp.zeros((e, 1), F32)
    out = []
    for c in range(t // LANES):
        blk = jnp.dot(x[:, c * LANES:(c + 1) * LANES], upper, preferred_element_type=F32) + off
        out.append(blk)
        off = blk[:, LANES - 1:LANES]
    return jnp.concatenate(out, axis=1)


def _select_kernel(aff_ref, pos_ref, starts_ref, *, cap):
    a = aff_ref[0]
    e, t = a.shape
    bits = pltpu.bitcast(a, jnp.int32)

    def search(i, thr):
        cand = thr | jnp.left_shift(jnp.int32(1), 30 - i)
        cnt = jnp.sum(jnp.where(bits >= cand, 1.0, 0.0), axis=1, keepdims=True)
        return jnp.where(cnt >= cap, cand, thr)

    thr = lax.fori_loop(0, 31, search, jnp.zeros((e, 1), jnp.int32))
    gt = bits > thr
    eq = bits == thr
    need = cap - jnp.sum(jnp.where(gt, 1.0, 0.0), axis=1, keepdims=True)
    sel = gt | (eq & (_lane_cumsum(eq) <= need))
    csel = _lane_cumsum(sel)
    pos_ref[0] = jnp.where(sel, csel - 1.0, -1.0).astype(jnp.int32)
    ti = lax.broadcasted_iota(jnp.int32, (t, STARTS_PAD), 0)
    ki = lax.broadcasted_iota(jnp.int32, (t, STARTS_PAD), 1)
    pick = jnp.where(ti == ki * ROUTE_BLOCK - 1, 1.0, 0.0)
    starts_ref[0] = (_dot_sel(csel, pick, 2) + 0.5).astype(jnp.int32)


def _select(aff, cap):
    b, e, t = aff.shape
    assert t % ROUTE_BLOCK == 0 and t // ROUTE_BLOCK < STARTS_PAD
    return pl.pallas_call(
        functools.partial(_select_kernel, cap=cap),
        grid=(b,),
        in_specs=[pl.BlockSpec((1, e, t), lambda i: (i, 0, 0))],
        out_specs=[pl.BlockSpec((1, e, t), lambda i: (i, 0, 0)), pl.BlockSpec((1, e, STARTS_PAD), lambda i: (i, 0, 0))],
        out_shape=[jax.ShapeDtypeStruct((b, e, t), jnp.int32), jax.ShapeDtypeStruct((b, e, STARTS_PAD), jnp.int32)],
        compiler_params=_cparams(("parallel",)),
        name="expert_select",
    )(aff)


def _window_start(start, align, cap, width):
    s = lax.shift_left(lax.shift_right_logical(start, align.bit_length() - 1), align.bit_length() - 1)
    return pl.multiple_of(jnp.minimum(s, cap - width), align)


def _gather_ffn_kernel(starts_ref, pos_ref, aff_ref, h_ref, w1_ref, w3_ref, w2_ref, y_ref, xs_scr, gate_scr, *,
                       width, rows, cap):
    ei, bi, k = pl.program_id(0), pl.program_id(1), pl.program_id(2)
    ne, nb = pl.num_programs(0), pl.num_programs(1)
    nsub = h_ref.shape[1] // ROUTE_BLOCK
    merged = xs_scr.shape[0] > cap
    base = pl.multiple_of(bi * cap, SUBLANES) if merged else 0

    @pl.when(k == 0)
    def _():
        xs_scr[pl.ds(base, cap), :] = jnp.zeros((cap, xs_scr.shape[1]), F32)
        gate_scr[pl.ds(base, cap), :] = jnp.zeros((cap, LANES), F32)

    def copy_rows(sb, start, w):
        s8 = _window_start(start, SUBLANES, cap, w)
        cols = slice(sb * ROUTE_BLOCK, (sb + 1) * ROUTE_BLOCK)
        jrow = lax.broadcasted_iota(jnp.int32, (w, ROUTE_BLOCK), 0)
        hit = pos_ref[0, :, cols] - s8 == jrow
        xs_scr[pl.ds(base + s8, w), :] += jnp.dot(jnp.where(hit, 1.0, 0.0).astype(BF16), h_ref[0, cols, :],
                                                  preferred_element_type=F32)
        gate = jnp.sum(jnp.where(hit, aff_ref[0, :, cols], 0.0), axis=1, keepdims=True)
        gate_scr[pl.ds(base + s8, w), :] += jnp.broadcast_to(gate, (w, LANES))

    for sb in range(nsub):
        copy_rows(sb, starts_ref[(bi * ne + ei) * STARTS_PAD + k * nsub + sb], width)

    last = k == pl.num_programs(2) - 1

    @pl.when(last & (bi == nb - 1) if merged else last)
    def _():
        for r in range(xs_scr.shape[0] // rows):
            x = xs_scr[r * rows:(r + 1) * rows, :].astype(BF16)
            h1 = jnp.dot(x, w1_ref[0], preferred_element_type=F32)
            h3 = jnp.dot(x, w3_ref[0], preferred_element_type=F32)
            hid = (h1 * jax.nn.sigmoid(h1)) * h3
            y = jnp.dot(hid.astype(BF16), w2_ref[0], preferred_element_type=F32)
            y = (y * gate_scr[r * rows:(r + 1) * rows, 0:1]).astype(y_ref.dtype)
            for s in range(max(rows // cap, 1)):
                n = min(rows, cap)
                row0 = r * rows + s * cap
                y_ref[row0 // cap, 0, row0 % cap:row0 % cap + n, :] = y[s * cap:s * cap + n]


def _gather_ffn(h, pos, aff, starts, w1, w3, w2, cap, tch=4096, rows=512):
    b, t, d = h.shape
    e, _, f = w1.shape
    tch = min(tch, t)
    merged = b * cap <= rows
    slots = b if merged else 1
    rows = min(rows, slots * cap)
    assert (slots * cap) % rows == 0 and (rows % cap == 0 or cap % rows == 0)
    width = min(ROUTE_BLOCK + SUBLANES, cap)
    grid_spec = pltpu.PrefetchScalarGridSpec(
        num_scalar_prefetch=1,
        grid=(e, b, t // tch),
        in_specs=[pl.BlockSpec((1, 1, tch), lambda ei, bi, k, s: (bi * e + ei, 0, k)),
                  pl.BlockSpec((1, 1, tch), lambda ei, bi, k, s: (bi * e + ei, 0, k)),
                  pl.BlockSpec((1, tch, d), lambda ei, bi, k, s: (bi, k, 0)),
                  pl.BlockSpec((1, d, f), lambda ei, bi, k, s: (ei, 0, 0)),
                  pl.BlockSpec((1, d, f), lambda ei, bi, k, s: (ei, 0, 0)),
                  pl.BlockSpec((1, f, d), lambda ei, bi, k, s: (ei, 0, 0))],
        out_specs=pl.BlockSpec((slots, 1, cap, d), (lambda ei, bi, k, s: (0, ei, 0, 0)) if merged
                               else (lambda ei, bi, k, s: (bi, ei, 0, 0))),
        scratch_shapes=[pltpu.VMEM((slots * cap, d), F32), pltpu.VMEM((slots * cap, LANES), F32)],
    )
    return pl.pallas_call(
        functools.partial(_gather_ffn_kernel, width=width, rows=rows, cap=cap),
        grid_spec=grid_spec,
        out_shape=jax.ShapeDtypeStruct((b, e, cap, d), BF16),
        compiler_params=_cparams(("parallel", "arbitrary", "arbitrary") if merged
                                 else ("parallel", "parallel", "arbitrary")),
        name="expert_gather_ffn",
    )(starts.reshape(-1), pos.reshape(b * e, 1, t), aff.reshape(b * e, 1, t), h, w1, w3, w2)


def _combine_kernel(starts_ref, post_ref, y_ref, x_ref, gf_ref, lg_ref, lb_ref, o_ref, acc_scr, *, width, group):
    bi, i, eg = pl.program_id(0), pl.program_id(1), pl.program_id(2)
    ne = pl.num_programs(2) * group
    cap = y_ref.shape[2]
    nsub = x_ref.shape[1] // ROUTE_BLOCK

    @pl.when(eg == 0)
    def _():
        acc_scr[...] = jnp.zeros_like(acc_scr)

    lane = lax.broadcasted_iota(jnp.int32, (ROUTE_BLOCK, ne), 1)
    jcol = lax.broadcasted_iota(jnp.int32, (ROUTE_BLOCK, width), 1)
    for sb in range(nsub):
        rs = slice(sb * ROUTE_BLOCK, (sb + 1) * ROUTE_BLOCK)
        pblk = post_ref[0, rs, :].astype(F32)
        total = None
        for j in range(group):
            ei = eg * group + j
            start = starts_ref[(bi * ne + ei) * STARTS_PAD + i * nsub + sb]
            s16 = _window_start(start, 2 * SUBLANES, cap, width)
            pcol = jnp.sum(jnp.where(lane == ei, pblk, 0.0), axis=1, keepdims=True)
            onehot = jnp.where(pcol.astype(jnp.int32) - s16 == jcol, 1.0, 0.0).astype(BF16)
            part = jnp.dot(onehot, y_ref[0, j, pl.ds(s16, width), :], preferred_element_type=F32)
            total = part if total is None else total + part
        acc_scr[rs, :] += total

    @pl.when(eg == pl.num_programs(2) - 1)
    def _():
        z = ALPHA * x_ref[0] + gf_ref[0] * acc_scr[...]
        o_ref[0] = _layer_norm_rows(z, lg_ref[...], lb_ref[...], LN_EPS)


def _combine(x, y, pos_t, starts, gf, ln_g, ln_b, tbo=1024, group=4):
    b, t, d = x.shape
    _, e, cap, _ = y.shape
    tbo = min(tbo, t)
    width = min(ROUTE_BLOCK + 2 * SUBLANES, cap)
    grid_spec = pltpu.PrefetchScalarGridSpec(
        num_scalar_prefetch=1,
        grid=(b, t // tbo, e // group),
        in_specs=[pl.BlockSpec((1, tbo, e), lambda bi, i, eg, s: (bi, i, 0)),
                  pl.BlockSpec((1, group, cap, d), lambda bi, i, eg, s: (bi, eg, 0, 0)),
                  pl.BlockSpec((1, tbo, d), lambda bi, i, eg, s: (bi, i, 0)),
                  pl.BlockSpec((1, 1, d), lambda bi, i, eg, s: (bi, 0, 0)),
                  pl.BlockSpec((1, d), lambda bi, i, eg, s: (0, 0)),
                  pl.BlockSpec((1, d), lambda bi, i, eg, s: (0, 0))],
        out_specs=pl.BlockSpec((1, tbo, d), lambda bi, i, eg, s: (bi, i, 0)),
        scratch_shapes=[pltpu.VMEM((tbo, d), F32)],
    )
    return pl.pallas_call(
        functools.partial(_combine_kernel, width=width, group=group),
        grid_spec=grid_spec,
        out_shape=jax.ShapeDtypeStruct((b, t, d), F32),
        compiler_params=_cparams(("parallel", "parallel", "arbitrary")),
        name="expert_combine_norm",
    )(starts.reshape(-1), pos_t, y, x, gf, ln_g.reshape(1, d), ln_b.reshape(1, d))


def _cast_kernel(x_ref, o_ref):
    o_ref[...] = x_ref[...].astype(o_ref.dtype)


def _to_bf16(w, l):
    _, e, r, c = w.shape
    return pl.pallas_call(
        _cast_kernel,
        grid=(e,),
        in_specs=[pl.BlockSpec((None, 1, r, c), lambda i: (l, i, 0, 0))],
        out_specs=pl.BlockSpec((1, r, c), lambda i: (i, 0, 0)),
        out_shape=jax.ShapeDtypeStruct((e, r, c), BF16),
        compiler_params=_cparams(("parallel",)),
        name="cast_bf16",
    )(w)


def _moe(x, h, aff, gf, w1, w3, w2, ln_g, ln_b):
    t = x.shape[1]
    cap = EC_FACTOR * t // N_EXPERTS
    pos, starts = _select(aff, cap)
    y = _gather_ffn(h, pos, aff, starts, w1, w3, w2, cap)
    return _combine(x, y, jnp.transpose(pos, (0, 2, 1)), starts, gf, ln_g, ln_b)


def _even_layer(hx_in, hc_in, prm, ctx_out):
    w_in = prm["w_in"]
    splits = (3 * D_CONV, RWKV_COLS)
    pcx, prx = _inproj(*hx_in, w_in, splits)
    pcc, prc = _inproj(*hc_in, w_in, splits)
    fx = _rwkv_features(prx, prm)
    fc = _rwkv_features(prc, prm)
    b = prx.shape[0]
    zero = jnp.zeros((b, N_PAIRS, LANES, LANES), F32)
    ys_x, ys_c = [], []
    for d, rev in ((0, False), (1, True)):
        def args(f):
            r, kk, v, kr0, kr1, lw0, lw1, b0, b1 = f[:9]
            return (r, (lw0, lw1)[d], kk, (b0, b1)[d], (kr0, kr1)[d], v)
        yc, s_ctx = _rwkv_scan(*args(fc), zero, rev)
        yx, _ = _rwkv_scan(*args(fx), s_ctx, rev)
        ys_x.append(yx)
        ys_c.append(yc)
    out_x = _even_post(ys_x[0], ys_x[1], fx[9], fx[10], pcx, prm["conv_w"], prm["gn_g"], prm["gn_b"])
    out_c = None
    if ctx_out:
        out_c = _even_post(ys_c[0], ys_c[1], fc[9], fc[10], pcc, prm["conv_w"], prm["gn_g"], prm["gn_b"])
    return out_c, out_x


def _odd_layer(hx_in, hc_in, prm, lam_init, ctx_out):
    w_in = prm["w_in"]
    qscale = DIFF_HD ** -0.5 * math.log2(math.e)
    qx, kx, vx, gx = _inproj_attn(*hx_in, w_in, _rope_tables(hx_in[0].shape[1]), qscale)
    qc, kc, vc, gc = _inproj_attn(*hc_in, w_in, None, qscale)
    k_all = jnp.concatenate([kc, kx], axis=1)
    v_all = jnp.concatenate([vc, vx], axis=1)
    lam = (jnp.exp(jnp.sum(prm["lam_q1"] * prm["lam_k1"])) - jnp.exp(jnp.sum(prm["lam_q2"] * prm["lam_k2"]))
           + lam_init).reshape(1).astype(F32)
    att_x = _diff_attention(qx, k_all, v_all, lam, prm["subln_g"], lam_init)
    gm_x = _chunk_gmlp(gx, prm["gmlp_ln_g"], prm["gmlp_ln_b"], prm["gmlp_ws"], prm["gmlp_bs"])
    out_c = None
    if ctx_out:
        att_c = _diff_attention(qc, kc, vc, lam, prm["subln_g"], lam_init)
        gm_c = _chunk_gmlp(gc, prm["gmlp_ln_g"], prm["gmlp_ln_b"], prm["gmlp_ws"], prm["gmlp_bs"])
        out_c = (att_c, gm_c)
    return out_c, (att_x, gm_x)


def kernel(x, c, ctx, c_ctx, w_mod, b_mod, ln_g, ln_b, even_w_in, even_w_out, conv_w, shift_mu, decay_up, decay_0, a_up, a_0, g_up, k_xi, k_alpha, r_bonus, gn_g, gn_b, odd_w_in, odd_w_out, lam_q1, lam_k1, lam_q2, lam_k2, subln_g, gmlp_ln_g, gmlp_ln_b, gmlp_ws, gmlp_bs, w_router, w_e1, w_e3, w_e2):
    bsz, _, d = x.shape
    assert bsz <= SUBLANES - 1
    c8 = jnp.zeros((SUBLANES, d), F32).at[:bsz].set(c).at[bsz].set(c_ctx)
    mod = _modulation(c8, w_mod, b_mod)
    for l in range(DEPTH):
        ctx_out = l < DEPTH - 1
        i = l // 2
        mx = mod[l, :bsz].reshape(bsz, 1, 6, d)
        mc = jnp.broadcast_to(mod[l, bsz].reshape(1, 1, 6, d), (bsz, 1, 6, d))
        part = lambda m, n: m[:, :, n]
        hx_in = (x, part(mx, 1), part(mx, 0))
        hc_in = (ctx, part(mc, 1), part(mc, 0))
        if l % 2 == 0:
            prm = dict(w_in=even_w_in[i].astype(BF16), conv_w=conv_w[i], shift_mu=shift_mu[i],
                       decay_up=decay_up[i], decay_0=decay_0[i], a_up=a_up[i], a_0=a_0[i], g_up=g_up[i],
                       k_xi=k_xi[i], k_alpha=k_alpha[i], r_bonus=r_bonus[i], gn_g=gn_g[i], gn_b=gn_b[i])
            out_c, out_x = _even_layer(hx_in, hc_in, prm, ctx_out)
            w_out = even_w_out[i].astype(BF16)
        else:
            lam_init = 0.8 - 0.6 * math.exp(-0.3 * l)
            prm = dict(w_in=odd_w_in[i].astype(BF16), lam_q1=lam_q1[i], lam_k1=lam_k1[i], lam_q2=lam_q2[i],
                       lam_k2=lam_k2[i], subln_g=subln_g[i], gmlp_ln_g=gmlp_ln_g[i], gmlp_ln_b=gmlp_ln_b[i],
                       gmlp_ws=gmlp_ws[i], gmlp_bs=gmlp_bs[i])
            out_c, out_x = _odd_layer(hx_in, hc_in, prm, lam_init, ctx_out)
            w_out = odd_w_out[i].astype(BF16)
        wr_t = jnp.transpose(w_router[l])
        w1, w3, w2 = _to_bf16(w_e1, l), _to_bf16(w_e3, l), _to_bf16(w_e2, l)
        x, h, aff = _outproj(out_x[0], out_x[1], w_out, x, part(mx, 2), ln_g[l, 0], ln_b[l, 0],
                             part(mx, 4), part(mx, 3), wr_t)
        x = _moe(x, h, aff, part(mx, 5), w1, w3, w2, ln_g[l, 1], ln_b[l, 1])
        if ctx_out:
            ctx, h, aff = _outproj(out_c[0], out_c[1], w_out, ctx, part(mc, 2), ln_g[l, 0], ln_b[l, 0],
                                   part(mc, 4), part(mc, 3), wr_t)
            ctx = _moe(ctx, h, aff, part(mc, 5), w1, w3, w2, ln_g[l, 1], ln_b[l, 1])
    return x
```

```python
import functools
import math

import jax
import jax.numpy as jnp
from jax import lax
from jax.experimental import pallas as pl
from jax.experimental.pallas import tpu as pltpu

F32 = jnp.float32
BF16 = jnp.bfloat16
HI = lax.Precision.HIGHEST

D_MODEL = 1024
DEPTH = 4
GRID_W = 64
D_CONV = 256
RWKV_HEADS = 12
RWKV_HD = 64
D_RWKV = RWKV_HEADS * RWKV_HD
LORA = 64
LORA_G = 128
RWKV_COLS = 3 * D_RWKV + 4 * LORA + LORA_G
DECAY_SCALE = math.exp(-0.5)
GN_EPS = 64e-5
N_PAIRS = RWKV_HEADS // 2
SCAN_CHUNK = 64
DIFF_HEADS = 6
DIFF_HD = 64
DIFF_VD = 2 * DIFF_HD
D_DIFF = DIFF_HEADS * DIFF_VD
AXIS_DIM = DIFF_HD // 2
ROPE_BASE = 10000.0
D_GMLP = 256
GMLP_CHUNK = 128
N_EXPERTS = 16
EC_FACTOR = 2
ALPHA = (2.0 * DEPTH) ** 0.25
LN_EPS = 1e-5
RMS_EPS = 1e-5

LANES = 128
SUBLANES = 8
VMEM_LIMIT = 56 * 1024 * 1024
HEAD_LANES = LANES // 2
assert RWKV_HD == DIFF_HD == D_GMLP // 4 == HEAD_LANES and SCAN_CHUNK == HEAD_LANES


def _cparams(sem):
    return pltpu.CompilerParams(dimension_semantics=sem, vmem_limit_bytes=VMEM_LIMIT)


def _bdot(a, b):
    return jnp.dot(a.astype(BF16), b.astype(BF16), preferred_element_type=F32)


def _hdot(a, b):
    return jnp.dot(a, b, preferred_element_type=F32, precision=HI)


def _split_bf16(x, terms):
    parts = []
    for _ in range(terms):
        p = x.astype(BF16)
        parts.append(p)
        x = x - p.astype(F32)
    return parts


def _dot3(a, b):
    a_hi, a_lo = _split_bf16(a, 2)
    b_hi, b_lo = _split_bf16(b, 2)
    d = lambda x, y: jnp.dot(x, y, preferred_element_type=F32)
    return d(a_hi, b_hi) + (d(a_hi, b_lo) + d(a_lo, b_hi))


def _dot_sel(a, b, terms, exact="rhs"):
    d = lambda x, y: jnp.dot(x, y, preferred_element_type=F32)
    if exact == "rhs":
        bb = b.astype(BF16)
        outs = [d(p, bb) for p in _split_bf16(a, terms)]
    else:
        ab = a.astype(BF16)
        outs = [d(ab, p) for p in _split_bf16(b, terms)]
    out = outs[-1]
    for o in reversed(outs[:-1]):
        out = out + o
    return out


def _half_mask(shape, dtype=F32):
    lane = lax.broadcasted_iota(jnp.int32, shape, len(shape) - 1)
    return (lane < HEAD_LANES).astype(dtype)


def _seg_ones():
    i = lax.broadcasted_iota(jnp.int32, (LANES, LANES), 0)
    j = lax.broadcasted_iota(jnp.int32, (LANES, LANES), 1)
    return ((i // HEAD_LANES) == (j // HEAD_LANES)).astype(F32)


def _layer_norm_rows(z, g, b, eps):
    mu = jnp.mean(z, axis=-1, keepdims=True)
    zc = z - mu
    var = jnp.mean(zc * zc, axis=-1, keepdims=True)
    return zc * lax.rsqrt(var + eps) * g + b


def _mod_kernel(c_ref, w_ref, b_ref, o_ref):
    a = c_ref[...]
    a = a * jax.nn.sigmoid(a)
    o_ref[0] = _hdot(a, w_ref[0]) + b_ref[0]


def _modulation(c8, w_mod, b_mod):
    tn = 1536
    nl, d, n6 = w_mod.shape
    return pl.pallas_call(
        _mod_kernel,
        grid=(nl, n6 // tn),
        in_specs=[pl.BlockSpec((8, d), lambda l, j: (0, 0)),
                  pl.BlockSpec((1, d, tn), lambda l, j: (l, 0, j)),
                  pl.BlockSpec((1, 1, tn), lambda l, j: (l, 0, j))],
        out_specs=pl.BlockSpec((1, 8, tn), lambda l, j: (l, 0, j)),
        out_shape=jax.ShapeDtypeStruct((nl, 8, n6), F32),
        compiler_params=_cparams(("parallel", "parallel")),
        name="modulation",
    )(c8, w_mod, b_mod.reshape(nl, 1, n6))


def _inproj_kernel(x_ref, sc_ref, sh_ref, w_ref, *o_refs, splits):
    h = x_ref[0] * (1.0 + sc_ref[0]) + sh_ref[0]
    y = _bdot(h, w_ref[...])
    off = 0
    for o, s in zip(o_refs, splits):
        o[0] = y[:, off:off + s].astype(o.dtype)
        off += s


def _inproj(x, sc, sh, w_bf16, splits, tm=512):
    b, t, d = x.shape
    tm = min(tm, t)
    dout = w_bf16.shape[1]
    return pl.pallas_call(
        functools.partial(_inproj_kernel, splits=splits),
        grid=(b, t // tm),
        in_specs=[pl.BlockSpec((1, tm, d), lambda i, j: (i, j, 0)),
                  pl.BlockSpec((1, 1, d), lambda i, j: (i, 0, 0)),
                  pl.BlockSpec((1, 1, d), lambda i, j: (i, 0, 0)),
                  pl.BlockSpec((d, dout), lambda i, j: (0, 0))],
        out_specs=[pl.BlockSpec((1, tm, s), lambda i, j: (i, j, 0)) for s in splits],
        out_shape=[jax.ShapeDtypeStruct((b, t, s), F32) for s in splits],
        compiler_params=_cparams(("parallel", "parallel")),
        name="inproj",
    )(x, sc, sh, w_bf16)


def _shifted(p, prev_row, next_row):
    tb = p.shape[0]
    row = lax.broadcasted_iota(jnp.int32, p.shape, 0)
    p_prev = jnp.where(row == 0, prev_row, pltpu.roll(p, 1, axis=0))
    p_next = jnp.where(row == tb - 1, next_row, pltpu.roll(p, tb - 1, axis=0))
    return p_prev, p_next


def _edge_rows(pp_ref, pn_ref):
    i = pl.program_id(1)
    nb = pl.num_programs(1)
    prev_row = jnp.where(i > 0, pp_ref[0, SUBLANES - 1:SUBLANES, :], 0.0)
    next_row = jnp.where(i < nb - 1, pn_ref[0, 0:1, :], 0.0)
    return prev_row, next_row


def _feat_kernel(p_ref, pp_ref, pn_ref, mu_ref, dup_ref, d0_ref, aup_ref, a0_ref, gup_ref, kxi_ref, kal_ref,
                 rb_ref, r_o, kk_o, v_o, kr0_o, kr1_o, lw0_o, lw1_o, b0_o, b1_o, g_o, gbv_o):
    p = p_ref[0]
    prev_row, next_row = _edge_rows(pp_ref, pn_ref)
    p_prev, p_next = _shifted(p, prev_row, next_row)
    p = p + mu_ref[...] * (0.5 * (p_prev + p_next) - p)
    dr = D_RWKV
    r = p[:, 0:dr]
    k = p[:, dr:2 * dr]
    v = p[:, 2 * dr:3 * dr]
    c0 = 3 * dr
    dd = jnp.tanh(p[:, c0:c0 + 2 * LORA])
    da = p[:, c0 + 2 * LORA:c0 + 4 * LORA]
    dg = jax.nn.sigmoid(p[:, c0 + 4 * LORA:c0 + 4 * LORA + LORA_G])
    g = _dot3(dg, gup_ref[...])
    lw, a = [], []
    for d in range(2):
        lw.append(-DECAY_SCALE * jax.nn.sigmoid(d0_ref[d:d + 1, :] + _dot3(dd[:, d * LORA:(d + 1) * LORA], dup_ref[d])))
        a.append(jax.nn.sigmoid(a0_ref[d:d + 1, :] + _dot3(da[:, d * LORA:(d + 1) * LORA], aup_ref[d])))
    kx = k * kxi_ref[...]
    kal = kal_ref[...]
    kr = [k * (1.0 + (a[d] - 1.0) * kal) for d in range(2)]
    bon = r * (0.5 * (kr[0] + kr[1])) * rb_ref[...]
    ones = _seg_ones()
    g_o[0] = g
    for j in range(N_PAIRS):
        sl = slice(j * LANES, (j + 1) * LANES)
        kxj = kx[:, sl]
        kk = kxj * lax.rsqrt(_dot_sel(kxj * kxj, ones, 2) + 1e-12)
        r_o[0, j] = r[:, sl].astype(r_o.dtype)
        kk_o[0, j] = kk.astype(kk_o.dtype)
        v_o[0, j] = v[:, sl].astype(v_o.dtype)
        kr0_o[0, j] = kr[0][:, sl].astype(kr0_o.dtype)
        kr1_o[0, j] = kr[1][:, sl].astype(kr1_o.dtype)
        lw0_o[0, j] = lw[0][:, sl]
        lw1_o[0, j] = lw[1][:, sl]
        b0_o[0, j] = (kk * a[0][:, sl]).astype(b0_o.dtype)
        b1_o[0, j] = (kk * a[1][:, sl]).astype(b1_o.dtype)
        gbv_o[0, :, sl] = g[:, sl] * _dot_sel(bon[:, sl], ones, 2) * v[:, sl]


def _rwkv_features(p, prm, tb=256):
    b, t, cols = p.shape
    tb = min(tb, t)
    nb8 = t // SUBLANES
    r8 = tb // SUBLANES
    full = lambda a: pl.BlockSpec(a.shape, lambda i, j: (0,) * a.ndim)
    params = [prm["shift_mu"].reshape(1, cols), prm["decay_up"], prm["decay_0"], prm["a_up"], prm["a_0"],
              prm["g_up"], prm["k_xi"].reshape(1, -1), prm["k_alpha"].reshape(1, -1), prm["r_bonus"].reshape(1, -1)]
    packed = lambda dt: jax.ShapeDtypeStruct((b, N_PAIRS, t, LANES), dt)
    flat = jax.ShapeDtypeStruct((b, t, D_RWKV), F32)
    pspec = pl.BlockSpec((1, N_PAIRS, tb, LANES), lambda i, j: (i, 0, j, 0))
    fspec = pl.BlockSpec((1, tb, D_RWKV), lambda i, j: (i, j, 0))
    return pl.pallas_call(
        _feat_kernel,
        grid=(b, t // tb),
        in_specs=[pl.BlockSpec((1, tb, cols), lambda i, j: (i, j, 0)),
                  pl.BlockSpec((1, SUBLANES, cols), lambda i, j: (i, jnp.maximum(j * r8 - 1, 0), 0)),
                  pl.BlockSpec((1, SUBLANES, cols), lambda i, j: (i, jnp.minimum((j + 1) * r8, nb8 - 1), 0))]
                 + [full(a) for a in params],
        out_specs=[pspec] * 9 + [fspec] * 2,
        out_shape=[packed(BF16)] * 5 + [packed(F32)] * 2 + [packed(BF16)] * 2 + [flat] * 2,
        compiler_params=_cparams(("parallel", "parallel")),
        name="rwkv_features",
    )(p, p, p, *params)


def _scan_chunks(feats, rev):
    n = SCAN_CHUNK
    m0 = _half_mask((n, LANES))
    m1 = 1.0 - m0
    ti = lax.broadcasted_iota(jnp.int32, (n, n), 0)
    tj = lax.broadcasted_iota(jnp.int32, (n, n), 1)
    tri = ((ti <= tj) if rev else (ti >= tj)).astype(F32)
    si = lax.broadcasted_iota(jnp.int32, (2 * n, 2 * n), 0)
    sj = lax.broadcasted_iota(jnp.int32, (2 * n, 2 * n), 1)
    same = (si // n) == (sj // n)
    ri, rj = si % n, sj % n
    strict = same & ((ri < rj) if rev else (ri > rj))
    incl = same & ((ri <= rj) if rev else (ri >= rj))
    eye = (si == sj).astype(F32)

    def stack(x):
        return jnp.concatenate([x * m0, x * m1], axis=0).astype(BF16)

    pre = []
    for r, lw, kap, bb, kr, v in feats:
        cs = _dot_sel(tri, lw, 3, exact="lhs")
        tot = jnp.sum(lw, axis=0, keepdims=True)
        e_neg = jnp.exp(-cs)
        e_tot = jnp.exp(tot - cs)
        rh = r * jnp.exp(cs)
        pre.append(dict(kh=stack(kap * jnp.exp(cs - lw)), bh=stack(bb * e_neg), kq=stack(kr * e_neg), rh=rh,
                        rhs=stack(rh), kt=stack(kr * e_tot), bt=stack(bb * e_tot), v=stack(v), gl=jnp.exp(tot)))
    nt = (((1,), (1,)), ((), ()))
    amats = [lax.dot_general(jnp.concatenate([p["kh"], p["rhs"]], axis=0), jnp.concatenate([p["bh"], p["kq"]], axis=0),
                             nt, preferred_element_type=F32) for p in pre]
    a1 = [jnp.where(strict, a[:2 * n, :2 * n], 0.0) for a in amats]
    a2 = [jnp.where(strict, a[:2 * n, 2 * n:], 0.0).astype(BF16) for a in amats]
    a4 = [jnp.where(incl, a[2 * n:, :2 * n], 0.0).astype(BF16) for a in amats]
    a3 = [jnp.where(incl, a[2 * n:, 2 * n:], 0.0).astype(BF16) for a in amats]
    tinv = [eye - a for a in a1]
    pw = a1
    for _ in range(5):
        pw = [_bdot(p, p) for p in pw]
        tinv = [t + _bdot(t, p) for t, p in zip(tinv, pw)]
    w_s = [_bdot(a, p["v"]) for a, p in zip(a2, pre)]
    mm = [_bdot(t, jnp.concatenate([p["kh"], w.astype(BF16)], axis=1)).astype(BF16)
          for t, p, w in zip(tinv, pre, w_s)]
    gmat = [_bdot(p["bt"].astype(F32).T, m) for p, m in zip(pre, mm)]
    phi_t = [eye * p["gl"] - g[:, :LANES] for p, g in zip(pre, gmat)]
    psi_t = [_bdot(p["kt"].astype(F32).T, p["v"]) - g[:, LANES:] for p, g in zip(pre, gmat)]
    qy = [_bdot(a, m) for a, m in zip(a4, mm)]
    y0_s = [_bdot(a, p["v"]) - q[:, LANES:] for a, p, q in zip(a3, pre, qy)]
    return [(p["rh"] - (q[:n, :LANES] + q[n:, :LANES]), y0[:n] + y0[n:], ph, ps)
            for p, q, y0, ph, ps in zip(pre, qy, y0_s, phi_t, psi_t)]


def _scan_kernel(r_ref, lw_ref, kap_ref, b_ref, kr_ref, v_ref, s0_ref, y_ref, sT_ref, st_scr, *, rev):
    c = pl.program_id(1)

    @pl.when(c == 0)
    def _():
        st_scr[...] = s0_ref[0]

    n = SCAN_CHUNK
    nsub = r_ref.shape[2] // n
    order = range(nsub - 1, -1, -1) if rev else range(nsub)
    rows = lambda ref, hp, k: ref[0, hp, k * n:(k + 1) * n, :]
    items = [(k, hp) for k in order for hp in range(N_PAIRS)]
    terms = _scan_chunks([tuple(rows(ref, hp, k) for ref in (r_ref, lw_ref, kap_ref, b_ref, kr_ref, v_ref))
                          for k, hp in items], rev)
    states = [st_scr[hp] for hp in range(N_PAIRS)]
    for (k, hp), (qm, y0, phi_t, psi_t) in zip(items, terms):
        y_ref[0, hp, k * n:(k + 1) * n, :] = _bdot(qm, states[hp]) + y0
        states[hp] = _bdot(phi_t, states[hp]) + psi_t
    for hp in range(N_PAIRS):
        st_scr[hp] = states[hp]

    @pl.when(c == pl.num_programs(1) - 1)
    def _():
        sT_ref[0] = st_scr[...]


def _rwkv_scan(r, lw, kap, bb, kr, v, s0, rev, chunks_per_step=4):
    b, _, t, _ = r.shape
    n = SCAN_CHUNK * math.gcd(chunks_per_step, t // SCAN_CHUNK)
    nc = t // n
    cidx = (lambda c: nc - 1 - c) if rev else (lambda c: c)
    fspec = pl.BlockSpec((1, N_PAIRS, n, LANES), lambda i, c: (i, 0, cidx(c), 0))
    sspec = pl.BlockSpec((1, N_PAIRS, LANES, LANES), lambda i, c: (i, 0, 0, 0))
    return pl.pallas_call(
        functools.partial(_scan_kernel, rev=rev),
        grid=(b, nc),
        in_specs=[fspec] * 6 + [sspec],
        out_specs=[fspec, sspec],
        out_shape=[jax.ShapeDtypeStruct(r.shape, F32), jax.ShapeDtypeStruct(s0.shape, F32)],
        scratch_shapes=[pltpu.VMEM((N_PAIRS, LANES, LANES), F32)],
        compiler_params=_cparams(("parallel", "arbitrary")),
        name="rwkv_scan_rev" if rev else "rwkv_scan_fwd",
    )(r, lw, kap, bb, kr, v, s0)


def _evenpost_kernel(y0_ref, y1_ref, g_ref, gbv_ref, pc_ref, pcp_ref, pcn_ref, cw_ref, gng_ref, gnb_ref,
                     conv_o, rw_o):
    pc = pc_ref[0]
    prev_row, next_row = _edge_rows(pcp_ref, pcn_ref)
    dc = D_CONV

    def gated(z):
        return z[:, dc:2 * dc] * z[:, 2 * dc:3 * dc]

    u = gated(pc)
    u_prev, u_next = _shifted(u, gated(prev_row), gated(next_row))
    cw = cw_ref[...]
    conv_o[0] = (pc[:, 0:dc] * (cw[0:1] * u_prev + cw[1:2] * u + cw[2:3] * u_next)).astype(conv_o.dtype)
    ones = _seg_ones() * (1.0 / RWKV_HD)
    for j in range(N_PAIRS):
        sl = slice(j * LANES, (j + 1) * LANES)
        y = y0_ref[0, j] + y1_ref[0, j]
        mu = _dot_sel(y, ones, 2)
        yc = y - mu
        var = _dot_sel(yc * yc, ones, 2)
        yn = yc * lax.rsqrt(var + GN_EPS) * gng_ref[:, sl] + gnb_ref[:, sl]
        rw_o[0, :, sl] = (g_ref[0, :, sl] * yn + gbv_ref[0, :, sl]).astype(rw_o.dtype)


def _even_post(y0, y1, g, gbv, pc, conv_w, gn_g, gn_b, tb=512):
    b, _, t, _ = y0.shape
    tb = min(tb, t)
    nb8 = t // SUBLANES
    r8 = tb // SUBLANES
    c3 = pc.shape[-1]
    pspec = pl.BlockSpec((1, N_PAIRS, tb, LANES), lambda i, j: (i, 0, j, 0))
    fspec = pl.BlockSpec((1, tb, D_RWKV), lambda i, j: (i, j, 0))
    full = lambda a: pl.BlockSpec(a.shape, lambda i, j: (0,) * a.ndim)
    gn_g = gn_g.reshape(1, -1)
    gn_b = gn_b.reshape(1, -1)
    return pl.pallas_call(
        _evenpost_kernel,
        grid=(b, t // tb),
        in_specs=[pspec, pspec, fspec, fspec,
                  pl.BlockSpec((1, tb, c3), lambda i, j: (i, j, 0)),
                  pl.BlockSpec((1, SUBLANES, c3), lambda i, j: (i, jnp.maximum(j * r8 - 1, 0), 0)),
                  pl.BlockSpec((1, SUBLANES, c3), lambda i, j: (i, jnp.minimum((j + 1) * r8, nb8 - 1), 0)),
                  full(conv_w), full(gn_g), full(gn_b)],
        out_specs=[pl.BlockSpec((1, tb, D_CONV), lambda i, j: (i, j, 0)), fspec],
        out_shape=[jax.ShapeDtypeStruct((b, t, D_CONV), BF16), jax.ShapeDtypeStruct((b, t, D_RWKV), BF16)],
        compiler_params=_cparams(("parallel", "parallel")),
        name="even_post",
    )(y0, y1, g, gbv, pc, pc, pc, conv_w, gn_g, gn_b)


def _outproj_kernel(a_ref, b_ref, wa_ref, wb_ref, x_ref, gm_ref, lg_ref, lb_ref, sc_ref, sh_ref, wr_ref,
                    o_ref, h_o, aff_o):
    yx = _bdot(a_ref[0], wa_ref[...]) + _bdot(b_ref[0], wb_ref[...])
    z = ALPHA * x_ref[0] + gm_ref[0] * yx
    xn = _layer_norm_rows(z, lg_ref[...], lb_ref[...], LN_EPS)
    o_ref[0] = xn
    h = xn * (1.0 + sc_ref[0]) + sh_ref[0]
    h_o[0] = h.astype(h_o.dtype)
    logits = lax.dot_general(wr_ref[...], h, (((1,), (1,)), ((), ())), preferred_element_type=F32, precision=HI)
    m = jnp.max(logits, axis=0, keepdims=True)
    e = jnp.exp(logits - m)
    aff_o[0] = e / jnp.sum(e, axis=0, keepdims=True)


def _outproj(ma, mb, w_out_bf16, x, gm, ln_g, ln_b, sc_f, sh_f, w_router_t, tm=512):
    b, t, d = x.shape
    tm = min(tm, t)
    e = w_router_t.shape[0]
    da, db = ma.shape[-1], mb.shape[-1]
    wa, wb = w_out_bf16[:da], w_out_bf16[da:]
    row = lambda w: pl.BlockSpec((1, tm, w), lambda i, j: (i, j, 0))
    full = lambda a: pl.BlockSpec(a.shape, lambda i, j: (0,) * a.ndim)
    vec = pl.BlockSpec((1, 1, d), lambda i, j: (i, 0, 0))
    ln_g = ln_g.reshape(1, d)
    ln_b = ln_b.reshape(1, d)
    return pl.pallas_call(
        _outproj_kernel,
        grid=(b, t // tm),
        in_specs=[row(da), row(db), full(wa), full(wb), row(d), vec, full(ln_g), full(ln_b), vec, vec,
                  full(w_router_t)],
        out_specs=[row(d), row(d), pl.BlockSpec((1, e, tm), lambda i, j: (i, 0, j))],
        out_shape=[jax.ShapeDtypeStruct((b, t, d), F32), jax.ShapeDtypeStruct((b, t, d), BF16),
                   jax.ShapeDtypeStruct((b, e, t), F32)],
        compiler_params=_cparams(("parallel", "parallel")),
        name="outproj_norm_router",
    )(ma, mb, wa, wb, x, gm, ln_g, ln_b, sc_f, sh_f, w_router_t)


def _inproj_attn_kernel(x_ref, sc_ref, sh_ref, w_ref, *refs, rope, qscale):
    if rope:
        cos_ref, sin_ref, q_o, k_o, v_o, g_o = refs
        cos, sin = cos_ref[...], sin_ref[...]
        lane = lax.broadcasted_iota(jnp.int32, cos.shape, 1)
        half = AXIS_DIM // 2
        first = (lane % (2 * half)) < half

        def rot(z):
            partner = jnp.where(first, -pltpu.roll(z, LANES - half, axis=1), pltpu.roll(z, half, axis=1))
            return z * cos + partner * sin
    else:
        q_o, k_o, v_o, g_o = refs
        rot = lambda z: z
    h = x_ref[0] * (1.0 + sc_ref[0]) + sh_ref[0]
    y = _bdot(h, w_ref[...])
    for j in range(DIFF_HEADS):
        sl = slice(j * LANES, (j + 1) * LANES)
        q_o[0, :, sl] = (rot(y[:, sl]) * qscale).astype(q_o.dtype)
        k_o[0, :, sl] = rot(y[:, D_DIFF + j * LANES:D_DIFF + (j + 1) * LANES]).astype(k_o.dtype)
    v_o[0] = y[:, 2 * D_DIFF:3 * D_DIFF].astype(v_o.dtype)
    g_o[0] = y[:, 3 * D_DIFF:]


def _inproj_attn(x, sc, sh, w_bf16, tables, qscale, tm=512):
    b, t, d = x.shape
    tm = min(tm, t)
    dout = w_bf16.shape[1]
    rope = tables is not None
    row = lambda w: pl.BlockSpec((1, tm, w), lambda i, j: (i, j, 0))
    vec = pl.BlockSpec((1, 1, d), lambda i, j: (i, 0, 0))
    tab = [pl.BlockSpec((tm, LANES), lambda i, j: (j, 0))] * 2 if rope else []
    return pl.pallas_call(
        functools.partial(_inproj_attn_kernel, rope=rope, qscale=qscale),
        grid=(b, t // tm),
        in_specs=[row(d), vec, vec, pl.BlockSpec((d, dout), lambda i, j: (0, 0))] + tab,
        out_specs=[row(D_DIFF)] * 3 + [row(dout - 3 * D_DIFF)],
        out_shape=[jax.ShapeDtypeStruct((b, t, D_DIFF), BF16)] * 3
                  + [jax.ShapeDtypeStruct((b, t, dout - 3 * D_DIFF), F32)],
        compiler_params=_cparams(("parallel", "parallel")),
        name="inproj_attn",
    )(x, sc, sh, w_bf16, *(tables if rope else ()))


def _rope_tables(t):
    rows = t // GRID_W
    row = jnp.repeat(jnp.arange(rows), GRID_W).astype(F32)
    col = jnp.tile(jnp.arange(GRID_W), rows).astype(F32)
    inv = ROPE_BASE ** (-jnp.arange(0, AXIS_DIM, 2, dtype=F32) / AXIS_DIM)
    ang_r = row[:, None] * inv
    ang_c = col[:, None] * inv
    ang = jnp.concatenate([ang_r, ang_r, ang_c, ang_c], axis=-1)
    ang = jnp.concatenate([ang, ang], axis=-1)
    return jnp.cos(ang), jnp.sin(ang)


def _attn_kernel(lam_ref, q_ref, k_ref, v_ref, g_ref, o_ref, *, tk, out_scale):
    q = q_ref[0]
    tq = q.shape[0]
    hm = _half_mask(q.shape, BF16)
    q1 = q * hm
    qs = (q1, q - q1)
    nk = k_ref.shape[1] // tk
    nsl = tk // LANES

    def scores(qh, j):
        kb = k_ref[0, pl.ds(pl.multiple_of(j * tk, tk), tk), :]
        return lax.dot_general(qh, kb, (((1,), (1,)), ((), ())), preferred_element_type=F32)

    def col(s, c):
        return s[:, c * LANES:(c + 1) * LANES]

    def body(j, carry):
        vb = v_ref[0, pl.ds(pl.multiple_of(j * tk, tk), tk), :]
        ss = [scores(qh, j) for qh in qs]
        out = []
        for s, (m, ls, acc) in zip(ss, (carry[:3], carry[3:])):
            mx = col(s, 0)
            for c in range(1, nsl):
                mx = jnp.maximum(mx, col(s, c))
            m_new = jnp.maximum(m, jnp.broadcast_to(jnp.max(mx, axis=-1, keepdims=True), (tq, LANES)))
            corr = jnp.exp2(m - m_new)
            ps = [jnp.exp2(col(s, c) - m_new) for c in range(nsl)]
            ls = corr * ls
            for pc in ps:
                ls = ls + pc
            p = jnp.concatenate([pc.astype(BF16) for pc in ps], axis=1)
            acc = corr * acc + jnp.dot(p, vb, preferred_element_type=F32)
            out += [m_new, ls, acc]
        return tuple(out)

    neg = jnp.full((tq, LANES), -jnp.inf, F32)
    zero = jnp.zeros((tq, LANES), F32)
    _, ls1, acc1, _, ls2, acc2 = lax.fori_loop(0, nk, body, (neg, zero, zero, neg, zero, zero), unroll=True)
    l1 = jnp.sum(ls1, axis=-1, keepdims=True)
    l2 = jnp.sum(ls2, axis=-1, keepdims=True)
    o = acc1 / l1 - lam_ref[0] * (acc2 / l2)
    o = o * lax.rsqrt(jnp.mean(o * o, axis=-1, keepdims=True) + RMS_EPS) * g_ref[...] * out_scale
    o_ref[0] = o.astype(o_ref.dtype)


def _diff_attention(q, k, v, lam, subln_g, lam_init, tq=512, tk=1408):
    b, t, w = q.shape
    tkk = k.shape[1]
    tq = min(tq, t)
    tk = math.gcd(tk, tkk)
    assert tk % LANES == 0 and t % tq == 0
    subln_g = subln_g.reshape(1, DIFF_VD)
    return pl.pallas_call(
        functools.partial(_attn_kernel, tk=tk, out_scale=1.0 - lam_init),
        grid=(b, DIFF_HEADS, t // tq),
        in_specs=[pl.BlockSpec(memory_space=pltpu.SMEM),
                  pl.BlockSpec((1, tq, LANES), lambda i, h, j: (i, j, h)),
                  pl.BlockSpec((1, tkk, LANES), lambda i, h, j: (i, 0, h)),
                  pl.BlockSpec((1, tkk, LANES), lambda i, h, j: (i, 0, h)),
                  pl.BlockSpec((1, DIFF_VD), lambda i, h, j: (0, 0))],
        out_specs=pl.BlockSpec((1, tq, LANES), lambda i, h, j: (i, j, h)),
        out_shape=jax.ShapeDtypeStruct((b, t, w), BF16),
        compiler_params=_cparams(("parallel", "parallel", "parallel")),
        name="diff_attention",
    )(lam, q, k, v, subln_g)


def _gmlp_kernel(p_ref, lg_ref, lb_ref, ws_ref, bs_ref, o_ref):
    p = p_ref[0]
    tb = p.shape[0]
    ge = 0.5 * p * (1.0 + lax.erf(p * (2.0 ** -0.5)))
    u = ge[:, :D_GMLP]
    v = _layer_norm_rows(ge[:, D_GMLP:], lg_ref[...], lb_ref[...], LN_EPS)
    hm = _half_mask((GMLP_CHUNK, LANES))
    for c in range(tb // GMLP_CHUNK):
        rs = slice(c * GMLP_CHUNK, (c + 1) * GMLP_CHUNK)
        for j in range(D_GMLP // LANES):
            sl = slice(j * LANES, (j + 1) * LANES)
            vc = v[rs, sl]
            va = vc * hm
            mixed = _bdot(ws_ref[2 * j], va) + _bdot(ws_ref[2 * j + 1], vc - va) + bs_ref[:, sl]
            o_ref[0, rs, sl] = (u[rs, sl] * mixed).astype(o_ref.dtype)


def _chunk_gmlp(p, ln_g, ln_b, ws, bs, tb=256):
    b, t, w = p.shape
    tb = min(tb, t)
    ln_g = ln_g.reshape(1, -1)
    ln_b = ln_b.reshape(1, -1)
    bs_t = jnp.repeat(jnp.transpose(bs), D_GMLP // ws.shape[0], axis=1)
    full = lambda a: pl.BlockSpec(a.shape, lambda i, j: (0,) * a.ndim)
    return pl.pallas_call(
        _gmlp_kernel,
        grid=(b, t // tb),
        in_specs=[pl.BlockSpec((1, tb, w), lambda i, j: (i, j, 0)), full(ln_g), full(ln_b), full(ws), full(bs_t)],
        out_specs=pl.BlockSpec((1, tb, D_GMLP), lambda i, j: (i, j, 0)),
        out_shape=jax.ShapeDtypeStruct((b, t, D_GMLP), BF16),
        compiler_params=_cparams(("parallel", "parallel")),
        name="chunk_gmlp",
    )(p, ln_g, ln_b, ws, bs_t)


ROUTE_BLOCK = 128
STARTS_PAD = LANES


def _lane_cumsum(mask):
    e, t = mask.shape
    i = lax.broadcasted_iota(jnp.int32, (LANES, LANES), 0)
    j = lax.broadcasted_iota(jnp.int32, (LANES, LANES), 1)
    upper = (i <= j).astype(BF16)
    x = jnp.where(mask, 1.0, 0.0).astype(BF16)
    off = jnp.zeros((e, 1), F32)
    out = []
    for c in range(t // LANES):
        blk = jnp.dot(x[:, c * LANES:(c + 1) * LANES], upper, preferred_element_type=F32) + off
        out.append(blk)
        off = blk[:, LANES - 1:LANES]
    return jnp.concatenate(out, axis=1)


def _select_kernel(aff_ref, pos_ref, starts_ref, *, cap):
    a = aff_ref[0]
    e, t = a.shape
    bits = pltpu.bitcast(a, jnp.int32)

    def search(i, thr):
        cand = thr | jnp.left_shift(jnp.int32(1), 30 - i)
        cnt = jnp.sum(jnp.where(bits >= cand, 1.0, 0.0), axis=1, keepdims=True)
        return jnp.where(cnt >= cap, cand, thr)

    thr = lax.fori_loop(0, 31, search, jnp.zeros((e, 1), jnp.int32))
    gt = bits > thr
    eq = bits == thr
    need = cap - jnp.sum(jnp.where(gt, 1.0, 0.0), axis=1, keepdims=True)
    sel = gt | (eq & (_lane_cumsum(eq) <= need))
    csel = _lane_cumsum(sel)
    pos_ref[0] = jnp.where(sel, csel - 1.0, -1.0).astype(jnp.int32)
    ti = lax.broadcasted_iota(jnp.int32, (t, STARTS_PAD), 0)
    ki = lax.broadcasted_iota(jnp.int32, (t, STARTS_PAD), 1)
    pick = jnp.where(ti == ki * ROUTE_BLOCK - 1, 1.0, 0.0)
    starts_ref[0] = (_dot_sel(csel, pick, 2) + 0.5).astype(jnp.int32)


def _select(aff, cap):
    b, e, t = aff.shape
    assert t % ROUTE_BLOCK == 0 and t // ROUTE_BLOCK < STARTS_PAD
    return pl.pallas_call(
        functools.partial(_select_kernel, cap=cap),
        grid=(b,),
        in_specs=[pl.BlockSpec((1, e, t), lambda i: (i, 0, 0))],
        out_specs=[pl.BlockSpec((1, e, t), lambda i: (i, 0, 0)), pl.BlockSpec((1, e, STARTS_PAD), lambda i: (i, 0, 0))],
        out_shape=[jax.ShapeDtypeStruct((b, e, t), jnp.int32), jax.ShapeDtypeStruct((b, e, STARTS_PAD), jnp.int32)],
        compiler_params=_cparams(("parallel",)),
        name="expert_select",
    )(aff)


def _window_start(start, align, cap, width):
    s = lax.shift_left(lax.shift_right_logical(start, align.bit_length() - 1), align.bit_length() - 1)
    return pl.multiple_of(jnp.minimum(s, cap - width), align)


def _gather_ffn_kernel(starts_ref, pos_ref, aff_ref, h_ref, w1_ref, w3_ref, w2_ref, y_ref, xs_scr, gate_scr, *,
                       width, rows, cap):
    ei, bi, k = pl.program_id(0), pl.program_id(1), pl.program_id(2)
    ne, nb = pl.num_programs(0), pl.num_programs(1)
    nsub = h_ref.shape[1] // ROUTE_BLOCK
    merged = xs_scr.shape[0] > cap
    base = pl.multiple_of(bi * cap, SUBLANES) if merged else 0

    @pl.when(k == 0)
    def _():
        xs_scr[pl.ds(base, cap), :] = jnp.zeros((cap, xs_scr.shape[1]), F32)
        gate_scr[pl.ds(base, cap), :] = jnp.zeros((cap, LANES), F32)

    def copy_rows(sb, start, w):
        s8 = _window_start(start, SUBLANES, cap, w)
        cols = slice(sb * ROUTE_BLOCK, (sb + 1) * ROUTE_BLOCK)
        jrow = lax.broadcasted_iota(jnp.int32, (w, ROUTE_BLOCK), 0)
        hit = pos_ref[0, :, cols] - s8 == jrow
        xs_scr[pl.ds(base + s8, w), :] += jnp.dot(jnp.where(hit, 1.0, 0.0).astype(BF16), h_ref[0, cols, :],
                                                  preferred_element_type=F32)
        gate = jnp.sum(jnp.where(hit, aff_ref[0, :, cols], 0.0), axis=1, keepdims=True)
        gate_scr[pl.ds(base + s8, w), :] += jnp.broadcast_to(gate, (w, LANES))

    for sb in range(nsub):
        copy_rows(sb, starts_ref[(bi * ne + ei) * STARTS_PAD + k * nsub + sb], width)

    last = k == pl.num_programs(2) - 1

    @pl.when(last & (bi == nb - 1) if merged else last)
    def _():
        for r in range(xs_scr.shape[0] // rows):
            x = xs_scr[r * rows:(r + 1) * rows, :].astype(BF16)
            h1 = jnp.dot(x, w1_ref[0], preferred_element_type=F32)
            h3 = jnp.dot(x, w3_ref[0], preferred_element_type=F32)
            hid = (h1 * jax.nn.sigmoid(h1)) * h3
            y = jnp.dot(hid.astype(BF16), w2_ref[0], preferred_element_type=F32)
            y = (y * gate_scr[r * rows:(r + 1) * rows, 0:1]).astype(y_ref.dtype)
            for s in range(max(rows // cap, 1)):
                n = min(rows, cap)
                row0 = r * rows + s * cap
                y_ref[row0 // cap, 0, row0 % cap:row0 % cap + n, :] = y[s * cap:s * cap + n]


def _gather_ffn(h, pos, aff, starts, w1, w3, w2, cap, tch=4096, rows=512):
    b, t, d = h.shape
    e, _, f = w1.shape
    tch = min(tch, t)
    merged = b * cap <= rows
    slots = b if merged else 1
    rows = min(rows, slots * cap)
    assert (slots * cap) % rows == 0 and (rows % cap == 0 or cap % rows == 0)
    width = min(ROUTE_BLOCK + SUBLANES, cap)
    grid_spec = pltpu.PrefetchScalarGridSpec(
        num_scalar_prefetch=1,
        grid=(e, b, t // tch),
        in_specs=[pl.BlockSpec((1, 1, tch), lambda ei, bi, k, s: (bi * e + ei, 0, k)),
                  pl.BlockSpec((1, 1, tch), lambda ei, bi, k, s: (bi * e + ei, 0, k)),
                  pl.BlockSpec((1, tch, d), lambda ei, bi, k, s: (bi, k, 0)),
                  pl.BlockSpec((1, d, f), lambda ei, bi, k, s: (ei, 0, 0)),
                  pl.BlockSpec((1, d, f), lambda ei, bi, k, s: (ei, 0, 0)),
                  pl.BlockSpec((1, f, d), lambda ei, bi, k, s: (ei, 0, 0))],
        out_specs=pl.BlockSpec((slots, 1, cap, d), (lambda ei, bi, k, s: (0, ei, 0, 0)) if merged
                               else (lambda ei, bi, k, s: (bi, ei, 0, 0))),
        scratch_shapes=[pltpu.VMEM((slots * cap, d), F32), pltpu.VMEM((slots * cap, LANES), F32)],
    )
    return pl.pallas_call(
        functools.partial(_gather_ffn_kernel, width=width, rows=rows, cap=cap),
        grid_spec=grid_spec,
        out_shape=jax.ShapeDtypeStruct((b, e, cap, d), BF16),
        compiler_params=_cparams(("parallel", "arbitrary", "arbitrary") if merged
                                 else ("parallel", "parallel", "arbitrary")),
        name="expert_gather_ffn",
    )(starts.reshape(-1), pos.reshape(b * e, 1, t), aff.reshape(b * e, 1, t), h, w1, w3, w2)


def _combine_kernel(starts_ref, post_ref, y_ref, x_ref, gf_ref, lg_ref, lb_ref, o_ref, acc_scr, *, width, group):
    bi, i, eg = pl.program_id(0), pl.program_id(1), pl.program_id(2)
    ne = pl.num_programs(2) * group
    cap = y_ref.shape[2]
    nsub = x_ref.shape[1] // ROUTE_BLOCK

    @pl.when(eg == 0)
    def _():
        acc_scr[...] = jnp.zeros_like(acc_scr)

    lane = lax.broadcasted_iota(jnp.int32, (ROUTE_BLOCK, ne), 1)
    jcol = lax.broadcasted_iota(jnp.int32, (ROUTE_BLOCK, width), 1)
    for sb in range(nsub):
        rs = slice(sb * ROUTE_BLOCK, (sb + 1) * ROUTE_BLOCK)
        pblk = post_ref[0, rs, :].astype(F32)
        total = None
        for j in range(group):
            ei = eg * group + j
            start = starts_ref[(bi * ne + ei) * STARTS_PAD + i * nsub + sb]
            s16 = _window_start(start, 2 * SUBLANES, cap, width)
            pcol = jnp.sum(jnp.where(lane == ei, pblk, 0.0), axis=1, keepdims=True)
            onehot = jnp.where(pcol.astype(jnp.int32) - s16 == jcol, 1.0, 0.0).astype(BF16)
            part = jnp.dot(onehot, y_ref[0, j, pl.ds(s16, width), :], preferred_element_type=F32)
            total = part if total is None else total + part
        acc_scr[rs, :] += total

    @pl.when(eg == pl.num_programs(2) - 1)
    def _():
        z = ALPHA * x_ref[0] + gf_ref[0] * acc_scr[...]
        o_ref[0] = _layer_norm_rows(z, lg_ref[...], lb_ref[...], LN_EPS)


def _combine(x, y, pos_t, starts, gf, ln_g, ln_b, tbo=1024, group=4):
    b, t, d = x.shape
    _, e, cap, _ = y.shape
    tbo = min(tbo, t)
    width = min(ROUTE_BLOCK + 2 * SUBLANES, cap)
    grid_spec = pltpu.PrefetchScalarGridSpec(
        num_scalar_prefetch=1,
        grid=(b, t // tbo, e // group),
        in_specs=[pl.BlockSpec((1, tbo, e), lambda bi, i, eg, s: (bi, i, 0)),
                  pl.BlockSpec((1, group, cap, d), lambda bi, i, eg, s: (bi, eg, 0, 0)),
                  pl.BlockSpec((1, tbo, d), lambda bi, i, eg, s: (bi, i, 0)),
                  pl.BlockSpec((1, 1, d), lambda bi, i, eg, s: (bi, 0, 0)),
                  pl.BlockSpec((1, d), lambda bi, i, eg, s: (0, 0)),
                  pl.BlockSpec((1, d), lambda bi, i, eg, s: (0, 0))],
        out_specs=pl.BlockSpec((1, tbo, d), lambda bi, i, eg, s: (bi, i, 0)),
        scratch_shapes=[pltpu.VMEM((tbo, d), F32)],
    )
    return pl.pallas_call(
        functools.partial(_combine_kernel, width=width, group=group),
        grid_spec=grid_spec,
        out_shape=jax.ShapeDtypeStruct((b, t, d), F32),
        compiler_params=_cparams(("parallel", "parallel", "arbitrary")),
        name="expert_combine_norm",
    )(starts.reshape(-1), pos_t, y, x, gf, ln_g.reshape(1, d), ln_b.reshape(1, d))


def _cast_kernel(x_ref, o_ref):
    o_ref[...] = x_ref[...].astype(o_ref.dtype)


def _to_bf16(w, l):
    _, e, r, c = w.shape
    return pl.pallas_call(
        _cast_kernel,
        grid=(e,),
        in_specs=[pl.BlockSpec((None, 1, r, c), lambda i: (l, i, 0, 0))],
        out_specs=pl.BlockSpec((1, r, c), lambda i: (i, 0, 0)),
        out_shape=jax.ShapeDtypeStruct((e, r, c), BF16),
        compiler_params=_cparams(("parallel",)),
        name="cast_bf16",
    )(w)


def _moe(x, h, aff, gf, w1, w3, w2, ln_g, ln_b):
    t = x.shape[1]
    cap = EC_FACTOR * t // N_EXPERTS
    pos, starts = _select(aff, cap)
    y = _gather_ffn(h, pos, aff, starts, w1, w3, w2, cap)
    return _combine(x, y, jnp.transpose(pos, (0, 2, 1)), starts, gf, ln_g, ln_b)


def _even_layer(hx_in, hc_in, prm, ctx_out):
    w_in = prm["w_in"]
    splits = (3 * D_CONV, RWKV_COLS)
    pcx, prx = _inproj(*hx_in, w_in, splits)
    pcc, prc = _inproj(*hc_in, w_in, splits)
    fx = _rwkv_features(prx, prm)
    fc = _rwkv_features(prc, prm)
    b = prx.shape[0]
    zero = jnp.zeros((b, N_PAIRS, LANES, LANES), F32)
    ys_x, ys_c = [], []
    for d, rev in ((0, False), (1, True)):
        def args(f):
            r, kk, v, kr0, kr1, lw0, lw1, b0, b1 = f[:9]
            return (r, (lw0, lw1)[d], kk, (b0, b1)[d], (kr0, kr1)[d], v)
        yc, s_ctx = _rwkv_scan(*args(fc), zero, rev)
        yx, _ = _rwkv_scan(*args(fx), s_ctx, rev)
        ys_x.append(yx)
        ys_c.append(yc)
    out_x = _even_post(ys_x[0], ys_x[1], fx[9], fx[10], pcx, prm["conv_w"], prm["gn_g"], prm["gn_b"])
    out_c = None
    if ctx_out:
        out_c = _even_post(ys_c[0], ys_c[1], fc[9], fc[10], pcc, prm["conv_w"], prm["gn_g"], prm["gn_b"])
    return out_c, out_x


def _odd_layer(hx_in, hc_in, prm, lam_init, ctx_out):
    w_in = prm["w_in"]
    qscale = DIFF_HD ** -0.5 * math.log2(math.e)
    qx, kx, vx, gx = _inproj_attn(*hx_in, w_in, _rope_tables(hx_in[0].shape[1]), qscale)
    qc, kc, vc, gc = _inproj_attn(*hc_in, w_in, None, qscale)
    k_all = jnp.concatenate([kc, kx], axis=1)
    v_all = jnp.concatenate([vc, vx], axis=1)
    lam = (jnp.exp(jnp.sum(prm["lam_q1"] * prm["lam_k1"])) - jnp.exp(jnp.sum(prm["lam_q2"] * prm["lam_k2"]))
           + lam_init).reshape(1).astype(F32)
    att_x = _diff_attention(qx, k_all, v_all, lam, prm["subln_g"], lam_init)
    gm_x = _chunk_gmlp(gx, prm["gmlp_ln_g"], prm["gmlp_ln_b"], prm["gmlp_ws"], prm["gmlp_bs"])
    out_c = None
    if ctx_out:
        att_c = _diff_attention(qc, kc, vc, lam, prm["subln_g"], lam_init)
        gm_c = _chunk_gmlp(gc, prm["gmlp_ln_g"], prm["gmlp_ln_b"], prm["gmlp_ws"], prm["gmlp_bs"])
        out_c = (att_c, gm_c)
    return out_c, (att_x, gm_x)


def kernel(x, c, ctx, c_ctx, w_mod, b_mod, ln_g, ln_b, even_w_in, even_w_out, conv_w, shift_mu, decay_up, decay_0, a_up, a_0, g_up, k_xi, k_alpha, r_bonus, gn_g, gn_b, odd_w_in, odd_w_out, lam_q1, lam_k1, lam_q2, lam_k2, subln_g, gmlp_ln_g, gmlp_ln_b, gmlp_ws, gmlp_bs, w_router, w_e1, w_e3, w_e2):
    bsz, _, d = x.shape
    assert d == D_MODEL and bsz <= SUBLANES - 1
    c8 = jnp.zeros((SUBLANES, d), F32).at[:bsz].set(c).at[bsz].set(c_ctx)
    mod = _modulation(c8, w_mod, b_mod)
    for l in range(DEPTH):
        ctx_out = l < DEPTH - 1
        i = l // 2
        mx = mod[l, :bsz].reshape(bsz, 1, 6, d)
        mc = jnp.broadcast_to(mod[l, bsz].reshape(1, 1, 6, d), (bsz, 1, 6, d))
        part = lambda m, n: m[:, :, n]
        hx_in = (x, part(mx, 1), part(mx, 0))
        hc_in = (ctx, part(mc, 1), part(mc, 0))
        if l % 2 == 0:
            prm = dict(w_in=even_w_in[i].astype(BF16), conv_w=conv_w[i], shift_mu=shift_mu[i],
                       decay_up=decay_up[i], decay_0=decay_0[i], a_up=a_up[i], a_0=a_0[i], g_up=g_up[i],
                       k_xi=k_xi[i], k_alpha=k_alpha[i], r_bonus=r_bonus[i], gn_g=gn_g[i], gn_b=gn_b[i])
            out_c, out_x = _even_layer(hx_in, hc_in, prm, ctx_out)
            w_out = even_w_out[i].astype(BF16)
        else:
            lam_init = 0.8 - 0.6 * math.exp(-0.3 * l)
            prm = dict(w_in=odd_w_in[i].astype(BF16), lam_q1=lam_q1[i], lam_k1=lam_k1[i], lam_q2=lam_q2[i],
                       lam_k2=lam_k2[i], subln_g=subln_g[i], gmlp_ln_g=gmlp_ln_g[i], gmlp_ln_b=gmlp_ln_b[i],
                       gmlp_ws=gmlp_ws[i], gmlp_bs=gmlp_bs[i])
            out_c, out_x = _odd_layer(hx_in, hc_in, prm, lam_init, ctx_out)
            w_out = odd_w_out[i].astype(BF16)
        wr_t = jnp.transpose(w_router[l])
        w1, w3, w2 = _to_bf16(w_e1, l), _to_bf16(w_e3, l), _to_bf16(w_e2, l)
        x, h, aff = _outproj(out_x[0], out_x[1], w_out, x, part(mx, 2), ln_g[l, 0], ln_b[l, 0],
                             part(mx, 4), part(mx, 3), wr_t)
        x = _moe(x, h, aff, part(mx, 5), w1, w3, w2, ln_g[l, 1], ln_b[l, 1])
        if ctx_out:
            ctx, h, aff = _outproj(out_c[0], out_c[1], w_out, ctx, part(mc, 2), ln_g[l, 0], ln_b[l, 0],
                                   part(mc, 4), part(mc, 3), wr_t)
            ctx = _moe(ctx, h, aff, part(mc, 5), w1, w3, w2, ln_g[l, 1], ln_b[l, 1])
    return x
```

```python
import functools
import math

import jax
import jax.numpy as jnp
from jax import lax
from jax.experimental import pallas as pl
from jax.experimental.pallas import tpu as pltpu

F32 = jnp.float32
BF16 = jnp.bfloat16
HI = lax.Precision.HIGHEST

D_MODEL = 1024
DEPTH = 4
GRID_W = 64
D_CONV = 256
RWKV_HEADS = 12
RWKV_HD = 64
D_RWKV = RWKV_HEADS * RWKV_HD
LORA = 64
LORA_G = 128
RWKV_COLS = 3 * D_RWKV + 4 * LORA + LORA_G
DECAY_SCALE = math.exp(-0.5)
GN_EPS = 64e-5
N_PAIRS = RWKV_HEADS // 2
SCAN_CHUNK = 64
DIFF_HEADS = 6
DIFF_HD = 64
DIFF_VD = 2 * DIFF_HD
D_DIFF = DIFF_HEADS * DIFF_VD
AXIS_DIM = DIFF_HD // 2
ROPE_BASE = 10000.0
D_GMLP = 256
GMLP_CHUNK = 128
N_EXPERTS = 16
EC_FACTOR = 2
ALPHA = (2.0 * DEPTH) ** 0.25
LN_EPS = 1e-5
RMS_EPS = 1e-5

LANES = 128
SUBLANES = 8
VMEM_LIMIT = 56 * 1024 * 1024
HEAD_LANES = LANES // 2
assert RWKV_HD == DIFF_HD == D_GMLP // 4 == HEAD_LANES and SCAN_CHUNK == HEAD_LANES


def _cparams(sem):
    return pltpu.CompilerParams(dimension_semantics=sem, vmem_limit_bytes=VMEM_LIMIT)


def _bdot(a, b):
    return jnp.dot(a.astype(BF16), b.astype(BF16), preferred_element_type=F32)


def _hdot(a, b):
    return jnp.dot(a, b, preferred_element_type=F32, precision=HI)


def _split_bf16(x, terms):
    parts = []
    for _ in range(terms):
        p = x.astype(BF16)
        parts.append(p)
        x = x - p.astype(F32)
    return parts


def _dot3(a, b):
    a_hi, a_lo = _split_bf16(a, 2)
    b_hi, b_lo = _split_bf16(b, 2)
    d = lambda x, y: jnp.dot(x, y, preferred_element_type=F32)
    return d(a_hi, b_hi) + (d(a_hi, b_lo) + d(a_lo, b_hi))


def _dot_sel(a, b, terms, exact="rhs"):
    d = lambda x, y: jnp.dot(x, y, preferred_element_type=F32)
    if exact == "rhs":
        bb = b.astype(BF16)
        outs = [d(p, bb) for p in _split_bf16(a, terms)]
    else:
        ab = a.astype(BF16)
        outs = [d(ab, p) for p in _split_bf16(b, terms)]
    out = outs[-1]
    for o in reversed(outs[:-1]):
        out = out + o
    return out


def _half_mask(shape, dtype=F32):
    lane = lax.broadcasted_iota(jnp.int32, shape, len(shape) - 1)
    return (lane < HEAD_LANES).astype(dtype)


def _seg_ones():
    i = lax.broadcasted_iota(jnp.int32, (LANES, LANES), 0)
    j = lax.broadcasted_iota(jnp.int32, (LANES, LANES), 1)
    return ((i // HEAD_LANES) == (j // HEAD_LANES)).astype(F32)


def _layer_norm_rows(z, g, b, eps):
    mu = jnp.mean(z, axis=-1, keepdims=True)
    zc = z - mu
    var = jnp.mean(zc * zc, axis=-1, keepdims=True)
    return zc * lax.rsqrt(var + eps) * g + b


def _mod_kernel(c_ref, w_ref, b_ref, o_ref):
    a = c_ref[...]
    a = a * jax.nn.sigmoid(a)
    o_ref[0] = _hdot(a, w_ref[0]) + b_ref[0]


def _modulation(c8, w_mod, b_mod):
    tn = 1536
    nl, d, n6 = w_mod.shape
    return pl.pallas_call(
        _mod_kernel,
        grid=(nl, n6 // tn),
        in_specs=[pl.BlockSpec((8, d), lambda l, j: (0, 0)),
                  pl.BlockSpec((1, d, tn), lambda l, j: (l, 0, j)),
                  pl.BlockSpec((1, 1, tn), lambda l, j: (l, 0, j))],
        out_specs=pl.BlockSpec((1, 8, tn), lambda l, j: (l, 0, j)),
        out_shape=jax.ShapeDtypeStruct((nl, 8, n6), F32),
        compiler_params=_cparams(("parallel", "parallel")),
        name="modulation",
    )(c8, w_mod, b_mod.reshape(nl, 1, n6))


def _inproj_kernel(x_ref, sc_ref, sh_ref, w_ref, *o_refs, splits):
    h = x_ref[0] * (1.0 + sc_ref[0]) + sh_ref[0]
    y = _bdot(h, w_ref[...])
    off = 0
    for o, s in zip(o_refs, splits):
        o[0] = y[:, off:off + s].astype(o.dtype)
        off += s


def _inproj(x, sc, sh, w_bf16, splits, tm=512):
    b, t, d = x.shape
    tm = min(tm, t)
    dout = w_bf16.shape[1]
    return pl.pallas_call(
        functools.partial(_inproj_kernel, splits=splits),
        grid=(b, t // tm),
        in_specs=[pl.BlockSpec((1, tm, d), lambda i, j: (i, j, 0)),
                  pl.BlockSpec((1, 1, d), lambda i, j: (i, 0, 0)),
                  pl.BlockSpec((1, 1, d), lambda i, j: (i, 0, 0)),
                  pl.BlockSpec((d, dout), lambda i, j: (0, 0))],
        out_specs=[pl.BlockSpec((1, tm, s), lambda i, j: (i, j, 0)) for s in splits],
        out_shape=[jax.ShapeDtypeStruct((b, t, s), F32) for s in splits],
        compiler_params=_cparams(("parallel", "parallel")),
        name="inproj",
    )(x, sc, sh, w_bf16)


def _shifted(p, prev_row, next_row):
    tb = p.shape[0]
    row = lax.broadcasted_iota(jnp.int32, p.shape, 0)
    p_prev = jnp.where(row == 0, prev_row, pltpu.roll(p, 1, axis=0))
    p_next = jnp.where(row == tb - 1, next_row, pltpu.roll(p, tb - 1, axis=0))
    return p_prev, p_next


def _edge_rows(pp_ref, pn_ref):
    i = pl.program_id(1)
    nb = pl.num_programs(1)
    prev_row = jnp.where(i > 0, pp_ref[0, SUBLANES - 1:SUBLANES, :], 0.0)
    next_row = jnp.where(i < nb - 1, pn_ref[0, 0:1, :], 0.0)
    return prev_row, next_row


def _feat_kernel(p_ref, pp_ref, pn_ref, mu_ref, dup_ref, d0_ref, aup_ref, a0_ref, gup_ref, kxi_ref, kal_ref,
                 rb_ref, r_o, kk_o, v_o, kr0_o, kr1_o, lw0_o, lw1_o, b0_o, b1_o, g_o, gbv_o):
    p = p_ref[0]
    prev_row, next_row = _edge_rows(pp_ref, pn_ref)
    p_prev, p_next = _shifted(p, prev_row, next_row)
    p = p + mu_ref[...] * (0.5 * (p_prev + p_next) - p)
    dr = D_RWKV
    r = p[:, 0:dr]
    k = p[:, dr:2 * dr]
    v = p[:, 2 * dr:3 * dr]
    c0 = 3 * dr
    dd = jnp.tanh(p[:, c0:c0 + 2 * LORA])
    da = p[:, c0 + 2 * LORA:c0 + 4 * LORA]
    dg = jax.nn.sigmoid(p[:, c0 + 4 * LORA:c0 + 4 * LORA + LORA_G])
    g = _dot3(dg, gup_ref[...])
    lw, a = [], []
    for d in range(2):
        lw.append(-DECAY_SCALE * jax.nn.sigmoid(d0_ref[d:d + 1, :] + _dot3(dd[:, d * LORA:(d + 1) * LORA], dup_ref[d])))
        a.append(jax.nn.sigmoid(a0_ref[d:d + 1, :] + _dot3(da[:, d * LORA:(d + 1) * LORA], aup_ref[d])))
    kx = k * kxi_ref[...]
    kal = kal_ref[...]
    kr = [k * (1.0 + (a[d] - 1.0) * kal) for d in range(2)]
    bon = r * (0.5 * (kr[0] + kr[1])) * rb_ref[...]
    ones = _seg_ones()
    g_o[0] = g
    for j in range(N_PAIRS):
        sl = slice(j * LANES, (j + 1) * LANES)
        kxj = kx[:, sl]
        kk = kxj * lax.rsqrt(_dot_sel(kxj * kxj, ones, 2) + 1e-12)
        r_o[0, j] = r[:, sl].astype(r_o.dtype)
        kk_o[0, j] = kk.astype(kk_o.dtype)
        v_o[0, j] = v[:, sl].astype(v_o.dtype)
        kr0_o[0, j] = kr[0][:, sl].astype(kr0_o.dtype)
        kr1_o[0, j] = kr[1][:, sl].astype(kr1_o.dtype)
        lw0_o[0, j] = lw[0][:, sl]
        lw1_o[0, j] = lw[1][:, sl]
        b0_o[0, j] = (kk * a[0][:, sl]).astype(b0_o.dtype)
        b1_o[0, j] = (kk * a[1][:, sl]).astype(b1_o.dtype)
        gbv_o[0, :, sl] = g[:, sl] * _dot_sel(bon[:, sl], ones, 2) * v[:, sl]


def _rwkv_features(p, prm, tb=256):
    b, t, cols = p.shape
    tb = min(tb, t)
    nb8 = t // SUBLANES
    r8 = tb // SUBLANES
    full = lambda a: pl.BlockSpec(a.shape, lambda i, j: (0,) * a.ndim)
    params = [prm["shift_mu"].reshape(1, cols), prm["decay_up"], prm["decay_0"], prm["a_up"], prm["a_0"],
              prm["g_up"], prm["k_xi"].reshape(1, -1), prm["k_alpha"].reshape(1, -1), prm["r_bonus"].reshape(1, -1)]
    packed = lambda dt: jax.ShapeDtypeStruct((b, N_PAIRS, t, LANES), dt)
    flat = jax.ShapeDtypeStruct((b, t, D_RWKV), F32)
    pspec = pl.BlockSpec((1, N_PAIRS, tb, LANES), lambda i, j: (i, 0, j, 0))
    fspec = pl.BlockSpec((1, tb, D_RWKV), lambda i, j: (i, j, 0))
    return pl.pallas_call(
        _feat_kernel,
        grid=(b, t // tb),
        in_specs=[pl.BlockSpec((1, tb, cols), lambda i, j: (i, j, 0)),
                  pl.BlockSpec((1, SUBLANES, cols), lambda i, j: (i, jnp.maximum(j * r8 - 1, 0), 0)),
                  pl.BlockSpec((1, SUBLANES, cols), lambda i, j: (i, jnp.minimum((j + 1) * r8, nb8 - 1), 0))]
                 + [full(a) for a in params],
        out_specs=[pspec] * 9 + [fspec] * 2,
        out_shape=[packed(BF16)] * 5 + [packed(F32)] * 2 + [packed(BF16)] * 2 + [flat] * 2,
        compiler_params=_cparams(("parallel", "parallel")),
        name="rwkv_features",
    )(p, p, p, *params)


def _scan_chunks(feats, rev):
    n = SCAN_CHUNK
    m0 = _half_mask((n, LANES))
    m1 = 1.0 - m0
    ti = lax.broadcasted_iota(jnp.int32, (n, n), 0)
    tj = lax.broadcasted_iota(jnp.int32, (n, n), 1)
    tri = ((ti <= tj) if rev else (ti >= tj)).astype(F32)
    si = lax.broadcasted_iota(jnp.int32, (2 * n, 2 * n), 0)
    sj = lax.broadcasted_iota(jnp.int32, (2 * n, 2 * n), 1)
    same = (si // n) == (sj // n)
    ri, rj = si % n, sj % n
    strict = same & ((ri < rj) if rev else (ri > rj))
    incl = same & ((ri <= rj) if rev else (ri >= rj))
    eye = (si == sj).astype(F32)

    def stack(x):
        return jnp.concatenate([x * m0, x * m1], axis=0).astype(BF16)

    pre = []
    for r, lw, kap, bb, kr, v in feats:
        cs = _dot_sel(tri, lw, 3, exact="lhs")
        tot = jnp.sum(lw, axis=0, keepdims=True)
        e_neg = jnp.exp(-cs)
        e_tot = jnp.exp(tot - cs)
        rh = r * jnp.exp(cs)
        pre.append(dict(kh=stack(kap * jnp.exp(cs - lw)), bh=stack(bb * e_neg), kq=stack(kr * e_neg), rh=rh,
                        rhs=stack(rh), kt=stack(kr * e_tot), bt=stack(bb * e_tot), v=stack(v), gl=jnp.exp(tot)))
    nt = (((1,), (1,)), ((), ()))
    amats = [lax.dot_general(jnp.concatenate([p["kh"], p["rhs"]], axis=0), jnp.concatenate([p["bh"], p["kq"]], axis=0),
                             nt, preferred_element_type=F32) for p in pre]
    a1 = [jnp.where(strict, a[:2 * n, :2 * n], 0.0) for a in amats]
    a2 = [jnp.where(strict, a[:2 * n, 2 * n:], 0.0).astype(BF16) for a in amats]
    a4 = [jnp.where(incl, a[2 * n:, :2 * n], 0.0).astype(BF16) for a in amats]
    a3 = [jnp.where(incl, a[2 * n:, 2 * n:], 0.0).astype(BF16) for a in amats]
    tinv = [eye - a for a in a1]
    pw = a1
    for _ in range(5):
        pw = [_bdot(p, p) for p in pw]
        tinv = [t + _bdot(t, p) for t, p in zip(tinv, pw)]
    w_s = [_bdot(a, p["v"]) for a, p in zip(a2, pre)]
    mm = [_bdot(t, jnp.concatenate([p["kh"], w.astype(BF16)], axis=1)).astype(BF16)
          for t, p, w in zip(tinv, pre, w_s)]
    gmat = [_bdot(p["bt"].astype(F32).T, m) for p, m in zip(pre, mm)]
    phi_t = [eye * p["gl"] - g[:, :LANES] for p, g in zip(pre, gmat)]
    psi_t = [_bdot(p["kt"].astype(F32).T, p["v"]) - g[:, LANES:] for p, g in zip(pre, gmat)]
    qy = [_bdot(a, m) for a, m in zip(a4, mm)]
    y0_s = [_bdot(a, p["v"]) - q[:, LANES:] for a, p, q in zip(a3, pre, qy)]
    return [(p["rh"] - (q[:n, :LANES] + q[n:, :LANES]), y0[:n] + y0[n:], ph, ps)
            for p, q, y0, ph, ps in zip(pre, qy, y0_s, phi_t, psi_t)]


def _scan_kernel(r_ref, lw_ref, kap_ref, b_ref, kr_ref, v_ref, s0_ref, y_ref, sT_ref, st_scr, *, rev):
    c = pl.program_id(1)

    @pl.when(c == 0)
    def _():
        st_scr[...] = s0_ref[0]

    n = SCAN_CHUNK
    nsub = r_ref.shape[2] // n
    order = range(nsub - 1, -1, -1) if rev else range(nsub)
    rows = lambda ref, hp, k: ref[0, hp, k * n:(k + 1) * n, :]
    items = [(k, hp) for k in order for hp in range(N_PAIRS)]
    terms = _scan_chunks([tuple(rows(ref, hp, k) for ref in (r_ref, lw_ref, kap_ref, b_ref, kr_ref, v_ref))
                          for k, hp in items], rev)
    states = [st_scr[hp] for hp in range(N_PAIRS)]
    for (k, hp), (qm, y0, phi_t, psi_t) in zip(items, terms):
        y_ref[0, hp, k * n:(k + 1) * n, :] = _bdot(qm, states[hp]) + y0
        states[hp] = _bdot(phi_t, states[hp]) + psi_t
    for hp in range(N_PAIRS):
        st_scr[hp] = states[hp]

    @pl.when(c == pl.num_programs(1) - 1)
    def _():
        sT_ref[0] = st_scr[...]


def _rwkv_scan(r, lw, kap, bb, kr, v, s0, rev, chunks_per_step=4):
    b, _, t, _ = r.shape
    n = SCAN_CHUNK * math.gcd(chunks_per_step, t // SCAN_CHUNK)
    nc = t // n
    cidx = (lambda c: nc - 1 - c) if rev else (lambda c: c)
    fspec = pl.BlockSpec((1, N_PAIRS, n, LANES), lambda i, c: (i, 0, cidx(c), 0))
    sspec = pl.BlockSpec((1, N_PAIRS, LANES, LANES), lambda i, c: (i, 0, 0, 0))
    return pl.pallas_call(
        functools.partial(_scan_kernel, rev=rev),
        grid=(b, nc),
        in_specs=[fspec] * 6 + [sspec],
        out_specs=[fspec, sspec],
        out_shape=[jax.ShapeDtypeStruct(r.shape, F32), jax.ShapeDtypeStruct(s0.shape, F32)],
        scratch_shapes=[pltpu.VMEM((N_PAIRS, LANES, LANES), F32)],
        compiler_params=_cparams(("parallel", "arbitrary")),
        name="rwkv_scan_rev" if rev else "rwkv_scan_fwd",
    )(r, lw, kap, bb, kr, v, s0)


def _even_mixer_rows(y0_ref, y1_ref, g_ref, gbv_ref, pc_ref, pcp_ref, pcn_ref, cw_ref, gng_ref, gnb_ref):
    pc = pc_ref[0]
    prev_row, next_row = _edge_rows(pcp_ref, pcn_ref)
    dc = D_CONV

    def gated(z):
        return z[:, dc:2 * dc] * z[:, 2 * dc:3 * dc]

    u = gated(pc)
    u_prev, u_next = _shifted(u, gated(prev_row), gated(next_row))
    cw = cw_ref[...]
    conv = (pc[:, 0:dc] * (cw[0:1] * u_prev + cw[1:2] * u + cw[2:3] * u_next)).astype(BF16)
    ones = _seg_ones() * (1.0 / RWKV_HD)
    rw = []
    for j in range(N_PAIRS):
        sl = slice(j * LANES, (j + 1) * LANES)
        y = y0_ref[0, j] + y1_ref[0, j]
        mu = _dot_sel(y, ones, 2)
        yc = y - mu
        var = _dot_sel(yc * yc, ones, 2)
        yn = yc * lax.rsqrt(var + GN_EPS) * gng_ref[:, sl] + gnb_ref[:, sl]
        rw.append((g_ref[0, :, sl] * yn + gbv_ref[0, :, sl]).astype(BF16))
    return conv, jnp.concatenate(rw, axis=1)


def _even_out_kernel(y0_ref, y1_ref, g_ref, gbv_ref, pc_ref, pcp_ref, pcn_ref, cw_ref, gng_ref, gnb_ref,
                     wa_ref, wb_ref, x_ref, gm_ref, lg_ref, lb_ref, sc_ref, sh_ref, wr_ref, o_ref, h_o, aff_o):
    conv, rw = _even_mixer_rows(y0_ref, y1_ref, g_ref, gbv_ref, pc_ref, pcp_ref, pcn_ref, cw_ref, gng_ref, gnb_ref)
    yx = jnp.dot(conv, wa_ref[...], preferred_element_type=F32) + jnp.dot(rw, wb_ref[...], preferred_element_type=F32)
    _norm_and_route(yx, x_ref, gm_ref, lg_ref, lb_ref, sc_ref, sh_ref, wr_ref, o_ref, h_o, aff_o)


def _even_out(y0, y1, g, gbv, pc, conv_w, gn_g, gn_b, w_out_bf16, x, gm, ln_g, ln_b, sc_f, sh_f, w_router_t,
              tb=512):
    b, t, d = x.shape
    tb = min(tb, t)
    nb8 = t // SUBLANES
    r8 = tb // SUBLANES
    c3 = pc.shape[-1]
    e = w_router_t.shape[0]
    wa, wb = w_out_bf16[:D_CONV], w_out_bf16[D_CONV:]
    pspec = pl.BlockSpec((1, N_PAIRS, tb, LANES), lambda i, j: (i, 0, j, 0))
    row = lambda w: pl.BlockSpec((1, tb, w), lambda i, j: (i, j, 0))
    full = lambda a: pl.BlockSpec(a.shape, lambda i, j: (0,) * a.ndim)
    vec = pl.BlockSpec((1, 1, d), lambda i, j: (i, 0, 0))
    gn_g, gn_b = gn_g.reshape(1, -1), gn_b.reshape(1, -1)
    ln_g, ln_b = ln_g.reshape(1, d), ln_b.reshape(1, d)
    return pl.pallas_call(
        _even_out_kernel,
        grid=(b, t // tb),
        in_specs=[pspec, pspec, row(D_RWKV), row(D_RWKV), row(c3),
                  pl.BlockSpec((1, SUBLANES, c3), lambda i, j: (i, jnp.maximum(j * r8 - 1, 0), 0)),
                  pl.BlockSpec((1, SUBLANES, c3), lambda i, j: (i, jnp.minimum((j + 1) * r8, nb8 - 1), 0)),
                  full(conv_w), full(gn_g), full(gn_b), full(wa), full(wb), row(d), vec, full(ln_g), full(ln_b),
                  vec, vec, full(w_router_t)],
        out_specs=[row(d), row(d), pl.BlockSpec((1, e, tb), lambda i, j: (i, 0, j))],
        out_shape=[jax.ShapeDtypeStruct((b, t, d), F32), jax.ShapeDtypeStruct((b, t, d), BF16),
                   jax.ShapeDtypeStruct((b, e, t), F32)],
        compiler_params=_cparams(("parallel", "parallel")),
        name="even_out_norm_router",
    )(y0, y1, g, gbv, pc, pc, pc, conv_w, gn_g, gn_b, wa, wb, x, gm, ln_g, ln_b, sc_f, sh_f, w_router_t)


def _outproj_kernel(a_ref, b_ref, wa_ref, wb_ref, x_ref, gm_ref, lg_ref, lb_ref, sc_ref, sh_ref, wr_ref,
                    o_ref, h_o, aff_o):
    yx = _bdot(a_ref[0], wa_ref[...]) + _bdot(b_ref[0], wb_ref[...])
    _norm_and_route(yx, x_ref, gm_ref, lg_ref, lb_ref, sc_ref, sh_ref, wr_ref, o_ref, h_o, aff_o)


def _norm_and_route(yx, x_ref, gm_ref, lg_ref, lb_ref, sc_ref, sh_ref, wr_ref, o_ref, h_o, aff_o):
    z = ALPHA * x_ref[0] + gm_ref[0] * yx
    xn = _layer_norm_rows(z, lg_ref[...], lb_ref[...], LN_EPS)
    o_ref[0] = xn
    h = xn * (1.0 + sc_ref[0]) + sh_ref[0]
    h_o[0] = h.astype(h_o.dtype)
    logits = lax.dot_general(wr_ref[...], h, (((1,), (1,)), ((), ())), preferred_element_type=F32, precision=HI)
    m = jnp.max(logits, axis=0, keepdims=True)
    e = jnp.exp(logits - m)
    aff_o[0] = e / jnp.sum(e, axis=0, keepdims=True)


def _outproj(ma, mb, w_out_bf16, x, gm, ln_g, ln_b, sc_f, sh_f, w_router_t, tm=512):
    b, t, d = x.shape
    tm = min(tm, t)
    e = w_router_t.shape[0]
    da, db = ma.shape[-1], mb.shape[-1]
    wa, wb = w_out_bf16[:da], w_out_bf16[da:]
    row = lambda w: pl.BlockSpec((1, tm, w), lambda i, j: (i, j, 0))
    full = lambda a: pl.BlockSpec(a.shape, lambda i, j: (0,) * a.ndim)
    vec = pl.BlockSpec((1, 1, d), lambda i, j: (i, 0, 0))
    ln_g = ln_g.reshape(1, d)
    ln_b = ln_b.reshape(1, d)
    return pl.pallas_call(
        _outproj_kernel,
        grid=(b, t // tm),
        in_specs=[row(da), row(db), full(wa), full(wb), row(d), vec, full(ln_g), full(ln_b), vec, vec,
                  full(w_router_t)],
        out_specs=[row(d), row(d), pl.BlockSpec((1, e, tm), lambda i, j: (i, 0, j))],
        out_shape=[jax.ShapeDtypeStruct((b, t, d), F32), jax.ShapeDtypeStruct((b, t, d), BF16),
                   jax.ShapeDtypeStruct((b, e, t), F32)],
        compiler_params=_cparams(("parallel", "parallel")),
        name="outproj_norm_router",
    )(ma, mb, wa, wb, x, gm, ln_g, ln_b, sc_f, sh_f, w_router_t)


def _inproj_attn_kernel(x_ref, sc_ref, sh_ref, w_ref, *refs, rope, qscale):
    if rope:
        cos_ref, sin_ref, q_o, k_o, v_o, g_o = refs
        cos, sin = cos_ref[...], sin_ref[...]
        lane = lax.broadcasted_iota(jnp.int32, cos.shape, 1)
        half = AXIS_DIM // 2
        first = (lane % (2 * half)) < half

        def rot(z):
            partner = jnp.where(first, -pltpu.roll(z, LANES - half, axis=1), pltpu.roll(z, half, axis=1))
            return z * cos + partner * sin
    else:
        q_o, k_o, v_o, g_o = refs
        rot = lambda z: z
    h = x_ref[0] * (1.0 + sc_ref[0]) + sh_ref[0]
    y = _bdot(h, w_ref[...])
    for j in range(DIFF_HEADS):
        sl = slice(j * LANES, (j + 1) * LANES)
        q_o[0, :, sl] = (rot(y[:, sl]) * qscale).astype(q_o.dtype)
        k_o[0, :, sl] = rot(y[:, D_DIFF + j * LANES:D_DIFF + (j + 1) * LANES]).astype(k_o.dtype)
    v_o[0] = y[:, 2 * D_DIFF:3 * D_DIFF].astype(v_o.dtype)
    g_o[0] = y[:, 3 * D_DIFF:]


def _inproj_attn(x, sc, sh, w_bf16, tables, qscale, tm=512):
    b, t, d = x.shape
    tm = min(tm, t)
    dout = w_bf16.shape[1]
    rope = tables is not None
    row = lambda w: pl.BlockSpec((1, tm, w), lambda i, j: (i, j, 0))
    vec = pl.BlockSpec((1, 1, d), lambda i, j: (i, 0, 0))
    tab = [pl.BlockSpec((tm, LANES), lambda i, j: (j, 0))] * 2 if rope else []
    return pl.pallas_call(
        functools.partial(_inproj_attn_kernel, rope=rope, qscale=qscale),
        grid=(b, t // tm),
        in_specs=[row(d), vec, vec, pl.BlockSpec((d, dout), lambda i, j: (0, 0))] + tab,
        out_specs=[row(D_DIFF)] * 3 + [row(dout - 3 * D_DIFF)],
        out_shape=[jax.ShapeDtypeStruct((b, t, D_DIFF), BF16)] * 3
                  + [jax.ShapeDtypeStruct((b, t, dout - 3 * D_DIFF), F32)],
        compiler_params=_cparams(("parallel", "parallel")),
        name="inproj_attn",
    )(x, sc, sh, w_bf16, *(tables if rope else ()))


def _rope_tables(t):
    rows = t // GRID_W
    row = jnp.repeat(jnp.arange(rows), GRID_W).astype(F32)
    col = jnp.tile(jnp.arange(GRID_W), rows).astype(F32)
    inv = ROPE_BASE ** (-jnp.arange(0, AXIS_DIM, 2, dtype=F32) / AXIS_DIM)
    ang_r = row[:, None] * inv
    ang_c = col[:, None] * inv
    ang = jnp.concatenate([ang_r, ang_r, ang_c, ang_c], axis=-1)
    ang = jnp.concatenate([ang, ang], axis=-1)
    return jnp.cos(ang), jnp.sin(ang)


def _attn_kernel(lam_ref, q_ref, k_ref, v_ref, g_ref, o_ref, *, tk, out_scale):
    q = q_ref[0]
    tq = q.shape[0]
    hm = _half_mask(q.shape, BF16)
    q1 = q * hm
    qs = (q1, q - q1)
    nk = k_ref.shape[1] // tk
    nsl = tk // LANES

    def scores(qh, j):
        kb = k_ref[0, pl.ds(pl.multiple_of(j * tk, tk), tk), :]
        return lax.dot_general(qh, kb, (((1,), (1,)), ((), ())), preferred_element_type=F32)

    def col(s, c):
        return s[:, c * LANES:(c + 1) * LANES]

    def body(j, carry):
        vb = v_ref[0, pl.ds(pl.multiple_of(j * tk, tk), tk), :]
        ss = [scores(qh, j) for qh in qs]
        out = []
        for s, (m, ls, acc) in zip(ss, (carry[:3], carry[3:])):
            mx = col(s, 0)
            for c in range(1, nsl):
                mx = jnp.maximum(mx, col(s, c))
            m_new = jnp.maximum(m, jnp.broadcast_to(jnp.max(mx, axis=-1, keepdims=True), (tq, LANES)))
            corr = jnp.exp2(m - m_new)
            ps = [jnp.exp2(col(s, c) - m_new) for c in range(nsl)]
            ls = corr * ls
            for pc in ps:
                ls = ls + pc
            p = jnp.concatenate([pc.astype(BF16) for pc in ps], axis=1)
            acc = corr * acc + jnp.dot(p, vb, preferred_element_type=F32)
            out += [m_new, ls, acc]
        return tuple(out)

    neg = jnp.full((tq, LANES), -jnp.inf, F32)
    zero = jnp.zeros((tq, LANES), F32)
    _, ls1, acc1, _, ls2, acc2 = lax.fori_loop(0, nk, body, (neg, zero, zero, neg, zero, zero), unroll=True)
    l1 = jnp.sum(ls1, axis=-1, keepdims=True)
    l2 = jnp.sum(ls2, axis=-1, keepdims=True)
    o = acc1 / l1 - lam_ref[0] * (acc2 / l2)
    o = o * lax.rsqrt(jnp.mean(o * o, axis=-1, keepdims=True) + RMS_EPS) * g_ref[...] * out_scale
    o_ref[0] = o.astype(o_ref.dtype)


def _diff_attention(q, k, v, lam, subln_g, lam_init, tq=512, tk=1408):
    b, t, w = q.shape
    tkk = k.shape[1]
    tq = min(tq, t)
    tk = math.gcd(tk, tkk)
    assert tk % LANES == 0 and t % tq == 0
    subln_g = subln_g.reshape(1, DIFF_VD)
    return pl.pallas_call(
        functools.partial(_attn_kernel, tk=tk, out_scale=1.0 - lam_init),
        grid=(b, DIFF_HEADS, t // tq),
        in_specs=[pl.BlockSpec(memory_space=pltpu.SMEM),
                  pl.BlockSpec((1, tq, LANES), lambda i, h, j: (i, j, h)),
                  pl.BlockSpec((1, tkk, LANES), lambda i, h, j: (i, 0, h)),
                  pl.BlockSpec((1, tkk, LANES), lambda i, h, j: (i, 0, h)),
                  pl.BlockSpec((1, DIFF_VD), lambda i, h, j: (0, 0))],
        out_specs=pl.BlockSpec((1, tq, LANES), lambda i, h, j: (i, j, h)),
        out_shape=jax.ShapeDtypeStruct((b, t, w), BF16),
        compiler_params=_cparams(("parallel", "parallel", "parallel")),
        name="diff_attention",
    )(lam, q, k, v, subln_g)


def _gmlp_kernel(p_ref, lg_ref, lb_ref, ws_ref, bs_ref, o_ref):
    p = p_ref[0]
    tb = p.shape[0]
    ge = 0.5 * p * (1.0 + lax.erf(p * (2.0 ** -0.5)))
    u = ge[:, :D_GMLP]
    v = _layer_norm_rows(ge[:, D_GMLP:], lg_ref[...], lb_ref[...], LN_EPS)
    hm = _half_mask((GMLP_CHUNK, LANES))
    for c in range(tb // GMLP_CHUNK):
        rs = slice(c * GMLP_CHUNK, (c + 1) * GMLP_CHUNK)
        for j in range(D_GMLP // LANES):
            sl = slice(j * LANES, (j + 1) * LANES)
            vc = v[rs, sl]
            va = vc * hm
            mixed = _bdot(ws_ref[2 * j], va) + _bdot(ws_ref[2 * j + 1], vc - va) + bs_ref[:, sl]
            o_ref[0, rs, sl] = (u[rs, sl] * mixed).astype(o_ref.dtype)


def _chunk_gmlp(p, ln_g, ln_b, ws, bs, tb=256):
    b, t, w = p.shape
    tb = min(tb, t)
    ln_g = ln_g.reshape(1, -1)
    ln_b = ln_b.reshape(1, -1)
    bs_t = jnp.repeat(jnp.transpose(bs), D_GMLP // ws.shape[0], axis=1)
    full = lambda a: pl.BlockSpec(a.shape, lambda i, j: (0,) * a.ndim)
    return pl.pallas_call(
        _gmlp_kernel,
        grid=(b, t // tb),
        in_specs=[pl.BlockSpec((1, tb, w), lambda i, j: (i, j, 0)), full(ln_g), full(ln_b), full(ws), full(bs_t)],
        out_specs=pl.BlockSpec((1, tb, D_GMLP), lambda i, j: (i, j, 0)),
        out_shape=jax.ShapeDtypeStruct((b, t, D_GMLP), BF16),
        compiler_params=_cparams(("parallel", "parallel")),
        name="chunk_gmlp",
    )(p, ln_g, ln_b, ws, bs_t)


ROUTE_BLOCK = 128
STARTS_PAD = LANES


def _lane_cumsum(mask):
    e, t = mask.shape
    i = lax.broadcasted_iota(jnp.int32, (LANES, LANES), 0)
    j = lax.broadcasted_iota(jnp.int32, (LANES, LANES), 1)
    upper = (i <= j).astype(BF16)
    x = jnp.where(mask, 1.0, 0.0).astype(BF16)
    off = jnp.zeros((e, 1), F32)
    out = []
    for c in range(t // LANES):
        blk = jnp.dot(x[:, c * LANES:(c + 1) * LANES], upper, preferred_element_type=F32) + off
        out.append(blk)
        off = blk[:, LANES - 1:LANES]
    return jnp.concatenate(out, axis=1)


def _select_kernel(aff_ref, pos_ref, starts_ref, *, cap):
    a = aff_ref[0]
    e, t = a.shape
    bits = pltpu.bitcast(a, jnp.int32)

    def search(i, thr):
        cand = thr | jnp.left_shift(jnp.int32(1), 30 - i)
        cnt = jnp.sum(jnp.where(bits >= cand, 1.0, 0.0), axis=1, keepdims=True)
        return jnp.where(cnt >= cap, cand, thr)

    thr = lax.fori_loop(0, 31, search, jnp.zeros((e, 1), jnp.int32))
    gt = bits > thr
    eq = bits == thr
    need = cap - jnp.sum(jnp.where(gt, 1.0, 0.0), axis=1, keepdims=True)
    sel = gt | (eq & (_lane_cumsum(eq) <= need))
    csel = _lane_cumsum(sel)
    pos_ref[0] = jnp.where(sel, csel - 1.0, -1.0).astype(jnp.int32)
    ti = lax.broadcasted_iota(jnp.int32, (t, STARTS_PAD), 0)
    ki = lax.broadcasted_iota(jnp.int32, (t, STARTS_PAD), 1)
    pick = jnp.where(ti == ki * ROUTE_BLOCK - 1, 1.0, 0.0)
    starts_ref[0] = (_dot_sel(csel, pick, 2) + 0.5).astype(jnp.int32)


def _select(aff, cap):
    b, e, t = aff.shape
    assert t % ROUTE_BLOCK == 0 and t // ROUTE_BLOCK < STARTS_PAD
    return pl.pallas_call(
        functools.partial(_select_kernel, cap=cap),
        grid=(b,),
        in_specs=[pl.BlockSpec((1, e, t), lambda i: (i, 0, 0))],
        out_specs=[pl.BlockSpec((1, e, t), lambda i: (i, 0, 0)), pl.BlockSpec((1, e, STARTS_PAD), lambda i: (i, 0, 0))],
        out_shape=[jax.ShapeDtypeStruct((b, e, t), jnp.int32), jax.ShapeDtypeStruct((b, e, STARTS_PAD), jnp.int32)],
        compiler_params=_cparams(("parallel",)),
        name="expert_select",
    )(aff)


def _window_start(start, align, cap, width):
    s = lax.shift_left(lax.shift_right_logical(start, align.bit_length() - 1), align.bit_length() - 1)
    return pl.multiple_of(jnp.minimum(s, cap - width), align)


def _gather_ffn_kernel(starts_ref, pos_ref, aff_ref, h_ref, w1_ref, w3_ref, w2_ref, y_ref, xs_scr, gate_scr, *,
                       width, rows, cap):
    ei, bi, k = pl.program_id(0), pl.program_id(1), pl.program_id(2)
    ne, nb = pl.num_programs(0), pl.num_programs(1)
    nsub = h_ref.shape[1] // ROUTE_BLOCK
    merged = xs_scr.shape[0] > cap
    base = pl.multiple_of(bi * cap, SUBLANES) if merged else 0

    @pl.when(k == 0)
    def _():
        xs_scr[pl.ds(base, cap), :] = jnp.zeros((cap, xs_scr.shape[1]), F32)
        gate_scr[pl.ds(base, cap), :] = jnp.zeros((cap, LANES), F32)

    def copy_rows(sb, start, w):
        s8 = _window_start(start, SUBLANES, cap, w)
        cols = slice(sb * ROUTE_BLOCK, (sb + 1) * ROUTE_BLOCK)
        jrow = lax.broadcasted_iota(jnp.int32, (w, ROUTE_BLOCK), 0)
        hit = pos_ref[0, :, cols] - s8 == jrow
        xs_scr[pl.ds(base + s8, w), :] += jnp.dot(jnp.where(hit, 1.0, 0.0).astype(BF16), h_ref[0, cols, :],
                                                  preferred_element_type=F32)
        gate = jnp.sum(jnp.where(hit, aff_ref[0, :, cols], 0.0), axis=1, keepdims=True)
        gate_scr[pl.ds(base + s8, w), :] += jnp.broadcast_to(gate, (w, LANES))

    for sb in range(nsub):
        copy_rows(sb, starts_ref[(bi * ne + ei) * STARTS_PAD + k * nsub + sb], width)

    last = k == pl.num_programs(2) - 1

    @pl.when(last & (bi == nb - 1) if merged else last)
    def _():
        for r in range(xs_scr.shape[0] // rows):
            x = xs_scr[r * rows:(r + 1) * rows, :].astype(BF16)
            h1 = jnp.dot(x, w1_ref[0], preferred_element_type=F32)
            h3 = jnp.dot(x, w3_ref[0], preferred_element_type=F32)
            hid = (h1 * jax.nn.sigmoid(h1)) * h3
            y = jnp.dot(hid.astype(BF16), w2_ref[0], preferred_element_type=F32)
            y = (y * gate_scr[r * rows:(r + 1) * rows, 0:1]).astype(y_ref.dtype)
            for s in range(max(rows // cap, 1)):
                n = min(rows, cap)
                row0 = r * rows + s * cap
                y_ref[row0 // cap, 0, row0 % cap:row0 % cap + n, :] = y[s * cap:s * cap + n]


def _gather_ffn(h, pos, aff, starts, w1, w3, w2, cap, tch=4096, rows=512):
    b, t, d = h.shape
    e, _, f = w1.shape
    tch = min(tch, t)
    merged = b * cap <= rows
    slots = b if merged else 1
    rows = min(rows, slots * cap)
    assert (slots * cap) % rows == 0 and (rows % cap == 0 or cap % rows == 0)
    width = min(ROUTE_BLOCK + SUBLANES, cap)
    grid_spec = pltpu.PrefetchScalarGridSpec(
        num_scalar_prefetch=1,
        grid=(e, b, t // tch),
        in_specs=[pl.BlockSpec((1, 1, tch), lambda ei, bi, k, s: (bi * e + ei, 0, k)),
                  pl.BlockSpec((1, 1, tch), lambda ei, bi, k, s: (bi * e + ei, 0, k)),
                  pl.BlockSpec((1, tch, d), lambda ei, bi, k, s: (bi, k, 0)),
                  pl.BlockSpec((1, d, f), lambda ei, bi, k, s: (ei, 0, 0)),
                  pl.BlockSpec((1, d, f), lambda ei, bi, k, s: (ei, 0, 0)),
                  pl.BlockSpec((1, f, d), lambda ei, bi, k, s: (ei, 0, 0))],
        out_specs=pl.BlockSpec((slots, 1, cap, d), (lambda ei, bi, k, s: (0, ei, 0, 0)) if merged
                               else (lambda ei, bi, k, s: (bi, ei, 0, 0))),
        scratch_shapes=[pltpu.VMEM((slots * cap, d), F32), pltpu.VMEM((slots * cap, LANES), F32)],
    )
    return pl.pallas_call(
        functools.partial(_gather_ffn_kernel, width=width, rows=rows, cap=cap),
        grid_spec=grid_spec,
        out_shape=jax.ShapeDtypeStruct((b, e, cap, d), BF16),
        compiler_params=_cparams(("parallel", "arbitrary", "arbitrary") if merged
                                 else ("parallel", "parallel", "arbitrary")),
        name="expert_gather_ffn",
    )(starts.reshape(-1), pos.reshape(b * e, 1, t), aff.reshape(b * e, 1, t), h, w1, w3, w2)


def _combine_kernel(starts_ref, post_ref, y_ref, x_ref, gf_ref, lg_ref, lb_ref, o_ref, acc_scr, *, width, group):
    bi, i, eg = pl.program_id(0), pl.program_id(1), pl.program_id(2)
    ne = pl.num_programs(2) * group
    cap = y_ref.shape[2]
    nsub = x_ref.shape[1] // ROUTE_BLOCK

    @pl.when(eg == 0)
    def _():
        acc_scr[...] = jnp.zeros_like(acc_scr)

    lane = lax.broadcasted_iota(jnp.int32, (ROUTE_BLOCK, ne), 1)
    jcol = lax.broadcasted_iota(jnp.int32, (ROUTE_BLOCK, width), 1)
    for sb in range(nsub):
        rs = slice(sb * ROUTE_BLOCK, (sb + 1) * ROUTE_BLOCK)
        pblk = post_ref[0, rs, :].astype(F32)
        total = None
        for j in range(group):
            ei = eg * group + j
            start = starts_ref[(bi * ne + ei) * STARTS_PAD + i * nsub + sb]
            s16 = _window_start(start, 2 * SUBLANES, cap, width)
            pcol = jnp.sum(jnp.where(lane == ei, pblk, 0.0), axis=1, keepdims=True)
            onehot = jnp.where(pcol.astype(jnp.int32) - s16 == jcol, 1.0, 0.0).astype(BF16)
            part = jnp.dot(onehot, y_ref[0, j, pl.ds(s16, width), :], preferred_element_type=F32)
            total = part if total is None else total + part
        acc_scr[rs, :] += total

    @pl.when(eg == pl.num_programs(2) - 1)
    def _():
        z = ALPHA * x_ref[0] + gf_ref[0] * acc_scr[...]
        o_ref[0] = _layer_norm_rows(z, lg_ref[...], lb_ref[...], LN_EPS)


def _combine(x, y, pos_t, starts, gf, ln_g, ln_b, tbo=1024, group=4):
    b, t, d = x.shape
    _, e, cap, _ = y.shape
    tbo = min(tbo, t)
    width = min(ROUTE_BLOCK + 2 * SUBLANES, cap)
    grid_spec = pltpu.PrefetchScalarGridSpec(
        num_scalar_prefetch=1,
        grid=(b, t // tbo, e // group),
        in_specs=[pl.BlockSpec((1, tbo, e), lambda bi, i, eg, s: (bi, i, 0)),
                  pl.BlockSpec((1, group, cap, d), lambda bi, i, eg, s: (bi, eg, 0, 0)),
                  pl.BlockSpec((1, tbo, d), lambda bi, i, eg, s: (bi, i, 0)),
                  pl.BlockSpec((1, 1, d), lambda bi, i, eg, s: (bi, 0, 0)),
                  pl.BlockSpec((1, d), lambda bi, i, eg, s: (0, 0)),
                  pl.BlockSpec((1, d), lambda bi, i, eg, s: (0, 0))],
        out_specs=pl.BlockSpec((1, tbo, d), lambda bi, i, eg, s: (bi, i, 0)),
        scratch_shapes=[pltpu.VMEM((tbo, d), F32)],
    )
    return pl.pallas_call(
        functools.partial(_combine_kernel, width=width, group=group),
        grid_spec=grid_spec,
        out_shape=jax.ShapeDtypeStruct((b, t, d), F32),
        compiler_params=_cparams(("parallel", "parallel", "arbitrary")),
        name="expert_combine_norm",
    )(starts.reshape(-1), pos_t, y, x, gf, ln_g.reshape(1, d), ln_b.reshape(1, d))


def _cast_kernel(x_ref, o_ref):
    o_ref[...] = x_ref[...].astype(o_ref.dtype)


def _to_bf16(w, l):
    _, e, r, c = w.shape
    return pl.pallas_call(
        _cast_kernel,
        grid=(e,),
        in_specs=[pl.BlockSpec((None, 1, r, c), lambda i: (l, i, 0, 0))],
        out_specs=pl.BlockSpec((1, r, c), lambda i: (i, 0, 0)),
        out_shape=jax.ShapeDtypeStruct((e, r, c), BF16),
        compiler_params=_cparams(("parallel",)),
        name="cast_bf16",
    )(w)


def _moe(x, h, aff, gf, w1, w3, w2, ln_g, ln_b):
    t = x.shape[1]
    cap = EC_FACTOR * t // N_EXPERTS
    pos, starts = _select(aff, cap)
    y = _gather_ffn(h, pos, aff, starts, w1, w3, w2, cap)
    return _combine(x, y, jnp.transpose(pos, (0, 2, 1)), starts, gf, ln_g, ln_b)


def _even_layer(hx_in, hc_in, prm, ctx_out):
    w_in = prm["w_in"]
    splits = (3 * D_CONV, RWKV_COLS)
    pcx, prx = _inproj(*hx_in, w_in, splits)
    pcc, prc = _inproj(*hc_in, w_in, splits)
    fx = _rwkv_features(prx, prm)
    fc = _rwkv_features(prc, prm)
    b = prx.shape[0]
    zero = jnp.zeros((b, N_PAIRS, LANES, LANES), F32)
    ys_x, ys_c = [], []
    for d, rev in ((0, False), (1, True)):
        def args(f):
            r, kk, v, kr0, kr1, lw0, lw1, b0, b1 = f[:9]
            return (r, (lw0, lw1)[d], kk, (b0, b1)[d], (kr0, kr1)[d], v)
        yc, s_ctx = _rwkv_scan(*args(fc), zero, rev)
        yx, _ = _rwkv_scan(*args(fx), s_ctx, rev)
        ys_x.append(yx)
        ys_c.append(yc)
    norm = (prm["conv_w"], prm["gn_g"], prm["gn_b"])
    out_x = (ys_x[0], ys_x[1], fx[9], fx[10], pcx) + norm
    out_c = (ys_c[0], ys_c[1], fc[9], fc[10], pcc) + norm if ctx_out else None
    return out_c, out_x


def _odd_layer(hx_in, hc_in, prm, lam_init, ctx_out):
    w_in = prm["w_in"]
    qscale = DIFF_HD ** -0.5 * math.log2(math.e)
    qx, kx, vx, gx = _inproj_attn(*hx_in, w_in, _rope_tables(hx_in[0].shape[1]), qscale)
    qc, kc, vc, gc = _inproj_attn(*hc_in, w_in, None, qscale)
    k_all = jnp.concatenate([kc, kx], axis=1)
    v_all = jnp.concatenate([vc, vx], axis=1)
    lam = (jnp.exp(jnp.sum(prm["lam_q1"] * prm["lam_k1"])) - jnp.exp(jnp.sum(prm["lam_q2"] * prm["lam_k2"]))
           + lam_init).reshape(1).astype(F32)
    att_x = _diff_attention(qx, k_all, v_all, lam, prm["subln_g"], lam_init)
    gm_x = _chunk_gmlp(gx, prm["gmlp_ln_g"], prm["gmlp_ln_b"], prm["gmlp_ws"], prm["gmlp_bs"])
    out_c = None
    if ctx_out:
        att_c = _diff_attention(qc, kc, vc, lam, prm["subln_g"], lam_init)
        gm_c = _chunk_gmlp(gc, prm["gmlp_ln_g"], prm["gmlp_ln_b"], prm["gmlp_ws"], prm["gmlp_bs"])
        out_c = (att_c, gm_c)
    return out_c, (att_x, gm_x)


def kernel(x, c, ctx, c_ctx, w_mod, b_mod, ln_g, ln_b, even_w_in, even_w_out, conv_w, shift_mu, decay_up, decay_0, a_up, a_0, g_up, k_xi, k_alpha, r_bonus, gn_g, gn_b, odd_w_in, odd_w_out, lam_q1, lam_k1, lam_q2, lam_k2, subln_g, gmlp_ln_g, gmlp_ln_b, gmlp_ws, gmlp_bs, w_router, w_e1, w_e3, w_e2):
    bsz, _, d = x.shape
    assert d == D_MODEL and bsz <= SUBLANES - 1
    c8 = jnp.zeros((SUBLANES, d), F32).at[:bsz].set(c).at[bsz].set(c_ctx)
    mod = _modulation(c8, w_mod, b_mod)
    for l in range(DEPTH):
        ctx_out = l < DEPTH - 1
        i = l // 2
        mx = mod[l, :bsz].reshape(bsz, 1, 6, d)
        mc = jnp.broadcast_to(mod[l, bsz].reshape(1, 1, 6, d), (bsz, 1, 6, d))
        part = lambda m, n: m[:, :, n]
        hx_in = (x, part(mx, 1), part(mx, 0))
        hc_in = (ctx, part(mc, 1), part(mc, 0))
        if l % 2 == 0:
            prm = dict(w_in=even_w_in[i].astype(BF16), conv_w=conv_w[i], shift_mu=shift_mu[i],
                       decay_up=decay_up[i], decay_0=decay_0[i], a_up=a_up[i], a_0=a_0[i], g_up=g_up[i],
                       k_xi=k_xi[i], k_alpha=k_alpha[i], r_bonus=r_bonus[i], gn_g=gn_g[i], gn_b=gn_b[i])
            out_c, out_x = _even_layer(hx_in, hc_in, prm, ctx_out)
            w_out = even_w_out[i].astype(BF16)
        else:
            lam_init = 0.8 - 0.6 * math.exp(-0.3 * l)
            prm = dict(w_in=odd_w_in[i].astype(BF16), lam_q1=lam_q1[i], lam_k1=lam_k1[i], lam_q2=lam_q2[i],
                       lam_k2=lam_k2[i], subln_g=subln_g[i], gmlp_ln_g=gmlp_ln_g[i], gmlp_ln_b=gmlp_ln_b[i],
                       gmlp_ws=gmlp_ws[i], gmlp_bs=gmlp_bs[i])
            out_c, out_x = _odd_layer(hx_in, hc_in, prm, lam_init, ctx_out)
            w_out = odd_w_out[i].astype(BF16)
        wr_t = jnp.transpose(w_router[l])
        w1, w3, w2 = _to_bf16(w_e1, l), _to_bf16(w_e3, l), _to_bf16(w_e2, l)
        project = _even_out if l % 2 == 0 else _outproj
        x, h, aff = project(*out_x, w_out, x, part(mx, 2), ln_g[l, 0], ln_b[l, 0], part(mx, 4), part(mx, 3), wr_t)
        x = _moe(x, h, aff, part(mx, 5), w1, w3, w2, ln_g[l, 1], ln_b[l, 1])
        if ctx_out:
            ctx, h, aff = project(*out_c, w_out, ctx, part(mc, 2), ln_g[l, 0], ln_b[l, 0],
                                  part(mc, 4), part(mc, 3), wr_t)
            ctx = _moe(ctx, h, aff, part(mc, 5), w1, w3, w2, ln_g[l, 1], ln_b[l, 1])
    return x
```

```python
import functools
import math

import jax
import jax.numpy as jnp
from jax import lax
from jax.experimental import pallas as pl
from jax.experimental.pallas import tpu as pltpu

F32 = jnp.float32
BF16 = jnp.bfloat16
HI = lax.Precision.HIGHEST

D_MODEL = 1024
DEPTH = 4
GRID_W = 64
D_CONV = 256
RWKV_HEADS = 12
RWKV_HD = 64
D_RWKV = RWKV_HEADS * RWKV_HD
LORA = 64
LORA_G = 128
RWKV_COLS = 3 * D_RWKV + 4 * LORA + LORA_G
DECAY_SCALE = math.exp(-0.5)
GN_EPS = 64e-5
N_PAIRS = RWKV_HEADS // 2
SCAN_CHUNK = 64
DIFF_HEADS = 6
DIFF_HD = 64
DIFF_VD = 2 * DIFF_HD
D_DIFF = DIFF_HEADS * DIFF_VD
AXIS_DIM = DIFF_HD // 2
ROPE_BASE = 10000.0
D_GMLP = 256
GMLP_CHUNK = 128
N_EXPERTS = 16
EC_FACTOR = 2
ALPHA = (2.0 * DEPTH) ** 0.25
LN_EPS = 1e-5
RMS_EPS = 1e-5

LANES = 128
SUBLANES = 8
VMEM_LIMIT = 56 * 1024 * 1024
HEAD_LANES = LANES // 2
assert RWKV_HD == DIFF_HD == D_GMLP // 4 == HEAD_LANES and SCAN_CHUNK == HEAD_LANES


def _cparams(sem):
    return pltpu.CompilerParams(dimension_semantics=sem, vmem_limit_bytes=VMEM_LIMIT)


def _bdot(a, b):
    return jnp.dot(a.astype(BF16), b.astype(BF16), preferred_element_type=F32)


def _hdot(a, b):
    return jnp.dot(a, b, preferred_element_type=F32, precision=HI)


def _split_bf16(x, terms):
    parts = []
    for _ in range(terms):
        p = x.astype(BF16)
        parts.append(p)
        x = x - p.astype(F32)
    return parts


def _dot3(a, b):
    a_hi, a_lo = _split_bf16(a, 2)
    b_hi, b_lo = _split_bf16(b, 2)
    d = lambda x, y: jnp.dot(x, y, preferred_element_type=F32)
    return d(a_hi, b_hi) + (d(a_hi, b_lo) + d(a_lo, b_hi))


def _dot_sel(a, b, terms, exact="rhs"):
    d = lambda x, y: jnp.dot(x, y, preferred_element_type=F32)
    if exact == "rhs":
        bb = b.astype(BF16)
        outs = [d(p, bb) for p in _split_bf16(a, terms)]
    else:
        ab = a.astype(BF16)
        outs = [d(ab, p) for p in _split_bf16(b, terms)]
    out = outs[-1]
    for o in reversed(outs[:-1]):
        out = out + o
    return out


def _half_mask(shape, dtype=F32):
    lane = lax.broadcasted_iota(jnp.int32, shape, len(shape) - 1)
    return (lane < HEAD_LANES).astype(dtype)


def _seg_ones():
    i = lax.broadcasted_iota(jnp.int32, (LANES, LANES), 0)
    j = lax.broadcasted_iota(jnp.int32, (LANES, LANES), 1)
    return ((i // HEAD_LANES) == (j // HEAD_LANES)).astype(F32)


def _layer_norm_rows(z, g, b, eps):
    mu = jnp.mean(z, axis=-1, keepdims=True)
    zc = z - mu
    var = jnp.mean(zc * zc, axis=-1, keepdims=True)
    return zc * lax.rsqrt(var + eps) * g + b


def _mod_kernel(c_ref, w_ref, b_ref, o_ref):
    a = c_ref[...]
    a = a * jax.nn.sigmoid(a)
    o_ref[0] = _hdot(a, w_ref[0]) + b_ref[0]


def _modulation(c8, w_mod, b_mod):
    tn = 1536
    nl, d, n6 = w_mod.shape
    return pl.pallas_call(
        _mod_kernel,
        grid=(nl, n6 // tn),
        in_specs=[pl.BlockSpec((8, d), lambda l, j: (0, 0)),
                  pl.BlockSpec((1, d, tn), lambda l, j: (l, 0, j)),
                  pl.BlockSpec((1, 1, tn), lambda l, j: (l, 0, j))],
        out_specs=pl.BlockSpec((1, 8, tn), lambda l, j: (l, 0, j)),
        out_shape=jax.ShapeDtypeStruct((nl, 8, n6), F32),
        compiler_params=_cparams(("parallel", "parallel")),
        name="modulation",
    )(c8, w_mod, b_mod.reshape(nl, 1, n6))


def _inproj_kernel(x_ref, sc_ref, sh_ref, w_ref, *o_refs, splits):
    h = x_ref[0] * (1.0 + sc_ref[0]) + sh_ref[0]
    y = _bdot(h, w_ref[...])
    off = 0
    for o, s in zip(o_refs, splits):
        o[0] = y[:, off:off + s].astype(o.dtype)
        off += s


def _inproj(x, sc, sh, w_bf16, splits, tm=512):
    b, t, d = x.shape
    tm = min(tm, t)
    dout = w_bf16.shape[1]
    return pl.pallas_call(
        functools.partial(_inproj_kernel, splits=splits),
        grid=(b, t // tm),
        in_specs=[pl.BlockSpec((1, tm, d), lambda i, j: (i, j, 0)),
                  pl.BlockSpec((1, 1, d), lambda i, j: (i, 0, 0)),
                  pl.BlockSpec((1, 1, d), lambda i, j: (i, 0, 0)),
                  pl.BlockSpec((d, dout), lambda i, j: (0, 0))],
        out_specs=[pl.BlockSpec((1, tm, s), lambda i, j: (i, j, 0)) for s in splits],
        out_shape=[jax.ShapeDtypeStruct((b, t, s), F32) for s in splits],
        compiler_params=_cparams(("parallel", "parallel")),
        name="inproj",
    )(x, sc, sh, w_bf16)


def _shifted(p, prev_row, next_row):
    tb = p.shape[0]
    row = lax.broadcasted_iota(jnp.int32, p.shape, 0)
    p_prev = jnp.where(row == 0, prev_row, pltpu.roll(p, 1, axis=0))
    p_next = jnp.where(row == tb - 1, next_row, pltpu.roll(p, tb - 1, axis=0))
    return p_prev, p_next


def _edge_rows(pp_ref, pn_ref):
    i = pl.program_id(1)
    nb = pl.num_programs(1)
    prev_row = jnp.where(i > 0, pp_ref[0, SUBLANES - 1:SUBLANES, :], 0.0)
    next_row = jnp.where(i < nb - 1, pn_ref[0, 0:1, :], 0.0)
    return prev_row, next_row


def _feat_kernel(p_ref, pp_ref, pn_ref, mu_ref, dup_ref, d0_ref, aup_ref, a0_ref, gup_ref, kxi_ref, kal_ref,
                 rb_ref, r_o, kk_o, v_o, kr0_o, kr1_o, lw0_o, lw1_o, b0_o, b1_o, g_o, gbv_o):
    p = p_ref[0]
    prev_row, next_row = _edge_rows(pp_ref, pn_ref)
    p_prev, p_next = _shifted(p, prev_row, next_row)
    p = p + mu_ref[...] * (0.5 * (p_prev + p_next) - p)
    dr = D_RWKV
    r = p[:, 0:dr]
    k = p[:, dr:2 * dr]
    v = p[:, 2 * dr:3 * dr]
    c0 = 3 * dr
    dd = jnp.tanh(p[:, c0:c0 + 2 * LORA])
    da = p[:, c0 + 2 * LORA:c0 + 4 * LORA]
    dg = jax.nn.sigmoid(p[:, c0 + 4 * LORA:c0 + 4 * LORA + LORA_G])
    g = _dot3(dg, gup_ref[...])
    lw, a = [], []
    for d in range(2):
        lw.append(-DECAY_SCALE * jax.nn.sigmoid(d0_ref[d:d + 1, :] + _dot3(dd[:, d * LORA:(d + 1) * LORA], dup_ref[d])))
        a.append(jax.nn.sigmoid(a0_ref[d:d + 1, :] + _dot3(da[:, d * LORA:(d + 1) * LORA], aup_ref[d])))
    kx = k * kxi_ref[...]
    kal = kal_ref[...]
    kr = [k * (1.0 + (a[d] - 1.0) * kal) for d in range(2)]
    bon = r * (0.5 * (kr[0] + kr[1])) * rb_ref[...]
    ones = _seg_ones()
    g_o[0] = g
    for j in range(N_PAIRS):
        sl = slice(j * LANES, (j + 1) * LANES)
        kxj = kx[:, sl]
        kk = kxj * lax.rsqrt(_dot_sel(kxj * kxj, ones, 2) + 1e-12)
        r_o[0, j] = r[:, sl].astype(r_o.dtype)
        kk_o[0, j] = kk.astype(kk_o.dtype)
        v_o[0, j] = v[:, sl].astype(v_o.dtype)
        kr0_o[0, j] = kr[0][:, sl].astype(kr0_o.dtype)
        kr1_o[0, j] = kr[1][:, sl].astype(kr1_o.dtype)
        lw0_o[0, j] = lw[0][:, sl]
        lw1_o[0, j] = lw[1][:, sl]
        b0_o[0, j] = (kk * a[0][:, sl]).astype(b0_o.dtype)
        b1_o[0, j] = (kk * a[1][:, sl]).astype(b1_o.dtype)
        gbv_o[0, :, sl] = g[:, sl] * _dot_sel(bon[:, sl], ones, 2) * v[:, sl]


def _rwkv_features(p, prm, tb=256):
    b, t, cols = p.shape
    tb = min(tb, t)
    nb8 = t // SUBLANES
    r8 = tb // SUBLANES
    full = lambda a: pl.BlockSpec(a.shape, lambda i, j: (0,) * a.ndim)
    params = [prm["shift_mu"].reshape(1, cols), prm["decay_up"], prm["decay_0"], prm["a_up"], prm["a_0"],
              prm["g_up"], prm["k_xi"].reshape(1, -1), prm["k_alpha"].reshape(1, -1), prm["r_bonus"].reshape(1, -1)]
    packed = lambda dt: jax.ShapeDtypeStruct((b, N_PAIRS, t, LANES), dt)
    flat = jax.ShapeDtypeStruct((b, t, D_RWKV), F32)
    pspec = pl.BlockSpec((1, N_PAIRS, tb, LANES), lambda i, j: (i, 0, j, 0))
    fspec = pl.BlockSpec((1, tb, D_RWKV), lambda i, j: (i, j, 0))
    return pl.pallas_call(
        _feat_kernel,
        grid=(b, t // tb),
        in_specs=[pl.BlockSpec((1, tb, cols), lambda i, j: (i, j, 0)),
                  pl.BlockSpec((1, SUBLANES, cols), lambda i, j: (i, jnp.maximum(j * r8 - 1, 0), 0)),
                  pl.BlockSpec((1, SUBLANES, cols), lambda i, j: (i, jnp.minimum((j + 1) * r8, nb8 - 1), 0))]
                 + [full(a) for a in params],
        out_specs=[pspec] * 9 + [fspec] * 2,
        out_shape=[packed(BF16)] * 5 + [packed(F32)] * 2 + [packed(BF16)] * 2 + [flat] * 2,
        compiler_params=_cparams(("parallel", "parallel")),
        name="rwkv_features",
    )(p, p, p, *params)


def _scan_chunks(feats, rev):
    n = SCAN_CHUNK
    m0 = _half_mask((n, LANES))
    m1 = 1.0 - m0
    ti = lax.broadcasted_iota(jnp.int32, (n, n), 0)
    tj = lax.broadcasted_iota(jnp.int32, (n, n), 1)
    tri = ((ti <= tj) if rev else (ti >= tj)).astype(F32)
    si = lax.broadcasted_iota(jnp.int32, (2 * n, 2 * n), 0)
    sj = lax.broadcasted_iota(jnp.int32, (2 * n, 2 * n), 1)
    same = (si // n) == (sj // n)
    ri, rj = si % n, sj % n
    strict = same & ((ri < rj) if rev else (ri > rj))
    incl = same & ((ri <= rj) if rev else (ri >= rj))
    eye = (si == sj).astype(F32)

    def stack(x):
        return jnp.concatenate([x * m0, x * m1], axis=0).astype(BF16)

    pre = []
    for r, lw, kap, bb, kr, v in feats:
        cs = _dot_sel(tri, lw, 3, exact="lhs")
        tot = jnp.sum(lw, axis=0, keepdims=True)
        e_neg = jnp.exp(-cs)
        e_tot = jnp.exp(tot - cs)
        rh = r * jnp.exp(cs)
        pre.append(dict(kh=stack(kap * jnp.exp(cs - lw)), bh=stack(bb * e_neg), kq=stack(kr * e_neg), rh=rh,
                        rhs=stack(rh), kt=stack(kr * e_tot), bt=stack(bb * e_tot), v=stack(v), gl=jnp.exp(tot)))
    nt = (((1,), (1,)), ((), ()))
    amats = [lax.dot_general(jnp.concatenate([p["kh"], p["rhs"]], axis=0), jnp.concatenate([p["bh"], p["kq"]], axis=0),
                             nt, preferred_element_type=F32) for p in pre]
    a1 = [jnp.where(strict, a[:2 * n, :2 * n], 0.0) for a in amats]
    a2 = [jnp.where(strict, a[:2 * n, 2 * n:], 0.0).astype(BF16) for a in amats]
    a4 = [jnp.where(incl, a[2 * n:, :2 * n], 0.0).astype(BF16) for a in amats]
    a3 = [jnp.where(incl, a[2 * n:, 2 * n:], 0.0).astype(BF16) for a in amats]
    tinv = [eye - a for a in a1]
    pw = a1
    for _ in range(5):
        pw = [_bdot(p, p) for p in pw]
        tinv = [t + _bdot(t, p) for t, p in zip(tinv, pw)]
    w_s = [_bdot(a, p["v"]) for a, p in zip(a2, pre)]
    mm = [_bdot(t, jnp.concatenate([p["kh"], w.astype(BF16)], axis=1)).astype(BF16)
          for t, p, w in zip(tinv, pre, w_s)]
    gmat = [_bdot(p["bt"].astype(F32).T, m) for p, m in zip(pre, mm)]
    phi_t = [eye * p["gl"] - g[:, :LANES] for p, g in zip(pre, gmat)]
    psi_t = [_bdot(p["kt"].astype(F32).T, p["v"]) - g[:, LANES:] for p, g in zip(pre, gmat)]
    qy = [_bdot(a, m) for a, m in zip(a4, mm)]
    y0_s = [_bdot(a, p["v"]) - q[:, LANES:] for a, p, q in zip(a3, pre, qy)]
    return [(p["rh"] - (q[:n, :LANES] + q[n:, :LANES]), y0[:n] + y0[n:], ph, ps)
            for p, q, y0, ph, ps in zip(pre, qy, y0_s, phi_t, psi_t)]


def _scan_kernel(r_ref, lw_ref, kap_ref, b_ref, kr_ref, v_ref, s0_ref, y_ref, sT_ref, st_scr, *, rev):
    c = pl.program_id(1)

    @pl.when(c == 0)
    def _():
        st_scr[...] = s0_ref[0]

    n = SCAN_CHUNK
    nsub = r_ref.shape[2] // n
    order = range(nsub - 1, -1, -1) if rev else range(nsub)
    rows = lambda ref, hp, k: ref[0, hp, k * n:(k + 1) * n, :]
    items = [(k, hp) for k in order for hp in range(N_PAIRS)]
    terms = _scan_chunks([tuple(rows(ref, hp, k) for ref in (r_ref, lw_ref, kap_ref, b_ref, kr_ref, v_ref))
                          for k, hp in items], rev)
    states = [st_scr[hp] for hp in range(N_PAIRS)]
    for (k, hp), (qm, y0, phi_t, psi_t) in zip(items, terms):
        y_ref[0, hp, k * n:(k + 1) * n, :] = _bdot(qm, states[hp]) + y0
        states[hp] = _bdot(phi_t, states[hp]) + psi_t
    for hp in range(N_PAIRS):
        st_scr[hp] = states[hp]

    @pl.when(c == pl.num_programs(1) - 1)
    def _():
        sT_ref[0] = st_scr[...]


def _rwkv_scan(r, lw, kap, bb, kr, v, s0, rev, chunks_per_step=4):
    b, _, t, _ = r.shape
    n = SCAN_CHUNK * math.gcd(chunks_per_step, t // SCAN_CHUNK)
    nc = t // n
    cidx = (lambda c: nc - 1 - c) if rev else (lambda c: c)
    fspec = pl.BlockSpec((1, N_PAIRS, n, LANES), lambda i, c: (i, 0, cidx(c), 0))
    sspec = pl.BlockSpec((1, N_PAIRS, LANES, LANES), lambda i, c: (i, 0, 0, 0))
    return pl.pallas_call(
        functools.partial(_scan_kernel, rev=rev),
        grid=(b, nc),
        in_specs=[fspec] * 6 + [sspec],
        out_specs=[fspec, sspec],
        out_shape=[jax.ShapeDtypeStruct(r.shape, F32), jax.ShapeDtypeStruct(s0.shape, F32)],
        scratch_shapes=[pltpu.VMEM((N_PAIRS, LANES, LANES), F32)],
        compiler_params=_cparams(("parallel", "arbitrary")),
        name="rwkv_scan_rev" if rev else "rwkv_scan_fwd",
    )(r, lw, kap, bb, kr, v, s0)


def _even_mixer_rows(y0_ref, y1_ref, g_ref, gbv_ref, pc_ref, pcp_ref, pcn_ref, cw_ref, gng_ref, gnb_ref):
    pc = pc_ref[0]
    prev_row, next_row = _edge_rows(pcp_ref, pcn_ref)
    dc = D_CONV

    def gated(z):
        return z[:, dc:2 * dc] * z[:, 2 * dc:3 * dc]

    u = gated(pc)
    u_prev, u_next = _shifted(u, gated(prev_row), gated(next_row))
    cw = cw_ref[...]
    conv = (pc[:, 0:dc] * (cw[0:1] * u_prev + cw[1:2] * u + cw[2:3] * u_next)).astype(BF16)
    ones = _seg_ones() * (1.0 / RWKV_HD)
    rw = []
    for j in range(N_PAIRS):
        sl = slice(j * LANES, (j + 1) * LANES)
        y = y0_ref[0, j] + y1_ref[0, j]
        mu = _dot_sel(y, ones, 2)
        yc = y - mu
        var = _dot_sel(yc * yc, ones, 2)
        yn = yc * lax.rsqrt(var + GN_EPS) * gng_ref[:, sl] + gnb_ref[:, sl]
        rw.append((g_ref[0, :, sl] * yn + gbv_ref[0, :, sl]).astype(BF16))
    return conv, jnp.concatenate(rw, axis=1)


def _even_out_kernel(y0_ref, y1_ref, g_ref, gbv_ref, pc_ref, pcp_ref, pcn_ref, cw_ref, gng_ref, gnb_ref,
                     wa_ref, wb_ref, x_ref, gm_ref, lg_ref, lb_ref, sc_ref, sh_ref, wr_ref, o_ref, h_o, aff_o):
    conv, rw = _even_mixer_rows(y0_ref, y1_ref, g_ref, gbv_ref, pc_ref, pcp_ref, pcn_ref, cw_ref, gng_ref, gnb_ref)
    yx = jnp.dot(conv, wa_ref[...], preferred_element_type=F32) + jnp.dot(rw, wb_ref[...], preferred_element_type=F32)
    _norm_and_route(yx, x_ref, gm_ref, lg_ref, lb_ref, sc_ref, sh_ref, wr_ref, o_ref, h_o, aff_o)


def _even_out(y0, y1, g, gbv, pc, conv_w, gn_g, gn_b, w_out_bf16, x, gm, ln_g, ln_b, sc_f, sh_f, w_router_t,
              tb=512):
    b, t, d = x.shape
    tb = min(tb, t)
    nb8 = t // SUBLANES
    r8 = tb // SUBLANES
    c3 = pc.shape[-1]
    e = w_router_t.shape[0]
    wa, wb = w_out_bf16[:D_CONV], w_out_bf16[D_CONV:]
    pspec = pl.BlockSpec((1, N_PAIRS, tb, LANES), lambda i, j: (i, 0, j, 0))
    row = lambda w: pl.BlockSpec((1, tb, w), lambda i, j: (i, j, 0))
    full = lambda a: pl.BlockSpec(a.shape, lambda i, j: (0,) * a.ndim)
    vec = pl.BlockSpec((1, 1, d), lambda i, j: (i, 0, 0))
    gn_g, gn_b = gn_g.reshape(1, -1), gn_b.reshape(1, -1)
    ln_g, ln_b = ln_g.reshape(1, d), ln_b.reshape(1, d)
    return pl.pallas_call(
        _even_out_kernel,
        grid=(b, t // tb),
        in_specs=[pspec, pspec, row(D_RWKV), row(D_RWKV), row(c3),
                  pl.BlockSpec((1, SUBLANES, c3), lambda i, j: (i, jnp.maximum(j * r8 - 1, 0), 0)),
                  pl.BlockSpec((1, SUBLANES, c3), lambda i, j: (i, jnp.minimum((j + 1) * r8, nb8 - 1), 0)),
                  full(conv_w), full(gn_g), full(gn_b), full(wa), full(wb), row(d), vec, full(ln_g), full(ln_b),
                  vec, vec, full(w_router_t)],
        out_specs=[row(d), row(d), pl.BlockSpec((1, e, tb), lambda i, j: (i, 0, j))],
        out_shape=[jax.ShapeDtypeStruct((b, t, d), F32), jax.ShapeDtypeStruct((b, t, d), BF16),
                   jax.ShapeDtypeStruct((b, e, t), F32)],
        compiler_params=_cparams(("parallel", "parallel")),
        name="even_out_norm_router",
    )(y0, y1, g, gbv, pc, pc, pc, conv_w, gn_g, gn_b, wa, wb, x, gm, ln_g, ln_b, sc_f, sh_f, w_router_t)


def _norm_and_route(yx, x_ref, gm_ref, lg_ref, lb_ref, sc_ref, sh_ref, wr_ref, o_ref, h_o, aff_o):
    z = ALPHA * x_ref[0] + gm_ref[0] * yx
    xn = _layer_norm_rows(z, lg_ref[...], lb_ref[...], LN_EPS)
    o_ref[0] = xn
    h = xn * (1.0 + sc_ref[0]) + sh_ref[0]
    h_o[0] = h.astype(h_o.dtype)
    logits = lax.dot_general(wr_ref[...], h, (((1,), (1,)), ((), ())), preferred_element_type=F32, precision=HI)
    m = jnp.max(logits, axis=0, keepdims=True)
    e = jnp.exp(logits - m)
    aff_o[0] = e / jnp.sum(e, axis=0, keepdims=True)


def _inproj_attn_kernel(x_ref, sc_ref, sh_ref, w_ref, *refs, rope, qscale):
    if rope:
        cos_ref, sin_ref, q_o, k_o, v_o, g_o = refs
        cos, sin = cos_ref[...], sin_ref[...]
        lane = lax.broadcasted_iota(jnp.int32, cos.shape, 1)
        half = AXIS_DIM // 2
        first = (lane % (2 * half)) < half

        def rot(z):
            partner = jnp.where(first, -pltpu.roll(z, LANES - half, axis=1), pltpu.roll(z, half, axis=1))
            return z * cos + partner * sin
    else:
        q_o, k_o, v_o, g_o = refs
        rot = lambda z: z
    h = x_ref[0] * (1.0 + sc_ref[0]) + sh_ref[0]
    y = _bdot(h, w_ref[...])
    for j in range(DIFF_HEADS):
        sl = slice(j * LANES, (j + 1) * LANES)
        q_o[0, :, sl] = (rot(y[:, sl]) * qscale).astype(q_o.dtype)
        k_o[0, :, sl] = rot(y[:, D_DIFF + j * LANES:D_DIFF + (j + 1) * LANES]).astype(k_o.dtype)
    v_o[0] = y[:, 2 * D_DIFF:3 * D_DIFF].astype(v_o.dtype)
    g_o[0] = y[:, 3 * D_DIFF:]


def _inproj_attn(x, sc, sh, w_bf16, tables, qscale, tm=512):
    b, t, d = x.shape
    tm = min(tm, t)
    dout = w_bf16.shape[1]
    rope = tables is not None
    row = lambda w: pl.BlockSpec((1, tm, w), lambda i, j: (i, j, 0))
    vec = pl.BlockSpec((1, 1, d), lambda i, j: (i, 0, 0))
    tab = [pl.BlockSpec((tm, LANES), lambda i, j: (j, 0))] * 2 if rope else []
    return pl.pallas_call(
        functools.partial(_inproj_attn_kernel, rope=rope, qscale=qscale),
        grid=(b, t // tm),
        in_specs=[row(d), vec, vec, pl.BlockSpec((d, dout), lambda i, j: (0, 0))] + tab,
        out_specs=[row(D_DIFF)] * 3 + [row(dout - 3 * D_DIFF)],
        out_shape=[jax.ShapeDtypeStruct((b, t, D_DIFF), BF16)] * 3
                  + [jax.ShapeDtypeStruct((b, t, dout - 3 * D_DIFF), F32)],
        compiler_params=_cparams(("parallel", "parallel")),
        name="inproj_attn",
    )(x, sc, sh, w_bf16, *(tables if rope else ()))


def _rope_tables(t):
    rows = t // GRID_W
    row = jnp.repeat(jnp.arange(rows), GRID_W).astype(F32)
    col = jnp.tile(jnp.arange(GRID_W), rows).astype(F32)
    inv = ROPE_BASE ** (-jnp.arange(0, AXIS_DIM, 2, dtype=F32) / AXIS_DIM)
    ang_r = row[:, None] * inv
    ang_c = col[:, None] * inv
    ang = jnp.concatenate([ang_r, ang_r, ang_c, ang_c], axis=-1)
    ang = jnp.concatenate([ang, ang], axis=-1)
    return jnp.cos(ang), jnp.sin(ang)


def _attn_kernel(lam_ref, q_ref, k_ref, v_ref, g_ref, o_ref, *, tk, out_scale):
    q = q_ref[0]
    tq = q.shape[0]
    hm = _half_mask(q.shape, BF16)
    q1 = q * hm
    qs = (q1, q - q1)
    nk = k_ref.shape[1] // tk
    nsl = tk // LANES

    def scores(qh, j):
        kb = k_ref[0, pl.ds(pl.multiple_of(j * tk, tk), tk), :]
        return lax.dot_general(qh, kb, (((1,), (1,)), ((), ())), preferred_element_type=F32)

    def col(s, c):
        return s[:, c * LANES:(c + 1) * LANES]

    def body(j, carry):
        vb = v_ref[0, pl.ds(pl.multiple_of(j * tk, tk), tk), :]
        ss = [scores(qh, j) for qh in qs]
        out = []
        for s, (m, ls, acc) in zip(ss, (carry[:3], carry[3:])):
            mx = col(s, 0)
            for c in range(1, nsl):
                mx = jnp.maximum(mx, col(s, c))
            m_new = jnp.maximum(m, jnp.broadcast_to(jnp.max(mx, axis=-1, keepdims=True), (tq, LANES)))
            corr = jnp.exp2(m - m_new)
            ps = [jnp.exp2(col(s, c) - m_new) for c in range(nsl)]
            ls = corr * ls
            for pc in ps:
                ls = ls + pc
            p = jnp.concatenate([pc.astype(BF16) for pc in ps], axis=1)
            acc = corr * acc + jnp.dot(p, vb, preferred_element_type=F32)
            out += [m_new, ls, acc]
        return tuple(out)

    neg = jnp.full((tq, LANES), -jnp.inf, F32)
    zero = jnp.zeros((tq, LANES), F32)
    _, ls1, acc1, _, ls2, acc2 = lax.fori_loop(0, nk, body, (neg, zero, zero, neg, zero, zero), unroll=True)
    l1 = jnp.sum(ls1, axis=-1, keepdims=True)
    l2 = jnp.sum(ls2, axis=-1, keepdims=True)
    o = acc1 / l1 - lam_ref[0] * (acc2 / l2)
    o = o * lax.rsqrt(jnp.mean(o * o, axis=-1, keepdims=True) + RMS_EPS) * g_ref[...] * out_scale
    o_ref[0] = o.astype(o_ref.dtype)


def _diff_attention(q, k, v, lam, subln_g, lam_init, tq=512, tk=1408):
    b, t, w = q.shape
    tkk = k.shape[1]
    tq = min(tq, t)
    tk = math.gcd(tk, tkk)
    assert tk % LANES == 0 and t % tq == 0
    subln_g = subln_g.reshape(1, DIFF_VD)
    return pl.pallas_call(
        functools.partial(_attn_kernel, tk=tk, out_scale=1.0 - lam_init),
        grid=(b, DIFF_HEADS, t // tq),
        in_specs=[pl.BlockSpec(memory_space=pltpu.SMEM),
                  pl.BlockSpec((1, tq, LANES), lambda i, h, j: (i, j, h)),
                  pl.BlockSpec((1, tkk, LANES), lambda i, h, j: (i, 0, h)),
                  pl.BlockSpec((1, tkk, LANES), lambda i, h, j: (i, 0, h)),
                  pl.BlockSpec((1, DIFF_VD), lambda i, h, j: (0, 0))],
        out_specs=pl.BlockSpec((1, tq, LANES), lambda i, h, j: (i, j, h)),
        out_shape=jax.ShapeDtypeStruct((b, t, w), BF16),
        compiler_params=_cparams(("parallel", "parallel", "parallel")),
        name="diff_attention",
    )(lam, q, k, v, subln_g)


def _gmlp_rows(p, lg_ref, lb_ref, ws_ref, bs_ref):
    tb = p.shape[0]
    ge = 0.5 * p * (1.0 + lax.erf(p * (2.0 ** -0.5)))
    u = ge[:, :D_GMLP]
    v = _layer_norm_rows(ge[:, D_GMLP:], lg_ref[...], lb_ref[...], LN_EPS)
    hm = _half_mask((GMLP_CHUNK, LANES))
    chunks = []
    for c in range(tb // GMLP_CHUNK):
        rs = slice(c * GMLP_CHUNK, (c + 1) * GMLP_CHUNK)
        cols = []
        for j in range(D_GMLP // LANES):
            sl = slice(j * LANES, (j + 1) * LANES)
            vc = v[rs, sl]
            va = vc * hm
            mixed = _bdot(ws_ref[2 * j], va) + _bdot(ws_ref[2 * j + 1], vc - va) + bs_ref[:, sl]
            cols.append((u[rs, sl] * mixed).astype(BF16))
        chunks.append(jnp.concatenate(cols, axis=1))
    return jnp.concatenate(chunks, axis=0)


def _odd_out_kernel(a_ref, p_ref, glg_ref, glb_ref, ws_ref, bs_ref, wa_ref, wb_ref, x_ref, gm_ref, lg_ref, lb_ref,
                    sc_ref, sh_ref, wr_ref, o_ref, h_o, aff_o):
    gm = _gmlp_rows(p_ref[0], glg_ref, glb_ref, ws_ref, bs_ref)
    yx = _bdot(a_ref[0], wa_ref[...]) + jnp.dot(gm, wb_ref[...], preferred_element_type=F32)
    _norm_and_route(yx, x_ref, gm_ref, lg_ref, lb_ref, sc_ref, sh_ref, wr_ref, o_ref, h_o, aff_o)


def _gmlp_bias(ws, bs):
    return jnp.repeat(jnp.transpose(bs), D_GMLP // ws.shape[0], axis=1)


def _odd_out(att, p, g_ln_g, g_ln_b, ws, bs, w_out_bf16, x, gm, ln_g, ln_b, sc_f, sh_f, w_router_t, tb=512):
    b, t, d = x.shape
    tb = min(tb, t)
    e = w_router_t.shape[0]
    da = att.shape[-1]
    wa, wb = w_out_bf16[:da], w_out_bf16[da:]
    row = lambda w: pl.BlockSpec((1, tb, w), lambda i, j: (i, j, 0))
    full = lambda a: pl.BlockSpec(a.shape, lambda i, j: (0,) * a.ndim)
    vec = pl.BlockSpec((1, 1, d), lambda i, j: (i, 0, 0))
    g_ln_g, g_ln_b = g_ln_g.reshape(1, -1), g_ln_b.reshape(1, -1)
    ln_g, ln_b = ln_g.reshape(1, d), ln_b.reshape(1, d)
    bs_t = _gmlp_bias(ws, bs)
    return pl.pallas_call(
        _odd_out_kernel,
        grid=(b, t // tb),
        in_specs=[row(da), row(p.shape[-1]), full(g_ln_g), full(g_ln_b), full(ws), full(bs_t), full(wa), full(wb),
                  row(d), vec, full(ln_g), full(ln_b), vec, vec, full(w_router_t)],
        out_specs=[row(d), row(d), pl.BlockSpec((1, e, tb), lambda i, j: (i, 0, j))],
        out_shape=[jax.ShapeDtypeStruct((b, t, d), F32), jax.ShapeDtypeStruct((b, t, d), BF16),
                   jax.ShapeDtypeStruct((b, e, t), F32)],
        compiler_params=_cparams(("parallel", "parallel")),
        name="odd_out_norm_router",
    )(att, p, g_ln_g, g_ln_b, ws, bs_t, wa, wb, x, gm, ln_g, ln_b, sc_f, sh_f, w_router_t)


ROUTE_BLOCK = 128
STARTS_PAD = LANES


def _lane_cumsum(mask):
    e, t = mask.shape
    i = lax.broadcasted_iota(jnp.int32, (LANES, LANES), 0)
    j = lax.broadcasted_iota(jnp.int32, (LANES, LANES), 1)
    upper = (i <= j).astype(BF16)
    x = jnp.where(mask, 1.0, 0.0).astype(BF16)
    off = jnp.zeros((e, 1), F32)
    out = []
    for c in range(t // LANES):
        blk = jnp.dot(x[:, c * LANES:(c + 1) * LANES], upper, preferred_element_type=F32) + off
        out.append(blk)
        off = blk[:, LANES - 1:LANES]
    return jnp.concatenate(out, axis=1)


def _select_kernel(aff_ref, pos_ref, starts_ref, *, cap):
    a = aff_ref[0]
    e, t = a.shape
    bits = pltpu.bitcast(a, jnp.int32)

    def search(i, thr):
        cand = thr | jnp.left_shift(jnp.int32(1), 30 - i)
        cnt = jnp.sum(jnp.where(bits >= cand, 1.0, 0.0), axis=1, keepdims=True)
        return jnp.where(cnt >= cap, cand, thr)

    thr = lax.fori_loop(0, 31, search, jnp.zeros((e, 1), jnp.int32))
    gt = bits > thr
    eq = bits == thr
    need = cap - jnp.sum(jnp.where(gt, 1.0, 0.0), axis=1, keepdims=True)
    sel = gt | (eq & (_lane_cumsum(eq) <= need))
    csel = _lane_cumsum(sel)
    pos_ref[0] = jnp.where(sel, csel - 1.0, -1.0).astype(jnp.int32)
    ti = lax.broadcasted_iota(jnp.int32, (t, STARTS_PAD), 0)
    ki = lax.broadcasted_iota(jnp.int32, (t, STARTS_PAD), 1)
    pick = jnp.where(ti == ki * ROUTE_BLOCK - 1, 1.0, 0.0)
    starts_ref[0] = (_dot_sel(csel, pick, 2) + 0.5).astype(jnp.int32)


def _select(aff, cap):
    b, e, t = aff.shape
    assert t % ROUTE_BLOCK == 0 and t // ROUTE_BLOCK < STARTS_PAD
    return pl.pallas_call(
        functools.partial(_select_kernel, cap=cap),
        grid=(b,),
        in_specs=[pl.BlockSpec((1, e, t), lambda i: (i, 0, 0))],
        out_specs=[pl.BlockSpec((1, e, t), lambda i: (i, 0, 0)), pl.BlockSpec((1, e, STARTS_PAD), lambda i: (i, 0, 0))],
        out_shape=[jax.ShapeDtypeStruct((b, e, t), jnp.int32), jax.ShapeDtypeStruct((b, e, STARTS_PAD), jnp.int32)],
        compiler_params=_cparams(("parallel",)),
        name="expert_select",
    )(aff)


def _window_start(start, align, cap, width):
    s = lax.shift_left(lax.shift_right_logical(start, align.bit_length() - 1), align.bit_length() - 1)
    return pl.multiple_of(jnp.minimum(s, cap - width), align)


def _gather_ffn_kernel(starts_ref, pos_ref, aff_ref, h_ref, w1_ref, w3_ref, w2_ref, y_ref, xs_scr, gate_scr, *,
                       width, rows, cap):
    ei, bi, k = pl.program_id(0), pl.program_id(1), pl.program_id(2)
    ne, nb = pl.num_programs(0), pl.num_programs(1)
    nsub = h_ref.shape[1] // ROUTE_BLOCK
    merged = xs_scr.shape[0] > cap
    base = pl.multiple_of(bi * cap, SUBLANES) if merged else 0

    @pl.when(k == 0)
    def _():
        xs_scr[pl.ds(base, cap), :] = jnp.zeros((cap, xs_scr.shape[1]), F32)
        gate_scr[pl.ds(base, cap), :] = jnp.zeros((cap, LANES), F32)

    def copy_rows(sb, start, w):
        s8 = _window_start(start, SUBLANES, cap, w)
        cols = slice(sb * ROUTE_BLOCK, (sb + 1) * ROUTE_BLOCK)
        jrow = lax.broadcasted_iota(jnp.int32, (w, ROUTE_BLOCK), 0)
        hit = pos_ref[0, :, cols] - s8 == jrow
        xs_scr[pl.ds(base + s8, w), :] += jnp.dot(jnp.where(hit, 1.0, 0.0).astype(BF16), h_ref[0, cols, :],
                                                  preferred_element_type=F32)
        gate = jnp.sum(jnp.where(hit, aff_ref[0, :, cols], 0.0), axis=1, keepdims=True)
        gate_scr[pl.ds(base + s8, w), :] += jnp.broadcast_to(gate, (w, LANES))

    for sb in range(nsub):
        copy_rows(sb, starts_ref[(bi * ne + ei) * STARTS_PAD + k * nsub + sb], width)

    last = k == pl.num_programs(2) - 1

    @pl.when(last & (bi == nb - 1) if merged else last)
    def _():
        for r in range(xs_scr.shape[0] // rows):
            x = xs_scr[r * rows:(r + 1) * rows, :].astype(BF16)
            h1 = jnp.dot(x, w1_ref[0], preferred_element_type=F32)
            h3 = jnp.dot(x, w3_ref[0], preferred_element_type=F32)
            hid = (h1 * jax.nn.sigmoid(h1)) * h3
            y = jnp.dot(hid.astype(BF16), w2_ref[0], preferred_element_type=F32)
            y = (y * gate_scr[r * rows:(r + 1) * rows, 0:1]).astype(y_ref.dtype)
            for s in range(max(rows // cap, 1)):
                n = min(rows, cap)
                row0 = r * rows + s * cap
                y_ref[row0 // cap, 0, row0 % cap:row0 % cap + n, :] = y[s * cap:s * cap + n]


def _gather_ffn(h, pos, aff, starts, w1, w3, w2, cap, tch=4096, rows=512):
    b, t, d = h.shape
    e, _, f = w1.shape
    tch = min(tch, t)
    merged = b * cap <= rows
    slots = b if merged else 1
    rows = min(rows, slots * cap)
    assert (slots * cap) % rows == 0 and (rows % cap == 0 or cap % rows == 0)
    width = min(ROUTE_BLOCK + SUBLANES, cap)
    grid_spec = pltpu.PrefetchScalarGridSpec(
        num_scalar_prefetch=1,
        grid=(e, b, t // tch),
        in_specs=[pl.BlockSpec((1, 1, tch), lambda ei, bi, k, s: (bi * e + ei, 0, k)),
                  pl.BlockSpec((1, 1, tch), lambda ei, bi, k, s: (bi * e + ei, 0, k)),
                  pl.BlockSpec((1, tch, d), lambda ei, bi, k, s: (bi, k, 0)),
                  pl.BlockSpec((1, d, f), lambda ei, bi, k, s: (ei, 0, 0)),
                  pl.BlockSpec((1, d, f), lambda ei, bi, k, s: (ei, 0, 0)),
                  pl.BlockSpec((1, f, d), lambda ei, bi, k, s: (ei, 0, 0))],
        out_specs=pl.BlockSpec((slots, 1, cap, d), (lambda ei, bi, k, s: (0, ei, 0, 0)) if merged
                               else (lambda ei, bi, k, s: (bi, ei, 0, 0))),
        scratch_shapes=[pltpu.VMEM((slots * cap, d), F32), pltpu.VMEM((slots * cap, LANES), F32)],
    )
    return pl.pallas_call(
        functools.partial(_gather_ffn_kernel, width=width, rows=rows, cap=cap),
        grid_spec=grid_spec,
        out_shape=jax.ShapeDtypeStruct((b, e, cap, d), BF16),
        compiler_params=_cparams(("parallel", "arbitrary", "arbitrary") if merged
                                 else ("parallel", "parallel", "arbitrary")),
        name="expert_gather_ffn",
    )(starts.reshape(-1), pos.reshape(b * e, 1, t), aff.reshape(b * e, 1, t), h, w1, w3, w2)


def _combine_kernel(starts_ref, post_ref, y_ref, x_ref, gf_ref, lg_ref, lb_ref, o_ref, acc_scr, *, width, group):
    bi, i, eg = pl.program_id(0), pl.program_id(1), pl.program_id(2)
    ne = pl.num_programs(2) * group
    cap = y_ref.shape[2]
    nsub = x_ref.shape[1] // ROUTE_BLOCK

    @pl.when(eg == 0)
    def _():
        acc_scr[...] = jnp.zeros_like(acc_scr)

    lane = lax.broadcasted_iota(jnp.int32, (ROUTE_BLOCK, ne), 1)
    jcol = lax.broadcasted_iota(jnp.int32, (ROUTE_BLOCK, width), 1)
    for sb in range(nsub):
        rs = slice(sb * ROUTE_BLOCK, (sb + 1) * ROUTE_BLOCK)
        pblk = post_ref[0, rs, :].astype(F32)
        total = None
        for j in range(group):
            ei = eg * group + j
            start = starts_ref[(bi * ne + ei) * STARTS_PAD + i * nsub + sb]
            s16 = _window_start(start, 2 * SUBLANES, cap, width)
            pcol = jnp.sum(jnp.where(lane == ei, pblk, 0.0), axis=1, keepdims=True)
            onehot = jnp.where(pcol.astype(jnp.int32) - s16 == jcol, 1.0, 0.0).astype(BF16)
            part = jnp.dot(onehot, y_ref[0, j, pl.ds(s16, width), :], preferred_element_type=F32)
            total = part if total is None else total + part
        acc_scr[rs, :] += total

    @pl.when(eg == pl.num_programs(2) - 1)
    def _():
        z = ALPHA * x_ref[0] + gf_ref[0] * acc_scr[...]
        o_ref[0] = _layer_norm_rows(z, lg_ref[...], lb_ref[...], LN_EPS)


def _combine(x, y, pos_t, starts, gf, ln_g, ln_b, tbo=1024, group=4):
    b, t, d = x.shape
    _, e, cap, _ = y.shape
    tbo = min(tbo, t)
    width = min(ROUTE_BLOCK + 2 * SUBLANES, cap)
    grid_spec = pltpu.PrefetchScalarGridSpec(
        num_scalar_prefetch=1,
        grid=(b, t // tbo, e // group),
        in_specs=[pl.BlockSpec((1, tbo, e), lambda bi, i, eg, s: (bi, i, 0)),
                  pl.BlockSpec((1, group, cap, d), lambda bi, i, eg, s: (bi, eg, 0, 0)),
                  pl.BlockSpec((1, tbo, d), lambda bi, i, eg, s: (bi, i, 0)),
                  pl.BlockSpec((1, 1, d), lambda bi, i, eg, s: (bi, 0, 0)),
                  pl.BlockSpec((1, d), lambda bi, i, eg, s: (0, 0)),
                  pl.BlockSpec((1, d), lambda bi, i, eg, s: (0, 0))],
        out_specs=pl.BlockSpec((1, tbo, d), lambda bi, i, eg, s: (bi, i, 0)),
        scratch_shapes=[pltpu.VMEM((tbo, d), F32)],
    )
    return pl.pallas_call(
        functools.partial(_combine_kernel, width=width, group=group),
        grid_spec=grid_spec,
        out_shape=jax.ShapeDtypeStruct((b, t, d), F32),
        compiler_params=_cparams(("parallel", "parallel", "arbitrary")),
        name="expert_combine_norm",
    )(starts.reshape(-1), pos_t, y, x, gf, ln_g.reshape(1, d), ln_b.reshape(1, d))


def _cast_kernel(x_ref, o_ref):
    o_ref[...] = x_ref[...].astype(o_ref.dtype)


def _to_bf16(w, l):
    _, e, r, c = w.shape
    return pl.pallas_call(
        _cast_kernel,
        grid=(e,),
        in_specs=[pl.BlockSpec((None, 1, r, c), lambda i: (l, i, 0, 0))],
        out_specs=pl.BlockSpec((1, r, c), lambda i: (i, 0, 0)),
        out_shape=jax.ShapeDtypeStruct((e, r, c), BF16),
        compiler_params=_cparams(("parallel",)),
        name="cast_bf16",
    )(w)


def _moe(x, h, aff, gf, w1, w3, w2, ln_g, ln_b):
    t = x.shape[1]
    cap = EC_FACTOR * t // N_EXPERTS
    pos, starts = _select(aff, cap)
    y = _gather_ffn(h, pos, aff, starts, w1, w3, w2, cap)
    return _combine(x, y, jnp.transpose(pos, (0, 2, 1)), starts, gf, ln_g, ln_b)


def _even_layer(hx_in, hc_in, prm, ctx_out):
    w_in = prm["w_in"]
    splits = (3 * D_CONV, RWKV_COLS)
    pcx, prx = _inproj(*hx_in, w_in, splits)
    pcc, prc = _inproj(*hc_in, w_in, splits)
    fx = _rwkv_features(prx, prm)
    fc = _rwkv_features(prc, prm)
    b = prx.shape[0]
    zero = jnp.zeros((b, N_PAIRS, LANES, LANES), F32)
    ys_x, ys_c = [], []
    for d, rev in ((0, False), (1, True)):
        def args(f):
            r, kk, v, kr0, kr1, lw0, lw1, b0, b1 = f[:9]
            return (r, (lw0, lw1)[d], kk, (b0, b1)[d], (kr0, kr1)[d], v)
        yc, s_ctx = _rwkv_scan(*args(fc), zero, rev)
        yx, _ = _rwkv_scan(*args(fx), s_ctx, rev)
        ys_x.append(yx)
        ys_c.append(yc)
    norm = (prm["conv_w"], prm["gn_g"], prm["gn_b"])
    out_x = (ys_x[0], ys_x[1], fx[9], fx[10], pcx) + norm
    out_c = (ys_c[0], ys_c[1], fc[9], fc[10], pcc) + norm if ctx_out else None
    return out_c, out_x


def _odd_layer(hx_in, hc_in, prm, lam_init, ctx_out):
    w_in = prm["w_in"]
    qscale = DIFF_HD ** -0.5 * math.log2(math.e)
    qx, kx, vx, gx = _inproj_attn(*hx_in, w_in, _rope_tables(hx_in[0].shape[1]), qscale)
    qc, kc, vc, gc = _inproj_attn(*hc_in, w_in, None, qscale)
    k_all = jnp.concatenate([kc, kx], axis=1)
    v_all = jnp.concatenate([vc, vx], axis=1)
    lam = (jnp.exp(jnp.sum(prm["lam_q1"] * prm["lam_k1"])) - jnp.exp(jnp.sum(prm["lam_q2"] * prm["lam_k2"]))
           + lam_init).reshape(1).astype(F32)
    att_x = _diff_attention(qx, k_all, v_all, lam, prm["subln_g"], lam_init)
    gmlp = (prm["gmlp_ln_g"], prm["gmlp_ln_b"], prm["gmlp_ws"], prm["gmlp_bs"])
    out_c = None
    if ctx_out:
        out_c = (_diff_attention(qc, kc, vc, lam, prm["subln_g"], lam_init), gc) + gmlp
    return out_c, (att_x, gx) + gmlp


def kernel(x, c, ctx, c_ctx, w_mod, b_mod, ln_g, ln_b, even_w_in, even_w_out, conv_w, shift_mu, decay_up, decay_0, a_up, a_0, g_up, k_xi, k_alpha, r_bonus, gn_g, gn_b, odd_w_in, odd_w_out, lam_q1, lam_k1, lam_q2, lam_k2, subln_g, gmlp_ln_g, gmlp_ln_b, gmlp_ws, gmlp_bs, w_router, w_e1, w_e3, w_e2):
    bsz, _, d = x.shape
    assert d == D_MODEL and bsz <= SUBLANES - 1
    c8 = jnp.zeros((SUBLANES, d), F32).at[:bsz].set(c).at[bsz].set(c_ctx)
    mod = _modulation(c8, w_mod, b_mod)
    for l in range(DEPTH):
        ctx_out = l < DEPTH - 1
        i = l // 2
        mx = mod[l, :bsz].reshape(bsz, 1, 6, d)
        mc = jnp.broadcast_to(mod[l, bsz].reshape(1, 1, 6, d), (bsz, 1, 6, d))
        part = lambda m, n: m[:, :, n]
        hx_in = (x, part(mx, 1), part(mx, 0))
        hc_in = (ctx, part(mc, 1), part(mc, 0))
        if l % 2 == 0:
            prm = dict(w_in=even_w_in[i].astype(BF16), conv_w=conv_w[i], shift_mu=shift_mu[i],
                       decay_up=decay_up[i], decay_0=decay_0[i], a_up=a_up[i], a_0=a_0[i], g_up=g_up[i],
                       k_xi=k_xi[i], k_alpha=k_alpha[i], r_bonus=r_bonus[i], gn_g=gn_g[i], gn_b=gn_b[i])
            out_c, out_x = _even_layer(hx_in, hc_in, prm, ctx_out)
            w_out = even_w_out[i].astype(BF16)
        else:
            lam_init = 0.8 - 0.6 * math.exp(-0.3 * l)
            prm = dict(w_in=odd_w_in[i].astype(BF16), lam_q1=lam_q1[i], lam_k1=lam_k1[i], lam_q2=lam_q2[i],
                       lam_k2=lam_k2[i], subln_g=subln_g[i], gmlp_ln_g=gmlp_ln_g[i], gmlp_ln_b=gmlp_ln_b[i],
                       gmlp_ws=gmlp_ws[i], gmlp_bs=gmlp_bs[i])
            out_c, out_x = _odd_layer(hx_in, hc_in, prm, lam_init, ctx_out)
            w_out = odd_w_out[i].astype(BF16)
        wr_t = jnp.transpose(w_router[l])
        w1, w3, w2 = _to_bf16(w_e1, l), _to_bf16(w_e3, l), _to_bf16(w_e2, l)
        project = _even_out if l % 2 == 0 else _odd_out
        x, h, aff = project(*out_x, w_out, x, part(mx, 2), ln_g[l, 0], ln_b[l, 0], part(mx, 4), part(mx, 3), wr_t)
        x = _moe(x, h, aff, part(mx, 5), w1, w3, w2, ln_g[l, 1], ln_b[l, 1])
        if ctx_out:
            ctx, h, aff = project(*out_c, w_out, ctx, part(mc, 2), ln_g[l, 0], ln_b[l, 0],
                                  part(mc, 4), part(mc, 3), wr_t)
            ctx = _moe(ctx, h, aff, part(mc, 5), w1, w3, w2, ln_g[l, 1], ln_b[l, 1])
    return x
```
